```python
import math
import jax, jax.numpy as jnp
from jax import lax
import numpy as np

D_MODEL = 1024
BATCH = 8
SEQ = 4096
DEPTH = 4

GRID_W = 64
CTX_LEN = 256
EPS = 1e-6
ROPE_BASE = 10000.0
NEG = -1e30

MLA_HEADS = 8
MLA_NOPE = 64
MLA_ROPE = 32
MLA_V = 64
MLA_QK = MLA_NOPE + MLA_ROPE
MLA_Q_RANK = 384
MLA_KV_RANK = 256
MLA_WIDTH = MLA_HEADS * MLA_V
Q_BLOCK = 128

SWA_HEADS = 8
SWA_KV_HEADS = 2
SWA_DIM = 64
SWA_WINDOW = 128
SWA_BLOCK = 128
SWA_WIDTH = SWA_HEADS * SWA_DIM

S5_GROUP = 16
S5_GROUPS = 32
S5_STATE = 64
S5_WIDTH = S5_GROUP * S5_GROUPS

MIX_WIDTH = MLA_WIDTH + SWA_WIDTH + S5_WIDTH
IN_SPLITS = (MLA_Q_RANK, MLA_KV_RANK, MLA_ROPE, MLA_WIDTH,
             SWA_HEADS * SWA_DIM, SWA_KV_HEADS * SWA_DIM, SWA_KV_HEADS * SWA_DIM, SWA_WIDTH,
             S5_WIDTH, S5_WIDTH)
IN_WIDTH = 3488

kernel_name = "hybrid_parallel_mla_swa_s5_dit"


def rmsnorm(x, g):
    xf = x.astype(jnp.float32)
    y = xf * lax.rsqrt(jnp.mean(xf * xf, axis=-1, keepdims=True) + EPS)
    return (y * g.astype(jnp.float32)).astype(x.dtype)


def modulate(x, g, shift, scale):
    return rmsnorm(x, g) * (1.0 + scale) + shift


def split_cols(z):
    offs = [int(o) for o in np.cumsum(IN_SPLITS)[:-1]]
    return jnp.split(z, offs, axis=-1)


def axial_rope(row, col, rot_dim):
    n_freq = rot_dim // 4
    freqs = ROPE_BASE ** (-jnp.arange(n_freq, dtype=jnp.float32) / n_freq)
    ang = jnp.concatenate([row.astype(jnp.float32)[:, None] * freqs,
                           col.astype(jnp.float32)[:, None] * freqs], axis=-1)
    return jnp.cos(ang), jnp.sin(ang)


def rope_tail(x, start, rope):
    cos, sin = rope
    xr = x[..., start:].astype(jnp.float32)
    d2 = xr.shape[-1] // 2
    x1, x2 = xr[..., :d2], xr[..., d2:]
    cs, sn = cos[:, None, :], sin[:, None, :]
    rot = jnp.concatenate([x1 * cs - x2 * sn, x2 * cs + x1 * sn], axis=-1).astype(x.dtype)
    return jnp.concatenate([x[..., :start], rot], axis=-1)


def mla_q(cq, g_cq, w_uq, g_qn, rope):
    B, L, _ = cq.shape
    q = rmsnorm((rmsnorm(cq, g_cq) @ w_uq).reshape(B, L, MLA_HEADS, MLA_QK), g_qn)
    return q if rope is None else rope_tail(q, MLA_NOPE, rope)


def mla_kv(ckv, kr, g_ckv, w_ukv, g_kn, rope):
    B, L, _ = ckv.shape
    kv = (rmsnorm(ckv, g_ckv) @ w_ukv).reshape(B, L, MLA_HEADS, MLA_NOPE + MLA_V)
    k_rope = jnp.broadcast_to(kr[:, :, None, :], (B, L, MLA_HEADS, MLA_ROPE))
    k = rmsnorm(jnp.concatenate([kv[..., :MLA_NOPE], k_rope], axis=-1), g_kn)
    if rope is not None:
        k = rope_tail(k, MLA_NOPE, rope)
    return k, kv[..., MLA_NOPE:]


def dense_attention(q, k, v):
    B, Lq, H, d = q.shape
    s = jnp.einsum('bqhd,bkhd->bhqk', q, k, preferred_element_type=jnp.float32) * (d ** -0.5)
    p = jax.nn.softmax(s, axis=-1).astype(v.dtype)
    return jnp.einsum('bhqk,bkhd->bqhd', p, v).reshape(B, Lq, -1)


def mla_latent_attention(q, k, v, k_c, v_c):
    B, L, H, d = q.shape
    nb = L // Q_BLOCK
    C = k_c.shape[1]
    scale = d ** -0.5
    qb = q.reshape(B, nb, Q_BLOCK, H, d).transpose(1, 0, 2, 3, 4)

    def one_block(qi):
        s_c = jnp.einsum('bqhd,bkhd->bhqk', qi, k_c, preferred_element_type=jnp.float32) * scale
        s_l = jnp.einsum('bqhd,bkhd->bhqk', qi, k, preferred_element_type=jnp.float32) * scale
        p = jax.nn.softmax(jnp.concatenate([s_c, s_l], axis=-1), axis=-1).astype(v.dtype)
        return (jnp.einsum('bhqk,bkhd->bqhd', p[..., :C], v_c)
                + jnp.einsum('bhqk,bkhd->bqhd', p[..., C:], v))

    out = lax.map(one_block, qb)
    return out.transpose(1, 0, 2, 3, 4).reshape(B, L, H * v.shape[-1])


def swa_heads(z, n_heads, g, rope):
    B, L, _ = z.shape
    h = rmsnorm(z.reshape(B, L, n_heads, SWA_DIM), g)
    return h if rope is None else rope_tail(h, 0, rope)


def sink_logits(sink, lead_shape):
    s = sink.astype(jnp.float32).reshape(SWA_KV_HEADS, SWA_HEADS // SWA_KV_HEADS)
    return jnp.broadcast_to(s[:, :, None, None], lead_shape + (1,))


def swa_latent_attention(q, k, v, k_c, v_c, sink):
    B, L, H, d = q.shape
    KV, G, W = SWA_KV_HEADS, H // SWA_KV_HEADS, SWA_BLOCK
    nb = L // W
    C = k_c.shape[1]
    scale = d ** -0.5
    qb = q.reshape(B, nb, W, KV, G, d)
    pad = ((0, 0), (W, W), (0, 0), (0, 0))
    kp = jnp.pad(k, pad).reshape(B, nb + 2, W, KV, d)
    vp = jnp.pad(v, pad).reshape(B, nb + 2, W, KV, d)
    band = lambda t: jnp.concatenate([t[:, :-2], t[:, 1:-1], t[:, 2:]], axis=2)
    kw, vw = band(kp), band(vp)
    rel = jnp.arange(3 * W)[None, :] - W - jnp.arange(W)[:, None]
    key_pos = jnp.arange(nb)[:, None] * W - W + jnp.arange(3 * W)[None, :]
    valid = (jnp.abs(rel) <= SWA_WINDOW)[None] & ((key_pos >= 0) & (key_pos < L))[:, None, :]
    s_w = jnp.einsum('bnqhgd,bnkhd->bnhgqk', qb, kw, preferred_element_type=jnp.float32) * scale
    s_w = jnp.where(valid[None, :, None, None], s_w, NEG)
    s_c = jnp.einsum('bnqhgd,bchd->bnhgqc', qb, k_c, preferred_element_type=jnp.float32) * scale
    s_s = sink_logits(sink, (B, nb, KV, G, W))[..., None, :, :, :, :].reshape(B, nb, KV, G, W, 1) \
        if False else jnp.broadcast_to(
            sink.astype(jnp.float32).reshape(KV, G)[None, None, :, :, None, None], (B, nb, KV, G, W, 1))
    p = jax.nn.softmax(jnp.concatenate([s_s, s_c, s_w], axis=-1), axis=-1).astype(v.dtype)
    o = (jnp.einsum('bnhgqc,bchd->bnqhgd', p[..., 1:1 + C], v_c)
         + jnp.einsum('bnhgqk,bnkhd->bnqhgd', p[..., 1 + C:], vw))
    return o.reshape(B, L, H * d)


def swa_context_attention(q, k, v, sink):
    B, C, H, d = q.shape
    KV, G = SWA_KV_HEADS, H // SWA_KV_HEADS
    qg = q.reshape(B, C, KV, G, d)
    s = jnp.einsum('bqhgd,bchd->bhgqc', qg, k, preferred_element_type=jnp.float32) * (d ** -0.5)
    s_s = jnp.broadcast_to(sink.astype(jnp.float32).reshape(KV, G)[None, :, :, None, None], (B, KV, G, C, 1))
    p = jax.nn.softmax(jnp.concatenate([s_s, s], axis=-1), axis=-1)[..., 1:].astype(v.dtype)
    return jnp.einsum('bhgqc,bchd->bqhgd', p, v).reshape(B, C, H * d)


def _linrec(e1, e2):
    a1, b1 = e1
    a2, b2 = e2
    return a1 * a2, a2 * b1 + b2


def s5_scan(a_bar, bu, reverse):
    a = jnp.broadcast_to(a_bar[None, None], (bu.shape[0], 1) + a_bar.shape)
    _, h = lax.associative_scan(_linrec, (a, bu), reverse=reverse, axis=0)
    return h


def s5_direction(ug, ugc, a_re, a_im, log_dt, b_re, b_im, c_re, c_im, reverse, with_ctx):
    f32 = jnp.float32
    A = lax.complex(a_re.astype(f32), a_im.astype(f32))
    a_bar = jnp.exp(jnp.exp(log_dt.astype(f32))[:, None] * A)
    b_bar = ((a_bar - 1.0) / A)[..., None] * lax.complex(b_re.astype(f32), b_im.astype(f32))
    c_mat = lax.complex(c_re.astype(f32), c_im.astype(f32))
    drive = lambda u: jnp.einsum('blgp,gnp->lbgn', u.astype(jnp.complex64), b_bar)
    readout = lambda h: jnp.einsum('lbgn,gpn->blgp', h, c_mat).real
    h_c = s5_scan(a_bar, drive(ugc), reverse)
    h_init = h_c[0] if reverse else h_c[-1]
    bu = drive(ug)
    bu = bu.at[-1 if reverse else 0].add(a_bar[None] * h_init)
    h = s5_scan(a_bar, bu, reverse)
    y = readout(h)
    y_c = readout(h_c) if with_ctx else None
    return y, y_c


def s5_glu(y, w_glu):
    z = jax.nn.gelu(y) @ w_glu
    za, zb = jnp.split(z, 2, axis=-1)
    return za * jax.nn.sigmoid(zb)


def s5_branch(u, u_c, a_re, a_im, log_dt, b_re, b_im, c_re, c_im, d, w_glu, with_ctx):
    B, L, _ = u.shape
    C = u_c.shape[1]
    ug = u.reshape(B, L, S5_GROUPS, S5_GROUP)
    ugc = u_c.reshape(B, C, S5_GROUPS, S5_GROUP)
    y = (d * u).astype(jnp.float32)
    y_c = (d * u_c).astype(jnp.float32)
    for direc in range(2):
        yl, yc = s5_direction(ug, ugc, a_re[direc], a_im[direc], log_dt[direc], b_re[direc], b_im[direc],
                              c_re[direc], c_im[direc], direc == 1, with_ctx)
        y = y + yl.reshape(B, L, S5_WIDTH)
        if with_ctx:
            y_c = y_c + yc.reshape(B, C, S5_WIDTH)
    out = s5_glu(y.astype(u.dtype), w_glu)
    out_c = s5_glu(y_c.astype(u.dtype), w_glu) if with_ctx else None
    return out, out_c


def setup_inputs(seed: int = 0) -> dict:
    key = jax.random.key(seed)
    ks = jax.random.split(key, 28)
    f32 = jnp.float32
    D = D_MODEL

    def nrm(k, shape, s):
        return jax.random.normal(k, shape, f32) * s

    n_idx = jnp.arange(S5_STATE, dtype=f32)
    s5_shape = (DEPTH, 2, S5_GROUPS, S5_STATE)
    return {
        "x": nrm(ks[0], (BATCH, SEQ, D), 1.0),
        "c": nrm(ks[1], (BATCH, D), 1.0),
        "ctx": nrm(ks[2], (BATCH, CTX_LEN, D), 1.0),
        "c_ctx": nrm(ks[3], (D,), 1.0),
        "norm_g": 1.0 + nrm(ks[4], (DEPTH, D), 0.02),
        "w_ada": nrm(ks[5], (DEPTH, D, 3 * D), 0.5 * D ** -0.5),
        "b_ada": nrm(ks[6], (DEPTH, 3 * D), 0.02),
        "w_in": nrm(ks[7], (DEPTH, D, IN_WIDTH), D ** -0.5),
        "w_out": nrm(ks[8], (DEPTH, MIX_WIDTH, D), MIX_WIDTH ** -0.5),
        "mla_g_cq": 1.0 + nrm(ks[9], (DEPTH, MLA_Q_RANK), 0.02),
        "mla_g_ckv": 1.0 + nrm(ks[10], (DEPTH, MLA_KV_RANK), 0.02),
        "mla_w_uq": nrm(ks[11], (DEPTH, MLA_Q_RANK, MLA_HEADS * MLA_QK), MLA_Q_RANK ** -0.5),
        "mla_w_ukv": nrm(ks[12], (DEPTH, MLA_KV_RANK, MLA_HEADS * (MLA_NOPE + MLA_V)), MLA_KV_RANK ** -0.5),
        "mla_g_qn": 1.0 + nrm(ks[13], (DEPTH, MLA_QK), 0.02),
        "mla_g_kn": 1.0 + nrm(ks[14], (DEPTH, MLA_QK), 0.02),
        "swa_g_qn": 1.0 + nrm(ks[15], (DEPTH, SWA_DIM), 0.02),
        "swa_g_kn": 1.0 + nrm(ks[16], (DEPTH, SWA_DIM), 0.02),
        "swa_sink": nrm(ks[17], (DEPTH, SWA_HEADS), 0.5),
        "s5_a_re": -0.5 + nrm(ks[18], s5_shape, 0.01),
        "s5_a_im": math.pi * n_idx + nrm(ks[19], s5_shape, 0.01),
        "s5_log_dt": jax.random.uniform(ks[20], (DEPTH, 2, S5_GROUPS), f32, math.log(1e-3), math.log(1e-1)),
        "s5_b_re": nrm(ks[21], (DEPTH, 2, S5_GROUPS, S5_STATE, S5_GROUP), (2 * S5_GROUP) ** -0.5),
        "s5_b_im": nrm(ks[22], (DEPTH, 2, S5_GROUPS, S5_STATE, S5_GROUP), (2 * S5_GROUP) ** -0.5),
        "s5_c_re": nrm(ks[23], (DEPTH, 2, S5_GROUPS, S5_GROUP, S5_STATE), S5_STATE ** -0.5),
        "s5_c_im": nrm(ks[24], (DEPTH, 2, S5_GROUPS, S5_GROUP, S5_STATE), S5_STATE ** -0.5),
        "s5_d": nrm(ks[25], (DEPTH, S5_WIDTH), 0.5),
        "s5_w_glu": nrm(ks[26], (DEPTH, S5_WIDTH, 2 * S5_WIDTH), S5_WIDTH ** -0.5),
    }


def reference(x, c, ctx, c_ctx, norm_g, w_ada, b_ada, w_in, w_out,
              mla_g_cq, mla_g_ckv, mla_w_uq, mla_w_ukv, mla_g_qn, mla_g_kn,
              swa_g_qn, swa_g_kn, swa_sink,
              s5_a_re, s5_a_im, s5_log_dt, s5_b_re, s5_b_im, s5_c_re, s5_c_im, s5_d, s5_w_glu):
    n_lat = x.shape[1]
    rows = n_lat // GRID_W
    r_idx, c_idx = jnp.meshgrid(jnp.arange(rows), jnp.arange(GRID_W), indexing="ij")
    r_idx, c_idx = r_idx.reshape(-1), c_idx.reshape(-1)
    rope_mla = axial_rope(r_idx, c_idx, MLA_ROPE)
    rope_swa = axial_rope(r_idx, c_idx, SWA_DIM)
    silu_c = jax.nn.silu(c)
    silu_cc = jax.nn.silu(c_ctx)

    for l in range(DEPTH):
        upd_ctx = l < DEPTH - 1
        shift, scale, gate = jnp.split((silu_c @ w_ada[l] + b_ada[l])[:, None, :], 3, axis=-1)
        shift_c, scale_c, gate_c = jnp.split(silu_cc @ w_ada[l] + b_ada[l], 3, axis=-1)
        (cq, ckv, kr, g_mla, sq, sk, sv, g_swa, u, g_s5) = split_cols(
            modulate(x, norm_g[l], shift, scale) @ w_in[l])
        (cq_c, ckv_c, kr_c, g_mla_c, sq_c, sk_c, sv_c, g_swa_c, u_c, g_s5_c) = split_cols(
            modulate(ctx, norm_g[l], shift_c, scale_c) @ w_in[l])

        k_a, v_a = mla_kv(ckv, kr, mla_g_ckv[l], mla_w_ukv[l], mla_g_kn[l], rope_mla)
        k_ac, v_ac = mla_kv(ckv_c, kr_c, mla_g_ckv[l], mla_w_ukv[l], mla_g_kn[l], None)
        q_a = mla_q(cq, mla_g_cq[l], mla_w_uq[l], mla_g_qn[l], rope_mla)
        o_a = mla_latent_attention(q_a, k_a, v_a, k_ac, v_ac)

        q_b = swa_heads(sq, SWA_HEADS, swa_g_qn[l], rope_swa)
        k_b = swa_heads(sk, SWA_KV_HEADS, swa_g_kn[l], rope_swa)
        v_b = sv.reshape(sv.shape[0], n_lat, SWA_KV_HEADS, SWA_DIM)
        k_bc = swa_heads(sk_c, SWA_KV_HEADS, swa_g_kn[l], None)
        v_bc = sv_c.reshape(sv_c.shape[0], sv_c.shape[1], SWA_KV_HEADS, SWA_DIM)
        o_b = swa_latent_attention(q_b, k_b, v_b, k_bc, v_bc, swa_sink[l])

        o_c, o_cc = s5_branch(u, u_c, s5_a_re[l], s5_a_im[l], s5_log_dt[l], s5_b_re[l], s5_b_im[l],
                              s5_c_re[l], s5_c_im[l], s5_d[l], s5_w_glu[l], upd_ctx)

        if upd_ctx:
            o_ac = dense_attention(mla_q(cq_c, mla_g_cq[l], mla_w_uq[l], mla_g_qn[l], None), k_ac, v_ac)
            o_bc = swa_context_attention(swa_heads(sq_c, SWA_HEADS, swa_g_qn[l], None), k_bc, v_bc, swa_sink[l])
            mix_c = jnp.concatenate([o_ac * jax.nn.silu(g_mla_c), o_bc * jax.nn.silu(g_swa_c),
                                     o_cc * jax.nn.silu(g_s5_c)], axis=-1)
            ctx_next = ctx + gate_c * (mix_c @ w_out[l])
        mix = jnp.concatenate([o_a * jax.nn.silu(g_mla), o_b * jax.nn.silu(g_swa),
                               o_c * jax.nn.silu(g_s5)], axis=-1)
        x = x + gate * (mix @ w_out[l])
        if upd_ctx:
            ctx = ctx_next
    return x
```

```python
import functools
import math

import jax
import jax.numpy as jnp
from jax import lax
from jax.experimental import pallas as pl
from jax.experimental.pallas import tpu as pltpu

F32 = jnp.float32
BF16 = jnp.bfloat16

GRID_W = 64
EPS = 1e-6
ROPE_BASE = 10000.0
NEG = -1e30
LOG2E = math.log2(math.e)

MLA_HEADS = 8
MLA_NOPE = 64
MLA_ROPE = 32
MLA_V = 64
MLA_QK = MLA_NOPE + MLA_ROPE
MLA_Q_RANK = 384
MLA_KV_RANK = 256

SWA_HEADS = 8
SWA_KV_HEADS = 2
SWA_DIM = 64
SWA_WINDOW = 128

S5_GROUP = 16
S5_GROUPS = 32
S5_STATE = 64
S5_CHUNK = 16

LANES = 128
TOK = 256
KV_CHUNK = 256
SWA_BLOCK = 128
MOD_ROWS = 16

C_CQ = 0
C_CKV = 384
C_KR = 640
C_SQ = 768
C_SK = 1280
C_SV = 1536
C_U = 1792
C_GATE = 2304
C_END = 3840

VMEM_LIMIT = 56 * 1024 * 1024


def _cparams(n_axes):
    return pltpu.CompilerParams(dimension_semantics=("arbitrary",) * n_axes,
                                vmem_limit_bytes=VMEM_LIMIT)


def _ada_kernel(c_ref, w_ref, b_ref, o_ref):
    cc = c_ref[...]
    s = cc * jax.nn.sigmoid(cc)
    o_ref[0] = jnp.dot(s, w_ref[0], preferred_element_type=F32,
                       precision=lax.Precision.HIGHEST) + b_ref[0]


def _ada_call(cc, w_ada, b_ada):
    depth, d, n3 = w_ada.shape
    tn = 768
    return pl.pallas_call(
        _ada_kernel,
        out_shape=jax.ShapeDtypeStruct((depth, MOD_ROWS, n3), F32),
        grid=(depth, n3 // tn),
        in_specs=[pl.BlockSpec((MOD_ROWS, d), lambda l, j: (0, 0)),
                  pl.BlockSpec((1, d, tn), lambda l, j: (l, 0, j)),
                  pl.BlockSpec((1, 1, tn), lambda l, j: (l, 0, j))],
        out_specs=pl.BlockSpec((1, MOD_ROWS, tn), lambda l, j: (l, 0, j)),
        compiler_params=_cparams(2),
        name="ada",
    )(cc, w_ada, b_ada.reshape(depth, 1, n3))


def _rope(y, tab_ref, half):
    return (y * tab_ref[0]
            + pltpu.roll(y, half, 1) * tab_ref[1]
            + pltpu.roll(y, LANES - half, 1) * tab_ref[2])


def _proj_kernel(x_ref, c_ref, mod_ref, ng_ref, win_ref, gcq_ref, wuq_ref, gq_ref,
                 gckv_ref, wukv_ref, gk_ref, gsq_ref, gsk_ref, rm_ref, rs_ref,
                 qa_ref, ka_ref, va_ref, qb_ref, kb_ref, vb_ref, u_ref, gt_ref):
    t = pl.program_id(1)
    x = jnp.where(t == 0, c_ref[0], x_ref[0])
    mod = mod_ref[0, 0]
    y = x * lax.rsqrt(jnp.mean(x * x, axis=-1, keepdims=True) + EPS) * ng_ref[0]
    xn = (y * (1.0 + mod[1:2]) + mod[0:1]).astype(BF16)

    def seg(a, b):
        return jnp.dot(xn, win_ref[0, :, a:b], preferred_element_type=F32)

    lane = lax.broadcasted_iota(jnp.int32, (TOK, LANES), 1)
    lo = lane < 64

    cq = seg(C_CQ, C_CKV)
    cqn = (cq * lax.rsqrt(jnp.mean(cq * cq, axis=-1, keepdims=True) + EPS) * gcq_ref[0]).astype(BF16)
    qf = jnp.dot(cqn, wuq_ref[0], preferred_element_type=F32)
    for h in range(MLA_HEADS):
        s = qf[:, h * LANES:(h + 1) * LANES]
        r = lax.rsqrt(jnp.sum(s * s, axis=-1, keepdims=True) * (1.0 / MLA_QK) + EPS)
        qa_ref[0, h] = _rope(s * r * gq_ref[0], rm_ref, MLA_ROPE // 2).astype(BF16)

    ckv = seg(C_CKV, C_KR)
    ckvn = (ckv * lax.rsqrt(jnp.mean(ckv * ckv, axis=-1, keepdims=True) + EPS) * gckv_ref[0]).astype(BF16)
    kvf = jnp.dot(ckvn, wukv_ref[0], preferred_element_type=F32)
    kr = seg(C_KR, C_SQ)
    for h in range(MLA_HEADS):
        s = kvf[:, h * LANES:(h + 1) * LANES] + kr
        r = lax.rsqrt(jnp.sum(s * s, axis=-1, keepdims=True) * (1.0 / MLA_QK) + EPS)
        ka_ref[0, h] = _rope(s * r * gk_ref[0], rm_ref, MLA_ROPE // 2).astype(BF16)
    va_ref[0] = kvf[:, MLA_HEADS * LANES:].astype(BF16)

    sq = seg(C_SQ, C_SK)
    for p in range(SWA_HEADS // 2):
        s = sq[:, p * LANES:(p + 1) * LANES]
        s2 = s * s
        r_lo = lax.rsqrt(jnp.sum(jnp.where(lo, s2, 0.0), axis=-1, keepdims=True) * (1.0 / SWA_DIM) + EPS)
        r_hi = lax.rsqrt(jnp.sum(jnp.where(lo, 0.0, s2), axis=-1, keepdims=True) * (1.0 / SWA_DIM) + EPS)
        yq = s * jnp.where(lo, r_lo, r_hi) * gsq_ref[0]
        qb_ref[0, :, p * LANES:(p + 1) * LANES] = _rope(yq, rs_ref, SWA_DIM // 2).astype(BF16)
    sk = seg(C_SK, C_SV)
    for j in range(SWA_KV_HEADS):
        s = sk[:, j * LANES:(j + 1) * LANES]
        r = lax.rsqrt(jnp.sum(jnp.where(lo, s * s, 0.0), axis=-1, keepdims=True) * (1.0 / SWA_DIM) + EPS)
        kb_ref[0, :, j * LANES:(j + 1) * LANES] = _rope(s * r * gsk_ref[0], rs_ref, SWA_DIM // 2).astype(BF16)
    vb_ref[0] = seg(C_SV, C_U).astype(BF16)

    u_ref[0] = seg(C_U, C_GATE).astype(BF16)
    g = seg(C_GATE, C_END)
    gt_ref[0] = (g * jax.nn.sigmoid(g)).astype(BF16)


def _proj_call(l, x, ctx, mod, P, rope_mla, rope_swa):
    B, L, D = x.shape
    C = ctx.shape[1]
    S = C + L
    nt = S // TOK

    def wspec(arr):
        shp = arr.shape
        return pl.BlockSpec((1,) + shp[1:], lambda b, t: (l,) + (0,) * (len(shp) - 1))

    weights = [P["norm_g"], P["w_in"], P["g_cq"], P["w_uq"], P["g_q"],
               P["g_ckv"], P["w_ukv"], P["g_k"], P["g_sq"], P["g_sk"]]
    in_specs = ([pl.BlockSpec((1, TOK, D), lambda b, t: (b, jnp.maximum(t - 1, 0), 0)),
                 pl.BlockSpec((1, TOK, D), lambda b, t: (b, 0, 0)),
                 pl.BlockSpec((1, 1, 3, D), lambda b, t: (l, jnp.where(t == 0, B, b), 0, 0))]
                + [wspec(w) for w in weights]
                + [pl.BlockSpec((3, TOK, LANES), lambda b, t: (0, t, 0)),
                   pl.BlockSpec((3, TOK, LANES), lambda b, t: (0, t, 0))])
    out_shape = [jax.ShapeDtypeStruct((B, MLA_HEADS, S, LANES), BF16),
                 jax.ShapeDtypeStruct((B, MLA_HEADS, S, LANES), BF16),
                 jax.ShapeDtypeStruct((B, S, 512), BF16),
                 jax.ShapeDtypeStruct((B, S, 512), BF16),
                 jax.ShapeDtypeStruct((B, S, 256), BF16),
                 jax.ShapeDtypeStruct((B, S, 256), BF16),
                 jax.ShapeDtypeStruct((B, S, 512), BF16),
                 jax.ShapeDtypeStruct((B, S, 1536), BF16)]
    out_specs = [pl.BlockSpec((1, MLA_HEADS, TOK, LANES), lambda b, t: (b, 0, t, 0)),
                 pl.BlockSpec((1, MLA_HEADS, TOK, LANES), lambda b, t: (b, 0, t, 0))]
    out_specs += [pl.BlockSpec((1, TOK, s.shape[2]), lambda b, t: (b, t, 0)) for s in out_shape[2:]]
    return pl.pallas_call(
        _proj_kernel, out_shape=out_shape, grid=(B, nt),
        in_specs=in_specs, out_specs=out_specs,
        compiler_params=_cparams(2), name="proj",
    )(x, ctx, mod, *weights, rope_mla, rope_swa)


def _mla_kernel(q_ref, k_ref, v_ref, o_ref, s_ref, *, n_chunks):
    t = pl.program_id(2)
    nt_dims = (((1,), (1,)), ((), ()))

    def attend(nc):
        outs = []
        for e in range(2):
            q = q_ref[0, e]

            def pass1(c, mx):
                s = lax.dot_general(q, k_ref[0, e, c], nt_dims, preferred_element_type=F32)
                s_ref[e, c] = s
                return jnp.maximum(mx, jnp.maximum(s[:, :LANES], s[:, LANES:]))

            mx = lax.fori_loop(0, nc, pass1, jnp.full((TOK, LANES), -jnp.inf, F32))
            m = jnp.max(mx, axis=-1, keepdims=True)

            def pass2(c, carry):
                lsum, acc = carry
                p = jnp.exp2(s_ref[e, c] - m)
                lsum = lsum + (p[:, :LANES] + p[:, LANES:])
                acc = acc + jnp.dot(p.astype(BF16), v_ref[0, c], preferred_element_type=F32)
                return lsum, acc

            zero = jnp.zeros((TOK, LANES), F32)
            lsum, acc = lax.fori_loop(0, nc, pass2, (zero, zero))
            outs.append(acc / jnp.sum(lsum, axis=-1, keepdims=True))
        lane = lax.broadcasted_iota(jnp.int32, (TOK, LANES), 1)
        o_ref[0] = jnp.where(lane < MLA_V, outs[0], outs[1]).astype(BF16)

    @pl.when(t == 0)
    def _():
        attend(1)

    @pl.when(t > 0)
    def _():
        attend(n_chunks)


def _mla_call(qa, ka, va):
    B, H, S, _ = qa.shape
    nc = S // KV_CHUNK
    k5 = ka.reshape(B, H, nc, KV_CHUNK, LANES)
    v4 = va.reshape(B, nc, KV_CHUNK, 512)
    return pl.pallas_call(
        functools.partial(_mla_kernel, n_chunks=nc),
        out_shape=jax.ShapeDtypeStruct((B, S, 512), BF16),
        grid=(B, H // 2, S // TOK),
        in_specs=[pl.BlockSpec((1, 2, TOK, LANES), lambda b, p, t: (b, p, t, 0)),
                  pl.BlockSpec((1, 2, nc, KV_CHUNK, LANES), lambda b, p, t: (b, p, 0, 0, 0)),
                  pl.BlockSpec((1, nc, KV_CHUNK, LANES), lambda b, p, t: (b, 0, 0, p))],
        out_specs=pl.BlockSpec((1, TOK, LANES), lambda b, p, t: (b, t, p)),
        scratch_shapes=[pltpu.VMEM((2, nc, TOK, KV_CHUNK), F32)],
        compiler_params=_cparams(3), name="mla_attn",
    )(qa, k5, v4)


def _swa_kernel(q_ref, k_ref, v_ref, sink_ref, o_ref, *, n_ctx, n_lat):
    rows4 = 4 * SWA_BLOCK
    nt_dims = (((1,), (1,)), ((), ()))
    lane = lax.broadcasted_iota(jnp.int32, (SWA_BLOCK, LANES), 1)
    lo = lane < 64
    sink = sink_ref[0]
    kc = k_ref[0, 0:n_ctx, :]
    vc = v_ref[0, 0:n_ctx, :]

    def stack_q(r0):
        qa = q_ref[0, pl.ds(r0, SWA_BLOCK), 0:LANES]
        qb = q_ref[0, pl.ds(r0, SWA_BLOCK), LANES:2 * LANES]
        z = jnp.zeros_like(qa)
        return jnp.concatenate([jnp.where(lo, qa, z), jnp.where(lo, z, qa),
                                jnp.where(lo, qb, z), jnp.where(lo, z, qb)], axis=0)

    def emit(r0, o):
        o_ref[0, pl.ds(r0, SWA_BLOCK), 0:LANES] = jnp.where(
            lo, o[0:SWA_BLOCK], o[SWA_BLOCK:2 * SWA_BLOCK]).astype(BF16)
        o_ref[0, pl.ds(r0, SWA_BLOCK), LANES:2 * LANES] = jnp.where(
            lo, o[2 * SWA_BLOCK:3 * SWA_BLOCK], o[3 * SWA_BLOCK:]).astype(BF16)

    for n in range(n_ctx // SWA_BLOCK):
        r0 = n * SWA_BLOCK
        q4 = stack_q(r0)
        s_c = lax.dot_general(q4, kc, nt_dims, preferred_element_type=F32)
        m = jnp.maximum(jnp.max(s_c, axis=-1, keepdims=True), sink[:, 0:1])
        p_c = jnp.exp2(s_c - m)
        l = jnp.sum(p_c, axis=-1, keepdims=True) + jnp.exp2(sink[:, 0:1] - m)
        o = jnp.dot(p_c.astype(BF16), vc, preferred_element_type=F32) / l
        emit(r0, o)

    win = 3 * SWA_BLOCK
    rel0 = (lax.broadcasted_iota(jnp.int32, (rows4, win), 1)
            - (lax.broadcasted_iota(jnp.int32, (rows4, win), 0) & (SWA_BLOCK - 1)))

    def lat_block(n, carry):
        r0 = pl.multiple_of(n_ctx + n * SWA_BLOCK, SWA_BLOCK)
        w_lat = jnp.clip((n - 1) * SWA_BLOCK, 0, n_lat - win)
        w0 = pl.multiple_of(n_ctx + w_lat, SWA_BLOCK)
        q4 = stack_q(r0)
        s_c = lax.dot_general(q4, kc, nt_dims, preferred_element_type=F32)
        s_w = lax.dot_general(q4, k_ref[0, pl.ds(w0, win), :], nt_dims, preferred_element_type=F32)
        rel = rel0 + (w_lat - n * SWA_BLOCK)
        s_w = jnp.where(jnp.abs(rel) <= SWA_WINDOW, s_w, NEG)
        m = jnp.maximum(jnp.maximum(jnp.max(s_c, axis=-1, keepdims=True),
                                    jnp.max(s_w, axis=-1, keepdims=True)), sink[:, 0:1])
        p_c = jnp.exp2(s_c - m)
        p_w = jnp.exp2(s_w - m)
        l = (jnp.sum(p_c, axis=-1, keepdims=True) + jnp.sum(p_w, axis=-1, keepdims=True)
             + jnp.exp2(sink[:, 0:1] - m))
        o = (jnp.dot(p_c.astype(BF16), vc, preferred_element_type=F32)
             + jnp.dot(p_w.astype(BF16), v_ref[0, pl.ds(w0, win), :], preferred_element_type=F32)) / l
        emit(r0, o)
        return carry

    lax.fori_loop(0, n_lat // SWA_BLOCK, lat_block, 0)


def _swa_call(l, qb, kb, vb, sinkcol, n_ctx):
    B, S, _ = qb.shape
    return pl.pallas_call(
        functools.partial(_swa_kernel, n_ctx=n_ctx, n_lat=S - n_ctx),
        out_shape=jax.ShapeDtypeStruct((B, S, 512), BF16),
        grid=(B, SWA_KV_HEADS),
        in_specs=[pl.BlockSpec((1, S, 256), lambda b, j: (b, 0, j)),
                  pl.BlockSpec((1, S, LANES), lambda b, j: (b, 0, j)),
                  pl.BlockSpec((1, S, LANES), lambda b, j: (b, 0, j)),
                  pl.BlockSpec((1, 4 * SWA_BLOCK, LANES), lambda b, j: (l * SWA_KV_HEADS + j, 0, 0))],
        out_specs=pl.BlockSpec((1, S, 256), lambda b, j: (b, 0, j)),
        compiler_params=_cparams(2), name="swa_attn",
    )(qb, kb, vb, sinkcol)


def _gelu(y):
    return 0.5 * y * (1.0 + jnp.tanh(math.sqrt(2.0 / math.pi) * (y + 0.044715 * (y * y * y))))


def _s5_kernel(u_ref, wloc_ref, wst_ref, wcar_ref, at_ref, o_ref,
               e_ref, hfr_ref, hfi_ref, hbr_ref, hbi_ref, *, nb, n_ctx_chunks, n_chunks):
    u = u_ref[0]
    e_ref[...] = jnp.dot(u, wst_ref[0], preferred_element_type=F32)
    a_re = at_ref[0, 0:1, :]
    a_im = at_ref[0, 1:2, :]
    lane = lax.broadcasted_iota(jnp.int32, (nb, LANES), 1)
    fwd = lane < S5_STATE

    def step(i, carry):
        h_re, h_im = carry
        cf = i
        cb = jnp.where(i < n_ctx_chunks, n_ctx_chunks - 1 - i, n_chunks + n_ctx_chunks - 1 - i)
        rf = pl.multiple_of(cf * nb, nb)
        rb = pl.multiple_of(cb * nb, nb)
        hfr_ref[pl.ds(rf, nb), :] = h_re
        hfi_ref[pl.ds(rf, nb), :] = h_im
        hbr_ref[pl.ds(rb, nb), :] = h_re
        hbi_ref[pl.ds(rb, nb), :] = h_im
        e_re = jnp.where(fwd, e_ref[pl.ds(rf, nb), 0:LANES], e_ref[pl.ds(rb, nb), 0:LANES])
        e_im = jnp.where(fwd, e_ref[pl.ds(rf, nb), LANES:], e_ref[pl.ds(rb, nb), LANES:])
        return (a_re * h_re - a_im * h_im + e_re, a_re * h_im + a_im * h_re + e_im)

    zero = jnp.zeros((nb, LANES), F32)
    lax.fori_loop(0, n_chunks, step, (zero, zero))

    rows = u.shape[0]
    lane_r = lax.broadcasted_iota(jnp.int32, (rows, LANES), 1)
    fwd_r = lane_r < S5_STATE
    h_cat = jnp.concatenate([jnp.where(fwd_r, hfr_ref[...], hbr_ref[...]),
                             jnp.where(fwd_r, hfi_ref[...], hbi_ref[...])], axis=-1).astype(BF16)
    y = (jnp.dot(u, wloc_ref[0], preferred_element_type=F32)
         + jnp.dot(h_cat, wcar_ref[0], preferred_element_type=F32))
    o_ref[0] = _gelu(y).astype(BF16)


def _s5_call(l, ug, P, nb, n_ctx_chunks):
    G, R, W = ug.shape
    n_chunks = R // nb
    return pl.pallas_call(
        functools.partial(_s5_kernel, nb=nb, n_ctx_chunks=n_ctx_chunks, n_chunks=n_chunks),
        out_shape=jax.ShapeDtypeStruct((G, R, W), BF16),
        grid=(G,),
        in_specs=[pl.BlockSpec((1, R, W), lambda g: (g, 0, 0)),
                  pl.BlockSpec((1, W, W), lambda g: (l * G + g, 0, 0)),
                  pl.BlockSpec((1, W, W), lambda g: (l * G + g, 0, 0)),
                  pl.BlockSpec((1, W, W), lambda g: (l * G + g, 0, 0)),
                  pl.BlockSpec((1, 2, LANES), lambda g: (l * G + g, 0, 0))],
        out_specs=pl.BlockSpec((1, R, W), lambda g: (g, 0, 0)),
        scratch_shapes=[pltpu.VMEM((R, W), F32)] + [pltpu.VMEM((R, LANES), F32)] * 4,
        compiler_params=_cparams(1), name="s5",
    )(ug, P["s5_wloc"], P["s5_wst"], P["s5_wcar"], P["s5_at"])


def _out_kernel(x_ref, c_ref, mod_ref, oa_ref, ob_ref, gy_ref, gt_ref, wglu_ref, wout_ref,
                xo_ref, co_ref):
    t = pl.program_id(1)
    z = jnp.dot(gy_ref[0], wglu_ref[0], preferred_element_type=F32)
    oc = z[:, :512] * jax.nn.sigmoid(z[:, 512:])
    g = gt_ref[0].astype(F32)
    m_a = (oa_ref[0].astype(F32) * g[:, 0:512]).astype(BF16)
    m_b = (ob_ref[0].astype(F32) * g[:, 512:1024]).astype(BF16)
    m_c = (oc * g[:, 1024:1536]).astype(BF16)
    upd = (jnp.dot(m_a, wout_ref[0, 0:512, :], preferred_element_type=F32)
           + jnp.dot(m_b, wout_ref[0, 512:1024, :], preferred_element_type=F32)
           + jnp.dot(m_c, wout_ref[0, 1024:1536, :], preferred_element_type=F32))
    gate = mod_ref[0, 0][2:3]

    @pl.when(t == 0)
    def _():
        co_ref[0] = c_ref[0] + gate * upd

    @pl.when(t > 0)
    def _():
        xo_ref[0] = x_ref[0] + gate * upd


def _out_call(l, x, ctx, mod, oa, ob, gy, gt, P):
    B, L, D = x.shape
    C = ctx.shape[1]
    S = C + L
    xmap = lambda b, t: (b, jnp.maximum(t - 1, 0), 0)
    cmap = lambda b, t: (b, 0, 0)
    tmap = lambda b, t: (b, t, 0)
    return pl.pallas_call(
        _out_kernel,
        out_shape=[jax.ShapeDtypeStruct(x.shape, F32), jax.ShapeDtypeStruct(ctx.shape, F32)],
        grid=(B, S // TOK),
        in_specs=[pl.BlockSpec((1, TOK, D), xmap),
                  pl.BlockSpec((1, TOK, D), cmap),
                  pl.BlockSpec((1, 1, 3, D), lambda b, t: (l, jnp.where(t == 0, B, b), 0, 0)),
                  pl.BlockSpec((1, TOK, 512), tmap),
                  pl.BlockSpec((1, TOK, 512), tmap),
                  pl.BlockSpec((1, TOK, 512), tmap),
                  pl.BlockSpec((1, TOK, 1536), tmap),
                  pl.BlockSpec((1, 512, 1024), lambda b, t: (l, 0, 0)),
                  pl.BlockSpec((1, 1536, D), lambda b, t: (l, 0, 0))],
        out_specs=[pl.BlockSpec((1, TOK, D), xmap), pl.BlockSpec((1, TOK, D), cmap)],
        input_output_aliases={0: 0, 1: 1},
        compiler_params=_cparams(2), name="out",
    )(x, ctx, mod, oa, ob, gy, gt, P["w_glu"], P["w_out"])


def _rope_tables(n_lat, n_ctx, rot_dim, lead, reps):
    rows = n_lat // GRID_W
    r_idx, c_idx = jnp.meshgrid(jnp.arange(rows), jnp.arange(GRID_W), indexing="ij")
    r_idx, c_idx = r_idx.reshape(-1), c_idx.reshape(-1)
    n_freq = rot_dim // 4
    freqs = ROPE_BASE ** (-jnp.arange(n_freq, dtype=F32) / n_freq)
    ang = jnp.concatenate([r_idx.astype(F32)[:, None] * freqs,
                           c_idx.astype(F32)[:, None] * freqs], axis=-1)
    ang = jnp.concatenate([jnp.zeros((n_ctx, rot_dim // 2), F32), ang], axis=0)
    cos, sin, zero = jnp.cos(ang), jnp.sin(ang), jnp.zeros_like(ang)
    n = ang.shape[0]
    tail = LANES // reps - lead - rot_dim

    def pack(x1, x2, fill):
        unit = [jnp.full((n, lead), fill, F32), x1, x2, jnp.full((n, tail), fill, F32)]
        return jnp.concatenate(unit * reps, axis=-1)

    return jnp.stack([pack(cos, cos, 1.0), pack(zero, sin, 0.0), pack(-sin, zero, 0.0)])


def _prep_params(norm_g, w_in, w_out, mla_g_cq, mla_g_ckv, mla_w_uq, mla_w_ukv, mla_g_qn, mla_g_kn,
                 swa_g_qn, swa_g_kn, swa_sink, s5_a_re, s5_a_im, s5_log_dt, s5_b_re, s5_b_im,
                 s5_c_re, s5_c_im, s5_d, s5_w_glu):
    depth, D, _ = w_in.shape
    o_cq, o_ckv, o_kr, o_gm, o_sq, o_sk, o_sv, o_gs, o_u, o_g5, o_end = (
        0, 384, 640, 672, 1184, 1696, 1824, 1952, 2464, 2976, 3488)
    z = lambda n: jnp.zeros((depth, D, n), F32)
    sk0, sk1 = w_in[:, :, o_sk:o_sk + 64], w_in[:, :, o_sk + 64:o_sv]
    sv0, sv1 = w_in[:, :, o_sv:o_sv + 64], w_in[:, :, o_sv + 64:o_gs]
    w_in_p = jnp.concatenate([
        w_in[:, :, o_cq:o_kr],
        z(64), w_in[:, :, o_kr:o_gm], z(32),
        w_in[:, :, o_sq:o_sk],
        sk0, sk0, sk1, sk1, sv0, sv0, sv1, sv1,
        w_in[:, :, o_u:o_g5],
        w_in[:, :, o_gm:o_sq], w_in[:, :, o_gs:o_u], w_in[:, :, o_g5:o_end],
    ], axis=-1).astype(BF16)
    assert w_in_p.shape[-1] == C_END

    wq = mla_w_uq.reshape(depth, MLA_Q_RANK, MLA_HEADS, MLA_QK)
    wq = jnp.pad(wq, ((0, 0), (0, 0), (0, 0), (0, LANES - MLA_QK)))
    w_uq_p = wq.reshape(depth, MLA_Q_RANK, MLA_HEADS * LANES).astype(BF16)
    wkv = mla_w_ukv.reshape(depth, MLA_KV_RANK, MLA_HEADS, MLA_NOPE + MLA_V)
    wk = jnp.pad(wkv[..., :MLA_NOPE], ((0, 0), (0, 0), (0, 0), (0, LANES - MLA_NOPE)))
    w_ukv_p = jnp.concatenate([wk.reshape(depth, MLA_KV_RANK, MLA_HEADS * LANES),
                               wkv[..., MLA_NOPE:].reshape(depth, MLA_KV_RANK, MLA_HEADS * MLA_V)],
                              axis=-1).astype(BF16)

    pad_qk = lambda g: jnp.pad(g, ((0, 0), (0, LANES - MLA_QK)))[:, None, :]
    g_q = pad_qk(mla_g_qn * (MLA_QK ** -0.5 * LOG2E))
    g_k = pad_qk(mla_g_kn)
    g_sq = jnp.tile(swa_g_qn * (SWA_DIM ** -0.5 * LOG2E), (1, 2))[:, None, :]
    g_sk = jnp.tile(swa_g_kn, (1, 2))[:, None, :]
    sink = (swa_sink * LOG2E).reshape(depth * SWA_KV_HEADS, SWA_HEADS // SWA_KV_HEADS)
    sinkcol = jnp.broadcast_to(jnp.repeat(sink, SWA_BLOCK, axis=1)[:, :, None],
                               (depth * SWA_KV_HEADS, 4 * SWA_BLOCK, LANES)).astype(F32)

    T = S5_CHUNK
    A = lax.complex(s5_a_re, s5_a_im)
    dt = jnp.exp(s5_log_dt)[..., None]
    a_bar = jnp.exp(dt * A)
    b_bar = ((a_bar - 1.0) / A)[..., None] * lax.complex(s5_b_re, s5_b_im)
    c_mat = lax.complex(s5_c_re, s5_c_im)
    k_idx = jnp.arange(T + 1, dtype=F32)
    pw = jnp.exp(k_idx[:, None, None, None, None] * (dt * A)[None])
    hi = lax.Precision.HIGHEST
    taps = jnp.einsum("ldgpn,kldgn,ldgnq->kldgpq", c_mat, pw[:T], b_bar, precision=hi).real
    kf, kb = taps[:, :, 0], taps[:, :, 1]
    tt = jnp.arange(T)
    dlt = tt[:, None] - tt[None, :]
    d_diag = s5_d.reshape(depth, S5_GROUPS, S5_GROUP)[..., None] * jnp.eye(S5_GROUP, dtype=F32)
    m_f = kf[jnp.clip(dlt, 0, T - 1)]
    m_b = kb[jnp.clip(-dlt, 0, T - 1)]
    sel = dlt[:, :, None, None, None, None]
    m_full = jnp.where(sel > 0, m_f, jnp.where(sel < 0, m_b, m_f + m_b + d_diag[None, None]))
    wloc = jnp.transpose(m_full, (2, 3, 1, 5, 0, 4)).reshape(depth * S5_GROUPS, T * S5_GROUP, T * S5_GROUP)
    inc_f = pw[T - 1 - tt][:, :, 0][..., None] * b_bar[None, :, 0]
    inc_b = pw[tt][:, :, 1][..., None] * b_bar[None, :, 1]
    to_rows = lambda m: jnp.transpose(m, (1, 2, 0, 4, 3)).reshape(depth * S5_GROUPS, T * S5_GROUP, S5_STATE)
    wst = jnp.concatenate([to_rows(inc_f.real), to_rows(inc_b.real),
                           to_rows(inc_f.imag), to_rows(inc_b.imag)], axis=-1)
    ro_f = c_mat[None, :, 0] * pw[tt + 1][:, :, 0][:, :, :, None, :]
    ro_b = c_mat[None, :, 1] * pw[T - tt][:, :, 1][:, :, :, None, :]
    to_cols = lambda m: jnp.transpose(m, (1, 2, 4, 0, 3)).reshape(depth * S5_GROUPS, S5_STATE, T * S5_GROUP)
    wcar = jnp.concatenate([to_cols(ro_f.real), to_cols(ro_b.real),
                            to_cols(-ro_f.imag), to_cols(-ro_b.imag)], axis=1)
    a_t = pw[T]
    at = jnp.stack([jnp.concatenate([a_t[:, 0].real, a_t[:, 1].real], axis=-1),
                    jnp.concatenate([a_t[:, 0].imag, a_t[:, 1].imag], axis=-1)], axis=2)
    at = at.reshape(depth * S5_GROUPS, 2, LANES).astype(F32)

    return dict(norm_g=norm_g[:, None, :], w_in=w_in_p, g_cq=mla_g_cq[:, None, :], w_uq=w_uq_p, g_q=g_q,
                g_ckv=mla_g_ckv[:, None, :], w_ukv=w_ukv_p, g_k=g_k, g_sq=g_sq, g_sk=g_sk,
                sinkcol=sinkcol, s5_wloc=wloc.astype(BF16), s5_wst=wst.astype(BF16),
                s5_wcar=wcar.astype(BF16), s5_at=at, w_glu=s5_w_glu.astype(BF16),
                w_out=w_out.astype(BF16))


def kernel(x, c, ctx, c_ctx, norm_g, w_ada, b_ada, w_in, w_out, mla_g_cq, mla_g_ckv, mla_w_uq, mla_w_ukv, mla_g_qn, mla_g_kn, swa_g_qn, swa_g_kn, swa_sink, s5_a_re, s5_a_im, s5_log_dt, s5_b_re, s5_b_im, s5_c_re, s5_c_im, s5_d, s5_w_glu):
    B, L, D = x.shape
    C = ctx.shape[1]
    S = C + L
    depth = w_in.shape[0]
    assert B < MOD_ROWS and C == TOK and L % TOK == 0 and L % GRID_W == 0 and L >= 3 * SWA_BLOCK

    P = _prep_params(norm_g, w_in, w_out, mla_g_cq, mla_g_ckv, mla_w_uq, mla_w_ukv, mla_g_qn, mla_g_kn,
                     swa_g_qn, swa_g_kn, swa_sink, s5_a_re, s5_a_im, s5_log_dt, s5_b_re, s5_b_im,
                     s5_c_re, s5_c_im, s5_d, s5_w_glu)
    rope_mla = _rope_tables(L, C, MLA_ROPE, MLA_NOPE, 1)
    rope_swa = _rope_tables(L, C, SWA_DIM, 0, 2)

    cc = jnp.concatenate([c, c_ctx[None, :], jnp.zeros((MOD_ROWS - B - 1, D), F32)], axis=0)
    mod = _ada_call(cc, w_ada, b_ada).reshape(depth, MOD_ROWS, 3, D)

    n_sc = S // S5_CHUNK
    for l in range(depth):
        qa, ka, va, qb, kb, vb, u, gt = _proj_call(l, x, ctx, mod, P, rope_mla, rope_swa)
        oa = _mla_call(qa, ka, va)
        ob = _swa_call(l, qb, kb, vb, P["sinkcol"], C)
        ug = jnp.transpose(u.reshape(B, n_sc, S5_CHUNK, S5_GROUPS, S5_GROUP), (3, 1, 0, 2, 4))
        gyg = _s5_call(l, ug.reshape(S5_GROUPS, n_sc * B, S5_CHUNK * S5_GROUP), P, B, C // S5_CHUNK)
        gy = jnp.transpose(gyg.reshape(S5_GROUPS, n_sc, B, S5_CHUNK, S5_GROUP), (2, 1, 3, 0, 4))
        x, ctx = _out_call(l, x, ctx, mod, oa, ob, gy.reshape(B, S, 512), gt, P)
    return x
```

```python
import functools
import math

import jax
import jax.numpy as jnp
from jax import lax
from jax.experimental import pallas as pl
from jax.experimental.pallas import tpu as pltpu

F32 = jnp.float32
BF16 = jnp.bfloat16

GRID_W = 64
EPS = 1e-6
ROPE_BASE = 10000.0
NEG = -1e30
LOG2E = math.log2(math.e)

MLA_HEADS = 8
MLA_NOPE = 64
MLA_ROPE = 32
MLA_V = 64
MLA_QK = MLA_NOPE + MLA_ROPE
MLA_Q_RANK = 384
MLA_KV_RANK = 256

SWA_HEADS = 8
SWA_KV_HEADS = 2
SWA_DIM = 64
SWA_WINDOW = 128

S5_GROUP = 16
S5_GROUPS = 32
S5_STATE = 64
S5_CHUNK = 16

LANES = 128
TOK = 256
KV_CHUNK = 256
SWA_BLOCK = 128
MOD_ROWS = 16

C_CQ = 0
C_CKV = 384
C_KR = 640
C_SQ = 768
C_SK = 1280
C_SV = 1536
C_U = 1792
C_GATE = 2304
C_END = 3840

VMEM_LIMIT = 56 * 1024 * 1024


def _cparams(n_axes):
    return pltpu.CompilerParams(dimension_semantics=("arbitrary",) * n_axes,
                                vmem_limit_bytes=VMEM_LIMIT)


def _ada_kernel(c_ref, w_ref, b_ref, o_ref):
    cc = c_ref[...]
    s = cc * jax.nn.sigmoid(cc)
    o_ref[0] = jnp.dot(s, w_ref[0], preferred_element_type=F32,
                       precision=lax.Precision.HIGHEST) + b_ref[0]


def _ada_call(cc, w_ada, b_ada):
    depth, d, n3 = w_ada.shape
    tn = 768
    return pl.pallas_call(
        _ada_kernel,
        out_shape=jax.ShapeDtypeStruct((depth, MOD_ROWS, n3), F32),
        grid=(depth, n3 // tn),
        in_specs=[pl.BlockSpec((MOD_ROWS, d), lambda l, j: (0, 0)),
                  pl.BlockSpec((1, d, tn), lambda l, j: (l, 0, j)),
                  pl.BlockSpec((1, 1, tn), lambda l, j: (l, 0, j))],
        out_specs=pl.BlockSpec((1, MOD_ROWS, tn), lambda l, j: (l, 0, j)),
        compiler_params=_cparams(2),
        name="ada",
    )(cc, w_ada, b_ada.reshape(depth, 1, n3))


def _rope(y, tab_ref, half):
    return (y * tab_ref[0]
            + pltpu.roll(y, half, 1) * tab_ref[1]
            + pltpu.roll(y, LANES - half, 1) * tab_ref[2])


def _proj_kernel(x_ref, c_ref, mod_ref, ng_ref, win_ref, gcq_ref, wuq_ref, gq_ref,
                 gckv_ref, wukv_ref, gk_ref, gsq_ref, gsk_ref, rm_ref, rs_ref,
                 qa_ref, ka_ref, va_ref, qb_ref, kb_ref, vb_ref, u_ref, gt_ref, *, n_lat_tiles):
    t = pl.program_id(1)
    x = jnp.where(t == n_lat_tiles, c_ref[0], x_ref[0])
    mod = mod_ref[0, 0]
    y = x * lax.rsqrt(jnp.mean(x * x, axis=-1, keepdims=True) + EPS) * ng_ref[0]
    xn = (y * (1.0 + mod[1:2]) + mod[0:1]).astype(BF16)

    def seg(a, b):
        return jnp.dot(xn, win_ref[0, :, a:b], preferred_element_type=F32)

    lane = lax.broadcasted_iota(jnp.int32, (TOK, LANES), 1)
    lo = lane < 64

    cq = seg(C_CQ, C_CKV)
    cqn = (cq * lax.rsqrt(jnp.mean(cq * cq, axis=-1, keepdims=True) + EPS) * gcq_ref[0]).astype(BF16)
    qf = jnp.dot(cqn, wuq_ref[0], preferred_element_type=F32)
    for h in range(MLA_HEADS):
        s = qf[:, h * LANES:(h + 1) * LANES]
        r = lax.rsqrt(jnp.sum(s * s, axis=-1, keepdims=True) * (1.0 / MLA_QK) + EPS)
        qa_ref[0, h] = _rope(s * r * gq_ref[0], rm_ref, MLA_ROPE // 2).astype(BF16)

    ckv = seg(C_CKV, C_KR)
    ckvn = (ckv * lax.rsqrt(jnp.mean(ckv * ckv, axis=-1, keepdims=True) + EPS) * gckv_ref[0]).astype(BF16)
    kvf = jnp.dot(ckvn, wukv_ref[0], preferred_element_type=F32)
    kr = seg(C_KR, C_SQ)
    for h in range(MLA_HEADS):
        s = kvf[:, h * LANES:(h + 1) * LANES] + kr
        r = lax.rsqrt(jnp.sum(s * s, axis=-1, keepdims=True) * (1.0 / MLA_QK) + EPS)
        ka_ref[0, h] = _rope(s * r * gk_ref[0], rm_ref, MLA_ROPE // 2).astype(BF16)
    va_ref[0] = kvf[:, MLA_HEADS * LANES:].astype(BF16)

    sq = seg(C_SQ, C_SK)
    for p in range(SWA_HEADS // 2):
        s = sq[:, p * LANES:(p + 1) * LANES]
        s2 = s * s
        r_lo = lax.rsqrt(jnp.sum(jnp.where(lo, s2, 0.0), axis=-1, keepdims=True) * (1.0 / SWA_DIM) + EPS)
        r_hi = lax.rsqrt(jnp.sum(jnp.where(lo, 0.0, s2), axis=-1, keepdims=True) * (1.0 / SWA_DIM) + EPS)
        yq = s * jnp.where(lo, r_lo, r_hi) * gsq_ref[0]
        qb_ref[0, :, p * LANES:(p + 1) * LANES] = _rope(yq, rs_ref, SWA_DIM // 2).astype(BF16)
    sk = seg(C_SK, C_SV)
    for j in range(SWA_KV_HEADS):
        s = sk[:, j * LANES:(j + 1) * LANES]
        r = lax.rsqrt(jnp.sum(jnp.where(lo, s * s, 0.0), axis=-1, keepdims=True) * (1.0 / SWA_DIM) + EPS)
        kb_ref[0, :, j * LANES:(j + 1) * LANES] = _rope(s * r * gsk_ref[0], rs_ref, SWA_DIM // 2).astype(BF16)
    vb_ref[0] = seg(C_SV, C_U).astype(BF16)

    u_ref[0] = seg(C_U, C_GATE).astype(BF16)
    g = seg(C_GATE, C_END)
    gt_ref[0] = (g * jax.nn.sigmoid(g)).astype(BF16)


def _proj_call(l, x, ctx, mod, P, rope_mla, rope_swa):
    B, L, D = x.shape
    C = ctx.shape[1]
    S = C + L
    nt = S // TOK

    def wspec(arr):
        shp = arr.shape
        return pl.BlockSpec((1,) + shp[1:], lambda b, t: (l,) + (0,) * (len(shp) - 1))

    weights = [P["norm_g"], P["w_in"], P["g_cq"], P["w_uq"], P["g_q"],
               P["g_ckv"], P["w_ukv"], P["g_k"], P["g_sq"], P["g_sk"]]
    nl = L // TOK
    in_specs = ([pl.BlockSpec((1, TOK, D), lambda b, t: (b, jnp.minimum(t, nl - 1), 0)),
                 pl.BlockSpec((1, TOK, D), lambda b, t: (b, 0, 0)),
                 pl.BlockSpec((1, 1, 3, D), lambda b, t: (l, jnp.where(t == nl, B, b), 0, 0))]
                + [wspec(w) for w in weights]
                + [pl.BlockSpec((3, TOK, LANES), lambda b, t: (0, t, 0)),
                   pl.BlockSpec((3, TOK, LANES), lambda b, t: (0, t, 0))])
    out_shape = [jax.ShapeDtypeStruct((B, MLA_HEADS, S, LANES), BF16),
                 jax.ShapeDtypeStruct((B, MLA_HEADS, S, LANES), BF16),
                 jax.ShapeDtypeStruct((B, S, 512), BF16),
                 jax.ShapeDtypeStruct((B, S, 512), BF16),
                 jax.ShapeDtypeStruct((B, S, 256), BF16),
                 jax.ShapeDtypeStruct((B, S, 256), BF16),
                 jax.ShapeDtypeStruct((B, S, 512), BF16),
                 jax.ShapeDtypeStruct((B, S, 1536), BF16)]
    out_specs = [pl.BlockSpec((1, MLA_HEADS, TOK, LANES), lambda b, t: (b, 0, t, 0)),
                 pl.BlockSpec((1, MLA_HEADS, TOK, LANES), lambda b, t: (b, 0, t, 0))]
    out_specs += [pl.BlockSpec((1, TOK, s.shape[2]), lambda b, t: (b, t, 0)) for s in out_shape[2:]]
    return pl.pallas_call(
        functools.partial(_proj_kernel, n_lat_tiles=nl), out_shape=out_shape, grid=(B, nt),
        in_specs=in_specs, out_specs=out_specs,
        compiler_params=_cparams(2), name="proj",
    )(x, ctx, mod, *weights, rope_mla, rope_swa)


def _mla_kernel(q_ref, k_ref, v_ref, *rest, n_chunks):
    o_ref, s_ref, vo_ref = rest[-3:]
    nt_dims = (((1,), (1,)), ((), ()))
    lane_v = lax.broadcasted_iota(jnp.int32, (KV_CHUNK, LANES), 1)

    @pl.when(pl.program_id(2) == 0)
    def _():
        for c in range(n_chunks):
            v = v_ref[0, c]
            vo_ref[0, c] = jnp.where(lane_v < MLA_V, v, jnp.ones_like(v))
            vo_ref[1, c] = jnp.where(lane_v < MLA_V, jnp.ones_like(v), v)

    accs = []
    for e in range(2):
        q = q_ref[0, e]
        mx = None
        for c in range(n_chunks):
            s = lax.dot_general(q, k_ref[0, e, c], nt_dims, preferred_element_type=F32)
            s_ref[e, c] = s
            sm = jnp.maximum(s[:, :LANES], s[:, LANES:])
            mx = sm if mx is None else jnp.maximum(mx, sm)
        m = jnp.max(mx, axis=-1, keepdims=True)
        acc = None
        for c in range(n_chunks):
            p = jnp.exp2(s_ref[e, c] - m).astype(BF16)
            d = jnp.dot(p, vo_ref[e, c], preferred_element_type=F32)
            acc = d if acc is None else acc + d
        accs.append(acc)
    lane = lax.broadcasted_iota(jnp.int32, accs[0].shape, 1)
    o0 = accs[0] / accs[0][:, MLA_V:MLA_V + 1]
    o1 = accs[1] / accs[1][:, 0:1]
    o_ref[0] = jnp.where(lane < MLA_V, o0, o1).astype(BF16)


def _mla_call(qa, ka, va, n_lat):
    B, H, S, _ = qa.shape
    nc = S // KV_CHUNK
    tq = 2 * TOK
    k5 = ka.reshape(B, H, nc, KV_CHUNK, LANES)
    v4 = va.reshape(B, nc, KV_CHUNK, 512)
    o = pl.pallas_call(
        functools.partial(_mla_kernel, n_chunks=nc),
        out_shape=jax.ShapeDtypeStruct((B, S, 512), BF16),
        grid=(B, H // 2, n_lat // tq),
        in_specs=[pl.BlockSpec((1, 2, tq, LANES), lambda b, p, t: (b, p, t, 0)),
                  pl.BlockSpec((1, 2, nc, KV_CHUNK, LANES), lambda b, p, t: (b, p, 0, 0, 0)),
                  pl.BlockSpec((1, nc, KV_CHUNK, LANES), lambda b, p, t: (b, 0, 0, p))],
        out_specs=pl.BlockSpec((1, tq, LANES), lambda b, p, t: (b, t, p)),
        scratch_shapes=[pltpu.VMEM((2, nc, tq, KV_CHUNK), F32),
                        pltpu.VMEM((2, nc, KV_CHUNK, LANES), BF16)],
        compiler_params=_cparams(3), name="mla_attn",
    )(qa, k5, v4)
    cblk = n_lat // KV_CHUNK
    return pl.pallas_call(
        functools.partial(_mla_kernel, n_chunks=1),
        out_shape=jax.ShapeDtypeStruct((B, S, 512), BF16),
        grid=(B, H // 2, 1),
        in_specs=[pl.BlockSpec((1, 2, TOK, LANES), lambda b, p, t: (b, p, cblk, 0)),
                  pl.BlockSpec((1, 2, 1, KV_CHUNK, LANES), lambda b, p, t: (b, p, cblk, 0, 0)),
                  pl.BlockSpec((1, 1, KV_CHUNK, LANES), lambda b, p, t: (b, cblk, 0, p)),
                  pl.BlockSpec(memory_space=pl.ANY)],
        out_specs=pl.BlockSpec((1, TOK, LANES), lambda b, p, t: (b, cblk, p)),
        scratch_shapes=[pltpu.VMEM((2, 1, TOK, KV_CHUNK), F32),
                        pltpu.VMEM((2, 1, KV_CHUNK, LANES), BF16)],
        input_output_aliases={3: 0},
        compiler_params=_cparams(3), name="mla_attn_ctx",
    )(qa, k5, v4, o)


def _swa_kernel(q_ref, k_ref, v_ref, sink_ref, o_ref, *, n_ctx, n_lat):
    rows4 = 4 * SWA_BLOCK
    nt_dims = (((1,), (1,)), ((), ()))
    lane = lax.broadcasted_iota(jnp.int32, (SWA_BLOCK, LANES), 1)
    lo = lane < 64
    sink = sink_ref[0]
    kc = k_ref[0, n_lat:n_lat + n_ctx, :]
    vc = v_ref[0, n_lat:n_lat + n_ctx, :]

    def stack_q(r0):
        qa = q_ref[0, pl.ds(r0, SWA_BLOCK), 0:LANES]
        qb = q_ref[0, pl.ds(r0, SWA_BLOCK), LANES:2 * LANES]
        z = jnp.zeros_like(qa)
        return jnp.concatenate([jnp.where(lo, qa, z), jnp.where(lo, z, qa),
                                jnp.where(lo, qb, z), jnp.where(lo, z, qb)], axis=0)

    def emit(r0, o):
        o_ref[0, pl.ds(r0, SWA_BLOCK), 0:LANES] = jnp.where(
            lo, o[0:SWA_BLOCK], o[SWA_BLOCK:2 * SWA_BLOCK]).astype(BF16)
        o_ref[0, pl.ds(r0, SWA_BLOCK), LANES:2 * LANES] = jnp.where(
            lo, o[2 * SWA_BLOCK:3 * SWA_BLOCK], o[3 * SWA_BLOCK:]).astype(BF16)

    for n in range(n_ctx // SWA_BLOCK):
        r0 = n_lat + n * SWA_BLOCK
        q4 = stack_q(r0)
        s_c = lax.dot_general(q4, kc, nt_dims, preferred_element_type=F32)
        m = jnp.maximum(jnp.max(s_c, axis=-1, keepdims=True), sink[:, 0:1])
        p_c = jnp.exp2(s_c - m)
        l = jnp.sum(p_c, axis=-1, keepdims=True) + jnp.exp2(sink[:, 0:1] - m)
        o = jnp.dot(p_c.astype(BF16), vc, preferred_element_type=F32) / l
        emit(r0, o)

    win = 3 * SWA_BLOCK
    rel0 = (lax.broadcasted_iota(jnp.int32, (rows4, win), 1)
            - (lax.broadcasted_iota(jnp.int32, (rows4, win), 0) & (SWA_BLOCK - 1)))

    def lat_block(n, carry):
        r0 = pl.multiple_of(n * SWA_BLOCK, SWA_BLOCK)
        w_lat = jnp.clip((n - 1) * SWA_BLOCK, 0, n_lat - win)
        w0 = pl.multiple_of(w_lat, SWA_BLOCK)
        q4 = stack_q(r0)
        s_c = lax.dot_general(q4, kc, nt_dims, preferred_element_type=F32)
        s_w = lax.dot_general(q4, k_ref[0, pl.ds(w0, win), :], nt_dims, preferred_element_type=F32)
        rel = rel0 + (w_lat - n * SWA_BLOCK)
        s_w = jnp.where(jnp.abs(rel) <= SWA_WINDOW, s_w, NEG)
        m = jnp.maximum(jnp.maximum(jnp.max(s_c, axis=-1, keepdims=True),
                                    jnp.max(s_w, axis=-1, keepdims=True)), sink[:, 0:1])
        p_c = jnp.exp2(s_c - m)
        p_w = jnp.exp2(s_w - m)
        l = (jnp.sum(p_c, axis=-1, keepdims=True) + jnp.sum(p_w, axis=-1, keepdims=True)
             + jnp.exp2(sink[:, 0:1] - m))
        o = (jnp.dot(p_c.astype(BF16), vc, preferred_element_type=F32)
             + jnp.dot(p_w.astype(BF16), v_ref[0, pl.ds(w0, win), :], preferred_element_type=F32)) / l
        emit(r0, o)
        return carry

    lax.fori_loop(0, n_lat // SWA_BLOCK, lat_block, 0)


def _swa_call(l, qb, kb, vb, sinkcol, n_ctx):
    B, S, _ = qb.shape
    return pl.pallas_call(
        functools.partial(_swa_kernel, n_ctx=n_ctx, n_lat=S - n_ctx),
        out_shape=jax.ShapeDtypeStruct((B, S, 512), BF16),
        grid=(B, SWA_KV_HEADS),
        in_specs=[pl.BlockSpec((1, S, 256), lambda b, j: (b, 0, j)),
                  pl.BlockSpec((1, S, LANES), lambda b, j: (b, 0, j)),
                  pl.BlockSpec((1, S, LANES), lambda b, j: (b, 0, j)),
                  pl.BlockSpec((1, 4 * SWA_BLOCK, LANES), lambda b, j: (l * SWA_KV_HEADS + j, 0, 0))],
        out_specs=pl.BlockSpec((1, S, 256), lambda b, j: (b, 0, j)),
        compiler_params=_cparams(2), name="swa_attn",
    )(qb, kb, vb, sinkcol)


def _gelu(y):
    return 0.5 * y * (1.0 + jnp.tanh(math.sqrt(2.0 / math.pi) * (y + 0.044715 * (y * y * y))))


def _s5_kernel(u_ref, wloc_ref, wst_ref, wcar_ref, at_ref, o_ref,
               e_ref, hfr_ref, hfi_ref, hbr_ref, hbi_ref, *, nb, n_ctx_chunks, n_chunks):
    u = u_ref[0]
    e_ref[...] = jnp.dot(u, wst_ref[0], preferred_element_type=F32)
    a_re = at_ref[0, 0:1, :]
    a_im = at_ref[0, 1:2, :]
    lane = lax.broadcasted_iota(jnp.int32, (nb, LANES), 1)
    fwd = lane < S5_STATE

    def step(i, carry):
        h_re, h_im = carry
        n_lat_chunks = n_chunks - n_ctx_chunks
        cf = jnp.where(i < n_ctx_chunks, n_lat_chunks + i, i - n_ctx_chunks)
        cb = n_chunks - 1 - i
        rf = pl.multiple_of(cf * nb, nb)
        rb = pl.multiple_of(cb * nb, nb)
        hfr_ref[pl.ds(rf, nb), :] = h_re
        hfi_ref[pl.ds(rf, nb), :] = h_im
        hbr_ref[pl.ds(rb, nb), :] = h_re
        hbi_ref[pl.ds(rb, nb), :] = h_im
        e_re = jnp.where(fwd, e_ref[pl.ds(rf, nb), 0:LANES], e_ref[pl.ds(rb, nb), 0:LANES])
        e_im = jnp.where(fwd, e_ref[pl.ds(rf, nb), LANES:], e_ref[pl.ds(rb, nb), LANES:])
        return (a_re * h_re - a_im * h_im + e_re, a_re * h_im + a_im * h_re + e_im)

    zero = jnp.zeros((nb, LANES), F32)
    lax.fori_loop(0, n_chunks, step, (zero, zero))

    rows = u.shape[0]
    lane_r = lax.broadcasted_iota(jnp.int32, (rows, LANES), 1)
    fwd_r = lane_r < S5_STATE
    h_cat = jnp.concatenate([jnp.where(fwd_r, hfr_ref[...], hbr_ref[...]),
                             jnp.where(fwd_r, hfi_ref[...], hbi_ref[...])], axis=-1).astype(BF16)
    y = (jnp.dot(u, wloc_ref[0], preferred_element_type=F32)
         + jnp.dot(h_cat, wcar_ref[0], preferred_element_type=F32))
    o_ref[0] = _gelu(y).astype(BF16)


def _s5_call(l, ug, P, nb, n_ctx_chunks):
    G, R, W = ug.shape
    n_chunks = R // nb
    return pl.pallas_call(
        functools.partial(_s5_kernel, nb=nb, n_ctx_chunks=n_ctx_chunks, n_chunks=n_chunks),
        out_shape=jax.ShapeDtypeStruct((G, R, W), BF16),
        grid=(G,),
        in_specs=[pl.BlockSpec((1, R, W), lambda g: (g, 0, 0)),
                  pl.BlockSpec((1, W, W), lambda g: (l * G + g, 0, 0)),
                  pl.BlockSpec((1, W, W), lambda g: (l * G + g, 0, 0)),
                  pl.BlockSpec((1, W, W), lambda g: (l * G + g, 0, 0)),
                  pl.BlockSpec((1, 2, LANES), lambda g: (l * G + g, 0, 0))],
        out_specs=pl.BlockSpec((1, R, W), lambda g: (g, 0, 0)),
        scratch_shapes=[pltpu.VMEM((R, W), F32)] + [pltpu.VMEM((R, LANES), F32)] * 4,
        compiler_params=_cparams(1), name="s5",
    )(ug, P["s5_wloc"], P["s5_wst"], P["s5_wcar"], P["s5_at"])


def _out_kernel(x_ref, c_ref, mod_ref, oa_ref, ob_ref, gy_ref, gt_ref, wglu_ref, wout_ref,
                xo_ref, co_ref, *, n_lat_tiles):
    t = pl.program_id(1)
    z = jnp.dot(gy_ref[0], wglu_ref[0], preferred_element_type=F32)
    oc = z[:, :512] * jax.nn.sigmoid(z[:, 512:])
    g = gt_ref[0].astype(F32)
    m_a = (oa_ref[0].astype(F32) * g[:, 0:512]).astype(BF16)
    m_b = (ob_ref[0].astype(F32) * g[:, 512:1024]).astype(BF16)
    m_c = (oc * g[:, 1024:1536]).astype(BF16)
    upd = (jnp.dot(m_a, wout_ref[0, 0:512, :], preferred_element_type=F32)
           + jnp.dot(m_b, wout_ref[0, 512:1024, :], preferred_element_type=F32)
           + jnp.dot(m_c, wout_ref[0, 1024:1536, :], preferred_element_type=F32))
    gate = mod_ref[0, 0][2:3]

    @pl.when(t == n_lat_tiles)
    def _():
        co_ref[0] = c_ref[0] + gate * upd

    @pl.when(t < n_lat_tiles)
    def _():
        xo_ref[0] = x_ref[0] + gate * upd


def _out_call(l, x, ctx, mod, oa, ob, gy, gt, P):
    B, L, D = x.shape
    C = ctx.shape[1]
    S = C + L
    nl = L // TOK
    xmap = lambda b, t: (b, jnp.minimum(t, nl - 1), 0)
    cmap = lambda b, t: (b, 0, 0)
    tmap = lambda b, t: (b, t, 0)
    return pl.pallas_call(
        functools.partial(_out_kernel, n_lat_tiles=nl),
        out_shape=[jax.ShapeDtypeStruct(x.shape, F32), jax.ShapeDtypeStruct(ctx.shape, F32)],
        grid=(B, S // TOK),
        in_specs=[pl.BlockSpec((1, TOK, D), xmap),
                  pl.BlockSpec((1, TOK, D), cmap),
                  pl.BlockSpec((1, 1, 3, D), lambda b, t: (l, jnp.where(t == nl, B, b), 0, 0)),
                  pl.BlockSpec((1, TOK, 512), tmap),
                  pl.BlockSpec((1, TOK, 512), tmap),
                  pl.BlockSpec((1, TOK, 512), tmap),
                  pl.BlockSpec((1, TOK, 1536), tmap),
                  pl.BlockSpec((1, 512, 1024), lambda b, t: (l, 0, 0)),
                  pl.BlockSpec((1, 1536, D), lambda b, t: (l, 0, 0))],
        out_specs=[pl.BlockSpec((1, TOK, D), xmap), pl.BlockSpec((1, TOK, D), cmap)],
        input_output_aliases={0: 0, 1: 1},
        compiler_params=_cparams(2), name="out",
    )(x, ctx, mod, oa, ob, gy, gt, P["w_glu"], P["w_out"])


def _rope_tables(n_lat, n_ctx, rot_dim, lead, reps):
    rows = n_lat // GRID_W
    r_idx, c_idx = jnp.meshgrid(jnp.arange(rows), jnp.arange(GRID_W), indexing="ij")
    r_idx, c_idx = r_idx.reshape(-1), c_idx.reshape(-1)
    n_freq = rot_dim // 4
    freqs = ROPE_BASE ** (-jnp.arange(n_freq, dtype=F32) / n_freq)
    ang = jnp.concatenate([r_idx.astype(F32)[:, None] * freqs,
                           c_idx.astype(F32)[:, None] * freqs], axis=-1)
    ang = jnp.concatenate([ang, jnp.zeros((n_ctx, rot_dim // 2), F32)], axis=0)
    cos, sin, zero = jnp.cos(ang), jnp.sin(ang), jnp.zeros_like(ang)
    n = ang.shape[0]
    tail = LANES // reps - lead - rot_dim

    def pack(x1, x2, fill):
        unit = [jnp.full((n, lead), fill, F32), x1, x2, jnp.full((n, tail), fill, F32)]
        return jnp.concatenate(unit * reps, axis=-1)

    return jnp.stack([pack(cos, cos, 1.0), pack(zero, sin, 0.0), pack(-sin, zero, 0.0)])


def _prep_params(norm_g, w_in, w_out, mla_g_cq, mla_g_ckv, mla_w_uq, mla_w_ukv, mla_g_qn, mla_g_kn,
                 swa_g_qn, swa_g_kn, swa_sink, s5_a_re, s5_a_im, s5_log_dt, s5_b_re, s5_b_im,
                 s5_c_re, s5_c_im, s5_d, s5_w_glu):
    depth, D, _ = w_in.shape
    o_cq, o_ckv, o_kr, o_gm, o_sq, o_sk, o_sv, o_gs, o_u, o_g5, o_end = (
        0, 384, 640, 672, 1184, 1696, 1824, 1952, 2464, 2976, 3488)
    z = lambda n: jnp.zeros((depth, D, n), F32)
    sk0, sk1 = w_in[:, :, o_sk:o_sk + 64], w_in[:, :, o_sk + 64:o_sv]
    sv0, sv1 = w_in[:, :, o_sv:o_sv + 64], w_in[:, :, o_sv + 64:o_gs]
    w_in_p = jnp.concatenate([
        w_in[:, :, o_cq:o_kr],
        z(64), w_in[:, :, o_kr:o_gm], z(32),
        w_in[:, :, o_sq:o_sk],
        sk0, sk0, sk1, sk1, sv0, sv0, sv1, sv1,
        w_in[:, :, o_u:o_g5],
        w_in[:, :, o_gm:o_sq], w_in[:, :, o_gs:o_u], w_in[:, :, o_g5:o_end],
    ], axis=-1).astype(BF16)
    assert w_in_p.shape[-1] == C_END

    wq = mla_w_uq.reshape(depth, MLA_Q_RANK, MLA_HEADS, MLA_QK)
    wq = jnp.pad(wq, ((0, 0), (0, 0), (0, 0), (0, LANES - MLA_QK)))
    w_uq_p = wq.reshape(depth, MLA_Q_RANK, MLA_HEADS * LANES).astype(BF16)
    wkv = mla_w_ukv.reshape(depth, MLA_KV_RANK, MLA_HEADS, MLA_NOPE + MLA_V)
    wk = jnp.pad(wkv[..., :MLA_NOPE], ((0, 0), (0, 0), (0, 0), (0, LANES - MLA_NOPE)))
    w_ukv_p = jnp.concatenate([wk.reshape(depth, MLA_KV_RANK, MLA_HEADS * LANES),
                               wkv[..., MLA_NOPE:].reshape(depth, MLA_KV_RANK, MLA_HEADS * MLA_V)],
                              axis=-1).astype(BF16)

    pad_qk = lambda g: jnp.pad(g, ((0, 0), (0, LANES - MLA_QK)))[:, None, :]
    g_q = pad_qk(mla_g_qn * (MLA_QK ** -0.5 * LOG2E))
    g_k = pad_qk(mla_g_kn)
    g_sq = jnp.tile(swa_g_qn * (SWA_DIM ** -0.5 * LOG2E), (1, 2))[:, None, :]
    g_sk = jnp.tile(swa_g_kn, (1, 2))[:, None, :]
    sink = (swa_sink * LOG2E).reshape(depth * SWA_KV_HEADS, SWA_HEADS // SWA_KV_HEADS)
    sinkcol = jnp.broadcast_to(jnp.repeat(sink, SWA_BLOCK, axis=1)[:, :, None],
                               (depth * SWA_KV_HEADS, 4 * SWA_BLOCK, LANES)).astype(F32)

    T = S5_CHUNK
    A = lax.complex(s5_a_re, s5_a_im)
    dt = jnp.exp(s5_log_dt)[..., None]
    a_bar = jnp.exp(dt * A)
    b_bar = ((a_bar - 1.0) / A)[..., None] * lax.complex(s5_b_re, s5_b_im)
    c_mat = lax.complex(s5_c_re, s5_c_im)
    k_idx = jnp.arange(T + 1, dtype=F32)
    pw = jnp.exp(k_idx[:, None, None, None, None] * (dt * A)[None])
    hi = lax.Precision.HIGHEST
    taps = jnp.einsum("ldgpn,kldgn,ldgnq->kldgpq", c_mat, pw[:T], b_bar, precision=hi).real
    kf, kb = taps[:, :, 0], taps[:, :, 1]
    tt = jnp.arange(T)
    dlt = tt[:, None] - tt[None, :]
    d_diag = s5_d.reshape(depth, S5_GROUPS, S5_GROUP)[..., None] * jnp.eye(S5_GROUP, dtype=F32)
    m_f = kf[jnp.clip(dlt, 0, T - 1)]
    m_b = kb[jnp.clip(-dlt, 0, T - 1)]
    sel = dlt[:, :, None, None, None, None]
    m_full = jnp.where(sel > 0, m_f, jnp.where(sel < 0, m_b, m_f + m_b + d_diag[None, None]))
    wloc = jnp.transpose(m_full, (2, 3, 1, 5, 0, 4)).reshape(depth * S5_GROUPS, T * S5_GROUP, T * S5_GROUP)
    inc_f = pw[T - 1 - tt][:, :, 0][..., None] * b_bar[None, :, 0]
    inc_b = pw[tt][:, :, 1][..., None] * b_bar[None, :, 1]
    to_rows = lambda m: jnp.transpose(m, (1, 2, 0, 4, 3)).reshape(depth * S5_GROUPS, T * S5_GROUP, S5_STATE)
    wst = jnp.concatenate([to_rows(inc_f.real), to_rows(inc_b.real),
                           to_rows(inc_f.imag), to_rows(inc_b.imag)], axis=-1)
    ro_f = c_mat[None, :, 0] * pw[tt + 1][:, :, 0][:, :, :, None, :]
    ro_b = c_mat[None, :, 1] * pw[T - tt][:, :, 1][:, :, :, None, :]
    to_cols = lambda m: jnp.transpose(m, (1, 2, 4, 0, 3)).reshape(depth * S5_GROUPS, S5_STATE, T * S5_GROUP)
    wcar = jnp.concatenate([to_cols(ro_f.real), to_cols(ro_b.real),
                            to_cols(-ro_f.imag), to_cols(-ro_b.imag)], axis=1)
    a_t = pw[T]
    at = jnp.stack([jnp.concatenate([a_t[:, 0].real, a_t[:, 1].real], axis=-1),
                    jnp.concatenate([a_t[:, 0].imag, a_t[:, 1].imag], axis=-1)], axis=2)
    at = at.reshape(depth * S5_GROUPS, 2, LANES).astype(F32)

    return dict(norm_g=norm_g[:, None, :], w_in=w_in_p, g_cq=mla_g_cq[:, None, :], w_uq=w_uq_p, g_q=g_q,
                g_ckv=mla_g_ckv[:, None, :], w_ukv=w_ukv_p, g_k=g_k, g_sq=g_sq, g_sk=g_sk,
                sinkcol=sinkcol, s5_wloc=wloc.astype(BF16), s5_wst=wst.astype(BF16),
                s5_wcar=wcar.astype(BF16), s5_at=at, w_glu=s5_w_glu.astype(BF16),
                w_out=w_out.astype(BF16))


def kernel(x, c, ctx, c_ctx, norm_g, w_ada, b_ada, w_in, w_out, mla_g_cq, mla_g_ckv, mla_w_uq, mla_w_ukv, mla_g_qn, mla_g_kn, swa_g_qn, swa_g_kn, swa_sink, s5_a_re, s5_a_im, s5_log_dt, s5_b_re, s5_b_im, s5_c_re, s5_c_im, s5_d, s5_w_glu):
    B, L, D = x.shape
    C = ctx.shape[1]
    S = C + L
    depth = w_in.shape[0]
    assert B < MOD_ROWS and C == TOK and L % (2 * TOK) == 0 and L % GRID_W == 0 and L >= 3 * SWA_BLOCK

    P = _prep_params(norm_g, w_in, w_out, mla_g_cq, mla_g_ckv, mla_w_uq, mla_w_ukv, mla_g_qn, mla_g_kn,
                     swa_g_qn, swa_g_kn, swa_sink, s5_a_re, s5_a_im, s5_log_dt, s5_b_re, s5_b_im,
                     s5_c_re, s5_c_im, s5_d, s5_w_glu)
    rope_mla = _rope_tables(L, C, MLA_ROPE, MLA_NOPE, 1)
    rope_swa = _rope_tables(L, C, SWA_DIM, 0, 2)

    cc = jnp.concatenate([c, c_ctx[None, :], jnp.zeros((MOD_ROWS - B - 1, D), F32)], axis=0)
    mod = _ada_call(cc, w_ada, b_ada).reshape(depth, MOD_ROWS, 3, D)

    n_sc = S // S5_CHUNK
    for l in range(depth):
        qa, ka, va, qb, kb, vb, u, gt = _proj_call(l, x, ctx, mod, P, rope_mla, rope_swa)
        oa = _mla_call(qa, ka, va, L)
        ob = _swa_call(l, qb, kb, vb, P["sinkcol"], C)
        ug = jnp.transpose(u.reshape(B, n_sc, S5_CHUNK, S5_GROUPS, S5_GROUP), (3, 1, 0, 2, 4))
        gyg = _s5_call(l, ug.reshape(S5_GROUPS, n_sc * B, S5_CHUNK * S5_GROUP), P, B, C // S5_CHUNK)
        gy = jnp.transpose(gyg.reshape(S5_GROUPS, n_sc, B, S5_CHUNK, S5_GROUP), (2, 1, 3, 0, 4))
        x, ctx = _out_call(l, x, ctx, mod, oa, ob, gy.reshape(B, S, 512), gt, P)
    return x
```

```python
import functools
import math

import jax
import jax.numpy as jnp
from jax import lax
from jax.experimental import pallas as pl
from jax.experimental.pallas import tpu as pltpu

F32 = jnp.float32
BF16 = jnp.bfloat16

GRID_W = 64
EPS = 1e-6
ROPE_BASE = 10000.0
NEG = -1e30
LOG2E = math.log2(math.e)

MLA_HEADS = 8
MLA_NOPE = 64
MLA_ROPE = 32
MLA_V = 64
MLA_QK = MLA_NOPE + MLA_ROPE
MLA_Q_RANK = 384
MLA_KV_RANK = 256

SWA_HEADS = 8
SWA_KV_HEADS = 2
SWA_DIM = 64
SWA_WINDOW = 128

S5_GROUP = 16
S5_GROUPS = 32
S5_STATE = 64
S5_CHUNK = 16

LANES = 128
TOK = 256
KV_CHUNK = 256
SWA_BLOCK = 128
MOD_ROWS = 16

C_CQ = 0
C_CKV = 384
C_KR = 640
C_SQ = 768
C_SK = 1280
C_SV = 1536
C_U = 1792
C_GATE = 2304
C_END = 3840

VMEM_LIMIT = 56 * 1024 * 1024


def _cparams(n_axes):
    return pltpu.CompilerParams(dimension_semantics=("arbitrary",) * n_axes,
                                vmem_limit_bytes=VMEM_LIMIT)


def _ada_kernel(c_ref, w_ref, b_ref, o_ref):
    cc = c_ref[...]
    s = cc * jax.nn.sigmoid(cc)
    o_ref[0] = jnp.dot(s, w_ref[0], preferred_element_type=F32,
                       precision=lax.Precision.HIGHEST) + b_ref[0]


def _ada_call(cc, w_ada, b_ada):
    depth, d, n3 = w_ada.shape
    tn = 768
    return pl.pallas_call(
        _ada_kernel,
        out_shape=jax.ShapeDtypeStruct((depth, MOD_ROWS, n3), F32),
        grid=(depth, n3 // tn),
        in_specs=[pl.BlockSpec((MOD_ROWS, d), lambda l, j: (0, 0)),
                  pl.BlockSpec((1, d, tn), lambda l, j: (l, 0, j)),
                  pl.BlockSpec((1, 1, tn), lambda l, j: (l, 0, j))],
        out_specs=pl.BlockSpec((1, MOD_ROWS, tn), lambda l, j: (l, 0, j)),
        compiler_params=_cparams(2),
        name="ada",
    )(cc, w_ada, b_ada.reshape(depth, 1, n3))


def _rope(y, tab_ref, half):
    return (y * tab_ref[0]
            + pltpu.roll(y, half, 1) * tab_ref[1]
            + pltpu.roll(y, LANES - half, 1) * tab_ref[2])


def _proj_kernel(x_ref, c_ref, mod_ref, ng_ref, win_ref, gcq_ref, wuq_ref, gq_ref,
                 gckv_ref, wukv_ref, gk_ref, gsq_ref, gsk_ref, rm_ref, rs_ref,
                 qa_ref, ka_ref, va_ref, qb_ref, kb_ref, vb_ref, u_ref, gt_ref, *, n_lat_tiles):
    t = pl.program_id(1)
    x = jnp.where(t == n_lat_tiles, c_ref[0], x_ref[0])
    mod = mod_ref[0, 0]
    y = x * lax.rsqrt(jnp.mean(x * x, axis=-1, keepdims=True) + EPS) * ng_ref[0]
    xn = (y * (1.0 + mod[1:2]) + mod[0:1]).astype(BF16)

    def seg(a, b):
        return jnp.dot(xn, win_ref[0, :, a:b], preferred_element_type=F32)

    lane = lax.broadcasted_iota(jnp.int32, (TOK, LANES), 1)
    lo = lane < 64

    cq = seg(C_CQ, C_CKV)
    cqn = (cq * lax.rsqrt(jnp.mean(cq * cq, axis=-1, keepdims=True) + EPS) * gcq_ref[0]).astype(BF16)
    qf = jnp.dot(cqn, wuq_ref[0], preferred_element_type=F32)
    for h in range(MLA_HEADS):
        s = qf[:, h * LANES:(h + 1) * LANES]
        r = lax.rsqrt(jnp.sum(s * s, axis=-1, keepdims=True) * (1.0 / MLA_QK) + EPS)
        qa_ref[0, h] = _rope(s * r * gq_ref[0], rm_ref, MLA_ROPE // 2).astype(BF16)

    ckv = seg(C_CKV, C_KR)
    ckvn = (ckv * lax.rsqrt(jnp.mean(ckv * ckv, axis=-1, keepdims=True) + EPS) * gckv_ref[0]).astype(BF16)
    kvf = jnp.dot(ckvn, wukv_ref[0], preferred_element_type=F32)
    kr = seg(C_KR, C_SQ)
    for h in range(MLA_HEADS):
        s = kvf[:, h * LANES:(h + 1) * LANES] + kr
        r = lax.rsqrt(jnp.sum(s * s, axis=-1, keepdims=True) * (1.0 / MLA_QK) + EPS)
        ka_ref[0, h] = _rope(s * r * gk_ref[0], rm_ref, MLA_ROPE // 2).astype(BF16)
    va_ref[0] = kvf[:, MLA_HEADS * LANES:].astype(BF16)

    sq = seg(C_SQ, C_SK)
    for p in range(SWA_HEADS // 2):
        s = sq[:, p * LANES:(p + 1) * LANES]
        s2 = s * s
        r_lo = lax.rsqrt(jnp.sum(jnp.where(lo, s2, 0.0), axis=-1, keepdims=True) * (1.0 / SWA_DIM) + EPS)
        r_hi = lax.rsqrt(jnp.sum(jnp.where(lo, 0.0, s2), axis=-1, keepdims=True) * (1.0 / SWA_DIM) + EPS)
        yq = s * jnp.where(lo, r_lo, r_hi) * gsq_ref[0]
        qb_ref[0, :, p * LANES:(p + 1) * LANES] = _rope(yq, rs_ref, SWA_DIM // 2).astype(BF16)
    sk = seg(C_SK, C_SV)
    for j in range(SWA_KV_HEADS):
        s = sk[:, j * LANES:(j + 1) * LANES]
        r = lax.rsqrt(jnp.sum(jnp.where(lo, s * s, 0.0), axis=-1, keepdims=True) * (1.0 / SWA_DIM) + EPS)
        kb_ref[0, :, j * LANES:(j + 1) * LANES] = _rope(s * r * gsk_ref[0], rs_ref, SWA_DIM // 2).astype(BF16)
    vb_ref[0] = seg(C_SV, C_U).astype(BF16)

    u_ref[0] = seg(C_U, C_GATE)
    g = seg(C_GATE, C_END)
    gt_ref[0] = (g * jax.nn.sigmoid(g)).astype(BF16)


def _proj_call(l, x, ctx, mod, P, rope_mla, rope_swa):
    B, L, D = x.shape
    C = ctx.shape[1]
    S = C + L
    nt = S // TOK

    def wspec(arr):
        shp = arr.shape
        return pl.BlockSpec((1,) + shp[1:], lambda b, t: (l,) + (0,) * (len(shp) - 1))

    weights = [P["norm_g"], P["w_in"], P["g_cq"], P["w_uq"], P["g_q"],
               P["g_ckv"], P["w_ukv"], P["g_k"], P["g_sq"], P["g_sk"]]
    nl = L // TOK
    in_specs = ([pl.BlockSpec((1, TOK, D), lambda b, t: (b, jnp.minimum(t, nl - 1), 0)),
                 pl.BlockSpec((1, TOK, D), lambda b, t: (b, 0, 0)),
                 pl.BlockSpec((1, 1, 3, D), lambda b, t: (l, jnp.where(t == nl, B, b), 0, 0))]
                + [wspec(w) for w in weights]
                + [pl.BlockSpec((3, TOK, LANES), lambda b, t: (0, t, 0)),
                   pl.BlockSpec((3, TOK, LANES), lambda b, t: (0, t, 0))])
    out_shape = [jax.ShapeDtypeStruct((B, MLA_HEADS, S, LANES), BF16),
                 jax.ShapeDtypeStruct((B, MLA_HEADS, S, LANES), BF16),
                 jax.ShapeDtypeStruct((B, S, 512), BF16),
                 jax.ShapeDtypeStruct((B, S, 512), BF16),
                 jax.ShapeDtypeStruct((B, S, 256), BF16),
                 jax.ShapeDtypeStruct((B, S, 256), BF16),
                 jax.ShapeDtypeStruct((B, S, 512), F32),
                 jax.ShapeDtypeStruct((B, S, 1536), BF16)]
    out_specs = [pl.BlockSpec((1, MLA_HEADS, TOK, LANES), lambda b, t: (b, 0, t, 0)),
                 pl.BlockSpec((1, MLA_HEADS, TOK, LANES), lambda b, t: (b, 0, t, 0))]
    out_specs += [pl.BlockSpec((1, TOK, s.shape[2]), lambda b, t: (b, t, 0)) for s in out_shape[2:]]
    return pl.pallas_call(
        functools.partial(_proj_kernel, n_lat_tiles=nl), out_shape=out_shape, grid=(B, nt),
        in_specs=in_specs, out_specs=out_specs,
        compiler_params=_cparams(2), name="proj",
    )(x, ctx, mod, *weights, rope_mla, rope_swa)


def _mla_kernel(q_ref, k_ref, v_ref, *rest, n_chunks):
    o_ref, s_ref, vo_ref = rest[-3:]
    nt_dims = (((1,), (1,)), ((), ()))
    lane_v = lax.broadcasted_iota(jnp.int32, (KV_CHUNK, LANES), 1)

    @pl.when(pl.program_id(2) == 0)
    def _():
        for c in range(n_chunks):
            v = v_ref[0, c]
            vo_ref[0, c] = jnp.where(lane_v < MLA_V, v, jnp.ones_like(v))
            vo_ref[1, c] = jnp.where(lane_v < MLA_V, jnp.ones_like(v), v)

    accs = []
    for e in range(2):
        q = q_ref[0, e]
        mx = None
        for c in range(n_chunks):
            s = lax.dot_general(q, k_ref[0, e, c], nt_dims, preferred_element_type=F32)
            s_ref[e, c] = s
            sm = jnp.maximum(s[:, :LANES], s[:, LANES:])
            mx = sm if mx is None else jnp.maximum(mx, sm)
        m = jnp.max(mx, axis=-1, keepdims=True)
        acc = None
        for c in range(n_chunks):
            p = jnp.exp2(s_ref[e, c] - m).astype(BF16)
            d = jnp.dot(p, vo_ref[e, c], preferred_element_type=F32)
            acc = d if acc is None else acc + d
        accs.append(acc)
    lane = lax.broadcasted_iota(jnp.int32, accs[0].shape, 1)
    o0 = accs[0] / accs[0][:, MLA_V:MLA_V + 1]
    o1 = accs[1] / accs[1][:, 0:1]
    o_ref[0] = jnp.where(lane < MLA_V, o0, o1).astype(BF16)


def _mla_call(qa, ka, va, n_lat):
    B, H, S, _ = qa.shape
    nc = S // KV_CHUNK
    tq = 2 * TOK
    k5 = ka.reshape(B, H, nc, KV_CHUNK, LANES)
    v4 = va.reshape(B, nc, KV_CHUNK, 512)
    o = pl.pallas_call(
        functools.partial(_mla_kernel, n_chunks=nc),
        out_shape=jax.ShapeDtypeStruct((B, S, 512), BF16),
        grid=(B, H // 2, n_lat // tq),
        in_specs=[pl.BlockSpec((1, 2, tq, LANES), lambda b, p, t: (b, p, t, 0)),
                  pl.BlockSpec((1, 2, nc, KV_CHUNK, LANES), lambda b, p, t: (b, p, 0, 0, 0)),
                  pl.BlockSpec((1, nc, KV_CHUNK, LANES), lambda b, p, t: (b, 0, 0, p))],
        out_specs=pl.BlockSpec((1, tq, LANES), lambda b, p, t: (b, t, p)),
        scratch_shapes=[pltpu.VMEM((2, nc, tq, KV_CHUNK), F32),
                        pltpu.VMEM((2, nc, KV_CHUNK, LANES), BF16)],
        compiler_params=_cparams(3), name="mla_attn",
    )(qa, k5, v4)
    cblk = n_lat // KV_CHUNK
    return pl.pallas_call(
        functools.partial(_mla_kernel, n_chunks=1),
        out_shape=jax.ShapeDtypeStruct((B, S, 512), BF16),
        grid=(B, H // 2, 1),
        in_specs=[pl.BlockSpec((1, 2, TOK, LANES), lambda b, p, t: (b, p, cblk, 0)),
                  pl.BlockSpec((1, 2, 1, KV_CHUNK, LANES), lambda b, p, t: (b, p, cblk, 0, 0)),
                  pl.BlockSpec((1, 1, KV_CHUNK, LANES), lambda b, p, t: (b, cblk, 0, p)),
                  pl.BlockSpec(memory_space=pl.ANY)],
        out_specs=pl.BlockSpec((1, TOK, LANES), lambda b, p, t: (b, cblk, p)),
        scratch_shapes=[pltpu.VMEM((2, 1, TOK, KV_CHUNK), F32),
                        pltpu.VMEM((2, 1, KV_CHUNK, LANES), BF16)],
        input_output_aliases={3: 0},
        compiler_params=_cparams(3), name="mla_attn_ctx",
    )(qa, k5, v4, o)


def _swa_kernel(q_ref, k_ref, v_ref, sink_ref, o_ref, ve_ref, vo_ref, bias_ref, *, n_ctx, n_lat):
    rows4 = 4 * SWA_BLOCK
    half = 2 * SWA_BLOCK
    win = 3 * SWA_BLOCK
    nt_dims = (((1,), (1,)), ((), ()))
    lane = lax.broadcasted_iota(jnp.int32, (SWA_BLOCK, LANES), 1)
    lo = lane < SWA_DIM
    sink = sink_ref[0][:, 0:1]
    kc = k_ref[0, n_lat:n_lat + n_ctx, :]

    v_all = v_ref[0]
    lane_s = lax.broadcasted_iota(jnp.int32, v_all.shape, 1) < SWA_DIM
    ve_ref[...] = jnp.where(lane_s, v_all, jnp.ones_like(v_all))
    vo_ref[...] = jnp.where(lane_s, jnp.ones_like(v_all), v_all)
    rel0 = (lax.broadcasted_iota(jnp.int32, (rows4, win), 1)
            - (lax.broadcasted_iota(jnp.int32, (rows4, win), 0) & (SWA_BLOCK - 1)))
    for kind in range(3):
        bias_ref[kind] = jnp.where(jnp.abs(rel0 - kind * SWA_BLOCK) <= SWA_WINDOW, 0.0, NEG)

    def stack_q(r0):
        qa = q_ref[0, pl.ds(r0, SWA_BLOCK), 0:LANES]
        qb = q_ref[0, pl.ds(r0, SWA_BLOCK), LANES:2 * LANES]
        z = jnp.zeros_like(qa)
        return jnp.concatenate([jnp.where(lo, qa, z), jnp.where(lo, qb, z),
                                jnp.where(lo, z, qa), jnp.where(lo, z, qb)], axis=0)

    def finish(r0, m, acc_e, acc_o):
        tail = jnp.exp2(sink - m)
        o_e = acc_e / (acc_e[:, SWA_DIM:SWA_DIM + 1] + tail[:half])
        o_o = acc_o / (acc_o[:, 0:1] + tail[half:])
        o_ref[0, pl.ds(r0, SWA_BLOCK), 0:LANES] = jnp.where(
            lo, o_e[:SWA_BLOCK], o_o[:SWA_BLOCK]).astype(BF16)
        o_ref[0, pl.ds(r0, SWA_BLOCK), LANES:2 * LANES] = jnp.where(
            lo, o_e[SWA_BLOCK:], o_o[SWA_BLOCK:]).astype(BF16)

    def lane_fold(s):
        out = s[:, :LANES]
        for i in range(1, s.shape[1] // LANES):
            out = jnp.maximum(out, s[:, i * LANES:(i + 1) * LANES])
        return out

    vce = ve_ref[n_lat:n_lat + n_ctx, :]
    vco = vo_ref[n_lat:n_lat + n_ctx, :]

    for n in range(n_ctx // SWA_BLOCK):
        r0 = n_lat + n * SWA_BLOCK
        q4 = stack_q(r0)
        s_c = lax.dot_general(q4, kc, nt_dims, preferred_element_type=F32)
        m = jnp.maximum(jnp.max(lane_fold(s_c), axis=-1, keepdims=True), sink)
        p_c = jnp.exp2(s_c - m).astype(BF16)
        finish(r0, m,
               jnp.dot(p_c[:half], vce, preferred_element_type=F32),
               jnp.dot(p_c[half:], vco, preferred_element_type=F32))

    def lat_block(n, carry):
        r0 = pl.multiple_of(n * SWA_BLOCK, SWA_BLOCK)
        w0 = pl.multiple_of(jnp.clip((n - 1) * SWA_BLOCK, 0, n_lat - win), SWA_BLOCK)
        kind = (r0 - w0) // SWA_BLOCK
        q4 = stack_q(r0)
        s_c = lax.dot_general(q4, kc, nt_dims, preferred_element_type=F32)
        s_w = lax.dot_general(q4, k_ref[0, pl.ds(w0, win), :], nt_dims,
                              preferred_element_type=F32) + bias_ref[kind]
        m = jnp.maximum(jnp.max(jnp.maximum(lane_fold(s_c), lane_fold(s_w)), axis=-1, keepdims=True), sink)
        p_c = jnp.exp2(s_c - m).astype(BF16)
        p_w = jnp.exp2(s_w - m).astype(BF16)
        acc_e = (jnp.dot(p_c[:half], vce, preferred_element_type=F32)
                 + jnp.dot(p_w[:half], ve_ref[pl.ds(w0, win), :], preferred_element_type=F32))
        acc_o = (jnp.dot(p_c[half:], vco, preferred_element_type=F32)
                 + jnp.dot(p_w[half:], vo_ref[pl.ds(w0, win), :], preferred_element_type=F32))
        finish(r0, m, acc_e, acc_o)
        return carry

    lax.fori_loop(0, n_lat // SWA_BLOCK, lat_block, 0, unroll=2)


def _swa_call(l, qb, kb, vb, sinkcol, n_ctx):
    B, S, _ = qb.shape
    return pl.pallas_call(
        functools.partial(_swa_kernel, n_ctx=n_ctx, n_lat=S - n_ctx),
        out_shape=jax.ShapeDtypeStruct((B, S, 512), BF16),
        grid=(B, SWA_KV_HEADS),
        in_specs=[pl.BlockSpec((1, S, 256), lambda b, j: (b, 0, j)),
                  pl.BlockSpec((1, S, LANES), lambda b, j: (b, 0, j)),
                  pl.BlockSpec((1, S, LANES), lambda b, j: (b, 0, j)),
                  pl.BlockSpec((1, 4 * SWA_BLOCK, LANES), lambda b, j: (l * SWA_KV_HEADS + j, 0, 0))],
        out_specs=pl.BlockSpec((1, S, 256), lambda b, j: (b, 0, j)),
        scratch_shapes=[pltpu.VMEM((S, LANES), BF16), pltpu.VMEM((S, LANES), BF16),
                        pltpu.VMEM((3, 4 * SWA_BLOCK, 3 * SWA_BLOCK), F32)],
        compiler_params=_cparams(2), name="swa_attn",
    )(qb, kb, vb, sinkcol)


def _gelu(y):
    return 0.5 * y * (1.0 + jnp.tanh(math.sqrt(2.0 / math.pi) * (y + 0.044715 * (y * y * y))))


def _s5_kernel(u_ref, wst_ref, wloc_ref, wcar_ref, at_ref, o_ref,
               ut_ref, utc_ref, yt_ref, ytc_ref, ere_ref, eim_ref, hfr_ref, hfi_ref, hbr_ref, hbi_ref,
               *, nb, n_lat_chunks, n_ctx_chunks):
    T, GP = S5_CHUNK, S5_GROUP
    n_chunks = n_lat_chunks + n_ctx_chunks
    n_lat = n_lat_chunks * T
    cw = nb * n_ctx_chunks
    tn_dims = (((0,), (0,)), ((), ()))
    nt_dims = (((1,), (1,)), ((), ()))
    gpl = LANES // GP

    for b in range(nb):
        for s in range(T):
            xs = u_ref[b, pl.ds(s, n_lat_chunks, stride=T), :]
            ut_ref[b, :, s] = xs.T.reshape(gpl, GP, n_lat_chunks).astype(BF16)
    zpad = jnp.zeros((LANES - cw, LANES), F32)
    for s in range(T):
        xs = jnp.concatenate([u_ref[b, pl.ds(n_lat + s, n_ctx_chunks, stride=T), :] for b in range(nb)]
                             + [zpad], axis=0)
        utc_ref[:, s] = xs.T.reshape(gpl, GP, LANES).astype(BF16)

    lane = lax.broadcasted_iota(jnp.int32, (2 * nb, LANES), 1)
    fwd = lane < S5_STATE
    lane_l = lax.broadcasted_iota(jnp.int32, (n_lat_chunks, LANES), 1) < S5_STATE
    lane_c = lax.broadcasted_iota(jnp.int32, (LANES, LANES), 1) < S5_STATE

    def pair(gp, carry):
        for gl in range(2):
            g = gp * 2 + gl
            wst = wst_ref[g]
            for b in range(nb):
                r0 = (gl * nb + b) * n_chunks
                e = lax.dot_general(ut_ref[b, g].reshape(T * GP, n_lat_chunks), wst, tn_dims,
                                    preferred_element_type=F32)
                ere_ref[r0:r0 + n_lat_chunks, :] = e[:, :LANES]
                eim_ref[r0:r0 + n_lat_chunks, :] = e[:, LANES:]
            ec = lax.dot_general(utc_ref[g].reshape(T * GP, LANES), wst, tn_dims, preferred_element_type=F32)
            for b in range(nb):
                r0 = (gl * nb + b) * n_chunks + n_lat_chunks
                ere_ref[r0:r0 + n_ctx_chunks, :] = ec[b * n_ctx_chunks:(b + 1) * n_ctx_chunks, :LANES]
                eim_ref[r0:r0 + n_ctx_chunks, :] = ec[b * n_ctx_chunks:(b + 1) * n_ctx_chunks, LANES:]

        a_re = jnp.concatenate([jnp.broadcast_to(at_ref[gp * 2 + gl, 0:1, :], (nb, LANES)) for gl in range(2)], 0)
        a_im = jnp.concatenate([jnp.broadcast_to(at_ref[gp * 2 + gl, 1:2, :], (nb, LANES)) for gl in range(2)], 0)

        def step(i, hc):
            h_re, h_im = hc
            cf = jnp.where(i < n_ctx_chunks, n_lat_chunks + i, i - n_ctx_chunks)
            cb = n_chunks - 1 - i
            sf = pl.ds(cf, 2 * nb, stride=n_chunks)
            sb = pl.ds(cb, 2 * nb, stride=n_chunks)
            hfr_ref[sf, :] = h_re
            hfi_ref[sf, :] = h_im
            hbr_ref[sb, :] = h_re
            hbi_ref[sb, :] = h_im
            e_re = jnp.where(fwd, ere_ref[sf, :], ere_ref[sb, :])
            e_im = jnp.where(fwd, eim_ref[sf, :], eim_ref[sb, :])
            return (a_re * h_re - a_im * h_im + e_re, a_re * h_im + a_im * h_re + e_im)

        zero = jnp.zeros((2 * nb, LANES), F32)
        lax.fori_loop(0, n_chunks, step, (zero, zero), unroll=4)

        for gl in range(2):
            g = gp * 2 + gl
            wloc, wcar = wloc_ref[g], wcar_ref[g]
            for b in range(nb):
                rows = pl.ds((gl * nb + b) * n_chunks, n_lat_chunks)
                h_cat = jnp.concatenate([jnp.where(lane_l, hfr_ref[rows, :], hbr_ref[rows, :]),
                                         jnp.where(lane_l, hfi_ref[rows, :], hbi_ref[rows, :])],
                                        axis=-1).astype(BF16)
                yt = (jnp.dot(wloc, ut_ref[b, g].reshape(T * GP, n_lat_chunks), preferred_element_type=F32)
                      + lax.dot_general(wcar, h_cat, nt_dims, preferred_element_type=F32))
                yt_ref[b, g] = _gelu(yt).astype(BF16).reshape(T, GP, n_lat_chunks)
            crow = lambda ref: jnp.concatenate(
                [ref[pl.ds((gl * nb + b) * n_chunks + n_lat_chunks, n_ctx_chunks), :] for b in range(nb)]
                + [jnp.zeros((LANES - cw, LANES), F32)], axis=0)
            h_cat = jnp.concatenate([jnp.where(lane_c, crow(hfr_ref), crow(hbr_ref)),
                                     jnp.where(lane_c, crow(hfi_ref), crow(hbi_ref))], axis=-1).astype(BF16)
            ytc = (jnp.dot(wloc, utc_ref[g].reshape(T * GP, LANES), preferred_element_type=F32)
                   + lax.dot_general(wcar, h_cat, nt_dims, preferred_element_type=F32))
            ytc_ref[g] = _gelu(ytc).astype(BF16).reshape(T, GP, LANES)
        return carry

    lax.fori_loop(0, gpl // 2, pair, 0)

    for b in range(nb):
        for t in range(T):
            z = yt_ref[b, :, t].astype(F32).reshape(LANES, n_lat_chunks)
            o_ref[b, pl.ds(t, n_lat_chunks, stride=T), :] = z.T
    for t in range(T):
        z = ytc_ref[:, t].astype(F32).reshape(LANES, LANES).T
        for b in range(nb):
            o_ref[b, pl.ds(n_lat + t, n_ctx_chunks, stride=T), :] = z[b * n_ctx_chunks:(b + 1) * n_ctx_chunks]


def _s5_call(l, u, P, n_lat):
    B, S, W = u.shape
    nb = 4 if B % 4 == 0 else B
    T, GP = S5_CHUNK, S5_GROUP
    nlc, ncc = n_lat // T, (S - n_lat) // T
    gpl = LANES // GP
    nblk = W // LANES
    big = lambda: pl.BlockSpec((nb, S, LANES), lambda j, hb: (hb, 0, j), pipeline_mode=pl.Buffered(1))
    wspec = lambda: pl.BlockSpec((gpl, T * GP, T * GP), lambda j, hb: (l * nblk + j, 0, 0))
    rows = 2 * nb * (nlc + ncc)
    return pl.pallas_call(
        functools.partial(_s5_kernel, nb=nb, n_lat_chunks=nlc, n_ctx_chunks=ncc),
        out_shape=jax.ShapeDtypeStruct((B, S, W), F32),
        grid=(nblk, B // nb),
        in_specs=[big(), wspec(), wspec(), wspec(),
                  pl.BlockSpec((gpl, 2, LANES), lambda j, hb: (l * nblk + j, 0, 0))],
        out_specs=big(),
        scratch_shapes=[pltpu.VMEM((nb, gpl, T, GP, nlc), BF16), pltpu.VMEM((gpl, T, GP, LANES), BF16),
                        pltpu.VMEM((nb, gpl, T, GP, nlc), BF16), pltpu.VMEM((gpl, T, GP, LANES), BF16)]
                       + [pltpu.VMEM((rows, LANES), F32)] * 6,
        compiler_params=_cparams(2), name="s5",
    )(u, P["s5_wst"], P["s5_wloc_t"], P["s5_wcar_t"], P["s5_at"])


def _out_kernel(x_ref, c_ref, mod_ref, oa_ref, ob_ref, gy_ref, gt_ref, wglu_ref, wout_ref,
                xo_ref, co_ref, *, n_lat_tiles):
    t = pl.program_id(1)
    z = jnp.dot(gy_ref[0].astype(BF16), wglu_ref[0], preferred_element_type=F32)
    oc = z[:, :512] * jax.nn.sigmoid(z[:, 512:])
    g = gt_ref[0].astype(F32)
    m_a = (oa_ref[0].astype(F32) * g[:, 0:512]).astype(BF16)
    m_b = (ob_ref[0].astype(F32) * g[:, 512:1024]).astype(BF16)
    m_c = (oc * g[:, 1024:1536]).astype(BF16)
    upd = (jnp.dot(m_a, wout_ref[0, 0:512, :], preferred_element_type=F32)
           + jnp.dot(m_b, wout_ref[0, 512:1024, :], preferred_element_type=F32)
           + jnp.dot(m_c, wout_ref[0, 1024:1536, :], preferred_element_type=F32))
    gate = mod_ref[0, 0][2:3]

    @pl.when(t == n_lat_tiles)
    def _():
        co_ref[0] = c_ref[0] + gate * upd

    @pl.when(t < n_lat_tiles)
    def _():
        xo_ref[0] = x_ref[0] + gate * upd


def _out_call(l, x, ctx, mod, oa, ob, gy, gt, P):
    B, L, D = x.shape
    C = ctx.shape[1]
    S = C + L
    nl = L // TOK
    xmap = lambda b, t: (b, jnp.minimum(t, nl - 1), 0)
    cmap = lambda b, t: (b, 0, 0)
    tmap = lambda b, t: (b, t, 0)
    return pl.pallas_call(
        functools.partial(_out_kernel, n_lat_tiles=nl),
        out_shape=[jax.ShapeDtypeStruct(x.shape, F32), jax.ShapeDtypeStruct(ctx.shape, F32)],
        grid=(B, S // TOK),
        in_specs=[pl.BlockSpec((1, TOK, D), xmap),
                  pl.BlockSpec((1, TOK, D), cmap),
                  pl.BlockSpec((1, 1, 3, D), lambda b, t: (l, jnp.where(t == nl, B, b), 0, 0)),
                  pl.BlockSpec((1, TOK, 512), tmap),
                  pl.BlockSpec((1, TOK, 512), tmap),
                  pl.BlockSpec((1, TOK, 512), tmap),
                  pl.BlockSpec((1, TOK, 1536), tmap),
                  pl.BlockSpec((1, 512, 1024), lambda b, t: (l, 0, 0)),
                  pl.BlockSpec((1, 1536, D), lambda b, t: (l, 0, 0))],
        out_specs=[pl.BlockSpec((1, TOK, D), xmap), pl.BlockSpec((1, TOK, D), cmap)],
        input_output_aliases={0: 0, 1: 1},
        compiler_params=_cparams(2), name="out",
    )(x, ctx, mod, oa, ob, gy, gt, P["w_glu"], P["w_out"])


def _rope_tables(n_lat, n_ctx, rot_dim, lead, reps):
    rows = n_lat // GRID_W
    r_idx, c_idx = jnp.meshgrid(jnp.arange(rows), jnp.arange(GRID_W), indexing="ij")
    r_idx, c_idx = r_idx.reshape(-1), c_idx.reshape(-1)
    n_freq = rot_dim // 4
    freqs = ROPE_BASE ** (-jnp.arange(n_freq, dtype=F32) / n_freq)
    ang = jnp.concatenate([r_idx.astype(F32)[:, None] * freqs,
                           c_idx.astype(F32)[:, None] * freqs], axis=-1)
    ang = jnp.concatenate([ang, jnp.zeros((n_ctx, rot_dim // 2), F32)], axis=0)
    cos, sin, zero = jnp.cos(ang), jnp.sin(ang), jnp.zeros_like(ang)
    n = ang.shape[0]
    tail = LANES // reps - lead - rot_dim

    def pack(x1, x2, fill):
        unit = [jnp.full((n, lead), fill, F32), x1, x2, jnp.full((n, tail), fill, F32)]
        return jnp.concatenate(unit * reps, axis=-1)

    return jnp.stack([pack(cos, cos, 1.0), pack(zero, sin, 0.0), pack(-sin, zero, 0.0)])


def _prep_params(norm_g, w_in, w_out, mla_g_cq, mla_g_ckv, mla_w_uq, mla_w_ukv, mla_g_qn, mla_g_kn,
                 swa_g_qn, swa_g_kn, swa_sink, s5_a_re, s5_a_im, s5_log_dt, s5_b_re, s5_b_im,
                 s5_c_re, s5_c_im, s5_d, s5_w_glu):
    depth, D, _ = w_in.shape
    o_cq, o_ckv, o_kr, o_gm, o_sq, o_sk, o_sv, o_gs, o_u, o_g5, o_end = (
        0, 384, 640, 672, 1184, 1696, 1824, 1952, 2464, 2976, 3488)
    z = lambda n: jnp.zeros((depth, D, n), F32)
    sk0, sk1 = w_in[:, :, o_sk:o_sk + 64], w_in[:, :, o_sk + 64:o_sv]
    sv0, sv1 = w_in[:, :, o_sv:o_sv + 64], w_in[:, :, o_sv + 64:o_gs]
    w_in_p = jnp.concatenate([
        w_in[:, :, o_cq:o_kr],
        z(64), w_in[:, :, o_kr:o_gm], z(32),
        w_in[:, :, o_sq:o_sk],
        sk0, sk0, sk1, sk1, sv0, sv0, sv1, sv1,
        w_in[:, :, o_u:o_g5],
        w_in[:, :, o_gm:o_sq], w_in[:, :, o_gs:o_u], w_in[:, :, o_g5:o_end],
    ], axis=-1).astype(BF16)
    assert w_in_p.shape[-1] == C_END

    wq = mla_w_uq.reshape(depth, MLA_Q_RANK, MLA_HEADS, MLA_QK)
    wq = jnp.pad(wq, ((0, 0), (0, 0), (0, 0), (0, LANES - MLA_QK)))
    w_uq_p = wq.reshape(depth, MLA_Q_RANK, MLA_HEADS * LANES).astype(BF16)
    wkv = mla_w_ukv.reshape(depth, MLA_KV_RANK, MLA_HEADS, MLA_NOPE + MLA_V)
    wk = jnp.pad(wkv[..., :MLA_NOPE], ((0, 0), (0, 0), (0, 0), (0, LANES - MLA_NOPE)))
    w_ukv_p = jnp.concatenate([wk.reshape(depth, MLA_KV_RANK, MLA_HEADS * LANES),
                               wkv[..., MLA_NOPE:].reshape(depth, MLA_KV_RANK, MLA_HEADS * MLA_V)],
                              axis=-1).astype(BF16)

    pad_qk = lambda g: jnp.pad(g, ((0, 0), (0, LANES - MLA_QK)))[:, None, :]
    g_q = pad_qk(mla_g_qn * (MLA_QK ** -0.5 * LOG2E))
    g_k = pad_qk(mla_g_kn)
    g_sq = jnp.tile(swa_g_qn * (SWA_DIM ** -0.5 * LOG2E), (1, 2))[:, None, :]
    g_sk = jnp.tile(swa_g_kn, (1, 2))[:, None, :]
    sink = (swa_sink * LOG2E).reshape(depth * SWA_KV_HEADS, SWA_HEADS // SWA_KV_HEADS)
    sink = sink[:, jnp.array([0, 2, 1, 3])]
    sinkcol = jnp.broadcast_to(jnp.repeat(sink, SWA_BLOCK, axis=1)[:, :, None],
                               (depth * SWA_KV_HEADS, 4 * SWA_BLOCK, LANES)).astype(F32)

    T = S5_CHUNK
    A = lax.complex(s5_a_re, s5_a_im)
    dt = jnp.exp(s5_log_dt)[..., None]
    a_bar = jnp.exp(dt * A)
    b_bar = ((a_bar - 1.0) / A)[..., None] * lax.complex(s5_b_re, s5_b_im)
    c_mat = lax.complex(s5_c_re, s5_c_im)
    k_idx = jnp.arange(T + 1, dtype=F32)
    pw = jnp.exp(k_idx[:, None, None, None, None] * (dt * A)[None])
    hi = lax.Precision.HIGHEST
    taps = jnp.einsum("ldgpn,kldgn,ldgnq->kldgpq", c_mat, pw[:T], b_bar, precision=hi).real
    kf, kb = taps[:, :, 0], taps[:, :, 1]
    tt = jnp.arange(T)
    dlt = tt[:, None] - tt[None, :]
    d_diag = s5_d.reshape(depth, S5_GROUPS, S5_GROUP)[..., None] * jnp.eye(S5_GROUP, dtype=F32)
    m_f = kf[jnp.clip(dlt, 0, T - 1)]
    m_b = kb[jnp.clip(-dlt, 0, T - 1)]
    sel = dlt[:, :, None, None, None, None]
    m_full = jnp.where(sel > 0, m_f, jnp.where(sel < 0, m_b, m_f + m_b + d_diag[None, None]))
    wloc_t = jnp.transpose(m_full, (2, 3, 0, 4, 1, 5)).reshape(depth * S5_GROUPS, T * S5_GROUP, T * S5_GROUP)
    inc_f = pw[T - 1 - tt][:, :, 0][..., None] * b_bar[None, :, 0]
    inc_b = pw[tt][:, :, 1][..., None] * b_bar[None, :, 1]
    to_rows = lambda m: jnp.transpose(m, (1, 2, 0, 4, 3)).reshape(depth * S5_GROUPS, T * S5_GROUP, S5_STATE)
    wst = jnp.concatenate([to_rows(inc_f.real), to_rows(inc_b.real),
                           to_rows(inc_f.imag), to_rows(inc_b.imag)], axis=-1)
    ro_f = c_mat[None, :, 0] * pw[tt + 1][:, :, 0][:, :, :, None, :]
    ro_b = c_mat[None, :, 1] * pw[T - tt][:, :, 1][:, :, :, None, :]
    to_tp = lambda m: jnp.transpose(m, (1, 2, 0, 3, 4)).reshape(depth * S5_GROUPS, T * S5_GROUP, S5_STATE)
    wcar_t = jnp.concatenate([to_tp(ro_f.real), to_tp(ro_b.real),
                              to_tp(-ro_f.imag), to_tp(-ro_b.imag)], axis=-1)
    a_t = pw[T]
    at = jnp.stack([jnp.concatenate([a_t[:, 0].real, a_t[:, 1].real], axis=-1),
                    jnp.concatenate([a_t[:, 0].imag, a_t[:, 1].imag], axis=-1)], axis=2)
    at = at.reshape(depth * S5_GROUPS, 2, LANES).astype(F32)

    return dict(norm_g=norm_g[:, None, :], w_in=w_in_p, g_cq=mla_g_cq[:, None, :], w_uq=w_uq_p, g_q=g_q,
                g_ckv=mla_g_ckv[:, None, :], w_ukv=w_ukv_p, g_k=g_k, g_sq=g_sq, g_sk=g_sk,
                sinkcol=sinkcol, s5_wloc_t=wloc_t.astype(BF16), s5_wst=wst.astype(BF16),
                s5_wcar_t=wcar_t.astype(BF16), s5_at=at, w_glu=s5_w_glu.astype(BF16),
                w_out=w_out.astype(BF16))


def kernel(x, c, ctx, c_ctx, norm_g, w_ada, b_ada, w_in, w_out, mla_g_cq, mla_g_ckv, mla_w_uq, mla_w_ukv, mla_g_qn, mla_g_kn, swa_g_qn, swa_g_kn, swa_sink, s5_a_re, s5_a_im, s5_log_dt, s5_b_re, s5_b_im, s5_c_re, s5_c_im, s5_d, s5_w_glu):
    B, L, D = x.shape
    C = ctx.shape[1]
    S = C + L
    depth = w_in.shape[0]
    assert B < MOD_ROWS and C == TOK and L % (2 * TOK) == 0 and L % GRID_W == 0 and L >= 3 * SWA_BLOCK

    P = _prep_params(norm_g, w_in, w_out, mla_g_cq, mla_g_ckv, mla_w_uq, mla_w_ukv, mla_g_qn, mla_g_kn,
                     swa_g_qn, swa_g_kn, swa_sink, s5_a_re, s5_a_im, s5_log_dt, s5_b_re, s5_b_im,
                     s5_c_re, s5_c_im, s5_d, s5_w_glu)
    rope_mla = _rope_tables(L, C, MLA_ROPE, MLA_NOPE, 1)
    rope_swa = _rope_tables(L, C, SWA_DIM, 0, 2)

    cc = jnp.concatenate([c, c_ctx[None, :], jnp.zeros((MOD_ROWS - B - 1, D), F32)], axis=0)
    mod = _ada_call(cc, w_ada, b_ada).reshape(depth, MOD_ROWS, 3, D)

    for l in range(depth):
        qa, ka, va, qb, kb, vb, u, gt = _proj_call(l, x, ctx, mod, P, rope_mla, rope_swa)
        oa = _mla_call(qa, ka, va, L)
        ob = _swa_call(l, qb, kb, vb, P["sinkcol"], C)
        gy = _s5_call(l, u, P, L)
        x, ctx = _out_call(l, x, ctx, mod, oa, ob, gy, gt, P)
    return x
```

```python
import functools
import math

import jax
import jax.numpy as jnp
from jax import lax
from jax.experimental import pallas as pl
from jax.experimental.pallas import tpu as pltpu

F32 = jnp.float32
BF16 = jnp.bfloat16

GRID_W = 64
EPS = 1e-6
ROPE_BASE = 10000.0
NEG = -1e30
LOG2E = math.log2(math.e)

MLA_HEADS = 8
MLA_NOPE = 64
MLA_ROPE = 32
MLA_V = 64
MLA_QK = MLA_NOPE + MLA_ROPE
MLA_Q_RANK = 384
MLA_KV_RANK = 256

SWA_HEADS = 8
SWA_KV_HEADS = 2
SWA_DIM = 64
SWA_WINDOW = 128

S5_GROUP = 16
S5_GROUPS = 32
S5_STATE = 64
S5_CHUNK = 16

LANES = 128
TOK = 256
KV_CHUNK = 256
SWA_BLOCK = 128
MOD_ROWS = 16
PROJ_NB = 2
PROJ_SKEW = 1

C_CQ = 0
C_CKV = 384
C_KR = 640
C_SQ = 768
C_SK = 1280
C_SV = 1536
C_U = 1792
C_GATE = 2304
C_END = 3840

VMEM_LIMIT = 56 * 1024 * 1024


def _cparams(n_axes):
    return pltpu.CompilerParams(dimension_semantics=("arbitrary",) * n_axes,
                                vmem_limit_bytes=VMEM_LIMIT)


def _ada_kernel(c_ref, w_ref, b_ref, o_ref):
    cc = c_ref[...]
    s = cc * jax.nn.sigmoid(cc)
    o_ref[0] = jnp.dot(s, w_ref[0], preferred_element_type=F32,
                       precision=lax.Precision.HIGHEST) + b_ref[0]


def _ada_call(cc, w_ada, b_ada):
    depth, d, n3 = w_ada.shape
    tn = 768
    return pl.pallas_call(
        _ada_kernel,
        out_shape=jax.ShapeDtypeStruct((depth, MOD_ROWS, n3), F32),
        grid=(depth, n3 // tn),
        in_specs=[pl.BlockSpec((MOD_ROWS, d), lambda l, j: (0, 0)),
                  pl.BlockSpec((1, d, tn), lambda l, j: (l, 0, j)),
                  pl.BlockSpec((1, 1, tn), lambda l, j: (l, 0, j))],
        out_specs=pl.BlockSpec((1, MOD_ROWS, tn), lambda l, j: (l, 0, j)),
        compiler_params=_cparams(2),
        name="ada",
    )(cc, w_ada, b_ada.reshape(depth, 1, n3))


def _sigmoid(v):
    return 0.5 * jnp.tanh(0.5 * v) + 0.5


def _norm_rope(slots, lo_masks, dim, gain, tab_ref, half, dup=False):
    sq = [s * s for s in slots]
    sums = []
    for s2, lo in zip(sq, lo_masks):
        if lo is None:
            sums.append((jnp.sum(s2, axis=-1, keepdims=True),))
        elif dup:
            sums.append((jnp.sum(jnp.where(lo, s2, 0.0), axis=-1, keepdims=True),))
        else:
            sums.append((jnp.sum(jnp.where(lo, s2, 0.0), axis=-1, keepdims=True),
                         jnp.sum(jnp.where(lo, 0.0, s2), axis=-1, keepdims=True)))
    ys = []
    for s, ss, lo in zip(slots, sums, lo_masks):
        rs = [lax.rsqrt(v * (1.0 / dim) + EPS) for v in ss]
        r = rs[0] if len(rs) == 1 else jnp.where(lo, rs[0], rs[1])
        ys.append(s * r * gain)
    up = [pltpu.roll(y, half, 1) for y in ys]
    dn = [pltpu.roll(y, LANES - half, 1) for y in ys]
    return [(y * tab_ref[0] + u * tab_ref[1] + d * tab_ref[2]).astype(BF16) for y, u, d in zip(ys, up, dn)]


N_PROJ_DATA = 3
N_PROJ_WEIGHTS = 12


def _proj_kernel(*refs, n_lat_tiles):
    data = refs[:N_PROJ_DATA]
    shared = refs[N_PROJ_DATA:N_PROJ_DATA + N_PROJ_WEIGHTS]
    outs = refs[N_PROJ_DATA + N_PROJ_WEIGHTS:]
    streams = []
    for i in range(data[0].shape[0]):
        one = pl.ds(i, 1)
        streams.append(_proj_tile(data[0].at[one], data[1].at[one], data[2].at[:, one], *shared,
                                  *[o.at[one] for o in outs], n_lat_tiles=n_lat_tiles))
    pending = list(enumerate(streams))
    step = 0
    while pending:
        for item in list(pending):
            if step >= item[0] * PROJ_SKEW and next(item[1], "done") == "done":
                pending.remove(item)
        step += 1


def _proj_tile(x_ref, c_ref, mod_ref, ng_ref, win_ref, gcq_ref, wuq_ref, gq_ref,
               gckv_ref, wukv_ref, gk_ref, gsq_ref, gsk_ref, rm_ref, rs_ref,
               qa_ref, ka_ref, va_ref, qb_ref, kb_ref, vb_ref, u_ref, gt_ref, *, n_lat_tiles):
    t = pl.program_id(1)
    x = jnp.where(t == n_lat_tiles, c_ref[0], x_ref[0])
    mod = mod_ref[0, 0]
    y = x * lax.rsqrt(jnp.mean(x * x, axis=-1, keepdims=True) + EPS) * ng_ref[0]
    xn = (y * (1.0 + mod[1:2]) + mod[0:1]).astype(BF16)

    def seg(a, b):
        return jnp.dot(xn, win_ref[0, :, a:b], preferred_element_type=F32)

    lane = lax.broadcasted_iota(jnp.int32, (TOK, LANES), 1)
    lo = lane < 64

    yield
    cq = seg(C_CQ, C_CKV)
    ckv = seg(C_CKV, C_KR)
    kr = seg(C_KR, C_SQ)
    yield
    cqn = (cq * lax.rsqrt(jnp.mean(cq * cq, axis=-1, keepdims=True) + EPS) * gcq_ref[0]).astype(BF16)
    qf = jnp.dot(cqn, wuq_ref[0], preferred_element_type=F32)
    ckvn = (ckv * lax.rsqrt(jnp.mean(ckv * ckv, axis=-1, keepdims=True) + EPS) * gckv_ref[0]).astype(BF16)
    kvf = jnp.dot(ckvn, wukv_ref[0], preferred_element_type=F32)
    sq = seg(C_SQ, C_SK)
    yield
    slots = [qf[:, h * LANES:(h + 1) * LANES] for h in range(MLA_HEADS)]
    for h, o in enumerate(_norm_rope(slots, [None] * MLA_HEADS, MLA_QK, gq_ref[0], rm_ref, MLA_ROPE // 2)):
        qa_ref[0, h] = o

    sk = seg(C_SK, C_SV)
    vb_ref[0] = seg(C_SV, C_U).astype(BF16)
    yield
    slots = [kvf[:, h * LANES:(h + 1) * LANES] + kr for h in range(MLA_HEADS)]
    for h, o in enumerate(_norm_rope(slots, [None] * MLA_HEADS, MLA_QK, gk_ref[0], rm_ref, MLA_ROPE // 2)):
        ka_ref[0, h] = o
    va_ref[0] = kvf[:, MLA_HEADS * LANES:].astype(BF16)
    u_ref[0] = seg(C_U, C_GATE)
    yield
    slots = [sq[:, p * LANES:(p + 1) * LANES] for p in range(SWA_HEADS // 2)]
    for p, o in enumerate(_norm_rope(slots, [lo] * len(slots), SWA_DIM, gsq_ref[0], rs_ref, SWA_DIM // 2)):
        qb_ref[0, :, p * LANES:(p + 1) * LANES] = o
    slots = [sk[:, j * LANES:(j + 1) * LANES] for j in range(SWA_KV_HEADS)]
    for j, o in enumerate(_norm_rope(slots, [lo] * len(slots), SWA_DIM, gsk_ref[0], rs_ref, SWA_DIM // 2, dup=True)):
        kb_ref[0, :, j * LANES:(j + 1) * LANES] = o
    g = seg(C_GATE, C_END)
    yield
    gt_ref[0] = (g * _sigmoid(g)).astype(BF16)


def _proj_call(l, x, ctx, mod, P, rope_mla, rope_swa):
    B, L, D = x.shape
    C = ctx.shape[1]
    S = C + L
    nt = S // TOK

    def wspec(arr):
        shp = arr.shape
        return pl.BlockSpec((1,) + shp[1:], lambda b, t: (l,) + (0,) * (len(shp) - 1))

    weights = [P["norm_g"], P["w_in"], P["g_cq"], P["w_uq"], P["g_q"],
               P["g_ckv"], P["w_ukv"], P["g_k"], P["g_sq"], P["g_sk"]]
    nl = L // TOK
    nb = PROJ_NB
    assert len(weights) + 2 == N_PROJ_WEIGHTS and B % nb == 0
    in_specs = ([pl.BlockSpec((nb, TOK, D), lambda b, t: (b, jnp.minimum(t, nl - 1), 0)),
                 pl.BlockSpec((nb, TOK, D), lambda b, t: (b, 0, 0)),
                 pl.BlockSpec((1, nb, 3, D), lambda b, t: (l, jnp.where(t == nl, B // nb, b), 0, 0))]
                + [wspec(w) for w in weights]
                + [pl.BlockSpec((3, TOK, LANES), lambda b, t: (0, t, 0)),
                   pl.BlockSpec((3, TOK, LANES), lambda b, t: (0, t, 0))])
    out_shape = [jax.ShapeDtypeStruct((B, MLA_HEADS, S, LANES), BF16),
                 jax.ShapeDtypeStruct((B, MLA_HEADS, S, LANES), BF16),
                 jax.ShapeDtypeStruct((B, S, 512), BF16),
                 jax.ShapeDtypeStruct((B, S, 512), BF16),
                 jax.ShapeDtypeStruct((B, S, 256), BF16),
                 jax.ShapeDtypeStruct((B, S, 256), BF16),
                 jax.ShapeDtypeStruct((B, S, 512), F32),
                 jax.ShapeDtypeStruct((B, S, 1536), BF16)]
    out_specs = [pl.BlockSpec((nb, MLA_HEADS, TOK, LANES), lambda b, t: (b, 0, t, 0)),
                 pl.BlockSpec((nb, MLA_HEADS, TOK, LANES), lambda b, t: (b, 0, t, 0))]
    out_specs += [pl.BlockSpec((nb, TOK, s.shape[2]), lambda b, t: (b, t, 0)) for s in out_shape[2:]]
    return pl.pallas_call(
        functools.partial(_proj_kernel, n_lat_tiles=nl), out_shape=out_shape, grid=(B // nb, nt),
        in_specs=in_specs, out_specs=out_specs,
        compiler_params=_cparams(2), name="proj",
    )(x, ctx, mod, *weights, rope_mla, rope_swa)


def _mla_kernel(q_ref, k_ref, v_ref, *rest, n_chunks):
    o_ref, vo_ref = rest[-2:]
    nt_dims = (((1,), (1,)), ((), ()))
    lane_v = lax.broadcasted_iota(jnp.int32, (KV_CHUNK, LANES), 1)

    @pl.when(pl.program_id(2) == 0)
    def _():
        for c in range(n_chunks):
            v = v_ref[0, c]
            vo_ref[0, c] = jnp.where(lane_v < MLA_V, v, jnp.ones_like(v))
            vo_ref[1, c] = jnp.where(lane_v < MLA_V, jnp.ones_like(v), v)

    qs = [q_ref[0, 0], q_ref[0, 1]]
    ms = [None, None]
    accs = [None, None]
    for c in range(n_chunks):
        for e in range(2):
            s = lax.dot_general(qs[e], k_ref[0, e, c], nt_dims, preferred_element_type=F32)
            mc = jnp.max(jnp.maximum(s[:, :LANES], s[:, LANES:]), axis=-1, keepdims=True)
            m_new = mc if c == 0 else jnp.maximum(ms[e], mc)
            p = jnp.exp2(s - m_new).astype(BF16)
            pv = jnp.dot(p, vo_ref[e, c], preferred_element_type=F32)
            accs[e] = pv if c == 0 else accs[e] * jnp.exp2(ms[e] - m_new) + pv
            ms[e] = m_new
    lane = lax.broadcasted_iota(jnp.int32, accs[0].shape, 1)
    o0 = accs[0] / accs[0][:, MLA_V:MLA_V + 1]
    o1 = accs[1] / accs[1][:, 0:1]
    o_ref[0] = jnp.where(lane < MLA_V, o0, o1).astype(BF16)


def _mla_call(qa, ka, va, n_lat):
    B, H, S, _ = qa.shape
    nc = S // KV_CHUNK
    tq = 2 * TOK
    k5 = ka.reshape(B, H, nc, KV_CHUNK, LANES)
    v4 = va.reshape(B, nc, KV_CHUNK, 512)
    o = pl.pallas_call(
        functools.partial(_mla_kernel, n_chunks=nc),
        out_shape=jax.ShapeDtypeStruct((B, S, 512), BF16),
        grid=(B, H // 2, n_lat // tq),
        in_specs=[pl.BlockSpec((1, 2, tq, LANES), lambda b, p, t: (b, p, t, 0)),
                  pl.BlockSpec((1, 2, nc, KV_CHUNK, LANES), lambda b, p, t: (b, p, 0, 0, 0)),
                  pl.BlockSpec((1, nc, KV_CHUNK, LANES), lambda b, p, t: (b, 0, 0, p))],
        out_specs=pl.BlockSpec((1, tq, LANES), lambda b, p, t: (b, t, p)),
        scratch_shapes=[pltpu.VMEM((2, nc, KV_CHUNK, LANES), BF16)],
        compiler_params=_cparams(3), name="mla_attn",
    )(qa, k5, v4)
    cblk = n_lat // KV_CHUNK
    return pl.pallas_call(
        functools.partial(_mla_kernel, n_chunks=1),
        out_shape=jax.ShapeDtypeStruct((B, S, 512), BF16),
        grid=(B, H // 2, 1),
        in_specs=[pl.BlockSpec((1, 2, TOK, LANES), lambda b, p, t: (b, p, cblk, 0)),
                  pl.BlockSpec((1, 2, 1, KV_CHUNK, LANES), lambda b, p, t: (b, p, cblk, 0, 0)),
                  pl.BlockSpec((1, 1, KV_CHUNK, LANES), lambda b, p, t: (b, cblk, 0, p)),
                  pl.BlockSpec(memory_space=pl.ANY)],
        out_specs=pl.BlockSpec((1, TOK, LANES), lambda b, p, t: (b, cblk, p)),
        scratch_shapes=[pltpu.VMEM((2, 1, KV_CHUNK, LANES), BF16)],
        input_output_aliases={3: 0},
        compiler_params=_cparams(3), name="mla_attn_ctx",
    )(qa, k5, v4, o)


def _swa_kernel(q_ref, k_ref, v_ref, sink_ref, o_ref, ve_ref, vo_ref, bias_ref, *, n_ctx, n_lat):
    rows4 = 4 * SWA_BLOCK
    half = 2 * SWA_BLOCK
    win = 3 * SWA_BLOCK
    nt_dims = (((1,), (1,)), ((), ()))
    lane = lax.broadcasted_iota(jnp.int32, (SWA_BLOCK, LANES), 1)
    lo = lane < SWA_DIM
    sink = sink_ref[0][:, 0:1]
    kc = k_ref[0, n_lat:n_lat + n_ctx, :]

    v_all = v_ref[0]
    lane_s = lax.broadcasted_iota(jnp.int32, v_all.shape, 1) < SWA_DIM
    ve_ref[...] = jnp.where(lane_s, v_all, jnp.ones_like(v_all))
    vo_ref[...] = jnp.where(lane_s, jnp.ones_like(v_all), v_all)
    rel0 = (lax.broadcasted_iota(jnp.int32, (rows4, win), 1)
            - (lax.broadcasted_iota(jnp.int32, (rows4, win), 0) & (SWA_BLOCK - 1)))
    for kind in range(3):
        bias_ref[kind] = jnp.where(jnp.abs(rel0 - kind * SWA_BLOCK) <= SWA_WINDOW, 0.0, NEG)

    def stack_q(r0):
        qa = q_ref[0, pl.ds(r0, SWA_BLOCK), 0:LANES]
        qb = q_ref[0, pl.ds(r0, SWA_BLOCK), LANES:2 * LANES]
        z = jnp.zeros_like(qa)
        return jnp.concatenate([jnp.where(lo, qa, z), jnp.where(lo, qb, z),
                                jnp.where(lo, z, qa), jnp.where(lo, z, qb)], axis=0)

    def finish(r0, m, acc_e, acc_o):
        tail = jnp.exp2(sink - m)
        o_e = acc_e / (acc_e[:, SWA_DIM:SWA_DIM + 1] + tail[:half])
        o_o = acc_o / (acc_o[:, 0:1] + tail[half:])
        o_ref[0, pl.ds(r0, SWA_BLOCK), 0:LANES] = jnp.where(
            lo, o_e[:SWA_BLOCK], o_o[:SWA_BLOCK]).astype(BF16)
        o_ref[0, pl.ds(r0, SWA_BLOCK), LANES:2 * LANES] = jnp.where(
            lo, o_e[SWA_BLOCK:], o_o[SWA_BLOCK:]).astype(BF16)

    def lane_fold(s):
        out = s[:, :LANES]
        for i in range(1, s.shape[1] // LANES):
            out = jnp.maximum(out, s[:, i * LANES:(i + 1) * LANES])
        return out

    vce = ve_ref[n_lat:n_lat + n_ctx, :]
    vco = vo_ref[n_lat:n_lat + n_ctx, :]

    for n in range(n_ctx // SWA_BLOCK):
        r0 = n_lat + n * SWA_BLOCK
        q4 = stack_q(r0)
        s_c = lax.dot_general(q4, kc, nt_dims, preferred_element_type=F32)
        m = jnp.maximum(jnp.max(lane_fold(s_c), axis=-1, keepdims=True), sink)
        p_c = jnp.exp2(s_c - m).astype(BF16)
        finish(r0, m,
               jnp.dot(p_c[:half], vce, preferred_element_type=F32),
               jnp.dot(p_c[half:], vco, preferred_element_type=F32))

    def lat_block(n, carry):
        r0 = pl.multiple_of(n * SWA_BLOCK, SWA_BLOCK)
        w0 = pl.multiple_of(jnp.clip((n - 1) * SWA_BLOCK, 0, n_lat - win), SWA_BLOCK)
        kind = (r0 - w0) // SWA_BLOCK
        q4 = stack_q(r0)
        s_c = lax.dot_general(q4, kc, nt_dims, preferred_element_type=F32)
        s_w = lax.dot_general(q4, k_ref[0, pl.ds(w0, win), :], nt_dims,
                              preferred_element_type=F32) + bias_ref[kind]
        m = jnp.maximum(jnp.max(jnp.maximum(lane_fold(s_c), lane_fold(s_w)), axis=-1, keepdims=True), sink)
        p_c = jnp.exp2(s_c - m).astype(BF16)
        p_w = jnp.exp2(s_w - m).astype(BF16)
        acc_e = (jnp.dot(p_c[:half], vce, preferred_element_type=F32)
                 + jnp.dot(p_w[:half], ve_ref[pl.ds(w0, win), :], preferred_element_type=F32))
        acc_o = (jnp.dot(p_c[half:], vco, preferred_element_type=F32)
                 + jnp.dot(p_w[half:], vo_ref[pl.ds(w0, win), :], preferred_element_type=F32))
        finish(r0, m, acc_e, acc_o)
        return carry

    lax.fori_loop(0, n_lat // SWA_BLOCK, lat_block, 0, unroll=2)


def _swa_call(l, qb, kb, vb, sinkcol, n_ctx):
    B, S, _ = qb.shape
    return pl.pallas_call(
        functools.partial(_swa_kernel, n_ctx=n_ctx, n_lat=S - n_ctx),
        out_shape=jax.ShapeDtypeStruct((B, S, 512), BF16),
        grid=(B, SWA_KV_HEADS),
        in_specs=[pl.BlockSpec((1, S, 256), lambda b, j: (b, 0, j)),
                  pl.BlockSpec((1, S, LANES), lambda b, j: (b, 0, j)),
                  pl.BlockSpec((1, S, LANES), lambda b, j: (b, 0, j)),
                  pl.BlockSpec((1, 4 * SWA_BLOCK, LANES), lambda b, j: (l * SWA_KV_HEADS + j, 0, 0))],
        out_specs=pl.BlockSpec((1, S, 256), lambda b, j: (b, 0, j)),
        scratch_shapes=[pltpu.VMEM((S, LANES), BF16), pltpu.VMEM((S, LANES), BF16),
                        pltpu.VMEM((3, 4 * SWA_BLOCK, 3 * SWA_BLOCK), F32)],
        compiler_params=_cparams(2), name="swa_attn",
    )(qb, kb, vb, sinkcol)


def _gelu(y):
    return 0.5 * y * (1.0 + jnp.tanh(math.sqrt(2.0 / math.pi) * (y + 0.044715 * (y * y * y))))


def _s5_kernel(u_ref, wst_ref, wloc_ref, wcar_ref, at_ref, o_ref,
               ut_ref, utc_ref, yt_ref, ytc_ref, ere_ref, eim_ref, hfr_ref, hfi_ref, hbr_ref, hbi_ref,
               *, nb, n_lat_chunks, n_ctx_chunks):
    T, GP = S5_CHUNK, S5_GROUP
    n_chunks = n_lat_chunks + n_ctx_chunks
    n_lat = n_lat_chunks * T
    cw = nb * n_ctx_chunks
    tn_dims = (((0,), (0,)), ((), ()))
    nt_dims = (((1,), (1,)), ((), ()))
    gpl = LANES // GP

    for b in range(nb):
        for s in range(T):
            xs = u_ref[b, pl.ds(s, n_lat_chunks, stride=T), :]
            ut_ref[b, :, s] = xs.T.reshape(gpl, GP, n_lat_chunks).astype(BF16)
    zpad = jnp.zeros((LANES - cw, LANES), F32)
    for s in range(T):
        xs = jnp.concatenate([u_ref[b, pl.ds(n_lat + s, n_ctx_chunks, stride=T), :] for b in range(nb)]
                             + [zpad], axis=0)
        utc_ref[:, s] = xs.T.reshape(gpl, GP, LANES).astype(BF16)

    lane = lax.broadcasted_iota(jnp.int32, (2 * nb, LANES), 1)
    fwd = lane < S5_STATE
    lane_l = lax.broadcasted_iota(jnp.int32, (n_lat_chunks, LANES), 1) < S5_STATE
    lane_c = lax.broadcasted_iota(jnp.int32, (LANES, LANES), 1) < S5_STATE

    def pair(gp, carry):
        for gl in range(2):
            g = gp * 2 + gl
            wst = wst_ref[g]
            for b in range(nb):
                r0 = (gl * nb + b) * n_chunks
                e = lax.dot_general(ut_ref[b, g].reshape(T * GP, n_lat_chunks), wst, tn_dims,
                                    preferred_element_type=F32)
                ere_ref[r0:r0 + n_lat_chunks, :] = e[:, :LANES]
                eim_ref[r0:r0 + n_lat_chunks, :] = e[:, LANES:]
            ec = lax.dot_general(utc_ref[g].reshape(T * GP, LANES), wst, tn_dims, preferred_element_type=F32)
            for b in range(nb):
                r0 = (gl * nb + b) * n_chunks + n_lat_chunks
                ere_ref[r0:r0 + n_ctx_chunks, :] = ec[b * n_ctx_chunks:(b + 1) * n_ctx_chunks, :LANES]
                eim_ref[r0:r0 + n_ctx_chunks, :] = ec[b * n_ctx_chunks:(b + 1) * n_ctx_chunks, LANES:]

        a_re = jnp.concatenate([jnp.broadcast_to(at_ref[gp * 2 + gl, 0:1, :], (nb, LANES)) for gl in range(2)], 0)
        a_im = jnp.concatenate([jnp.broadcast_to(at_ref[gp * 2 + gl, 1:2, :], (nb, LANES)) for gl in range(2)], 0)

        def step(i, hc):
            h_re, h_im = hc
            cf = jnp.where(i < n_ctx_chunks, n_lat_chunks + i, i - n_ctx_chunks)
            cb = n_chunks - 1 - i
            sf = pl.ds(cf, 2 * nb, stride=n_chunks)
            sb = pl.ds(cb, 2 * nb, stride=n_chunks)
            hfr_ref[sf, :] = h_re
            hfi_ref[sf, :] = h_im
            hbr_ref[sb, :] = h_re
            hbi_ref[sb, :] = h_im
            e_re = jnp.where(fwd, ere_ref[sf, :], ere_ref[sb, :])
            e_im = jnp.where(fwd, eim_ref[sf, :], eim_ref[sb, :])
            return (a_re * h_re - a_im * h_im + e_re, a_re * h_im + a_im * h_re + e_im)

        zero = jnp.zeros((2 * nb, LANES), F32)
        lax.fori_loop(0, n_chunks, step, (zero, zero), unroll=4)

        for gl in range(2):
            g = gp * 2 + gl
            wloc, wcar = wloc_ref[g], wcar_ref[g]
            for b in range(nb):
                rows = pl.ds((gl * nb + b) * n_chunks, n_lat_chunks)
                h_cat = jnp.concatenate([jnp.where(lane_l, hfr_ref[rows, :], hbr_ref[rows, :]),
                                         jnp.where(lane_l, hfi_ref[rows, :], hbi_ref[rows, :])],
                                        axis=-1).astype(BF16)
                yt = (jnp.dot(wloc, ut_ref[b, g].reshape(T * GP, n_lat_chunks), preferred_element_type=F32)
                      + lax.dot_general(wcar, h_cat, nt_dims, preferred_element_type=F32))
                yt_ref[b, g] = _gelu(yt).astype(BF16).reshape(T, GP, n_lat_chunks)
            crow = lambda ref: jnp.concatenate(
                [ref[pl.ds((gl * nb + b) * n_chunks + n_lat_chunks, n_ctx_chunks), :] for b in range(nb)]
                + [jnp.zeros((LANES - cw, LANES), F32)], axis=0)
            h_cat = jnp.concatenate([jnp.where(lane_c, crow(hfr_ref), crow(hbr_ref)),
                                     jnp.where(lane_c, crow(hfi_ref), crow(hbi_ref))], axis=-1).astype(BF16)
            ytc = (jnp.dot(wloc, utc_ref[g].reshape(T * GP, LANES), preferred_element_type=F32)
                   + lax.dot_general(wcar, h_cat, nt_dims, preferred_element_type=F32))
            ytc_ref[g] = _gelu(ytc).astype(BF16).reshape(T, GP, LANES)
        return carry

    lax.fori_loop(0, gpl // 2, pair, 0)

    for b in range(nb):
        for t in range(T):
            z = yt_ref[b, :, t].astype(F32).reshape(LANES, n_lat_chunks)
            o_ref[b, pl.ds(t, n_lat_chunks, stride=T), :] = z.T
    for t in range(T):
        z = ytc_ref[:, t].astype(F32).reshape(LANES, LANES).T
        for b in range(nb):
            o_ref[b, pl.ds(n_lat + t, n_ctx_chunks, stride=T), :] = z[b * n_ctx_chunks:(b + 1) * n_ctx_chunks]


def _s5_call(l, u, P, n_lat):
    B, S, W = u.shape
    nb = 4 if B % 4 == 0 else B
    T, GP = S5_CHUNK, S5_GROUP
    nlc, ncc = n_lat // T, (S - n_lat) // T
    gpl = LANES // GP
    nblk = W // LANES
    big = lambda: pl.BlockSpec((nb, S, LANES), lambda j, hb: (hb, 0, j), pipeline_mode=pl.Buffered(1))
    wspec = lambda: pl.BlockSpec((gpl, T * GP, T * GP), lambda j, hb: (l * nblk + j, 0, 0))
    rows = 2 * nb * (nlc + ncc)
    return pl.pallas_call(
        functools.partial(_s5_kernel, nb=nb, n_lat_chunks=nlc, n_ctx_chunks=ncc),
        out_shape=jax.ShapeDtypeStruct((B, S, W), F32),
        grid=(nblk, B // nb),
        in_specs=[big(), wspec(), wspec(), wspec(),
                  pl.BlockSpec((gpl, 2, LANES), lambda j, hb: (l * nblk + j, 0, 0))],
        out_specs=big(),
        scratch_shapes=[pltpu.VMEM((nb, gpl, T, GP, nlc), BF16), pltpu.VMEM((gpl, T, GP, LANES), BF16),
                        pltpu.VMEM((nb, gpl, T, GP, nlc), BF16), pltpu.VMEM((gpl, T, GP, LANES), BF16)]
                       + [pltpu.VMEM((rows, LANES), F32)] * 6,
        compiler_params=_cparams(2), name="s5",
    )(u, P["s5_wst"], P["s5_wloc_t"], P["s5_wcar_t"], P["s5_at"])


def _out_kernel(x_ref, c_ref, mod_ref, oa_ref, ob_ref, gy_ref, gt_ref, wglu_ref, wout_ref,
                xo_ref, co_ref, *, n_lat_tiles):
    t = pl.program_id(1)
    z = jnp.dot(gy_ref[0].astype(BF16), wglu_ref[0], preferred_element_type=F32)
    oc = z[:, :512] * _sigmoid(z[:, 512:])
    g = gt_ref[0].astype(F32)
    m_a = (oa_ref[0].astype(F32) * g[:, 0:512]).astype(BF16)
    m_b = (ob_ref[0].astype(F32) * g[:, 512:1024]).astype(BF16)
    m_c = (oc * g[:, 1024:1536]).astype(BF16)
    upd = (jnp.dot(m_a, wout_ref[0, 0:512, :], preferred_element_type=F32)
           + jnp.dot(m_b, wout_ref[0, 512:1024, :], preferred_element_type=F32)
           + jnp.dot(m_c, wout_ref[0, 1024:1536, :], preferred_element_type=F32))
    gate = mod_ref[0, 0][2:3]

    @pl.when(t == n_lat_tiles)
    def _():
        co_ref[0] = c_ref[0] + gate * upd

    @pl.when(t < n_lat_tiles)
    def _():
        xo_ref[0] = x_ref[0] + gate * upd


def _out_call(l, x, ctx, mod, oa, ob, gy, gt, P):
    B, L, D = x.shape
    C = ctx.shape[1]
    S = C + L
    nl = L // TOK
    xmap = lambda b, t: (b, jnp.minimum(t, nl - 1), 0)
    cmap = lambda b, t: (b, 0, 0)
    tmap = lambda b, t: (b, t, 0)
    return pl.pallas_call(
        functools.partial(_out_kernel, n_lat_tiles=nl),
        out_shape=[jax.ShapeDtypeStruct(x.shape, F32), jax.ShapeDtypeStruct(ctx.shape, F32)],
        grid=(B, S // TOK),
        in_specs=[pl.BlockSpec((1, TOK, D), xmap),
                  pl.BlockSpec((1, TOK, D), cmap),
                  pl.BlockSpec((1, 1, 3, D), lambda b, t: (l, jnp.where(t == nl, B, b), 0, 0)),
                  pl.BlockSpec((1, TOK, 512), tmap),
                  pl.BlockSpec((1, TOK, 512), tmap),
                  pl.BlockSpec((1, TOK, 512), tmap),
                  pl.BlockSpec((1, TOK, 1536), tmap),
                  pl.BlockSpec((1, 512, 1024), lambda b, t: (l, 0, 0)),
                  pl.BlockSpec((1, 1536, D), lambda b, t: (l, 0, 0))],
        out_specs=[pl.BlockSpec((1, TOK, D), xmap), pl.BlockSpec((1, TOK, D), cmap)],
        compiler_params=_cparams(2), name="out",
    )(x, ctx, mod, oa, ob, gy, gt, P["w_glu"], P["w_out"])


def _rope_tables(n_lat, n_ctx, rot_dim, lead, reps):
    rows = n_lat // GRID_W
    r_idx, c_idx = jnp.meshgrid(jnp.arange(rows), jnp.arange(GRID_W), indexing="ij")
    r_idx, c_idx = r_idx.reshape(-1), c_idx.reshape(-1)
    n_freq = rot_dim // 4
    freqs = ROPE_BASE ** (-jnp.arange(n_freq, dtype=F32) / n_freq)
    ang = jnp.concatenate([r_idx.astype(F32)[:, None] * freqs,
                           c_idx.astype(F32)[:, None] * freqs], axis=-1)
    ang = jnp.concatenate([ang, jnp.zeros((n_ctx, rot_dim // 2), F32)], axis=0)
    cos, sin, zero = jnp.cos(ang), jnp.sin(ang), jnp.zeros_like(ang)
    n = ang.shape[0]
    tail = LANES // reps - lead - rot_dim

    def pack(x1, x2, fill):
        unit = [jnp.full((n, lead), fill, F32), x1, x2, jnp.full((n, tail), fill, F32)]
        return jnp.concatenate(unit * reps, axis=-1)

    return jnp.stack([pack(cos, cos, 1.0), pack(zero, sin, 0.0), pack(-sin, zero, 0.0)])


def _prep_params(norm_g, w_in, w_out, mla_g_cq, mla_g_ckv, mla_w_uq, mla_w_ukv, mla_g_qn, mla_g_kn,
                 swa_g_qn, swa_g_kn, swa_sink, s5_a_re, s5_a_im, s5_log_dt, s5_b_re, s5_b_im,
                 s5_c_re, s5_c_im, s5_d, s5_w_glu):
    depth, D, _ = w_in.shape
    o_cq, o_ckv, o_kr, o_gm, o_sq, o_sk, o_sv, o_gs, o_u, o_g5, o_end = (
        0, 384, 640, 672, 1184, 1696, 1824, 1952, 2464, 2976, 3488)
    z = lambda n: jnp.zeros((depth, D, n), F32)
    sk0, sk1 = w_in[:, :, o_sk:o_sk + 64], w_in[:, :, o_sk + 64:o_sv]
    sv0, sv1 = w_in[:, :, o_sv:o_sv + 64], w_in[:, :, o_sv + 64:o_gs]
    w_in_p = jnp.concatenate([
        w_in[:, :, o_cq:o_kr],
        z(64), w_in[:, :, o_kr:o_gm], z(32),
        w_in[:, :, o_sq:o_sk],
        sk0, sk0, sk1, sk1, sv0, sv0, sv1, sv1,
        w_in[:, :, o_u:o_g5],
        w_in[:, :, o_gm:o_sq], w_in[:, :, o_gs:o_u], w_in[:, :, o_g5:o_end],
    ], axis=-1).astype(BF16)
    assert w_in_p.shape[-1] == C_END

    wq = mla_w_uq.reshape(depth, MLA_Q_RANK, MLA_HEADS, MLA_QK)
    wq = jnp.pad(wq, ((0, 0), (0, 0), (0, 0), (0, LANES - MLA_QK)))
    w_uq_p = wq.reshape(depth, MLA_Q_RANK, MLA_HEADS * LANES).astype(BF16)
    wkv = mla_w_ukv.reshape(depth, MLA_KV_RANK, MLA_HEADS, MLA_NOPE + MLA_V)
    wk = jnp.pad(wkv[..., :MLA_NOPE], ((0, 0), (0, 0), (0, 0), (0, LANES - MLA_NOPE)))
    w_ukv_p = jnp.concatenate([wk.reshape(depth, MLA_KV_RANK, MLA_HEADS * LANES),
                               wkv[..., MLA_NOPE:].reshape(depth, MLA_KV_RANK, MLA_HEADS * MLA_V)],
                              axis=-1).astype(BF16)

    pad_qk = lambda g: jnp.pad(g, ((0, 0), (0, LANES - MLA_QK)))[:, None, :]
    g_q = pad_qk(mla_g_qn * (MLA_QK ** -0.5 * LOG2E))
    g_k = pad_qk(mla_g_kn)
    g_sq = jnp.tile(swa_g_qn * (SWA_DIM ** -0.5 * LOG2E), (1, 2))[:, None, :]
    g_sk = jnp.tile(swa_g_kn, (1, 2))[:, None, :]
    sink = (swa_sink * LOG2E).reshape(depth * SWA_KV_HEADS, SWA_HEADS // SWA_KV_HEADS)
    sink = sink[:, jnp.array([0, 2, 1, 3])]
    sinkcol = jnp.broadcast_to(jnp.repeat(sink, SWA_BLOCK, axis=1)[:, :, None],
                               (depth * SWA_KV_HEADS, 4 * SWA_BLOCK, LANES)).astype(F32)

    T = S5_CHUNK
    A = lax.complex(s5_a_re, s5_a_im)
    dt = jnp.exp(s5_log_dt)[..., None]
    a_bar = jnp.exp(dt * A)
    b_bar = ((a_bar - 1.0) / A)[..., None] * lax.complex(s5_b_re, s5_b_im)
    c_mat = lax.complex(s5_c_re, s5_c_im)
    k_idx = jnp.arange(T + 1, dtype=F32)
    pw = jnp.exp(k_idx[None, None, None, :, None] * (dt * A)[:, :, :, None, :])
    hi = lax.Precision.HIGHEST
    tt = jnp.arange(T)
    GP, NG = S5_GROUP, depth * S5_GROUPS
    pw_f, pw_b = pw[:, 0], pw[:, 1]
    tap_f = jnp.einsum("lgpn,lgkn,lgnq->lgpkq", c_mat[:, 0], pw_f[:, :, T - 1 - tt], b_bar[:, 0],
                       precision=hi).real
    tap_b = jnp.einsum("lgpn,lgkn,lgnq->lgpkq", c_mat[:, 1], pw_b[:, :, tt], b_bar[:, 1],
                       precision=hi).real
    d_diag = s5_d.reshape(depth, S5_GROUPS, GP)[..., None] * jnp.eye(GP, dtype=F32)
    centre = tap_f[:, :, :, T - 1:] + tap_b[:, :, :, :1] + d_diag[:, :, :, None, :]
    krev = jnp.concatenate([tap_f[:, :, :, :T - 1], centre, tap_b[:, :, :, 1:]], axis=3)
    krev = krev.reshape(depth, S5_GROUPS, GP, (2 * T - 1) * GP)
    wloc_t = jnp.stack([krev[..., (T - 1 - t) * GP:(T - 1 - t) * GP + T * GP] for t in range(T)], axis=2)
    wloc_t = wloc_t.reshape(NG, T * GP, T * GP)
    b_t = jnp.swapaxes(b_bar, -1, -2)
    inc_f = pw_f[:, :, T - 1 - tt][:, :, :, None, :] * b_t[:, 0][:, :, None]
    inc_b = pw_b[:, :, tt][:, :, :, None, :] * b_t[:, 1][:, :, None]
    wst = jnp.concatenate([inc_f.real, inc_b.real, inc_f.imag, inc_b.imag], axis=-1)
    wst = wst.reshape(NG, T * GP, 4 * S5_STATE)
    ro_f = c_mat[:, 0][:, :, None] * pw_f[:, :, tt + 1][:, :, :, None, :]
    ro_b = c_mat[:, 1][:, :, None] * pw_b[:, :, T - tt][:, :, :, None, :]
    wcar_t = jnp.concatenate([ro_f.real, ro_b.real, -ro_f.imag, -ro_b.imag], axis=-1)
    wcar_t = wcar_t.reshape(NG, T * GP, 4 * S5_STATE)
    a_t = pw[:, :, :, T]
    at = jnp.stack([jnp.concatenate([a_t[:, 0].real, a_t[:, 1].real], axis=-1),
                    jnp.concatenate([a_t[:, 0].imag, a_t[:, 1].imag], axis=-1)], axis=2)
    at = at.reshape(NG, 2, LANES).astype(F32)

    return dict(norm_g=norm_g[:, None, :], w_in=w_in_p, g_cq=mla_g_cq[:, None, :], w_uq=w_uq_p, g_q=g_q,
                g_ckv=mla_g_ckv[:, None, :], w_ukv=w_ukv_p, g_k=g_k, g_sq=g_sq, g_sk=g_sk,
                sinkcol=sinkcol, s5_wloc_t=wloc_t.astype(BF16), s5_wst=wst.astype(BF16),
                s5_wcar_t=wcar_t.astype(BF16), s5_at=at, w_glu=s5_w_glu.astype(BF16),
                w_out=w_out.astype(BF16))


def kernel(x, c, ctx, c_ctx, norm_g, w_ada, b_ada, w_in, w_out, mla_g_cq, mla_g_ckv, mla_w_uq, mla_w_ukv, mla_g_qn, mla_g_kn, swa_g_qn, swa_g_kn, swa_sink, s5_a_re, s5_a_im, s5_log_dt, s5_b_re, s5_b_im, s5_c_re, s5_c_im, s5_d, s5_w_glu):
    B, L, D = x.shape
    C = ctx.shape[1]
    S = C + L
    depth = w_in.shape[0]
    assert B + PROJ_NB <= MOD_ROWS and C == TOK and L % (2 * TOK) == 0 and L % GRID_W == 0 and L >= 3 * SWA_BLOCK

    P = _prep_params(norm_g, w_in, w_out, mla_g_cq, mla_g_ckv, mla_w_uq, mla_w_ukv, mla_g_qn, mla_g_kn,
                     swa_g_qn, swa_g_kn, swa_sink, s5_a_re, s5_a_im, s5_log_dt, s5_b_re, s5_b_im,
                     s5_c_re, s5_c_im, s5_d, s5_w_glu)
    rope_mla = _rope_tables(L, C, MLA_ROPE, MLA_NOPE, 1)
    rope_swa = _rope_tables(L, C, SWA_DIM, 0, 2)

    cc = jnp.concatenate([c, jnp.tile(c_ctx[None, :], (PROJ_NB, 1)),
                          jnp.zeros((MOD_ROWS - B - PROJ_NB, D), F32)], axis=0)
    mod = _ada_call(cc, w_ada, b_ada).reshape(depth, MOD_ROWS, 3, D)

    for l in range(depth):
        qa, ka, va, qb, kb, vb, u, gt = _proj_call(l, x, ctx, mod, P, rope_mla, rope_swa)
        oa = _mla_call(qa, ka, va, L)
        ob = _swa_call(l, qb, kb, vb, P["sinkcol"], C)
        gy = _s5_call(l, u, P, L)
        x, ctx = _out_call(l, x, ctx, mod, oa, ob, gy, gt, P)
    return x
```

```python
import functools
import math

import jax
import jax.numpy as jnp
from jax import lax
from jax.experimental import pallas as pl
from jax.experimental.pallas import tpu as pltpu

F32 = jnp.float32
BF16 = jnp.bfloat16

GRID_W = 64
EPS = 1e-6
ROPE_BASE = 10000.0
NEG = -1e30
LOG2E = math.log2(math.e)

MLA_HEADS = 8
MLA_NOPE = 64
MLA_ROPE = 32
MLA_V = 64
MLA_QK = MLA_NOPE + MLA_ROPE
MLA_Q_RANK = 384
MLA_KV_RANK = 256

SWA_HEADS = 8
SWA_KV_HEADS = 2
SWA_DIM = 64
SWA_WINDOW = 128

S5_GROUP = 16
S5_GROUPS = 32
S5_STATE = 64
S5_CHUNK = 16

LANES = 128
TOK = 256
KV_CHUNK = 256
SWA_BLOCK = 128
MOD_ROWS = 16
PROJ_NB = 2
SWA_UNROLL = 16
OUT_COLS = 256
PROJ_SKEW = 1

C_CQ = 0
C_CKV = 384
C_KR = 640
C_SQ = 768
C_SK = 1280
C_SV = 1536
C_U = 1792
C_GATE = 2304
C_END = 3840

VMEM_LIMIT = 56 * 1024 * 1024


def _cparams(n_axes):
    return pltpu.CompilerParams(dimension_semantics=("arbitrary",) * n_axes,
                                vmem_limit_bytes=VMEM_LIMIT)


def _ada_kernel(c_ref, w_ref, b_ref, o_ref):
    cc = c_ref[...]
    s = cc * jax.nn.sigmoid(cc)
    o_ref[0] = jnp.dot(s, w_ref[0], preferred_element_type=F32,
                       precision=lax.Precision.HIGHEST) + b_ref[0]


def _ada_call(cc, w_ada, b_ada):
    depth, d, n3 = w_ada.shape
    tn = 768
    return pl.pallas_call(
        _ada_kernel,
        out_shape=jax.ShapeDtypeStruct((depth, MOD_ROWS, n3), F32),
        grid=(depth, n3 // tn),
        in_specs=[pl.BlockSpec((MOD_ROWS, d), lambda l, j: (0, 0)),
                  pl.BlockSpec((1, d, tn), lambda l, j: (l, 0, j)),
                  pl.BlockSpec((1, 1, tn), lambda l, j: (l, 0, j))],
        out_specs=pl.BlockSpec((1, MOD_ROWS, tn), lambda l, j: (l, 0, j)),
        compiler_params=_cparams(2),
        name="ada",
    )(cc, w_ada, b_ada.reshape(depth, 1, n3))


def _sigmoid(v):
    return 0.5 * jnp.tanh(0.5 * v) + 0.5


def _norm_rope(slots, lo_masks, dim, gain, tab_ref, half, dup=False):
    sq = [s * s for s in slots]
    sums = []
    for s2, lo in zip(sq, lo_masks):
        if lo is None:
            sums.append((jnp.sum(s2, axis=-1, keepdims=True),))
        elif dup:
            sums.append((jnp.sum(jnp.where(lo, s2, 0.0), axis=-1, keepdims=True),))
        else:
            sums.append((jnp.sum(jnp.where(lo, s2, 0.0), axis=-1, keepdims=True),
                         jnp.sum(jnp.where(lo, 0.0, s2), axis=-1, keepdims=True)))
    ys = []
    for s, ss, lo in zip(slots, sums, lo_masks):
        rs = [lax.rsqrt(v * (1.0 / dim) + EPS) for v in ss]
        r = rs[0] if len(rs) == 1 else jnp.where(lo, rs[0], rs[1])
        ys.append(s * r * gain)
    up = [pltpu.roll(y, half, 1) for y in ys]
    dn = [pltpu.roll(y, LANES - half, 1) for y in ys]
    return [(y * tab_ref[0] + u * tab_ref[1] + d * tab_ref[2]).astype(BF16) for y, u, d in zip(ys, up, dn)]


N_PROJ_DATA = 3
N_PROJ_WEIGHTS = 12


def _proj_kernel(*refs, n_lat_tiles):
    data = refs[:N_PROJ_DATA]
    shared = refs[N_PROJ_DATA:N_PROJ_DATA + N_PROJ_WEIGHTS]
    outs = refs[N_PROJ_DATA + N_PROJ_WEIGHTS:]
    streams = []
    for i in range(data[0].shape[0]):
        one = pl.ds(i, 1)
        streams.append(_proj_tile(data[0].at[one], data[1].at[one], data[2].at[:, one], *shared,
                                  *[o.at[one] for o in outs], n_lat_tiles=n_lat_tiles))
    pending = list(enumerate(streams))
    step = 0
    while pending:
        for item in list(pending):
            if step >= item[0] * PROJ_SKEW and next(item[1], "done") == "done":
                pending.remove(item)
        step += 1


def _proj_tile(x_ref, c_ref, mod_ref, ng_ref, win_ref, gcq_ref, wuq_ref, gq_ref,
               gckv_ref, wukv_ref, gk_ref, gsq_ref, gsk_ref, rm_ref, rs_ref,
               qa_ref, ka_ref, va_ref, qb_ref, kb_ref, vb_ref, u_ref, gt_ref, *, n_lat_tiles):
    t = pl.program_id(1)
    x = jnp.where(t == n_lat_tiles, c_ref[0], x_ref[0])
    mod = mod_ref[0, 0]
    y = x * lax.rsqrt(jnp.mean(x * x, axis=-1, keepdims=True) + EPS) * ng_ref[0]
    xn = (y * (1.0 + mod[1:2]) + mod[0:1]).astype(BF16)

    def seg(a, b):
        return jnp.dot(xn, win_ref[0, :, a:b], preferred_element_type=F32)

    lane = lax.broadcasted_iota(jnp.int32, (TOK, LANES), 1)
    lo = lane < 64

    yield
    cq = seg(C_CQ, C_CKV)
    ckv = seg(C_CKV, C_KR)
    kr = seg(C_KR, C_SQ)
    yield
    cqn = (cq * lax.rsqrt(jnp.mean(cq * cq, axis=-1, keepdims=True) + EPS) * gcq_ref[0]).astype(BF16)
    qf = jnp.dot(cqn, wuq_ref[0], preferred_element_type=F32)
    ckvn = (ckv * lax.rsqrt(jnp.mean(ckv * ckv, axis=-1, keepdims=True) + EPS) * gckv_ref[0]).astype(BF16)
    kvf = jnp.dot(ckvn, wukv_ref[0], preferred_element_type=F32)
    sq = seg(C_SQ, C_SK)
    yield
    slots = [qf[:, h * LANES:(h + 1) * LANES] for h in range(MLA_HEADS)]
    for h, o in enumerate(_norm_rope(slots, [None] * MLA_HEADS, MLA_QK, gq_ref[0], rm_ref, MLA_ROPE // 2)):
        qa_ref[0, h] = o

    sk = seg(C_SK, C_SV)
    vb_ref[0] = seg(C_SV, C_U).astype(BF16)
    yield
    slots = [kvf[:, h * LANES:(h + 1) * LANES] + kr for h in range(MLA_HEADS)]
    for h, o in enumerate(_norm_rope(slots, [None] * MLA_HEADS, MLA_QK, gk_ref[0], rm_ref, MLA_ROPE // 2)):
        ka_ref[0, h] = o
    va_ref[0] = kvf[:, MLA_HEADS * LANES:].astype(BF16)
    u_ref[0] = seg(C_U, C_GATE)
    yield
    slots = [sq[:, p * LANES:(p + 1) * LANES] for p in range(SWA_HEADS // 2)]
    for p, o in enumerate(_norm_rope(slots, [lo] * len(slots), SWA_DIM, gsq_ref[0], rs_ref, SWA_DIM // 2)):
        qb_ref[0, :, p * LANES:(p + 1) * LANES] = o
    slots = [sk[:, j * LANES:(j + 1) * LANES] for j in range(SWA_KV_HEADS)]
    for j, o in enumerate(_norm_rope(slots, [lo] * len(slots), SWA_DIM, gsk_ref[0], rs_ref, SWA_DIM // 2, dup=True)):
        kb_ref[0, :, j * LANES:(j + 1) * LANES] = o
    g = seg(C_GATE, C_END)
    yield
    gt_ref[0] = (g * _sigmoid(g)).astype(BF16)


def _proj_call(l, x, ctx, mod, P, rope_mla, rope_swa):
    B, L, D = x.shape
    C = ctx.shape[1]
    S = C + L
    nt = S // TOK

    def wspec(arr):
        shp = arr.shape
        return pl.BlockSpec((1,) + shp[1:], lambda b, t: (l,) + (0,) * (len(shp) - 1))

    weights = [P["norm_g"], P["w_in"], P["g_cq"], P["w_uq"], P["g_q"],
               P["g_ckv"], P["w_ukv"], P["g_k"], P["g_sq"], P["g_sk"]]
    nl = L // TOK
    nb = PROJ_NB
    assert len(weights) + 2 == N_PROJ_WEIGHTS and B % nb == 0
    in_specs = ([pl.BlockSpec((nb, TOK, D), lambda b, t: (b, jnp.minimum(t, nl - 1), 0)),
                 pl.BlockSpec((nb, TOK, D), lambda b, t: (b, 0, 0)),
                 pl.BlockSpec((1, nb, 3, D), lambda b, t: (l, jnp.where(t == nl, B // nb, b), 0, 0))]
                + [wspec(w) for w in weights]
                + [pl.BlockSpec((3, TOK, LANES), lambda b, t: (0, t, 0)),
                   pl.BlockSpec((3, TOK, LANES), lambda b, t: (0, t, 0))])
    out_shape = [jax.ShapeDtypeStruct((B, MLA_HEADS, S, LANES), BF16),
                 jax.ShapeDtypeStruct((B, MLA_HEADS, S, LANES), BF16),
                 jax.ShapeDtypeStruct((B, S, 512), BF16),
                 jax.ShapeDtypeStruct((B, S, 512), BF16),
                 jax.ShapeDtypeStruct((B, S, 256), BF16),
                 jax.ShapeDtypeStruct((B, S, 256), BF16),
                 jax.ShapeDtypeStruct((B, S, 512), F32),
                 jax.ShapeDtypeStruct((B, S, 1536), BF16)]
    out_specs = [pl.BlockSpec((nb, MLA_HEADS, TOK, LANES), lambda b, t: (b, 0, t, 0)),
                 pl.BlockSpec((nb, MLA_HEADS, TOK, LANES), lambda b, t: (b, 0, t, 0))]
    out_specs += [pl.BlockSpec((nb, TOK, s.shape[2]), lambda b, t: (b, t, 0)) for s in out_shape[2:]]
    return pl.pallas_call(
        functools.partial(_proj_kernel, n_lat_tiles=nl), out_shape=out_shape, grid=(B // nb, nt),
        in_specs=in_specs, out_specs=out_specs,
        compiler_params=_cparams(2), name="proj",
    )(x, ctx, mod, *weights, rope_mla, rope_swa)


def _mla_kernel(q_ref, k_ref, v_ref, *rest, n_chunks):
    o_ref, vo_ref = rest[-2:]
    nt_dims = (((1,), (1,)), ((), ()))
    lane_v = lax.broadcasted_iota(jnp.int32, (KV_CHUNK, LANES), 1)

    @pl.when(pl.program_id(2) == 0)
    def _():
        for c in range(n_chunks):
            v = v_ref[0, c]
            vo_ref[0, c] = jnp.where(lane_v < MLA_V, v, jnp.ones_like(v))
            vo_ref[1, c] = jnp.where(lane_v < MLA_V, jnp.ones_like(v), v)

    qs = [q_ref[0, 0], q_ref[0, 1]]
    ms = [None, None]
    accs = [None, None]
    for c in range(n_chunks):
        for e in range(2):
            s = lax.dot_general(qs[e], k_ref[0, e, c], nt_dims, preferred_element_type=F32)
            mc = jnp.max(jnp.maximum(s[:, :LANES], s[:, LANES:]), axis=-1, keepdims=True)
            m_new = mc if c == 0 else jnp.maximum(ms[e], mc)
            p = jnp.exp2(s - m_new).astype(BF16)
            pv = jnp.dot(p, vo_ref[e, c], preferred_element_type=F32)
            accs[e] = pv if c == 0 else accs[e] * jnp.exp2(ms[e] - m_new) + pv
            ms[e] = m_new
    lane = lax.broadcasted_iota(jnp.int32, accs[0].shape, 1)
    o0 = accs[0] / accs[0][:, MLA_V:MLA_V + 1]
    o1 = accs[1] / accs[1][:, 0:1]
    o_ref[0] = jnp.where(lane < MLA_V, o0, o1).astype(BF16)


def _mla_call(qa, ka, va, n_lat):
    B, H, S, _ = qa.shape
    nc = S // KV_CHUNK
    tq = 2 * TOK
    k5 = ka.reshape(B, H, nc, KV_CHUNK, LANES)
    v4 = va.reshape(B, nc, KV_CHUNK, 512)
    o = pl.pallas_call(
        functools.partial(_mla_kernel, n_chunks=nc),
        out_shape=jax.ShapeDtypeStruct((B, S, 512), BF16),
        grid=(B, H // 2, n_lat // tq),
        in_specs=[pl.BlockSpec((1, 2, tq, LANES), lambda b, p, t: (b, p, t, 0)),
                  pl.BlockSpec((1, 2, nc, KV_CHUNK, LANES), lambda b, p, t: (b, p, 0, 0, 0)),
                  pl.BlockSpec((1, nc, KV_CHUNK, LANES), lambda b, p, t: (b, 0, 0, p))],
        out_specs=pl.BlockSpec((1, tq, LANES), lambda b, p, t: (b, t, p)),
        scratch_shapes=[pltpu.VMEM((2, nc, KV_CHUNK, LANES), BF16)],
        compiler_params=_cparams(3), name="mla_attn",
    )(qa, k5, v4)
    cblk = n_lat // KV_CHUNK
    return pl.pallas_call(
        functools.partial(_mla_kernel, n_chunks=1),
        out_shape=jax.ShapeDtypeStruct((B, S, 512), BF16),
        grid=(B, H // 2, 1),
        in_specs=[pl.BlockSpec((1, 2, TOK, LANES), lambda b, p, t: (b, p, cblk, 0)),
                  pl.BlockSpec((1, 2, 1, KV_CHUNK, LANES), lambda b, p, t: (b, p, cblk, 0, 0)),
                  pl.BlockSpec((1, 1, KV_CHUNK, LANES), lambda b, p, t: (b, cblk, 0, p)),
                  pl.BlockSpec(memory_space=pl.ANY)],
        out_specs=pl.BlockSpec((1, TOK, LANES), lambda b, p, t: (b, cblk, p)),
        scratch_shapes=[pltpu.VMEM((2, 1, KV_CHUNK, LANES), BF16)],
        input_output_aliases={3: 0},
        compiler_params=_cparams(3), name="mla_attn_ctx",
    )(qa, k5, v4, o)


def _swa_kernel(q_ref, k_ref, v_ref, sink_ref, o_ref, ve_ref, vo_ref, bias_ref, *, n_ctx, n_lat):
    rows4 = 4 * SWA_BLOCK
    half = 2 * SWA_BLOCK
    win = 3 * SWA_BLOCK
    nt_dims = (((1,), (1,)), ((), ()))
    lane = lax.broadcasted_iota(jnp.int32, (SWA_BLOCK, LANES), 1)
    lo = lane < SWA_DIM
    sink = sink_ref[0][:, 0:1]
    kc = k_ref[0, n_lat:n_lat + n_ctx, :]

    v_all = v_ref[0]
    lane_s = lax.broadcasted_iota(jnp.int32, v_all.shape, 1) < SWA_DIM
    ve_ref[...] = jnp.where(lane_s, v_all, jnp.ones_like(v_all))
    vo_ref[...] = jnp.where(lane_s, jnp.ones_like(v_all), v_all)
    rel0 = (lax.broadcasted_iota(jnp.int32, (rows4, win), 1)
            - (lax.broadcasted_iota(jnp.int32, (rows4, win), 0) & (SWA_BLOCK - 1)))
    for kind in range(3):
        bias_ref[kind] = jnp.where(jnp.abs(rel0 - kind * SWA_BLOCK) <= SWA_WINDOW, 0.0, NEG)

    def stack_q(r0):
        qa = q_ref[0, pl.ds(r0, SWA_BLOCK), 0:LANES]
        qb = q_ref[0, pl.ds(r0, SWA_BLOCK), LANES:2 * LANES]
        z = jnp.zeros_like(qa)
        return jnp.concatenate([jnp.where(lo, qa, z), jnp.where(lo, qb, z),
                                jnp.where(lo, z, qa), jnp.where(lo, z, qb)], axis=0)

    def finish(r0, m, acc_e, acc_o):
        tail = jnp.exp2(sink - m)
        o_e = acc_e / (acc_e[:, SWA_DIM:SWA_DIM + 1] + tail[:half])
        o_o = acc_o / (acc_o[:, 0:1] + tail[half:])
        o_ref[0, pl.ds(r0, SWA_BLOCK), 0:LANES] = jnp.where(
            lo, o_e[:SWA_BLOCK], o_o[:SWA_BLOCK]).astype(BF16)
        o_ref[0, pl.ds(r0, SWA_BLOCK), LANES:2 * LANES] = jnp.where(
            lo, o_e[SWA_BLOCK:], o_o[SWA_BLOCK:]).astype(BF16)

    def lane_fold(s):
        out = s[:, :LANES]
        for i in range(1, s.shape[1] // LANES):
            out = jnp.maximum(out, s[:, i * LANES:(i + 1) * LANES])
        return out

    vce = ve_ref[n_lat:n_lat + n_ctx, :]
    vco = vo_ref[n_lat:n_lat + n_ctx, :]

    for n in range(n_ctx // SWA_BLOCK):
        r0 = n_lat + n * SWA_BLOCK
        q4 = stack_q(r0)
        s_c = lax.dot_general(q4, kc, nt_dims, preferred_element_type=F32)
        m = jnp.maximum(jnp.max(lane_fold(s_c), axis=-1, keepdims=True), sink)
        p_c = jnp.exp2(s_c - m).astype(BF16)
        finish(r0, m,
               jnp.dot(p_c[:half], vce, preferred_element_type=F32),
               jnp.dot(p_c[half:], vco, preferred_element_type=F32))

    def lat_block(n, carry):
        r0 = pl.multiple_of(n * SWA_BLOCK, SWA_BLOCK)
        w0 = pl.multiple_of(jnp.clip((n - 1) * SWA_BLOCK, 0, n_lat - win), SWA_BLOCK)
        kind = (r0 - w0) // SWA_BLOCK
        q4 = stack_q(r0)
        s_c = lax.dot_general(q4, kc, nt_dims, preferred_element_type=F32)
        s_w = lax.dot_general(q4, k_ref[0, pl.ds(w0, win), :], nt_dims,
                              preferred_element_type=F32) + bias_ref[kind]
        m = jnp.maximum(jnp.max(jnp.maximum(lane_fold(s_c), lane_fold(s_w)), axis=-1, keepdims=True), sink)
        p_c = jnp.exp2(s_c - m).astype(BF16)
        p_w = jnp.exp2(s_w - m).astype(BF16)
        acc_e = (jnp.dot(p_c[:half], vce, preferred_element_type=F32)
                 + jnp.dot(p_w[:half], ve_ref[pl.ds(w0, win), :], preferred_element_type=F32))
        acc_o = (jnp.dot(p_c[half:], vco, preferred_element_type=F32)
                 + jnp.dot(p_w[half:], vo_ref[pl.ds(w0, win), :], preferred_element_type=F32))
        finish(r0, m, acc_e, acc_o)
        return carry

    lax.fori_loop(0, n_lat // SWA_BLOCK, lat_block, 0, unroll=SWA_UNROLL)


def _swa_call(l, qb, kb, vb, sinkcol, n_ctx):
    B, S, _ = qb.shape
    return pl.pallas_call(
        functools.partial(_swa_kernel, n_ctx=n_ctx, n_lat=S - n_ctx),
        out_shape=jax.ShapeDtypeStruct((B, S, 512), BF16),
        grid=(B, SWA_KV_HEADS),
        in_specs=[pl.BlockSpec((1, S, 256), lambda b, j: (b, 0, j)),
                  pl.BlockSpec((1, S, LANES), lambda b, j: (b, 0, j)),
                  pl.BlockSpec((1, S, LANES), lambda b, j: (b, 0, j)),
                  pl.BlockSpec((1, 4 * SWA_BLOCK, LANES), lambda b, j: (l * SWA_KV_HEADS + j, 0, 0))],
        out_specs=pl.BlockSpec((1, S, 256), lambda b, j: (b, 0, j)),
        scratch_shapes=[pltpu.VMEM((S, LANES), BF16), pltpu.VMEM((S, LANES), BF16),
                        pltpu.VMEM((3, 4 * SWA_BLOCK, 3 * SWA_BLOCK), F32)],
        compiler_params=_cparams(2), name="swa_attn",
    )(qb, kb, vb, sinkcol)


def _gelu(y):
    return 0.5 * y * (1.0 + jnp.tanh(math.sqrt(2.0 / math.pi) * (y + 0.044715 * (y * y * y))))


def _s5_kernel(u_ref, wst_ref, wloc_ref, wcar_ref, at_ref, o_ref,
               ut_ref, utc_ref, yt_ref, ytc_ref, ere_ref, eim_ref, hfr_ref, hfi_ref, hbr_ref, hbi_ref,
               *, nb, n_lat_chunks, n_ctx_chunks):
    T, GP = S5_CHUNK, S5_GROUP
    n_chunks = n_lat_chunks + n_ctx_chunks
    n_lat = n_lat_chunks * T
    cw = nb * n_ctx_chunks
    tn_dims = (((0,), (0,)), ((), ()))
    nt_dims = (((1,), (1,)), ((), ()))
    gpl = LANES // GP

    for b in range(nb):
        for s in range(T):
            xs = u_ref[b, pl.ds(s, n_lat_chunks, stride=T), :]
            ut_ref[b, :, s] = xs.T.reshape(gpl, GP, n_lat_chunks).astype(BF16)
    zpad = jnp.zeros((LANES - cw, LANES), F32)
    for s in range(T):
        xs = jnp.concatenate([u_ref[b, pl.ds(n_lat + s, n_ctx_chunks, stride=T), :] for b in range(nb)]
                             + [zpad], axis=0)
        utc_ref[:, s] = xs.T.reshape(gpl, GP, LANES).astype(BF16)

    lane = lax.broadcasted_iota(jnp.int32, (2 * nb, LANES), 1)
    fwd = lane < S5_STATE
    lane_l = lax.broadcasted_iota(jnp.int32, (n_lat_chunks, LANES), 1) < S5_STATE
    lane_c = lax.broadcasted_iota(jnp.int32, (LANES, LANES), 1) < S5_STATE

    slab = 2 * nb
    lat_rows = lambda gb: pl.ds(gb, n_lat_chunks, stride=slab)
    ctx_rows = lambda gb: pl.ds(n_lat_chunks * slab + gb, n_ctx_chunks, stride=slab)

    def pair(gp, carry):
        for gl in range(2):
            g = gp * 2 + gl
            wst = wst_ref[g]
            for b in range(nb):
                e = lax.dot_general(ut_ref[b, g].reshape(T * GP, n_lat_chunks), wst, tn_dims,
                                    preferred_element_type=F32)
                ere_ref[lat_rows(gl * nb + b), :] = e[:, :LANES]
                eim_ref[lat_rows(gl * nb + b), :] = e[:, LANES:]
            ec = lax.dot_general(utc_ref[g].reshape(T * GP, LANES), wst, tn_dims, preferred_element_type=F32)
            for b in range(nb):
                ere_ref[ctx_rows(gl * nb + b), :] = ec[b * n_ctx_chunks:(b + 1) * n_ctx_chunks, :LANES]
                eim_ref[ctx_rows(gl * nb + b), :] = ec[b * n_ctx_chunks:(b + 1) * n_ctx_chunks, LANES:]

        a_re = jnp.concatenate([jnp.broadcast_to(at_ref[gp * 2 + gl, 0:1, :], (nb, LANES)) for gl in range(2)], 0)
        a_im = jnp.concatenate([jnp.broadcast_to(at_ref[gp * 2 + gl, 1:2, :], (nb, LANES)) for gl in range(2)], 0)

        def step(i, hc):
            h_re, h_im = hc
            cf = jnp.where(i < n_ctx_chunks, n_lat_chunks + i, i - n_ctx_chunks)
            cb = n_chunks - 1 - i
            sf = pl.ds(pl.multiple_of(cf * slab, slab), slab)
            sb = pl.ds(pl.multiple_of(cb * slab, slab), slab)
            hfr_ref[sf, :] = h_re
            hfi_ref[sf, :] = h_im
            hbr_ref[sb, :] = h_re
            hbi_ref[sb, :] = h_im
            e_re = jnp.where(fwd, ere_ref[sf, :], ere_ref[sb, :])
            e_im = jnp.where(fwd, eim_ref[sf, :], eim_ref[sb, :])
            return (a_re * h_re - a_im * h_im + e_re, a_re * h_im + a_im * h_re + e_im)

        zero = jnp.zeros((2 * nb, LANES), F32)
        lax.fori_loop(0, n_chunks, step, (zero, zero), unroll=8)

        for gl in range(2):
            g = gp * 2 + gl
            wloc, wcar = wloc_ref[g], wcar_ref[g]
            for b in range(nb):
                rows = lat_rows(gl * nb + b)
                h_cat = jnp.concatenate([jnp.where(lane_l, hfr_ref[rows, :], hbr_ref[rows, :]),
                                         jnp.where(lane_l, hfi_ref[rows, :], hbi_ref[rows, :])],
                                        axis=-1).astype(BF16)
                yt = (jnp.dot(wloc, ut_ref[b, g].reshape(T * GP, n_lat_chunks), preferred_element_type=F32)
                      + lax.dot_general(wcar, h_cat, nt_dims, preferred_element_type=F32))
                yt_ref[b, g] = _gelu(yt).astype(BF16).reshape(T, GP, n_lat_chunks)
            crow = lambda ref: jnp.concatenate(
                [ref[ctx_rows(gl * nb + b), :] for b in range(nb)]
                + [jnp.zeros((LANES - cw, LANES), F32)], axis=0)
            h_cat = jnp.concatenate([jnp.where(lane_c, crow(hfr_ref), crow(hbr_ref)),
                                     jnp.where(lane_c, crow(hfi_ref), crow(hbi_ref))], axis=-1).astype(BF16)
            ytc = (jnp.dot(wloc, utc_ref[g].reshape(T * GP, LANES), preferred_element_type=F32)
                   + lax.dot_general(wcar, h_cat, nt_dims, preferred_element_type=F32))
            ytc_ref[g] = _gelu(ytc).astype(BF16).reshape(T, GP, LANES)
        return carry

    lax.fori_loop(0, gpl // 2, pair, 0)

    for b in range(nb):
        for t in range(T):
            z = yt_ref[b, :, t].astype(F32).reshape(LANES, n_lat_chunks)
            o_ref[b, pl.ds(t, n_lat_chunks, stride=T), :] = z.T
    for t in range(T):
        z = ytc_ref[:, t].astype(F32).reshape(LANES, LANES).T
        for b in range(nb):
            o_ref[b, pl.ds(n_lat + t, n_ctx_chunks, stride=T), :] = z[b * n_ctx_chunks:(b + 1) * n_ctx_chunks]


def _s5_call(l, u, P, n_lat):
    B, S, W = u.shape
    nb = 4 if B % 4 == 0 else B
    T, GP = S5_CHUNK, S5_GROUP
    nlc, ncc = n_lat // T, (S - n_lat) // T
    gpl = LANES // GP
    nblk = W // LANES
    big = lambda: pl.BlockSpec((nb, S, LANES), lambda j, hb: (hb, 0, j), pipeline_mode=pl.Buffered(1))
    wspec = lambda: pl.BlockSpec((gpl, T * GP, T * GP), lambda j, hb: (l * nblk + j, 0, 0))
    rows = 2 * nb * (nlc + ncc)
    return pl.pallas_call(
        functools.partial(_s5_kernel, nb=nb, n_lat_chunks=nlc, n_ctx_chunks=ncc),
        out_shape=jax.ShapeDtypeStruct((B, S, W), F32),
        grid=(nblk, B // nb),
        in_specs=[big(), wspec(), wspec(), wspec(),
                  pl.BlockSpec((gpl, 2, LANES), lambda j, hb: (l * nblk + j, 0, 0))],
        out_specs=big(),
        scratch_shapes=[pltpu.VMEM((nb, gpl, T, GP, nlc), BF16), pltpu.VMEM((gpl, T, GP, LANES), BF16),
                        pltpu.VMEM((nb, gpl, T, GP, nlc), BF16), pltpu.VMEM((gpl, T, GP, LANES), BF16)]
                       + [pltpu.VMEM((rows, LANES), F32)] * 6,
        compiler_params=_cparams(2), name="s5",
    )(u, P["s5_wst"], P["s5_wloc_t"], P["s5_wcar_t"], P["s5_at"])


def _out_kernel(x_ref, c_ref, mod_ref, oa_ref, ob_ref, gy_ref, gt_ref, wglu_ref, wout_ref,
                xo_ref, co_ref, *, n_lat_tiles):
    t = pl.program_id(1)
    cb = OUT_COLS
    gyb = gy_ref[0].astype(BF16)
    g = gt_ref[0].astype(F32)
    m_a = (oa_ref[0].astype(F32) * g[:, 0:512]).astype(BF16)
    m_b = (ob_ref[0].astype(F32) * g[:, 512:1024]).astype(BF16)
    oc = []
    for j in range(512 // cb):
        za = jnp.dot(gyb, wglu_ref[0, :, j * cb:(j + 1) * cb], preferred_element_type=F32)
        zb = jnp.dot(gyb, wglu_ref[0, :, 512 + j * cb:512 + (j + 1) * cb], preferred_element_type=F32)
        oc.append((za * _sigmoid(zb) * g[:, 1024 + j * cb:1024 + (j + 1) * cb]).astype(BF16))
    m_c = jnp.concatenate(oc, axis=-1)
    gate = mod_ref[0, 0][2:3]
    resid = jnp.where(t == n_lat_tiles, c_ref[0], x_ref[0])
    new = []
    for j in range(resid.shape[1] // cb):
        cols = slice(j * cb, (j + 1) * cb)
        upd = (jnp.dot(m_a, wout_ref[0, 0:512, cols], preferred_element_type=F32)
               + jnp.dot(m_b, wout_ref[0, 512:1024, cols], preferred_element_type=F32)
               + jnp.dot(m_c, wout_ref[0, 1024:1536, cols], preferred_element_type=F32))
        new.append(resid[:, cols] + gate[:, cols] * upd)

    @pl.when(t == n_lat_tiles)
    def _():
        for j, v in enumerate(new):
            co_ref[0, :, j * cb:(j + 1) * cb] = v

    @pl.when(t < n_lat_tiles)
    def _():
        for j, v in enumerate(new):
            xo_ref[0, :, j * cb:(j + 1) * cb] = v


def _out_call(l, x, ctx, mod, oa, ob, gy, gt, P):
    B, L, D = x.shape
    C = ctx.shape[1]
    S = C + L
    nl = L // TOK
    xmap = lambda b, t: (b, jnp.minimum(t, nl - 1), 0)
    cmap = lambda b, t: (b, 0, 0)
    tmap = lambda b, t: (b, t, 0)
    return pl.pallas_call(
        functools.partial(_out_kernel, n_lat_tiles=nl),
        out_shape=[jax.ShapeDtypeStruct(x.shape, F32), jax.ShapeDtypeStruct(ctx.shape, F32)],
        grid=(B, S // TOK),
        in_specs=[pl.BlockSpec((1, TOK, D), xmap),
                  pl.BlockSpec((1, TOK, D), cmap),
                  pl.BlockSpec((1, 1, 3, D), lambda b, t: (l, jnp.where(t == nl, B, b), 0, 0)),
                  pl.BlockSpec((1, TOK, 512), tmap),
                  pl.BlockSpec((1, TOK, 512), tmap),
                  pl.BlockSpec((1, TOK, 512), tmap),
                  pl.BlockSpec((1, TOK, 1536), tmap),
                  pl.BlockSpec((1, 512, 1024), lambda b, t: (l, 0, 0)),
                  pl.BlockSpec((1, 1536, D), lambda b, t: (l, 0, 0))],
        out_specs=[pl.BlockSpec((1, TOK, D), xmap), pl.BlockSpec((1, TOK, D), cmap)],
        compiler_params=_cparams(2), name="out",
    )(x, ctx, mod, oa, ob, gy, gt, P["w_glu"], P["w_out"])


def _rope_tables(n_lat, n_ctx, rot_dim, lead, reps):
    rows = n_lat // GRID_W
    r_idx, c_idx = jnp.meshgrid(jnp.arange(rows), jnp.arange(GRID_W), indexing="ij")
    r_idx, c_idx = r_idx.reshape(-1), c_idx.reshape(-1)
    n_freq = rot_dim // 4
    freqs = ROPE_BASE ** (-jnp.arange(n_freq, dtype=F32) / n_freq)
    ang = jnp.concatenate([r_idx.astype(F32)[:, None] * freqs,
                           c_idx.astype(F32)[:, None] * freqs], axis=-1)
    ang = jnp.concatenate([ang, jnp.zeros((n_ctx, rot_dim // 2), F32)], axis=0)
    cos, sin, zero = jnp.cos(ang), jnp.sin(ang), jnp.zeros_like(ang)
    n = ang.shape[0]
    tail = LANES // reps - lead - rot_dim

    def pack(x1, x2, fill):
        unit = [jnp.full((n, lead), fill, F32), x1, x2, jnp.full((n, tail), fill, F32)]
        return jnp.concatenate(unit * reps, axis=-1)

    return jnp.stack([pack(cos, cos, 1.0), pack(zero, sin, 0.0), pack(-sin, zero, 0.0)])


def _prep_params(norm_g, w_in, w_out, mla_g_cq, mla_g_ckv, mla_w_uq, mla_w_ukv, mla_g_qn, mla_g_kn,
                 swa_g_qn, swa_g_kn, swa_sink, s5_a_re, s5_a_im, s5_log_dt, s5_b_re, s5_b_im,
                 s5_c_re, s5_c_im, s5_d, s5_w_glu):
    depth, D, _ = w_in.shape
    o_cq, o_ckv, o_kr, o_gm, o_sq, o_sk, o_sv, o_gs, o_u, o_g5, o_end = (
        0, 384, 640, 672, 1184, 1696, 1824, 1952, 2464, 2976, 3488)
    z = lambda n: jnp.zeros((depth, D, n), F32)
    sk0, sk1 = w_in[:, :, o_sk:o_sk + 64], w_in[:, :, o_sk + 64:o_sv]
    sv0, sv1 = w_in[:, :, o_sv:o_sv + 64], w_in[:, :, o_sv + 64:o_gs]
    w_in_p = jnp.concatenate([
        w_in[:, :, o_cq:o_kr],
        z(64), w_in[:, :, o_kr:o_gm], z(32),
        w_in[:, :, o_sq:o_sk],
        sk0, sk0, sk1, sk1, sv0, sv0, sv1, sv1,
        w_in[:, :, o_u:o_g5],
        w_in[:, :, o_gm:o_sq], w_in[:, :, o_gs:o_u], w_in[:, :, o_g5:o_end],
    ], axis=-1).astype(BF16)
    assert w_in_p.shape[-1] == C_END

    wq = mla_w_uq.reshape(depth, MLA_Q_RANK, MLA_HEADS, MLA_QK)
    wq = jnp.pad(wq, ((0, 0), (0, 0), (0, 0), (0, LANES - MLA_QK)))
    w_uq_p = wq.reshape(depth, MLA_Q_RANK, MLA_HEADS * LANES).astype(BF16)
    wkv = mla_w_ukv.reshape(depth, MLA_KV_RANK, MLA_HEADS, MLA_NOPE + MLA_V)
    wk = jnp.pad(wkv[..., :MLA_NOPE], ((0, 0), (0, 0), (0, 0), (0, LANES - MLA_NOPE)))
    w_ukv_p = jnp.concatenate([wk.reshape(depth, MLA_KV_RANK, MLA_HEADS * LANES),
                               wkv[..., MLA_NOPE:].reshape(depth, MLA_KV_RANK, MLA_HEADS * MLA_V)],
                              axis=-1).astype(BF16)

    pad_qk = lambda g: jnp.pad(g, ((0, 0), (0, LANES - MLA_QK)))[:, None, :]
    g_q = pad_qk(mla_g_qn * (MLA_QK ** -0.5 * LOG2E))
    g_k = pad_qk(mla_g_kn)
    g_sq = jnp.tile(swa_g_qn * (SWA_DIM ** -0.5 * LOG2E), (1, 2))[:, None, :]
    g_sk = jnp.tile(swa_g_kn, (1, 2))[:, None, :]
    sink = (swa_sink * LOG2E).reshape(depth * SWA_KV_HEADS, SWA_HEADS // SWA_KV_HEADS)
    sink = sink[:, jnp.array([0, 2, 1, 3])]
    sinkcol = jnp.broadcast_to(jnp.repeat(sink, SWA_BLOCK, axis=1)[:, :, None],
                               (depth * SWA_KV_HEADS, 4 * SWA_BLOCK, LANES)).astype(F32)

    T = S5_CHUNK
    A = lax.complex(s5_a_re, s5_a_im)
    dt = jnp.exp(s5_log_dt)[..., None]
    a_bar = jnp.exp(dt * A)
    b_bar = ((a_bar - 1.0) / A)[..., None] * lax.complex(s5_b_re, s5_b_im)
    c_mat = lax.complex(s5_c_re, s5_c_im)
    k_idx = jnp.arange(T + 1, dtype=F32)
    pw = jnp.exp(k_idx[None, None, None, :, None] * (dt * A)[:, :, :, None, :])
    hi = lax.Precision.HIGHEST
    tt = jnp.arange(T)
    GP, NG = S5_GROUP, depth * S5_GROUPS
    pw_f, pw_b = pw[:, 0], pw[:, 1]
    tap_f = jnp.einsum("lgpn,lgkn,lgnq->lgpkq", c_mat[:, 0], pw_f[:, :, T - 1 - tt], b_bar[:, 0],
                       precision=hi).real
    tap_b = jnp.einsum("lgpn,lgkn,lgnq->lgpkq", c_mat[:, 1], pw_b[:, :, tt], b_bar[:, 1],
                       precision=hi).real
    d_diag = s5_d.reshape(depth, S5_GROUPS, GP)[..., None] * jnp.eye(GP, dtype=F32)
    centre = tap_f[:, :, :, T - 1:] + tap_b[:, :, :, :1] + d_diag[:, :, :, None, :]
    krev = jnp.concatenate([tap_f[:, :, :, :T - 1], centre, tap_b[:, :, :, 1:]], axis=3)
    krev = krev.reshape(depth, S5_GROUPS, GP, (2 * T - 1) * GP)
    wloc_t = jnp.stack([krev[..., (T - 1 - t) * GP:(T - 1 - t) * GP + T * GP] for t in range(T)], axis=2)
    wloc_t = wloc_t.reshape(NG, T * GP, T * GP)
    b_t = jnp.swapaxes(b_bar, -1, -2)
    inc_f = pw_f[:, :, T - 1 - tt][:, :, :, None, :] * b_t[:, 0][:, :, None]
    inc_b = pw_b[:, :, tt][:, :, :, None, :] * b_t[:, 1][:, :, None]
    wst = jnp.concatenate([inc_f.real, inc_b.real, inc_f.imag, inc_b.imag], axis=-1)
    wst = wst.reshape(NG, T * GP, 4 * S5_STATE)
    ro_f = c_mat[:, 0][:, :, None] * pw_f[:, :, tt + 1][:, :, :, None, :]
    ro_b = c_mat[:, 1][:, :, None] * pw_b[:, :, T - tt][:, :, :, None, :]
    wcar_t = jnp.concatenate([ro_f.real, ro_b.real, -ro_f.imag, -ro_b.imag], axis=-1)
    wcar_t = wcar_t.reshape(NG, T * GP, 4 * S5_STATE)
    a_t = pw[:, :, :, T]
    at = jnp.stack([jnp.concatenate([a_t[:, 0].real, a_t[:, 1].real], axis=-1),
                    jnp.concatenate([a_t[:, 0].imag, a_t[:, 1].imag], axis=-1)], axis=2)
    at = at.reshape(NG, 2, LANES).astype(F32)

    return dict(norm_g=norm_g[:, None, :], w_in=w_in_p, g_cq=mla_g_cq[:, None, :], w_uq=w_uq_p, g_q=g_q,
                g_ckv=mla_g_ckv[:, None, :], w_ukv=w_ukv_p, g_k=g_k, g_sq=g_sq, g_sk=g_sk,
                sinkcol=sinkcol, s5_wloc_t=wloc_t.astype(BF16), s5_wst=wst.astype(BF16),
                s5_wcar_t=wcar_t.astype(BF16), s5_at=at, w_glu=s5_w_glu.astype(BF16),
                w_out=w_out.astype(BF16))


def kernel(x, c, ctx, c_ctx, norm_g, w_ada, b_ada, w_in, w_out, mla_g_cq, mla_g_ckv, mla_w_uq, mla_w_ukv, mla_g_qn, mla_g_kn, swa_g_qn, swa_g_kn, swa_sink, s5_a_re, s5_a_im, s5_log_dt, s5_b_re, s5_b_im, s5_c_re, s5_c_im, s5_d, s5_w_glu):
    B, L, D = x.shape
    C = ctx.shape[1]
    S = C + L
    depth = w_in.shape[0]
    assert B + PROJ_NB <= MOD_ROWS and C == TOK and L % (2 * TOK) == 0 and L % GRID_W == 0 and L >= 3 * SWA_BLOCK

    P = _prep_params(norm_g, w_in, w_out, mla_g_cq, mla_g_ckv, mla_w_uq, mla_w_ukv, mla_g_qn, mla_g_kn,
                     swa_g_qn, swa_g_kn, swa_sink, s5_a_re, s5_a_im, s5_log_dt, s5_b_re, s5_b_im,
                     s5_c_re, s5_c_im, s5_d, s5_w_glu)
    rope_mla = _rope_tables(L, C, MLA_ROPE, MLA_NOPE, 1)
    rope_swa = _rope_tables(L, C, SWA_DIM, 0, 2)

    cc = jnp.concatenate([c, jnp.tile(c_ctx[None, :], (PROJ_NB, 1)),
                          jnp.zeros((MOD_ROWS - B - PROJ_NB, D), F32)], axis=0)
    mod = _ada_call(cc, w_ada, b_ada).reshape(depth, MOD_ROWS, 3, D)

    for l in range(depth):
        qa, ka, va, qb, kb, vb, u, gt = _proj_call(l, x, ctx, mod, P, rope_mla, rope_swa)
        oa = _mla_call(qa, ka, va, L)
        ob = _swa_call(l, qb, kb, vb, P["sinkcol"], C)
        gy = _s5_call(l, u, P, L)
        x, ctx = _out_call(l, x, ctx, mod, oa, ob, gy, gt, P)
    return x
```

```python
import functools
import math

import jax
import jax.numpy as jnp
from jax import lax
from jax.experimental import pallas as pl
from jax.experimental.pallas import tpu as pltpu

F32 = jnp.float32
BF16 = jnp.bfloat16

GRID_W = 64
EPS = 1e-6
ROPE_BASE = 10000.0
NEG = -1e30
LOG2E = math.log2(math.e)

MLA_HEADS = 8
MLA_NOPE = 64
MLA_ROPE = 32
MLA_V = 64
MLA_QK = MLA_NOPE + MLA_ROPE
MLA_Q_RANK = 384
MLA_KV_RANK = 256

SWA_HEADS = 8
SWA_KV_HEADS = 2
SWA_DIM = 64
SWA_WINDOW = 128

S5_GROUP = 16
S5_GROUPS = 32
S5_STATE = 64
S5_CHUNK = 16

LANES = 128
TOK = 256
KV_CHUNK = 256
SWA_BLOCK = 128
MOD_ROWS = 16
PROJ_NB = 2
SWA_UNROLL = 16
OUT_COLS = 256
PROJ_SKEW = 1

C_CQ = 0
C_CKV = 384
C_KR = 640
C_SQ = 768
C_SK = 1280
C_SV = 1536
C_U = 1792
C_GATE = 2304
C_END = 3840

VMEM_LIMIT = 56 * 1024 * 1024


def _cparams(n_axes):
    return pltpu.CompilerParams(dimension_semantics=("arbitrary",) * n_axes,
                                vmem_limit_bytes=VMEM_LIMIT)


def _ada_kernel(c_ref, w_ref, b_ref, o_ref):
    cc = c_ref[...]
    s = cc * jax.nn.sigmoid(cc)
    o_ref[0] = jnp.dot(s, w_ref[0], preferred_element_type=F32,
                       precision=lax.Precision.HIGHEST) + b_ref[0]


def _ada_call(cc, w_ada, b_ada):
    depth, d, n3 = w_ada.shape
    tn = 768
    return pl.pallas_call(
        _ada_kernel,
        out_shape=jax.ShapeDtypeStruct((depth, MOD_ROWS, n3), F32),
        grid=(depth, n3 // tn),
        in_specs=[pl.BlockSpec((MOD_ROWS, d), lambda l, j: (0, 0)),
                  pl.BlockSpec((1, d, tn), lambda l, j: (l, 0, j)),
                  pl.BlockSpec((1, 1, tn), lambda l, j: (l, 0, j))],
        out_specs=pl.BlockSpec((1, MOD_ROWS, tn), lambda l, j: (l, 0, j)),
        compiler_params=_cparams(2),
        name="ada",
    )(cc, w_ada, b_ada.reshape(depth, 1, n3))


def _sigmoid(v):
    return 0.5 * jnp.tanh(0.5 * v) + 0.5


def _norm_rope(slots, lo_masks, dim, gain, tab_ref, half, dup=False):
    sq = [s * s for s in slots]
    sums = []
    for s2, lo in zip(sq, lo_masks):
        if lo is None:
            sums.append((jnp.sum(s2, axis=-1, keepdims=True),))
        elif dup:
            sums.append((jnp.sum(jnp.where(lo, s2, 0.0), axis=-1, keepdims=True),))
        else:
            sums.append((jnp.sum(jnp.where(lo, s2, 0.0), axis=-1, keepdims=True),
                         jnp.sum(jnp.where(lo, 0.0, s2), axis=-1, keepdims=True)))
    ys = []
    for s, ss, lo in zip(slots, sums, lo_masks):
        rs = [lax.rsqrt(v * (1.0 / dim) + EPS) for v in ss]
        r = rs[0] if len(rs) == 1 else jnp.where(lo, rs[0], rs[1])
        ys.append(s * r * gain)
    up = [pltpu.roll(y, half, 1) for y in ys]
    dn = [pltpu.roll(y, LANES - half, 1) for y in ys]
    return [(y * tab_ref[0] + u * tab_ref[1] + d * tab_ref[2]).astype(BF16) for y, u, d in zip(ys, up, dn)]


N_PROJ_DATA = 3
N_PROJ_WEIGHTS = 12


def _proj_kernel(*refs, n_lat_tiles):
    data = refs[:N_PROJ_DATA]
    shared = refs[N_PROJ_DATA:N_PROJ_DATA + N_PROJ_WEIGHTS]
    outs = refs[N_PROJ_DATA + N_PROJ_WEIGHTS:]
    streams = []
    for i in range(data[0].shape[0]):
        one = pl.ds(i, 1)
        streams.append(_proj_tile(data[0].at[one], data[1].at[one], data[2].at[:, one], *shared,
                                  *[o.at[one] for o in outs], n_lat_tiles=n_lat_tiles))
    pending = list(enumerate(streams))
    step = 0
    while pending:
        for item in list(pending):
            if step >= item[0] * PROJ_SKEW and next(item[1], "done") == "done":
                pending.remove(item)
        step += 1


def _proj_tile(x_ref, c_ref, mod_ref, ng_ref, win_ref, gcq_ref, wuq_ref, gq_ref,
               gckv_ref, wukv_ref, gk_ref, gsq_ref, gsk_ref, rm_ref, rs_ref,
               qa_ref, ka_ref, va_ref, qb_ref, kb_ref, vb_ref, u_ref, gt_ref, *, n_lat_tiles):
    t = pl.program_id(1)
    x = jnp.where(t == n_lat_tiles, c_ref[0], x_ref[0])
    mod = mod_ref[0, 0]
    y = x * lax.rsqrt(jnp.mean(x * x, axis=-1, keepdims=True) + EPS) * ng_ref[0]
    xn = (y * (1.0 + mod[1:2]) + mod[0:1]).astype(BF16)

    def seg(a, b):
        return jnp.dot(xn, win_ref[0, :, a:b], preferred_element_type=F32)

    lane = lax.broadcasted_iota(jnp.int32, (TOK, LANES), 1)
    lo = lane < 64

    yield
    cq = seg(C_CQ, C_CKV)
    ckv = seg(C_CKV, C_KR)
    kr = seg(C_KR, C_SQ)
    yield
    cqn = (cq * lax.rsqrt(jnp.mean(cq * cq, axis=-1, keepdims=True) + EPS) * gcq_ref[0]).astype(BF16)
    qf = jnp.dot(cqn, wuq_ref[0], preferred_element_type=F32)
    ckvn = (ckv * lax.rsqrt(jnp.mean(ckv * ckv, axis=-1, keepdims=True) + EPS) * gckv_ref[0]).astype(BF16)
    kvf = jnp.dot(ckvn, wukv_ref[0], preferred_element_type=F32)
    sq = seg(C_SQ, C_SK)
    yield
    slots = [qf[:, h * LANES:(h + 1) * LANES] for h in range(MLA_HEADS)]
    for h, o in enumerate(_norm_rope(slots, [None] * MLA_HEADS, MLA_QK, gq_ref[0], rm_ref, MLA_ROPE // 2)):
        qa_ref[0, h] = o

    sk = seg(C_SK, C_SV)
    vb_ref[0] = seg(C_SV, C_U).astype(BF16)
    yield
    slots = [kvf[:, h * LANES:(h + 1) * LANES] + kr for h in range(MLA_HEADS)]
    for h, o in enumerate(_norm_rope(slots, [None] * MLA_HEADS, MLA_QK, gk_ref[0], rm_ref, MLA_ROPE // 2)):
        ka_ref[0, h] = o
    va_ref[0] = kvf[:, MLA_HEADS * LANES:].astype(BF16)
    u_ref[0] = seg(C_U, C_GATE)
    yield
    slots = [sq[:, p * LANES:(p + 1) * LANES] for p in range(SWA_HEADS // 2)]
    for p, o in enumerate(_norm_rope(slots, [lo] * len(slots), SWA_DIM, gsq_ref[0], rs_ref, SWA_DIM // 2)):
        qb_ref[0, :, p * LANES:(p + 1) * LANES] = o
    slots = [sk[:, j * LANES:(j + 1) * LANES] for j in range(SWA_KV_HEADS)]
    for j, o in enumerate(_norm_rope(slots, [lo] * len(slots), SWA_DIM, gsk_ref[0], rs_ref, SWA_DIM // 2, dup=True)):
        kb_ref[0, :, j * LANES:(j + 1) * LANES] = o
    g = seg(C_GATE, C_END)
    yield
    gt_ref[0] = (g * _sigmoid(g)).astype(BF16)


def _proj_call(l, x, ctx, mod, P, rope_mla, rope_swa):
    B, L, D = x.shape
    C = ctx.shape[1]
    S = C + L
    nt = S // TOK

    def wspec(arr):
        shp = arr.shape
        return pl.BlockSpec((1,) + shp[1:], lambda b, t: (l,) + (0,) * (len(shp) - 1))

    weights = [P["norm_g"], P["w_in"], P["g_cq"], P["w_uq"], P["g_q"],
               P["g_ckv"], P["w_ukv"], P["g_k"], P["g_sq"], P["g_sk"]]
    nl = L // TOK
    nb = PROJ_NB
    assert len(weights) + 2 == N_PROJ_WEIGHTS and B % nb == 0
    in_specs = ([pl.BlockSpec((nb, TOK, D), lambda b, t: (b, jnp.minimum(t, nl - 1), 0)),
                 pl.BlockSpec((nb, TOK, D), lambda b, t: (b, 0, 0)),
                 pl.BlockSpec((1, nb, 3, D), lambda b, t: (l, jnp.where(t == nl, B // nb, b), 0, 0))]
                + [wspec(w) for w in weights]
                + [pl.BlockSpec((3, TOK, LANES), lambda b, t: (0, t, 0)),
                   pl.BlockSpec((3, TOK, LANES), lambda b, t: (0, t, 0))])
    out_shape = [jax.ShapeDtypeStruct((B, MLA_HEADS, S, LANES), BF16),
                 jax.ShapeDtypeStruct((B, MLA_HEADS, S, LANES), BF16),
                 jax.ShapeDtypeStruct((B, S, 512), BF16),
                 jax.ShapeDtypeStruct((B, S, 512), BF16),
                 jax.ShapeDtypeStruct((B, S, 256), BF16),
                 jax.ShapeDtypeStruct((B, S, 256), BF16),
                 jax.ShapeDtypeStruct((B, S, 512), F32),
                 jax.ShapeDtypeStruct((B, S, 1536), BF16)]
    out_specs = [pl.BlockSpec((nb, MLA_HEADS, TOK, LANES), lambda b, t: (b, 0, t, 0)),
                 pl.BlockSpec((nb, MLA_HEADS, TOK, LANES), lambda b, t: (b, 0, t, 0))]
    out_specs += [pl.BlockSpec((nb, TOK, s.shape[2]), lambda b, t: (b, t, 0)) for s in out_shape[2:]]
    return pl.pallas_call(
        functools.partial(_proj_kernel, n_lat_tiles=nl), out_shape=out_shape, grid=(B // nb, nt),
        in_specs=in_specs, out_specs=out_specs,
        compiler_params=_cparams(2), name="proj",
    )(x, ctx, mod, *weights, rope_mla, rope_swa)


MLA_VT_ROWS = 80


def _mla_kernel(q_ref, k_ref, v_ref, *rest, n_chunks):
    o_ref, vt_ref = rest[-2:]
    ones = jnp.ones((MLA_VT_ROWS - MLA_V, KV_CHUNK), BF16)

    @pl.when(pl.program_id(2) == 0)
    def _():
        for c in range(n_chunks):
            vt = v_ref[0, c].astype(F32).T.astype(BF16)
            vt_ref[0, c] = jnp.concatenate([vt[:MLA_V], ones], axis=0)
            vt_ref[1, c] = jnp.concatenate([vt[MLA_V:], ones], axis=0)

    qts = [q_ref[0, e].astype(F32).T.astype(BF16) for e in range(2)]
    score = lambda c, e: jnp.dot(k_ref[0, e, c], qts[e], preferred_element_type=F32)
    ms = [None, None]
    accs = [None, None]
    sts = [score(0, e) for e in range(2)]
    for c in range(n_chunks):
        nxt = [score(c + 1, e) for e in range(2)] if c + 1 < n_chunks else None
        for e in range(2):
            mc = jnp.max(sts[e], axis=0, keepdims=True)
            m_new = mc if c == 0 else jnp.maximum(ms[e], mc)
            pt = jnp.exp2(sts[e] - m_new).astype(BF16)
            pv = jnp.dot(vt_ref[e, c], pt, preferred_element_type=F32)
            accs[e] = pv if c == 0 else accs[e] * jnp.exp2(ms[e] - m_new) + pv
            ms[e] = m_new
        sts = nxt
    ot = jnp.concatenate([accs[e][:MLA_V] / accs[e][MLA_V:MLA_V + 1] for e in range(2)], axis=0)
    o_ref[0] = ot.T.astype(BF16)


def _mla_call(qa, ka, va, n_lat):
    B, H, S, _ = qa.shape
    nc = S // KV_CHUNK
    tq = 2 * TOK
    k5 = ka.reshape(B, H, nc, KV_CHUNK, LANES)
    v4 = va.reshape(B, nc, KV_CHUNK, 512)
    o = pl.pallas_call(
        functools.partial(_mla_kernel, n_chunks=nc),
        out_shape=jax.ShapeDtypeStruct((B, S, 512), BF16),
        grid=(B, H // 2, n_lat // tq),
        in_specs=[pl.BlockSpec((1, 2, tq, LANES), lambda b, p, t: (b, p, t, 0)),
                  pl.BlockSpec((1, 2, nc, KV_CHUNK, LANES), lambda b, p, t: (b, p, 0, 0, 0)),
                  pl.BlockSpec((1, nc, KV_CHUNK, LANES), lambda b, p, t: (b, 0, 0, p))],
        out_specs=pl.BlockSpec((1, tq, LANES), lambda b, p, t: (b, t, p)),
        scratch_shapes=[pltpu.VMEM((2, nc, MLA_VT_ROWS, KV_CHUNK), BF16)],
        compiler_params=_cparams(3), name="mla_attn",
    )(qa, k5, v4)
    cblk = n_lat // KV_CHUNK
    return pl.pallas_call(
        functools.partial(_mla_kernel, n_chunks=1),
        out_shape=jax.ShapeDtypeStruct((B, S, 512), BF16),
        grid=(B, H // 2, 1),
        in_specs=[pl.BlockSpec((1, 2, TOK, LANES), lambda b, p, t: (b, p, cblk, 0)),
                  pl.BlockSpec((1, 2, 1, KV_CHUNK, LANES), lambda b, p, t: (b, p, cblk, 0, 0)),
                  pl.BlockSpec((1, 1, KV_CHUNK, LANES), lambda b, p, t: (b, cblk, 0, p)),
                  pl.BlockSpec(memory_space=pl.ANY)],
        out_specs=pl.BlockSpec((1, TOK, LANES), lambda b, p, t: (b, cblk, p)),
        scratch_shapes=[pltpu.VMEM((2, 1, MLA_VT_ROWS, KV_CHUNK), BF16)],
        input_output_aliases={3: 0},
        compiler_params=_cparams(3), name="mla_attn_ctx",
    )(qa, k5, v4, o)


SWA_VT_ROWS = 80


def _swa_kernel(q_ref, k_ref, v_ref, sink_ref, o_ref, vt_ref, bias_ref, *, n_ctx, n_lat):
    blk = SWA_BLOCK
    heads = SWA_HEADS // SWA_KV_HEADS
    win = 3 * blk
    n_blocks = n_lat // blk
    ctx_blk = n_blocks
    group = min(SWA_UNROLL, n_blocks)
    sink = sink_ref[0, 0:1, :]

    ones = jnp.ones((SWA_VT_ROWS - SWA_DIM, blk), BF16)
    for i in range((n_lat + n_ctx) // blk):
        vt = v_ref[0, i * blk:(i + 1) * blk, :].astype(F32).T
        vt_ref[i] = jnp.concatenate([vt[:SWA_DIM].astype(BF16), ones], axis=0)
    rel0 = (lax.broadcasted_iota(jnp.int32, (win, heads * blk), 0)
            - (lax.broadcasted_iota(jnp.int32, (win, heads * blk), 1) & (blk - 1)))
    for kind in range(3):
        bias_ref[kind] = jnp.where(jnp.abs(rel0 - kind * blk) <= SWA_WINDOW, 0.0, NEG)

    kc = k_ref[0, n_lat:n_lat + n_ctx, :]
    vtc = jnp.concatenate([vt_ref[ctx_blk + i] for i in range(n_ctx // blk)], axis=1)
    zrows = jnp.zeros((LANES - SWA_DIM, heads * blk), BF16)

    def qmat(r0):
        qt = q_ref[0, pl.ds(r0, blk), :].astype(F32).T
        w = jnp.concatenate([qt[h * SWA_DIM:(h + 1) * SWA_DIM] for h in range(heads)], axis=1)
        return jnp.concatenate([w.astype(BF16), zrows], axis=0)

    def finish(r0, m, acc):
        ot = acc[:SWA_DIM] / (acc[SWA_DIM:SWA_DIM + 1] + jnp.exp2(sink - m))
        o4 = jnp.concatenate([ot[:, h * blk:(h + 1) * blk] for h in range(heads)], axis=0)
        o_ref[0, pl.ds(r0, blk), :] = o4.T.astype(BF16)

    for n in range(n_ctx // blk):
        r0 = n_lat + n * blk
        s_c = jnp.dot(kc, qmat(r0), preferred_element_type=F32)
        m = jnp.maximum(jnp.max(s_c, axis=0, keepdims=True), sink)
        finish(r0, m, jnp.dot(vtc, jnp.exp2(s_c - m).astype(BF16), preferred_element_type=F32))

    def window_block(n):
        return jnp.clip(n - 1, 0, n_blocks - 3)

    def scores(n):
        wb = window_block(n)
        w = qmat(pl.multiple_of(n * blk, blk))
        kw = k_ref[0, pl.ds(pl.multiple_of(wb * blk, blk), win), :]
        return (jnp.dot(kc, w, preferred_element_type=F32),
                jnp.dot(kw, w, preferred_element_type=F32) + bias_ref[n - wb])

    def blocks(gi, carry):
        n0 = gi * group
        cur = scores(n0)
        for i in range(group):
            n = n0 + i
            nxt = scores(n + 1) if i + 1 < group else None
            s_c, s_w = cur
            m = jnp.maximum(jnp.maximum(jnp.max(s_c, axis=0, keepdims=True),
                                        jnp.max(s_w, axis=0, keepdims=True)), sink)
            wb = window_block(n)
            vtw = jnp.concatenate([vt_ref[wb + j] for j in range(3)], axis=1)
            acc = (jnp.dot(vtc, jnp.exp2(s_c - m).astype(BF16), preferred_element_type=F32)
                   + jnp.dot(vtw, jnp.exp2(s_w - m).astype(BF16), preferred_element_type=F32))
            finish(pl.multiple_of(n * blk, blk), m, acc)
            cur = nxt
        return carry

    assert n_blocks % group == 0
    lax.fori_loop(0, n_blocks // group, blocks, 0)


def _swa_call(l, qb, kb, vb, sinkrow, n_ctx):
    B, S, _ = qb.shape
    nq = (SWA_HEADS // SWA_KV_HEADS) * SWA_BLOCK
    return pl.pallas_call(
        functools.partial(_swa_kernel, n_ctx=n_ctx, n_lat=S - n_ctx),
        out_shape=jax.ShapeDtypeStruct((B, S, 512), BF16),
        grid=(B, SWA_KV_HEADS),
        in_specs=[pl.BlockSpec((1, S, 256), lambda b, j: (b, 0, j)),
                  pl.BlockSpec((1, S, LANES), lambda b, j: (b, 0, j)),
                  pl.BlockSpec((1, S, LANES), lambda b, j: (b, 0, j)),
                  pl.BlockSpec((1, 8, nq), lambda b, j: (l * SWA_KV_HEADS + j, 0, 0))],
        out_specs=pl.BlockSpec((1, S, 256), lambda b, j: (b, 0, j)),
        scratch_shapes=[pltpu.VMEM((S // SWA_BLOCK, SWA_VT_ROWS, SWA_BLOCK), BF16),
                        pltpu.VMEM((3, 3 * SWA_BLOCK, nq), F32)],
        compiler_params=_cparams(2), name="swa_attn",
    )(qb, kb, vb, sinkrow)


def _gelu(y):
    return 0.5 * y * (1.0 + jnp.tanh(math.sqrt(2.0 / math.pi) * (y + 0.044715 * (y * y * y))))


def _s5_kernel(u_ref, wst_ref, wloc_ref, wcar_ref, at_ref, o_ref,
               ut_ref, utc_ref, yt_ref, ytc_ref, ere_ref, eim_ref, hfr_ref, hfi_ref, hbr_ref, hbi_ref,
               *, nb, n_lat_chunks, n_ctx_chunks):
    T, GP = S5_CHUNK, S5_GROUP
    n_chunks = n_lat_chunks + n_ctx_chunks
    n_lat = n_lat_chunks * T
    cw = nb * n_ctx_chunks
    tn_dims = (((0,), (0,)), ((), ()))
    nt_dims = (((1,), (1,)), ((), ()))
    gpl = LANES // GP

    for b in range(nb):
        for s in range(T):
            xs = u_ref[b, pl.ds(s, n_lat_chunks, stride=T), :]
            ut_ref[b, :, s] = xs.T.reshape(gpl, GP, n_lat_chunks).astype(BF16)
    zpad = jnp.zeros((LANES - cw, LANES), F32)
    for s in range(T):
        xs = jnp.concatenate([u_ref[b, pl.ds(n_lat + s, n_ctx_chunks, stride=T), :] for b in range(nb)]
                             + [zpad], axis=0)
        utc_ref[:, s] = xs.T.reshape(gpl, GP, LANES).astype(BF16)

    lane = lax.broadcasted_iota(jnp.int32, (2 * nb, LANES), 1)
    fwd = lane < S5_STATE
    lane_l = lax.broadcasted_iota(jnp.int32, (n_lat_chunks, LANES), 1) < S5_STATE
    lane_c = lax.broadcasted_iota(jnp.int32, (LANES, LANES), 1) < S5_STATE

    slab = 2 * nb
    lat_rows = lambda gb: pl.ds(gb, n_lat_chunks, stride=slab)
    ctx_rows = lambda gb: pl.ds(n_lat_chunks * slab + gb, n_ctx_chunks, stride=slab)

    def pair(gp, carry):
        for gl in range(2):
            g = gp * 2 + gl
            wst = wst_ref[g]
            for b in range(nb):
                e = lax.dot_general(ut_ref[b, g].reshape(T * GP, n_lat_chunks), wst, tn_dims,
                                    preferred_element_type=F32)
                ere_ref[lat_rows(gl * nb + b), :] = e[:, :LANES]
                eim_ref[lat_rows(gl * nb + b), :] = e[:, LANES:]
            ec = lax.dot_general(utc_ref[g].reshape(T * GP, LANES), wst, tn_dims, preferred_element_type=F32)
            for b in range(nb):
                ere_ref[ctx_rows(gl * nb + b), :] = ec[b * n_ctx_chunks:(b + 1) * n_ctx_chunks, :LANES]
                eim_ref[ctx_rows(gl * nb + b), :] = ec[b * n_ctx_chunks:(b + 1) * n_ctx_chunks, LANES:]

        a_re = jnp.concatenate([jnp.broadcast_to(at_ref[gp * 2 + gl, 0:1, :], (nb, LANES)) for gl in range(2)], 0)
        a_im = jnp.concatenate([jnp.broadcast_to(at_ref[gp * 2 + gl, 1:2, :], (nb, LANES)) for gl in range(2)], 0)

        def step(i, hc):
            h_re, h_im = hc
            cf = jnp.where(i < n_ctx_chunks, n_lat_chunks + i, i - n_ctx_chunks)
            cb = n_chunks - 1 - i
            sf = pl.ds(pl.multiple_of(cf * slab, slab), slab)
            sb = pl.ds(pl.multiple_of(cb * slab, slab), slab)
            hfr_ref[sf, :] = h_re
            hfi_ref[sf, :] = h_im
            hbr_ref[sb, :] = h_re
            hbi_ref[sb, :] = h_im
            e_re = jnp.where(fwd, ere_ref[sf, :], ere_ref[sb, :])
            e_im = jnp.where(fwd, eim_ref[sf, :], eim_ref[sb, :])
            return (a_re * h_re - a_im * h_im + e_re, a_re * h_im + a_im * h_re + e_im)

        zero = jnp.zeros((2 * nb, LANES), F32)
        lax.fori_loop(0, n_chunks, step, (zero, zero), unroll=8)

        for gl in range(2):
            g = gp * 2 + gl
            wloc, wcar = wloc_ref[g], wcar_ref[g]
            for b in range(nb):
                rows = lat_rows(gl * nb + b)
                h_cat = jnp.concatenate([jnp.where(lane_l, hfr_ref[rows, :], hbr_ref[rows, :]),
                                         jnp.where(lane_l, hfi_ref[rows, :], hbi_ref[rows, :])],
                                        axis=-1).astype(BF16)
                yt = (jnp.dot(wloc, ut_ref[b, g].reshape(T * GP, n_lat_chunks), preferred_element_type=F32)
                      + lax.dot_general(wcar, h_cat, nt_dims, preferred_element_type=F32))
                yt_ref[b, g] = _gelu(yt).astype(BF16).reshape(T, GP, n_lat_chunks)
            crow = lambda ref: jnp.concatenate(
                [ref[ctx_rows(gl * nb + b), :] for b in range(nb)]
                + [jnp.zeros((LANES - cw, LANES), F32)], axis=0)
            h_cat = jnp.concatenate([jnp.where(lane_c, crow(hfr_ref), crow(hbr_ref)),
                                     jnp.where(lane_c, crow(hfi_ref), crow(hbi_ref))], axis=-1).astype(BF16)
            ytc = (jnp.dot(wloc, utc_ref[g].reshape(T * GP, LANES), preferred_element_type=F32)
                   + lax.dot_general(wcar, h_cat, nt_dims, preferred_element_type=F32))
            ytc_ref[g] = _gelu(ytc).astype(BF16).reshape(T, GP, LANES)
        return carry

    lax.fori_loop(0, gpl // 2, pair, 0)

    for b in range(nb):
        for t in range(T):
            z = yt_ref[b, :, t].astype(F32).reshape(LANES, n_lat_chunks)
            o_ref[b, pl.ds(t, n_lat_chunks, stride=T), :] = z.T
    for t in range(T):
        z = ytc_ref[:, t].astype(F32).reshape(LANES, LANES).T
        for b in range(nb):
            o_ref[b, pl.ds(n_lat + t, n_ctx_chunks, stride=T), :] = z[b * n_ctx_chunks:(b + 1) * n_ctx_chunks]


def _s5_call(l, u, P, n_lat):
    B, S, W = u.shape
    nb = 4 if B % 4 == 0 else B
    T, GP = S5_CHUNK, S5_GROUP
    nlc, ncc = n_lat // T, (S - n_lat) // T
    gpl = LANES // GP
    nblk = W // LANES
    big = lambda: pl.BlockSpec((nb, S, LANES), lambda j, hb: (hb, 0, j), pipeline_mode=pl.Buffered(1))
    wspec = lambda: pl.BlockSpec((gpl, T * GP, T * GP), lambda j, hb: (l * nblk + j, 0, 0))
    rows = 2 * nb * (nlc + ncc)
    return pl.pallas_call(
        functools.partial(_s5_kernel, nb=nb, n_lat_chunks=nlc, n_ctx_chunks=ncc),
        out_shape=jax.ShapeDtypeStruct((B, S, W), F32),
        grid=(nblk, B // nb),
        in_specs=[big(), wspec(), wspec(), wspec(),
                  pl.BlockSpec((gpl, 2, LANES), lambda j, hb: (l * nblk + j, 0, 0))],
        out_specs=big(),
        scratch_shapes=[pltpu.VMEM((nb, gpl, T, GP, nlc), BF16), pltpu.VMEM((gpl, T, GP, LANES), BF16),
                        pltpu.VMEM((nb, gpl, T, GP, nlc), BF16), pltpu.VMEM((gpl, T, GP, LANES), BF16)]
                       + [pltpu.VMEM((rows, LANES), F32)] * 6,
        compiler_params=_cparams(2), name="s5",
    )(u, P["s5_wst"], P["s5_wloc_t"], P["s5_wcar_t"], P["s5_at"])


def _out_kernel(x_ref, c_ref, mod_ref, oa_ref, ob_ref, gy_ref, gt_ref, wglu_ref, wout_ref,
                xo_ref, co_ref, *, n_lat_tiles):
    t = pl.program_id(1)
    cb = OUT_COLS
    gyb = gy_ref[0].astype(BF16)
    g = gt_ref[0].astype(F32)
    m_a = (oa_ref[0].astype(F32) * g[:, 0:512]).astype(BF16)
    m_b = (ob_ref[0].astype(F32) * g[:, 512:1024]).astype(BF16)
    oc = []
    for j in range(512 // cb):
        za = jnp.dot(gyb, wglu_ref[0, :, j * cb:(j + 1) * cb], preferred_element_type=F32)
        zb = jnp.dot(gyb, wglu_ref[0, :, 512 + j * cb:512 + (j + 1) * cb], preferred_element_type=F32)
        oc.append((za * _sigmoid(zb) * g[:, 1024 + j * cb:1024 + (j + 1) * cb]).astype(BF16))
    m_c = jnp.concatenate(oc, axis=-1)
    gate = mod_ref[0, 0][2:3]
    resid = jnp.where(t == n_lat_tiles, c_ref[0], x_ref[0])
    new = []
    for j in range(resid.shape[1] // cb):
        cols = slice(j * cb, (j + 1) * cb)
        upd = (jnp.dot(m_a, wout_ref[0, 0:512, cols], preferred_element_type=F32)
               + jnp.dot(m_b, wout_ref[0, 512:1024, cols], preferred_element_type=F32)
               + jnp.dot(m_c, wout_ref[0, 1024:1536, cols], preferred_element_type=F32))
        new.append(resid[:, cols] + gate[:, cols] * upd)

    @pl.when(t == n_lat_tiles)
    def _():
        for j, v in enumerate(new):
            co_ref[0, :, j * cb:(j + 1) * cb] = v

    @pl.when(t < n_lat_tiles)
    def _():
        for j, v in enumerate(new):
            xo_ref[0, :, j * cb:(j + 1) * cb] = v


def _out_call(l, x, ctx, mod, oa, ob, gy, gt, P):
    B, L, D = x.shape
    C = ctx.shape[1]
    S = C + L
    nl = L // TOK
    xmap = lambda b, t: (b, jnp.minimum(t, nl - 1), 0)
    cmap = lambda b, t: (b, 0, 0)
    tmap = lambda b, t: (b, t, 0)
    return pl.pallas_call(
        functools.partial(_out_kernel, n_lat_tiles=nl),
        out_shape=[jax.ShapeDtypeStruct(x.shape, F32), jax.ShapeDtypeStruct(ctx.shape, F32)],
        grid=(B, S // TOK),
        in_specs=[pl.BlockSpec((1, TOK, D), xmap),
                  pl.BlockSpec((1, TOK, D), cmap),
                  pl.BlockSpec((1, 1, 3, D), lambda b, t: (l, jnp.where(t == nl, B, b), 0, 0)),
                  pl.BlockSpec((1, TOK, 512), tmap),
                  pl.BlockSpec((1, TOK, 512), tmap),
                  pl.BlockSpec((1, TOK, 512), tmap),
                  pl.BlockSpec((1, TOK, 1536), tmap),
                  pl.BlockSpec((1, 512, 1024), lambda b, t: (l, 0, 0)),
                  pl.BlockSpec((1, 1536, D), lambda b, t: (l, 0, 0))],
        out_specs=[pl.BlockSpec((1, TOK, D), xmap), pl.BlockSpec((1, TOK, D), cmap)],
        compiler_params=_cparams(2), name="out",
    )(x, ctx, mod, oa, ob, gy, gt, P["w_glu"], P["w_out"])


def _rope_tables(n_lat, n_ctx, rot_dim, lead, reps):
    rows = n_lat // GRID_W
    r_idx, c_idx = jnp.meshgrid(jnp.arange(rows), jnp.arange(GRID_W), indexing="ij")
    r_idx, c_idx = r_idx.reshape(-1), c_idx.reshape(-1)
    n_freq = rot_dim // 4
    freqs = ROPE_BASE ** (-jnp.arange(n_freq, dtype=F32) / n_freq)
    ang = jnp.concatenate([r_idx.astype(F32)[:, None] * freqs,
                           c_idx.astype(F32)[:, None] * freqs], axis=-1)
    ang = jnp.concatenate([ang, jnp.zeros((n_ctx, rot_dim // 2), F32)], axis=0)
    cos, sin, zero = jnp.cos(ang), jnp.sin(ang), jnp.zeros_like(ang)
    n = ang.shape[0]
    tail = LANES // reps - lead - rot_dim

    def pack(x1, x2, fill):
        unit = [jnp.full((n, lead), fill, F32), x1, x2, jnp.full((n, tail), fill, F32)]
        return jnp.concatenate(unit * reps, axis=-1)

    return jnp.stack([pack(cos, cos, 1.0), pack(zero, sin, 0.0), pack(-sin, zero, 0.0)])


def _prep_params(norm_g, w_in, w_out, mla_g_cq, mla_g_ckv, mla_w_uq, mla_w_ukv, mla_g_qn, mla_g_kn,
                 swa_g_qn, swa_g_kn, swa_sink, s5_a_re, s5_a_im, s5_log_dt, s5_b_re, s5_b_im,
                 s5_c_re, s5_c_im, s5_d, s5_w_glu):
    depth, D, _ = w_in.shape
    o_cq, o_ckv, o_kr, o_gm, o_sq, o_sk, o_sv, o_gs, o_u, o_g5, o_end = (
        0, 384, 640, 672, 1184, 1696, 1824, 1952, 2464, 2976, 3488)
    z = lambda n: jnp.zeros((depth, D, n), F32)
    sk0, sk1 = w_in[:, :, o_sk:o_sk + 64], w_in[:, :, o_sk + 64:o_sv]
    sv0, sv1 = w_in[:, :, o_sv:o_sv + 64], w_in[:, :, o_sv + 64:o_gs]
    w_in_p = jnp.concatenate([
        w_in[:, :, o_cq:o_kr],
        z(64), w_in[:, :, o_kr:o_gm], z(32),
        w_in[:, :, o_sq:o_sk],
        sk0, sk0, sk1, sk1, sv0, sv0, sv1, sv1,
        w_in[:, :, o_u:o_g5],
        w_in[:, :, o_gm:o_sq], w_in[:, :, o_gs:o_u], w_in[:, :, o_g5:o_end],
    ], axis=-1).astype(BF16)
    assert w_in_p.shape[-1] == C_END

    wq = mla_w_uq.reshape(depth, MLA_Q_RANK, MLA_HEADS, MLA_QK)
    wq = jnp.pad(wq, ((0, 0), (0, 0), (0, 0), (0, LANES - MLA_QK)))
    w_uq_p = wq.reshape(depth, MLA_Q_RANK, MLA_HEADS * LANES).astype(BF16)
    wkv = mla_w_ukv.reshape(depth, MLA_KV_RANK, MLA_HEADS, MLA_NOPE + MLA_V)
    wk = jnp.pad(wkv[..., :MLA_NOPE], ((0, 0), (0, 0), (0, 0), (0, LANES - MLA_NOPE)))
    w_ukv_p = jnp.concatenate([wk.reshape(depth, MLA_KV_RANK, MLA_HEADS * LANES),
                               wkv[..., MLA_NOPE:].reshape(depth, MLA_KV_RANK, MLA_HEADS * MLA_V)],
                              axis=-1).astype(BF16)

    pad_qk = lambda g: jnp.pad(g, ((0, 0), (0, LANES - MLA_QK)))[:, None, :]
    g_q = pad_qk(mla_g_qn * (MLA_QK ** -0.5 * LOG2E))
    g_k = pad_qk(mla_g_kn)
    g_sq = jnp.tile(swa_g_qn * (SWA_DIM ** -0.5 * LOG2E), (1, 2))[:, None, :]
    g_sk = jnp.tile(swa_g_kn, (1, 2))[:, None, :]
    sink = (swa_sink * LOG2E).reshape(depth * SWA_KV_HEADS, 1, SWA_HEADS // SWA_KV_HEADS, 1)
    sinkrow = jnp.broadcast_to(sink, (depth * SWA_KV_HEADS, 8, SWA_HEADS // SWA_KV_HEADS, SWA_BLOCK))
    sinkrow = sinkrow.reshape(depth * SWA_KV_HEADS, 8, -1).astype(F32)

    T = S5_CHUNK
    A = lax.complex(s5_a_re, s5_a_im)
    dt = jnp.exp(s5_log_dt)[..., None]
    a_bar = jnp.exp(dt * A)
    b_bar = ((a_bar - 1.0) / A)[..., None] * lax.complex(s5_b_re, s5_b_im)
    c_mat = lax.complex(s5_c_re, s5_c_im)
    k_idx = jnp.arange(T + 1, dtype=F32)
    pw = jnp.exp(k_idx[None, None, None, :, None] * (dt * A)[:, :, :, None, :])
    hi = lax.Precision.HIGHEST
    tt = jnp.arange(T)
    GP, NG = S5_GROUP, depth * S5_GROUPS
    pw_f, pw_b = pw[:, 0], pw[:, 1]
    tap_f = jnp.einsum("lgpn,lgkn,lgnq->lgpkq", c_mat[:, 0], pw_f[:, :, T - 1 - tt], b_bar[:, 0],
                       precision=hi).real
    tap_b = jnp.einsum("lgpn,lgkn,lgnq->lgpkq", c_mat[:, 1], pw_b[:, :, tt], b_bar[:, 1],
                       precision=hi).real
    d_diag = s5_d.reshape(depth, S5_GROUPS, GP)[..., None] * jnp.eye(GP, dtype=F32)
    centre = tap_f[:, :, :, T - 1:] + tap_b[:, :, :, :1] + d_diag[:, :, :, None, :]
    krev = jnp.concatenate([tap_f[:, :, :, :T - 1], centre, tap_b[:, :, :, 1:]], axis=3)
    krev = krev.reshape(depth, S5_GROUPS, GP, (2 * T - 1) * GP)
    wloc_t = jnp.stack([krev[..., (T - 1 - t) * GP:(T - 1 - t) * GP + T * GP] for t in range(T)], axis=2)
    wloc_t = wloc_t.reshape(NG, T * GP, T * GP)
    b_t = jnp.swapaxes(b_bar, -1, -2)
    inc_f = pw_f[:, :, T - 1 - tt][:, :, :, None, :] * b_t[:, 0][:, :, None]
    inc_b = pw_b[:, :, tt][:, :, :, None, :] * b_t[:, 1][:, :, None]
    wst = jnp.concatenate([inc_f.real, inc_b.real, inc_f.imag, inc_b.imag], axis=-1)
    wst = wst.reshape(NG, T * GP, 4 * S5_STATE)
    ro_f = c_mat[:, 0][:, :, None] * pw_f[:, :, tt + 1][:, :, :, None, :]
    ro_b = c_mat[:, 1][:, :, None] * pw_b[:, :, T - tt][:, :, :, None, :]
    wcar_t = jnp.concatenate([ro_f.real, ro_b.real, -ro_f.imag, -ro_b.imag], axis=-1)
    wcar_t = wcar_t.reshape(NG, T * GP, 4 * S5_STATE)
    a_t = pw[:, :, :, T]
    at = jnp.stack([jnp.concatenate([a_t[:, 0].real, a_t[:, 1].real], axis=-1),
                    jnp.concatenate([a_t[:, 0].imag, a_t[:, 1].imag], axis=-1)], axis=2)
    at = at.reshape(NG, 2, LANES).astype(F32)

    return dict(norm_g=norm_g[:, None, :], w_in=w_in_p, g_cq=mla_g_cq[:, None, :], w_uq=w_uq_p, g_q=g_q,
                g_ckv=mla_g_ckv[:, None, :], w_ukv=w_ukv_p, g_k=g_k, g_sq=g_sq, g_sk=g_sk,
                sinkrow=sinkrow, s5_wloc_t=wloc_t.astype(BF16), s5_wst=wst.astype(BF16),
                s5_wcar_t=wcar_t.astype(BF16), s5_at=at, w_glu=s5_w_glu.astype(BF16),
                w_out=w_out.astype(BF16))


def kernel(x, c, ctx, c_ctx, norm_g, w_ada, b_ada, w_in, w_out, mla_g_cq, mla_g_ckv, mla_w_uq, mla_w_ukv, mla_g_qn, mla_g_kn, swa_g_qn, swa_g_kn, swa_sink, s5_a_re, s5_a_im, s5_log_dt, s5_b_re, s5_b_im, s5_c_re, s5_c_im, s5_d, s5_w_glu):
    B, L, D = x.shape
    C = ctx.shape[1]
    S = C + L
    depth = w_in.shape[0]
    assert B + PROJ_NB <= MOD_ROWS and C == TOK and L % (2 * TOK) == 0 and L % GRID_W == 0 and L >= 3 * SWA_BLOCK

    P = _prep_params(norm_g, w_in, w_out, mla_g_cq, mla_g_ckv, mla_w_uq, mla_w_ukv, mla_g_qn, mla_g_kn,
                     swa_g_qn, swa_g_kn, swa_sink, s5_a_re, s5_a_im, s5_log_dt, s5_b_re, s5_b_im,
                     s5_c_re, s5_c_im, s5_d, s5_w_glu)
    rope_mla = _rope_tables(L, C, MLA_ROPE, MLA_NOPE, 1)
    rope_swa = _rope_tables(L, C, SWA_DIM, 0, 2)

    cc = jnp.concatenate([c, jnp.tile(c_ctx[None, :], (PROJ_NB, 1)),
                          jnp.zeros((MOD_ROWS - B - PROJ_NB, D), F32)], axis=0)
    mod = _ada_call(cc, w_ada, b_ada).reshape(depth, MOD_ROWS, 3, D)

    for l in range(depth):
        qa, ka, va, qb, kb, vb, u, gt = _proj_call(l, x, ctx, mod, P, rope_mla, rope_swa)
        oa = _mla_call(qa, ka, va, L)
        ob = _swa_call(l, qb, kb, vb, P["sinkrow"], C)
        gy = _s5_call(l, u, P, L)
        x, ctx = _out_call(l, x, ctx, mod, oa, ob, gy, gt, P)
    return x
```

```python
import functools
import math

import jax
import jax.numpy as jnp
from jax import lax
from jax.experimental import pallas as pl
from jax.experimental.pallas import tpu as pltpu

F32 = jnp.float32
BF16 = jnp.bfloat16

GRID_W = 64
EPS = 1e-6
ROPE_BASE = 10000.0
NEG = -1e30
LOG2E = math.log2(math.e)

MLA_HEADS = 8
MLA_NOPE = 64
MLA_ROPE = 32
MLA_V = 64
MLA_QK = MLA_NOPE + MLA_ROPE
MLA_Q_RANK = 384
MLA_KV_RANK = 256

SWA_HEADS = 8
SWA_KV_HEADS = 2
SWA_DIM = 64
SWA_WINDOW = 128

S5_GROUP = 16
S5_GROUPS = 32
S5_STATE = 64
S5_CHUNK = 16

LANES = 128
TOK = 256
KV_CHUNK = 256
SWA_BLOCK = 128
MOD_ROWS = 16
PROJ_NB = 2
SWA_UNROLL = 16
OUT_COLS = 256
PROJ_SKEW = 1

C_CQ = 0
C_CKV = 384
C_KR = 640
C_SQ = 768
C_SK = 1280
C_SV = 1536
C_U = 1792
C_GATE = 2304
C_END = 3840

VMEM_LIMIT = 56 * 1024 * 1024


def _cparams(n_axes):
    return pltpu.CompilerParams(dimension_semantics=("arbitrary",) * n_axes,
                                vmem_limit_bytes=VMEM_LIMIT)


def _ada_kernel(c_ref, w_ref, b_ref, o_ref):
    cc = c_ref[...]
    s = cc * jax.nn.sigmoid(cc)
    o_ref[0] = jnp.dot(s, w_ref[0], preferred_element_type=F32,
                       precision=lax.Precision.HIGHEST) + b_ref[0]


def _ada_call(cc, w_ada, b_ada):
    depth, d, n3 = w_ada.shape
    tn = 768
    return pl.pallas_call(
        _ada_kernel,
        out_shape=jax.ShapeDtypeStruct((depth, MOD_ROWS, n3), F32),
        grid=(depth, n3 // tn),
        in_specs=[pl.BlockSpec((MOD_ROWS, d), lambda l, j: (0, 0)),
                  pl.BlockSpec((1, d, tn), lambda l, j: (l, 0, j)),
                  pl.BlockSpec((1, 1, tn), lambda l, j: (l, 0, j))],
        out_specs=pl.BlockSpec((1, MOD_ROWS, tn), lambda l, j: (l, 0, j)),
        compiler_params=_cparams(2),
        name="ada",
    )(cc, w_ada, b_ada.reshape(depth, 1, n3))


def _sigmoid(v):
    return 0.5 * jnp.tanh(0.5 * v) + 0.5


def _norm_rope(slots, lo_masks, dim, gain, tab_ref, half, dup=False):
    sq = [s * s for s in slots]
    sums = []
    for s2, lo in zip(sq, lo_masks):
        if lo is None:
            sums.append((jnp.sum(s2, axis=-1, keepdims=True),))
        elif dup:
            sums.append((jnp.sum(jnp.where(lo, s2, 0.0), axis=-1, keepdims=True),))
        else:
            sums.append((jnp.sum(jnp.where(lo, s2, 0.0), axis=-1, keepdims=True),
                         jnp.sum(jnp.where(lo, 0.0, s2), axis=-1, keepdims=True)))
    ys = []
    for s, ss, lo in zip(slots, sums, lo_masks):
        rs = [lax.rsqrt(v * (1.0 / dim) + EPS) for v in ss]
        r = rs[0] if len(rs) == 1 else jnp.where(lo, rs[0], rs[1])
        ys.append(s * r * gain)
    up = [pltpu.roll(y, half, 1) for y in ys]
    dn = [pltpu.roll(y, LANES - half, 1) for y in ys]
    return [(y * tab_ref[0] + u * tab_ref[1] + d * tab_ref[2]).astype(BF16) for y, u, d in zip(ys, up, dn)]


N_PROJ_DATA = 3
N_PROJ_WEIGHTS = 12


def _proj_kernel(*refs, n_lat_tiles):
    data = refs[:N_PROJ_DATA]
    shared = refs[N_PROJ_DATA:N_PROJ_DATA + N_PROJ_WEIGHTS]
    outs = refs[N_PROJ_DATA + N_PROJ_WEIGHTS:]
    streams = []
    for i in range(data[0].shape[0]):
        one = pl.ds(i, 1)
        streams.append(_proj_tile(data[0].at[one], data[1].at[one], data[2].at[:, one], *shared,
                                  *[o.at[one] for o in outs], n_lat_tiles=n_lat_tiles))
    pending = list(enumerate(streams))
    step = 0
    while pending:
        for item in list(pending):
            if step >= item[0] * PROJ_SKEW and next(item[1], "done") == "done":
                pending.remove(item)
        step += 1


def _proj_tile(x_ref, c_ref, mod_ref, ng_ref, win_ref, gcq_ref, wuq_ref, gq_ref,
               gckv_ref, wukv_ref, gk_ref, gsq_ref, gsk_ref, rm_ref, rs_ref,
               qa_ref, ka_ref, va_ref, qb_ref, kb_ref, vb_ref, u_ref, gt_ref, *, n_lat_tiles):
    t = pl.program_id(1)
    x = jnp.where(t == n_lat_tiles, c_ref[0], x_ref[0])
    mod = mod_ref[0, 0]
    y = x * lax.rsqrt(jnp.mean(x * x, axis=-1, keepdims=True) + EPS) * ng_ref[0]
    xn = (y * (1.0 + mod[1:2]) + mod[0:1]).astype(BF16)

    def seg(a, b):
        return jnp.dot(xn, win_ref[0, :, a:b], preferred_element_type=F32)

    lane = lax.broadcasted_iota(jnp.int32, (TOK, LANES), 1)
    lo = lane < 64

    yield
    cq = seg(C_CQ, C_CKV)
    ckv = seg(C_CKV, C_KR)
    kr = seg(C_KR, C_SQ)
    yield
    cqn = (cq * lax.rsqrt(jnp.mean(cq * cq, axis=-1, keepdims=True) + EPS) * gcq_ref[0]).astype(BF16)
    qf = jnp.dot(cqn, wuq_ref[0], preferred_element_type=F32)
    ckvn = (ckv * lax.rsqrt(jnp.mean(ckv * ckv, axis=-1, keepdims=True) + EPS) * gckv_ref[0]).astype(BF16)
    kvf = jnp.dot(ckvn, wukv_ref[0], preferred_element_type=F32)
    sq = seg(C_SQ, C_SK)
    yield
    slots = [qf[:, h * LANES:(h + 1) * LANES] for h in range(MLA_HEADS)]
    for h, o in enumerate(_norm_rope(slots, [None] * MLA_HEADS, MLA_QK, gq_ref[0], rm_ref, MLA_ROPE // 2)):
        qa_ref[0, h] = o

    sk = seg(C_SK, C_SV)
    vb_ref[0] = seg(C_SV, C_U).astype(BF16)
    yield
    slots = [kvf[:, h * LANES:(h + 1) * LANES] + kr for h in range(MLA_HEADS)]
    for h, o in enumerate(_norm_rope(slots, [None] * MLA_HEADS, MLA_QK, gk_ref[0], rm_ref, MLA_ROPE // 2)):
        ka_ref[0, h] = o
    va_ref[0] = kvf[:, MLA_HEADS * LANES:].astype(BF16)
    u_ref[0] = seg(C_U, C_GATE)
    yield
    slots = [sq[:, p * LANES:(p + 1) * LANES] for p in range(SWA_HEADS // 2)]
    for p, o in enumerate(_norm_rope(slots, [lo] * len(slots), SWA_DIM, gsq_ref[0], rs_ref, SWA_DIM // 2)):
        qb_ref[0, :, p * LANES:(p + 1) * LANES] = o
    slots = [sk[:, j * LANES:(j + 1) * LANES] for j in range(SWA_KV_HEADS)]
    for j, o in enumerate(_norm_rope(slots, [lo] * len(slots), SWA_DIM, gsk_ref[0], rs_ref, SWA_DIM // 2, dup=True)):
        kb_ref[0, :, j * LANES:(j + 1) * LANES] = o
    g = seg(C_GATE, C_END)
    yield
    gt_ref[0] = (g * _sigmoid(g)).astype(BF16)


def _proj_call(l, x, ctx, mod, P, rope_mla, rope_swa):
    B, L, D = x.shape
    C = ctx.shape[1]
    S = C + L
    nt = S // TOK

    def wspec(arr):
        shp = arr.shape
        return pl.BlockSpec((1,) + shp[1:], lambda b, t: (l,) + (0,) * (len(shp) - 1))

    weights = [P["norm_g"], P["w_in"], P["g_cq"], P["w_uq"], P["g_q"],
               P["g_ckv"], P["w_ukv"], P["g_k"], P["g_sq"], P["g_sk"]]
    nl = L // TOK
    nb = PROJ_NB
    assert len(weights) + 2 == N_PROJ_WEIGHTS and B % nb == 0
    in_specs = ([pl.BlockSpec((nb, TOK, D), lambda b, t: (b, jnp.minimum(t, nl - 1), 0)),
                 pl.BlockSpec((nb, TOK, D), lambda b, t: (b, 0, 0)),
                 pl.BlockSpec((1, nb, 3, D), lambda b, t: (l, jnp.where(t == nl, B // nb, b), 0, 0))]
                + [wspec(w) for w in weights]
                + [pl.BlockSpec((3, TOK, LANES), lambda b, t: (0, t, 0)),
                   pl.BlockSpec((3, TOK, LANES), lambda b, t: (0, t, 0))])
    out_shape = [jax.ShapeDtypeStruct((B, MLA_HEADS, S, LANES), BF16),
                 jax.ShapeDtypeStruct((B, MLA_HEADS, S, LANES), BF16),
                 jax.ShapeDtypeStruct((B, S, 512), BF16),
                 jax.ShapeDtypeStruct((B, S, 512), BF16),
                 jax.ShapeDtypeStruct((B, S, 256), BF16),
                 jax.ShapeDtypeStruct((B, S, 256), BF16),
                 jax.ShapeDtypeStruct((B, S, 512), F32),
                 jax.ShapeDtypeStruct((B, S, 1536), BF16)]
    out_specs = [pl.BlockSpec((nb, MLA_HEADS, TOK, LANES), lambda b, t: (b, 0, t, 0)),
                 pl.BlockSpec((nb, MLA_HEADS, TOK, LANES), lambda b, t: (b, 0, t, 0))]
    out_specs += [pl.BlockSpec((nb, TOK, s.shape[2]), lambda b, t: (b, t, 0)) for s in out_shape[2:]]
    return pl.pallas_call(
        functools.partial(_proj_kernel, n_lat_tiles=nl), out_shape=out_shape, grid=(B // nb, nt),
        in_specs=in_specs, out_specs=out_specs,
        compiler_params=_cparams(2), name="proj",
    )(x, ctx, mod, *weights, rope_mla, rope_swa)


MLA_VT_ROWS = 80


def _mla_kernel(q_ref, k_ref, v_ref, o_ref, vt_ref, *, n_chunks):
    ones = jnp.ones((MLA_VT_ROWS - MLA_V, KV_CHUNK), BF16)

    @pl.when(pl.program_id(2) == 0)
    def _():
        for c in range(n_chunks):
            vt = v_ref[0, c].astype(F32).T.astype(BF16)
            vt_ref[0, c] = jnp.concatenate([vt[:MLA_V], ones], axis=0)
            vt_ref[1, c] = jnp.concatenate([vt[MLA_V:], ones], axis=0)

    qts = [q_ref[0, e].astype(F32).T.astype(BF16) for e in range(2)]
    score = lambda c, e: jnp.dot(k_ref[0, e, c], qts[e], preferred_element_type=F32)
    ms = [None, None]
    accs = [None, None]
    sts = [score(0, e) for e in range(2)]
    for c in range(n_chunks):
        nxt = [score(c + 1, e) for e in range(2)] if c + 1 < n_chunks else None
        for e in range(2):
            mc = jnp.max(sts[e], axis=0, keepdims=True)
            m_new = mc if c == 0 else jnp.maximum(ms[e], mc)
            pt = jnp.exp2(sts[e] - m_new).astype(BF16)
            pv = jnp.dot(vt_ref[e, c], pt, preferred_element_type=F32)
            accs[e] = pv if c == 0 else accs[e] * jnp.exp2(ms[e] - m_new) + pv
            ms[e] = m_new
        sts = nxt
    ot = jnp.concatenate([accs[e][:MLA_V] / accs[e][MLA_V:MLA_V + 1] for e in range(2)], axis=0)
    o_ref[0] = ot.T.astype(BF16)


def _mla_call(qa, ka, va, n_lat):
    B, H, S, _ = qa.shape
    nc = S // KV_CHUNK
    tq = 2 * TOK
    k5 = ka.reshape(B, H, nc, KV_CHUNK, LANES)
    v4 = va.reshape(B, nc, KV_CHUNK, 512)
    o_lat = pl.pallas_call(
        functools.partial(_mla_kernel, n_chunks=nc),
        out_shape=jax.ShapeDtypeStruct((B, n_lat, 512), BF16),
        grid=(B, H // 2, n_lat // tq),
        in_specs=[pl.BlockSpec((1, 2, tq, LANES), lambda b, p, t: (b, p, t, 0)),
                  pl.BlockSpec((1, 2, nc, KV_CHUNK, LANES), lambda b, p, t: (b, p, 0, 0, 0)),
                  pl.BlockSpec((1, nc, KV_CHUNK, LANES), lambda b, p, t: (b, 0, 0, p))],
        out_specs=pl.BlockSpec((1, tq, LANES), lambda b, p, t: (b, t, p)),
        scratch_shapes=[pltpu.VMEM((2, nc, MLA_VT_ROWS, KV_CHUNK), BF16)],
        compiler_params=_cparams(3), name="mla_attn",
    )(qa, k5, v4)
    cblk = n_lat // KV_CHUNK
    o_ctx = pl.pallas_call(
        functools.partial(_mla_kernel, n_chunks=1),
        out_shape=jax.ShapeDtypeStruct((B, S - n_lat, 512), BF16),
        grid=(B, H // 2, 1),
        in_specs=[pl.BlockSpec((1, 2, TOK, LANES), lambda b, p, t: (b, p, cblk, 0)),
                  pl.BlockSpec((1, 2, 1, KV_CHUNK, LANES), lambda b, p, t: (b, p, cblk, 0, 0)),
                  pl.BlockSpec((1, 1, KV_CHUNK, LANES), lambda b, p, t: (b, cblk, 0, p))],
        out_specs=pl.BlockSpec((1, TOK, LANES), lambda b, p, t: (b, 0, p)),
        scratch_shapes=[pltpu.VMEM((2, 1, MLA_VT_ROWS, KV_CHUNK), BF16)],
        compiler_params=_cparams(3), name="mla_attn_ctx",
    )(qa, k5, v4)
    return o_lat, o_ctx


SWA_VT_ROWS = 80


def _swa_kernel(q_ref, k_ref, v_ref, sink_ref, o_ref, vt_ref, bias_ref, *, n_ctx, n_lat):
    blk = SWA_BLOCK
    heads = SWA_HEADS // SWA_KV_HEADS
    win = 3 * blk
    n_blocks = n_lat // blk
    ctx_blk = n_blocks
    group = min(SWA_UNROLL, n_blocks)
    sink = sink_ref[0, 0:1, :]

    ones = jnp.ones((SWA_VT_ROWS - SWA_DIM, blk), BF16)
    for i in range((n_lat + n_ctx) // blk):
        vt = v_ref[0, i * blk:(i + 1) * blk, :].astype(F32).T
        vt_ref[i] = jnp.concatenate([vt[:SWA_DIM].astype(BF16), ones], axis=0)
    rel0 = (lax.broadcasted_iota(jnp.int32, (win, heads * blk), 0)
            - (lax.broadcasted_iota(jnp.int32, (win, heads * blk), 1) & (blk - 1)))
    for kind in range(3):
        bias_ref[kind] = jnp.where(jnp.abs(rel0 - kind * blk) <= SWA_WINDOW, 0.0, NEG)

    kc = k_ref[0, n_lat:n_lat + n_ctx, :]
    vtc = jnp.concatenate([vt_ref[ctx_blk + i] for i in range(n_ctx // blk)], axis=1)
    zrows = jnp.zeros((LANES - SWA_DIM, heads * blk), BF16)

    def qmat(r0):
        qt = q_ref[0, pl.ds(r0, blk), :].astype(F32).T
        w = jnp.concatenate([qt[h * SWA_DIM:(h + 1) * SWA_DIM] for h in range(heads)], axis=1)
        return jnp.concatenate([w.astype(BF16), zrows], axis=0)

    def finish(r0, m, acc):
        ot = acc[:SWA_DIM] / (acc[SWA_DIM:SWA_DIM + 1] + jnp.exp2(sink - m))
        o4 = jnp.concatenate([ot[:, h * blk:(h + 1) * blk] for h in range(heads)], axis=0)
        o_ref[0, pl.ds(r0, blk), :] = o4.T.astype(BF16)

    for n in range(n_ctx // blk):
        r0 = n_lat + n * blk
        s_c = jnp.dot(kc, qmat(r0), preferred_element_type=F32)
        m = jnp.maximum(jnp.max(s_c, axis=0, keepdims=True), sink)
        finish(r0, m, jnp.dot(vtc, jnp.exp2(s_c - m).astype(BF16), preferred_element_type=F32))

    def window_block(n):
        return jnp.clip(n - 1, 0, n_blocks - 3)

    def scores(n):
        wb = window_block(n)
        w = qmat(pl.multiple_of(n * blk, blk))
        kw = k_ref[0, pl.ds(pl.multiple_of(wb * blk, blk), win), :]
        return (jnp.dot(kc, w, preferred_element_type=F32),
                jnp.dot(kw, w, preferred_element_type=F32) + bias_ref[n - wb])

    def blocks(gi, carry):
        n0 = gi * group
        cur = scores(n0)
        for i in range(group):
            n = n0 + i
            nxt = scores(n + 1) if i + 1 < group else None
            s_c, s_w = cur
            m = jnp.maximum(jnp.maximum(jnp.max(s_c, axis=0, keepdims=True),
                                        jnp.max(s_w, axis=0, keepdims=True)), sink)
            wb = window_block(n)
            vtw = jnp.concatenate([vt_ref[wb + j] for j in range(3)], axis=1)
            acc = (jnp.dot(vtc, jnp.exp2(s_c - m).astype(BF16), preferred_element_type=F32)
                   + jnp.dot(vtw, jnp.exp2(s_w - m).astype(BF16), preferred_element_type=F32))
            finish(pl.multiple_of(n * blk, blk), m, acc)
            cur = nxt
        return carry

    assert n_blocks % group == 0
    lax.fori_loop(0, n_blocks // group, blocks, 0)


def _swa_call(l, qb, kb, vb, sinkrow, n_ctx):
    B, S, _ = qb.shape
    nq = (SWA_HEADS // SWA_KV_HEADS) * SWA_BLOCK
    return pl.pallas_call(
        functools.partial(_swa_kernel, n_ctx=n_ctx, n_lat=S - n_ctx),
        out_shape=jax.ShapeDtypeStruct((B, S, 512), BF16),
        grid=(B, SWA_KV_HEADS),
        in_specs=[pl.BlockSpec((1, S, 256), lambda b, j: (b, 0, j)),
                  pl.BlockSpec((1, S, LANES), lambda b, j: (b, 0, j)),
                  pl.BlockSpec((1, S, LANES), lambda b, j: (b, 0, j)),
                  pl.BlockSpec((1, 8, nq), lambda b, j: (l * SWA_KV_HEADS + j, 0, 0))],
        out_specs=pl.BlockSpec((1, S, 256), lambda b, j: (b, 0, j)),
        scratch_shapes=[pltpu.VMEM((S // SWA_BLOCK, SWA_VT_ROWS, SWA_BLOCK), BF16),
                        pltpu.VMEM((3, 3 * SWA_BLOCK, nq), F32)],
        compiler_params=_cparams(2), name="swa_attn",
    )(qb, kb, vb, sinkrow)


def _gelu(y):
    return 0.5 * y * (1.0 + jnp.tanh(math.sqrt(2.0 / math.pi) * (y + 0.044715 * (y * y * y))))


def _s5_kernel(u_ref, wst_ref, wloc_ref, wcar_ref, at_ref, o_ref,
               ut_ref, utc_ref, yt_ref, ytc_ref, ere_ref, eim_ref, hfr_ref, hfi_ref, hbr_ref, hbi_ref,
               *, nb, n_lat_chunks, n_ctx_chunks):
    T, GP = S5_CHUNK, S5_GROUP
    n_chunks = n_lat_chunks + n_ctx_chunks
    n_lat = n_lat_chunks * T
    cw = nb * n_ctx_chunks
    tn_dims = (((0,), (0,)), ((), ()))
    nt_dims = (((1,), (1,)), ((), ()))
    gpl = LANES // GP

    for b in range(nb):
        for s in range(T):
            xs = u_ref[b, pl.ds(s, n_lat_chunks, stride=T), :]
            ut_ref[b, :, s] = xs.T.reshape(gpl, GP, n_lat_chunks).astype(BF16)
    zpad = jnp.zeros((LANES - cw, LANES), F32)
    for s in range(T):
        xs = jnp.concatenate([u_ref[b, pl.ds(n_lat + s, n_ctx_chunks, stride=T), :] for b in range(nb)]
                             + [zpad], axis=0)
        utc_ref[:, s] = xs.T.reshape(gpl, GP, LANES).astype(BF16)

    lane = lax.broadcasted_iota(jnp.int32, (2 * nb, LANES), 1)
    fwd = lane < S5_STATE
    lane_l = lax.broadcasted_iota(jnp.int32, (n_lat_chunks, LANES), 1) < S5_STATE
    lane_c = lax.broadcasted_iota(jnp.int32, (LANES, LANES), 1) < S5_STATE

    slab = 2 * nb
    lat_rows = lambda gb: pl.ds(gb, n_lat_chunks, stride=slab)
    ctx_rows = lambda gb: pl.ds(n_lat_chunks * slab + gb, n_ctx_chunks, stride=slab)

    def pair(gp, carry):
        for gl in range(2):
            g = gp * 2 + gl
            wst = wst_ref[g]
            for b in range(nb):
                e = lax.dot_general(ut_ref[b, g].reshape(T * GP, n_lat_chunks), wst, tn_dims,
                                    preferred_element_type=F32)
                ere_ref[lat_rows(gl * nb + b), :] = e[:, :LANES]
                eim_ref[lat_rows(gl * nb + b), :] = e[:, LANES:]
            ec = lax.dot_general(utc_ref[g].reshape(T * GP, LANES), wst, tn_dims, preferred_element_type=F32)
            for b in range(nb):
                ere_ref[ctx_rows(gl * nb + b), :] = ec[b * n_ctx_chunks:(b + 1) * n_ctx_chunks, :LANES]
                eim_ref[ctx_rows(gl * nb + b), :] = ec[b * n_ctx_chunks:(b + 1) * n_ctx_chunks, LANES:]

        a_re = jnp.concatenate([jnp.broadcast_to(at_ref[gp * 2 + gl, 0:1, :], (nb, LANES)) for gl in range(2)], 0)
        a_im = jnp.concatenate([jnp.broadcast_to(at_ref[gp * 2 + gl, 1:2, :], (nb, LANES)) for gl in range(2)], 0)

        def step(i, hc):
            h_re, h_im = hc
            cf = jnp.where(i < n_ctx_chunks, n_lat_chunks + i, i - n_ctx_chunks)
            cb = n_chunks - 1 - i
            sf = pl.ds(pl.multiple_of(cf * slab, slab), slab)
            sb = pl.ds(pl.multiple_of(cb * slab, slab), slab)
            hfr_ref[sf, :] = h_re
            hfi_ref[sf, :] = h_im
            hbr_ref[sb, :] = h_re
            hbi_ref[sb, :] = h_im
            e_re = jnp.where(fwd, ere_ref[sf, :], ere_ref[sb, :])
            e_im = jnp.where(fwd, eim_ref[sf, :], eim_ref[sb, :])
            return (a_re * h_re - a_im * h_im + e_re, a_re * h_im + a_im * h_re + e_im)

        zero = jnp.zeros((2 * nb, LANES), F32)
        lax.fori_loop(0, n_chunks, step, (zero, zero), unroll=8)

        for gl in range(2):
            g = gp * 2 + gl
            wloc, wcar = wloc_ref[g], wcar_ref[g]
            for b in range(nb):
                rows = lat_rows(gl * nb + b)
                h_cat = jnp.concatenate([jnp.where(lane_l, hfr_ref[rows, :], hbr_ref[rows, :]),
                                         jnp.where(lane_l, hfi_ref[rows, :], hbi_ref[rows, :])],
                                        axis=-1).astype(BF16)
                yt = (jnp.dot(wloc, ut_ref[b, g].reshape(T * GP, n_lat_chunks), preferred_element_type=F32)
                      + lax.dot_general(wcar, h_cat, nt_dims, preferred_element_type=F32))
                yt_ref[b, g] = _gelu(yt).astype(BF16).reshape(T, GP, n_lat_chunks)
            crow = lambda ref: jnp.concatenate(
                [ref[ctx_rows(gl * nb + b), :] for b in range(nb)]
                + [jnp.zeros((LANES - cw, LANES), F32)], axis=0)
            h_cat = jnp.concatenate([jnp.where(lane_c, crow(hfr_ref), crow(hbr_ref)),
                                     jnp.where(lane_c, crow(hfi_ref), crow(hbi_ref))], axis=-1).astype(BF16)
            ytc = (jnp.dot(wloc, utc_ref[g].reshape(T * GP, LANES), preferred_element_type=F32)
                   + lax.dot_general(wcar, h_cat, nt_dims, preferred_element_type=F32))
            ytc_ref[g] = _gelu(ytc).astype(BF16).reshape(T, GP, LANES)
        return carry

    lax.fori_loop(0, gpl // 2, pair, 0)

    for b in range(nb):
        for t in range(T):
            z = yt_ref[b, :, t].astype(F32).reshape(LANES, n_lat_chunks)
            o_ref[b, pl.ds(t, n_lat_chunks, stride=T), :] = z.T
    for t in range(T):
        z = ytc_ref[:, t].astype(F32).reshape(LANES, LANES).T
        for b in range(nb):
            o_ref[b, pl.ds(n_lat + t, n_ctx_chunks, stride=T), :] = z[b * n_ctx_chunks:(b + 1) * n_ctx_chunks]


def _s5_call(l, u, P, n_lat):
    B, S, W = u.shape
    nb = 4 if B % 4 == 0 else B
    T, GP = S5_CHUNK, S5_GROUP
    nlc, ncc = n_lat // T, (S - n_lat) // T
    gpl = LANES // GP
    nblk = W // LANES
    big = lambda n: pl.BlockSpec((nb, S, LANES), lambda j, hb: (hb, 0, j), pipeline_mode=pl.Buffered(n))
    wspec = lambda: pl.BlockSpec((gpl, T * GP, T * GP), lambda j, hb: (l * nblk + j, 0, 0))
    rows = 2 * nb * (nlc + ncc)
    return pl.pallas_call(
        functools.partial(_s5_kernel, nb=nb, n_lat_chunks=nlc, n_ctx_chunks=ncc),
        out_shape=jax.ShapeDtypeStruct((B, S, W), F32),
        grid=(nblk, B // nb),
        in_specs=[big(2), wspec(), wspec(), wspec(),
                  pl.BlockSpec((gpl, 2, LANES), lambda j, hb: (l * nblk + j, 0, 0))],
        out_specs=big(1),
        scratch_shapes=[pltpu.VMEM((nb, gpl, T, GP, nlc), BF16), pltpu.VMEM((gpl, T, GP, LANES), BF16),
                        pltpu.VMEM((nb, gpl, T, GP, nlc), BF16), pltpu.VMEM((gpl, T, GP, LANES), BF16)]
                       + [pltpu.VMEM((rows, LANES), F32)] * 6,
        compiler_params=_cparams(2), name="s5",
    )(u, P["s5_wst"], P["s5_wloc_t"], P["s5_wcar_t"], P["s5_at"])


def _out_kernel(x_ref, c_ref, mod_ref, oa_ref, oac_ref, ob_ref, gy_ref, gt_ref, wglu_ref, wout_ref,
                xo_ref, co_ref, *, n_lat_tiles):
    t = pl.program_id(1)
    cb = OUT_COLS
    gyb = gy_ref[0].astype(BF16)
    g = gt_ref[0].astype(F32)
    oa = jnp.where(t == n_lat_tiles, oac_ref[0], oa_ref[0])
    m_a = (oa.astype(F32) * g[:, 0:512]).astype(BF16)
    m_b = (ob_ref[0].astype(F32) * g[:, 512:1024]).astype(BF16)
    oc = []
    for j in range(512 // cb):
        za = jnp.dot(gyb, wglu_ref[0, :, j * cb:(j + 1) * cb], preferred_element_type=F32)
        zb = jnp.dot(gyb, wglu_ref[0, :, 512 + j * cb:512 + (j + 1) * cb], preferred_element_type=F32)
        oc.append((za * _sigmoid(zb) * g[:, 1024 + j * cb:1024 + (j + 1) * cb]).astype(BF16))
    m_c = jnp.concatenate(oc, axis=-1)
    gate = mod_ref[0, 0][2:3]
    resid = jnp.where(t == n_lat_tiles, c_ref[0], x_ref[0])
    new = []
    for j in range(resid.shape[1] // cb):
        cols = slice(j * cb, (j + 1) * cb)
        upd = (jnp.dot(m_a, wout_ref[0, 0:512, cols], preferred_element_type=F32)
               + jnp.dot(m_b, wout_ref[0, 512:1024, cols], preferred_element_type=F32)
               + jnp.dot(m_c, wout_ref[0, 1024:1536, cols], preferred_element_type=F32))
        new.append(resid[:, cols] + gate[:, cols] * upd)

    @pl.when(t == n_lat_tiles)
    def _():
        for j, v in enumerate(new):
            co_ref[0, :, j * cb:(j + 1) * cb] = v

    @pl.when(t < n_lat_tiles)
    def _():
        for j, v in enumerate(new):
            xo_ref[0, :, j * cb:(j + 1) * cb] = v


def _out_call(l, x, ctx, mod, oa, oa_ctx, ob, gy, gt, P):
    B, L, D = x.shape
    C = ctx.shape[1]
    S = C + L
    nl = L // TOK
    xmap = lambda b, t: (b, jnp.minimum(t, nl - 1), 0)
    cmap = lambda b, t: (b, 0, 0)
    tmap = lambda b, t: (b, t, 0)
    return pl.pallas_call(
        functools.partial(_out_kernel, n_lat_tiles=nl),
        out_shape=[jax.ShapeDtypeStruct(x.shape, F32), jax.ShapeDtypeStruct(ctx.shape, F32)],
        grid=(B, S // TOK),
        in_specs=[pl.BlockSpec((1, TOK, D), xmap),
                  pl.BlockSpec((1, TOK, D), cmap),
                  pl.BlockSpec((1, 1, 3, D), lambda b, t: (l, jnp.where(t == nl, B, b), 0, 0)),
                  pl.BlockSpec((1, TOK, 512), xmap),
                  pl.BlockSpec((1, TOK, 512), cmap),
                  pl.BlockSpec((1, TOK, 512), tmap),
                  pl.BlockSpec((1, TOK, 512), tmap),
                  pl.BlockSpec((1, TOK, 1536), tmap),
                  pl.BlockSpec((1, 512, 1024), lambda b, t: (l, 0, 0)),
                  pl.BlockSpec((1, 1536, D), lambda b, t: (l, 0, 0))],
        out_specs=[pl.BlockSpec((1, TOK, D), xmap), pl.BlockSpec((1, TOK, D), cmap)],
        compiler_params=_cparams(2), name="out",
    )(x, ctx, mod, oa, oa_ctx, ob, gy, gt, P["w_glu"], P["w_out"])


def _rope_tables(n_lat, n_ctx, rot_dim, lead, reps):
    rows = n_lat // GRID_W
    r_idx, c_idx = jnp.meshgrid(jnp.arange(rows), jnp.arange(GRID_W), indexing="ij")
    r_idx, c_idx = r_idx.reshape(-1), c_idx.reshape(-1)
    n_freq = rot_dim // 4
    freqs = ROPE_BASE ** (-jnp.arange(n_freq, dtype=F32) / n_freq)
    ang = jnp.concatenate([r_idx.astype(F32)[:, None] * freqs,
                           c_idx.astype(F32)[:, None] * freqs], axis=-1)
    ang = jnp.concatenate([ang, jnp.zeros((n_ctx, rot_dim // 2), F32)], axis=0)
    cos, sin, zero = jnp.cos(ang), jnp.sin(ang), jnp.zeros_like(ang)
    n = ang.shape[0]
    tail = LANES // reps - lead - rot_dim

    def pack(x1, x2, fill):
        unit = [jnp.full((n, lead), fill, F32), x1, x2, jnp.full((n, tail), fill, F32)]
        return jnp.concatenate(unit * reps, axis=-1)

    return jnp.stack([pack(cos, cos, 1.0), pack(zero, sin, 0.0), pack(-sin, zero, 0.0)])


def _prep_params(norm_g, w_in, w_out, mla_g_cq, mla_g_ckv, mla_w_uq, mla_w_ukv, mla_g_qn, mla_g_kn,
                 swa_g_qn, swa_g_kn, swa_sink, s5_a_re, s5_a_im, s5_log_dt, s5_b_re, s5_b_im,
                 s5_c_re, s5_c_im, s5_d, s5_w_glu):
    depth, D, _ = w_in.shape
    o_cq, o_ckv, o_kr, o_gm, o_sq, o_sk, o_sv, o_gs, o_u, o_g5, o_end = (
        0, 384, 640, 672, 1184, 1696, 1824, 1952, 2464, 2976, 3488)
    z = lambda n: jnp.zeros((depth, D, n), F32)
    sk0, sk1 = w_in[:, :, o_sk:o_sk + 64], w_in[:, :, o_sk + 64:o_sv]
    sv0, sv1 = w_in[:, :, o_sv:o_sv + 64], w_in[:, :, o_sv + 64:o_gs]
    w_in_p = jnp.concatenate([
        w_in[:, :, o_cq:o_kr],
        z(64), w_in[:, :, o_kr:o_gm], z(32),
        w_in[:, :, o_sq:o_sk],
        sk0, sk0, sk1, sk1, sv0, sv0, sv1, sv1,
        w_in[:, :, o_u:o_g5],
        w_in[:, :, o_gm:o_sq], w_in[:, :, o_gs:o_u], w_in[:, :, o_g5:o_end],
    ], axis=-1).astype(BF16)
    assert w_in_p.shape[-1] == C_END

    wq = mla_w_uq.reshape(depth, MLA_Q_RANK, MLA_HEADS, MLA_QK)
    wq = jnp.pad(wq, ((0, 0), (0, 0), (0, 0), (0, LANES - MLA_QK)))
    w_uq_p = wq.reshape(depth, MLA_Q_RANK, MLA_HEADS * LANES).astype(BF16)
    wkv = mla_w_ukv.reshape(depth, MLA_KV_RANK, MLA_HEADS, MLA_NOPE + MLA_V)
    wk = jnp.pad(wkv[..., :MLA_NOPE], ((0, 0), (0, 0), (0, 0), (0, LANES - MLA_NOPE)))
    w_ukv_p = jnp.concatenate([wk.reshape(depth, MLA_KV_RANK, MLA_HEADS * LANES),
                               wkv[..., MLA_NOPE:].reshape(depth, MLA_KV_RANK, MLA_HEADS * MLA_V)],
                              axis=-1).astype(BF16)

    pad_qk = lambda g: jnp.pad(g, ((0, 0), (0, LANES - MLA_QK)))[:, None, :]
    g_q = pad_qk(mla_g_qn * (MLA_QK ** -0.5 * LOG2E))
    g_k = pad_qk(mla_g_kn)
    g_sq = jnp.tile(swa_g_qn * (SWA_DIM ** -0.5 * LOG2E), (1, 2))[:, None, :]
    g_sk = jnp.tile(swa_g_kn, (1, 2))[:, None, :]
    sink = (swa_sink * LOG2E).reshape(depth * SWA_KV_HEADS, 1, SWA_HEADS // SWA_KV_HEADS, 1)
    sinkrow = jnp.broadcast_to(sink, (depth * SWA_KV_HEADS, 8, SWA_HEADS // SWA_KV_HEADS, SWA_BLOCK))
    sinkrow = sinkrow.reshape(depth * SWA_KV_HEADS, 8, -1).astype(F32)

    T = S5_CHUNK
    A = lax.complex(s5_a_re, s5_a_im)
    dt = jnp.exp(s5_log_dt)[..., None]
    a_bar = jnp.exp(dt * A)
    b_bar = ((a_bar - 1.0) / A)[..., None] * lax.complex(s5_b_re, s5_b_im)
    c_mat = lax.complex(s5_c_re, s5_c_im)
    k_idx = jnp.arange(T + 1, dtype=F32)
    pw = jnp.exp(k_idx[None, None, None, :, None] * (dt * A)[:, :, :, None, :])
    hi = lax.Precision.HIGHEST
    tt = jnp.arange(T)
    GP, NG = S5_GROUP, depth * S5_GROUPS
    pw_f, pw_b = pw[:, 0], pw[:, 1]
    tap_f = jnp.einsum("lgpn,lgkn,lgnq->lgpkq", c_mat[:, 0], pw_f[:, :, T - 1 - tt], b_bar[:, 0],
                       precision=hi).real
    tap_b = jnp.einsum("lgpn,lgkn,lgnq->lgpkq", c_mat[:, 1], pw_b[:, :, tt], b_bar[:, 1],
                       precision=hi).real
    d_diag = s5_d.reshape(depth, S5_GROUPS, GP)[..., None] * jnp.eye(GP, dtype=F32)
    centre = tap_f[:, :, :, T - 1:] + tap_b[:, :, :, :1] + d_diag[:, :, :, None, :]
    krev = jnp.concatenate([tap_f[:, :, :, :T - 1], centre, tap_b[:, :, :, 1:]], axis=3)
    krev = krev.reshape(depth, S5_GROUPS, GP, (2 * T - 1) * GP)
    wloc_t = jnp.stack([krev[..., (T - 1 - t) * GP:(T - 1 - t) * GP + T * GP] for t in range(T)], axis=2)
    wloc_t = wloc_t.reshape(NG, T * GP, T * GP)
    b_t = jnp.swapaxes(b_bar, -1, -2)
    inc_f = pw_f[:, :, T - 1 - tt][:, :, :, None, :] * b_t[:, 0][:, :, None]
    inc_b = pw_b[:, :, tt][:, :, :, None, :] * b_t[:, 1][:, :, None]
    wst = jnp.concatenate([inc_f.real, inc_b.real, inc_f.imag, inc_b.imag], axis=-1)
    wst = wst.reshape(NG, T * GP, 4 * S5_STATE)
    ro_f = c_mat[:, 0][:, :, None] * pw_f[:, :, tt + 1][:, :, :, None, :]
    ro_b = c_mat[:, 1][:, :, None] * pw_b[:, :, T - tt][:, :, :, None, :]
    wcar_t = jnp.concatenate([ro_f.real, ro_b.real, -ro_f.imag, -ro_b.imag], axis=-1)
    wcar_t = wcar_t.reshape(NG, T * GP, 4 * S5_STATE)
    a_t = pw[:, :, :, T]
    at = jnp.stack([jnp.concatenate([a_t[:, 0].real, a_t[:, 1].real], axis=-1),
                    jnp.concatenate([a_t[:, 0].imag, a_t[:, 1].imag], axis=-1)], axis=2)
    at = at.reshape(NG, 2, LANES).astype(F32)

    return dict(norm_g=norm_g[:, None, :], w_in=w_in_p, g_cq=mla_g_cq[:, None, :], w_uq=w_uq_p, g_q=g_q,
                g_ckv=mla_g_ckv[:, None, :], w_ukv=w_ukv_p, g_k=g_k, g_sq=g_sq, g_sk=g_sk,
                sinkrow=sinkrow, s5_wloc_t=wloc_t.astype(BF16), s5_wst=wst.astype(BF16),
                s5_wcar_t=wcar_t.astype(BF16), s5_at=at, w_glu=s5_w_glu.astype(BF16),
                w_out=w_out.astype(BF16))


def kernel(x, c, ctx, c_ctx, norm_g, w_ada, b_ada, w_in, w_out, mla_g_cq, mla_g_ckv, mla_w_uq, mla_w_ukv, mla_g_qn, mla_g_kn, swa_g_qn, swa_g_kn, swa_sink, s5_a_re, s5_a_im, s5_log_dt, s5_b_re, s5_b_im, s5_c_re, s5_c_im, s5_d, s5_w_glu):
    B, L, D = x.shape
    C = ctx.shape[1]
    S = C + L
    depth = w_in.shape[0]
    assert B + PROJ_NB <= MOD_ROWS and C == TOK and L % (2 * TOK) == 0 and L % GRID_W == 0 and L >= 3 * SWA_BLOCK

    P = _prep_params(norm_g, w_in, w_out, mla_g_cq, mla_g_ckv, mla_w_uq, mla_w_ukv, mla_g_qn, mla_g_kn,
                     swa_g_qn, swa_g_kn, swa_sink, s5_a_re, s5_a_im, s5_log_dt, s5_b_re, s5_b_im,
                     s5_c_re, s5_c_im, s5_d, s5_w_glu)
    rope_mla = _rope_tables(L, C, MLA_ROPE, MLA_NOPE, 1)
    rope_swa = _rope_tables(L, C, SWA_DIM, 0, 2)

    cc = jnp.concatenate([c, jnp.tile(c_ctx[None, :], (PROJ_NB, 1)),
                          jnp.zeros((MOD_ROWS - B - PROJ_NB, D), F32)], axis=0)
    mod = _ada_call(cc, w_ada, b_ada).reshape(depth, MOD_ROWS, 3, D)

    for l in range(depth):
        qa, ka, va, qb, kb, vb, u, gt = _proj_call(l, x, ctx, mod, P, rope_mla, rope_swa)
        oa, oa_ctx = _mla_call(qa, ka, va, L)
        ob = _swa_call(l, qb, kb, vb, P["sinkrow"], C)
        gy = _s5_call(l, u, P, L)
        x, ctx = _out_call(l, x, ctx, mod, oa, oa_ctx, ob, gy, gt, P)
    return x
```

```python
import functools
import math

import jax
import jax.numpy as jnp
from jax import lax
from jax.experimental import pallas as pl
from jax.experimental.pallas import tpu as pltpu

F32 = jnp.float32
BF16 = jnp.bfloat16

GRID_W = 64
EPS = 1e-6
ROPE_BASE = 10000.0
NEG = -1e30
LOG2E = math.log2(math.e)

MLA_HEADS = 8
MLA_NOPE = 64
MLA_ROPE = 32
MLA_V = 64
MLA_QK = MLA_NOPE + MLA_ROPE
MLA_Q_RANK = 384
MLA_KV_RANK = 256

SWA_HEADS = 8
SWA_KV_HEADS = 2
SWA_DIM = 64
SWA_WINDOW = 128

S5_GROUP = 16
S5_GROUPS = 32
S5_STATE = 64
S5_CHUNK = 16

LANES = 128
TOK = 256
KV_CHUNK = 256
SWA_BLOCK = 128
MOD_ROWS = 16
PROJ_NB = 2
SWA_UNROLL = 32
OUT_COLS = 256
PROJ_SKEW = 1

C_CQ = 0
C_CKV = 384
C_KR = 640
C_SQ = 768
C_SK = 1280
C_SV = 1536
C_U = 1792
C_GATE = 2304
C_END = 3840

VMEM_LIMIT = 56 * 1024 * 1024


def _cparams(n_axes):
    return pltpu.CompilerParams(dimension_semantics=("arbitrary",) * n_axes,
                                vmem_limit_bytes=VMEM_LIMIT)


def _ada_kernel(c_ref, w_ref, b_ref, o_ref):
    cc = c_ref[...]
    s = cc * jax.nn.sigmoid(cc)
    o_ref[0] = jnp.dot(s, w_ref[0], preferred_element_type=F32,
                       precision=lax.Precision.HIGHEST) + b_ref[0]


def _ada_call(cc, w_ada, b_ada):
    depth, d, n3 = w_ada.shape
    tn = 768
    return pl.pallas_call(
        _ada_kernel,
        out_shape=jax.ShapeDtypeStruct((depth, MOD_ROWS, n3), F32),
        grid=(depth, n3 // tn),
        in_specs=[pl.BlockSpec((MOD_ROWS, d), lambda l, j: (0, 0)),
                  pl.BlockSpec((1, d, tn), lambda l, j: (l, 0, j)),
                  pl.BlockSpec((1, 1, tn), lambda l, j: (l, 0, j))],
        out_specs=pl.BlockSpec((1, MOD_ROWS, tn), lambda l, j: (l, 0, j)),
        compiler_params=_cparams(2),
        name="ada",
    )(cc, w_ada, b_ada.reshape(depth, 1, n3))


def _sigmoid(v):
    return 0.5 * jnp.tanh(0.5 * v) + 0.5


def _norm_rope(slots, lo_masks, dim, gain, tab_ref, half, dup=False):
    sq = [s * s for s in slots]
    sums = []
    for s2, lo in zip(sq, lo_masks):
        if lo is None:
            sums.append((jnp.sum(s2, axis=-1, keepdims=True),))
        elif dup:
            sums.append((jnp.sum(jnp.where(lo, s2, 0.0), axis=-1, keepdims=True),))
        else:
            sums.append((jnp.sum(jnp.where(lo, s2, 0.0), axis=-1, keepdims=True),
                         jnp.sum(jnp.where(lo, 0.0, s2), axis=-1, keepdims=True)))
    ys = []
    for s, ss, lo in zip(slots, sums, lo_masks):
        rs = [lax.rsqrt(v * (1.0 / dim) + EPS) for v in ss]
        r = rs[0] if len(rs) == 1 else jnp.where(lo, rs[0], rs[1])
        ys.append(s * r * gain)
    up = [pltpu.roll(y, half, 1) for y in ys]
    dn = [pltpu.roll(y, LANES - half, 1) for y in ys]
    return [(y * tab_ref[0] + u * tab_ref[1] + d * tab_ref[2]).astype(BF16) for y, u, d in zip(ys, up, dn)]


N_PROJ_DATA = 3
N_PROJ_WEIGHTS = 12


def _proj_kernel(*refs, n_lat_tiles):
    data = refs[:N_PROJ_DATA]
    shared = refs[N_PROJ_DATA:N_PROJ_DATA + N_PROJ_WEIGHTS]
    outs = refs[N_PROJ_DATA + N_PROJ_WEIGHTS:]
    streams = []
    for i in range(data[0].shape[0]):
        one = pl.ds(i, 1)
        streams.append(_proj_tile(data[0].at[one], data[1].at[one], data[2].at[:, one], *shared,
                                  *[o.at[one] for o in outs], n_lat_tiles=n_lat_tiles))
    pending = list(enumerate(streams))
    step = 0
    while pending:
        for item in list(pending):
            if step >= item[0] * PROJ_SKEW and next(item[1], "done") == "done":
                pending.remove(item)
        step += 1


def _proj_tile(x_ref, c_ref, mod_ref, ng_ref, win_ref, gcq_ref, wuq_ref, gq_ref,
               gckv_ref, wukv_ref, gk_ref, gsq_ref, gsk_ref, rm_ref, rs_ref,
               qa_ref, ka_ref, va_ref, qb_ref, kb_ref, vb_ref, u_ref, gt_ref, *, n_lat_tiles):
    t = pl.program_id(1)
    x = jnp.where(t == n_lat_tiles, c_ref[0], x_ref[0])
    mod = mod_ref[0, 0]
    y = x * lax.rsqrt(jnp.mean(x * x, axis=-1, keepdims=True) + EPS) * ng_ref[0]
    xn = (y * (1.0 + mod[1:2]) + mod[0:1]).astype(BF16)

    def seg(a, b):
        return jnp.dot(xn, win_ref[0, :, a:b], preferred_element_type=F32)

    lane = lax.broadcasted_iota(jnp.int32, (TOK, LANES), 1)
    lo = lane < 64

    yield
    cq = seg(C_CQ, C_CKV)
    ckv = seg(C_CKV, C_KR)
    kr = seg(C_KR, C_SQ)
    yield
    cqn = (cq * lax.rsqrt(jnp.mean(cq * cq, axis=-1, keepdims=True) + EPS) * gcq_ref[0]).astype(BF16)
    qf = jnp.dot(cqn, wuq_ref[0], preferred_element_type=F32)
    ckvn = (ckv * lax.rsqrt(jnp.mean(ckv * ckv, axis=-1, keepdims=True) + EPS) * gckv_ref[0]).astype(BF16)
    kvf = jnp.dot(ckvn, wukv_ref[0], preferred_element_type=F32)
    sq = seg(C_SQ, C_SK)
    yield
    slots = [qf[:, h * LANES:(h + 1) * LANES] for h in range(MLA_HEADS)]
    for h, o in enumerate(_norm_rope(slots, [None] * MLA_HEADS, MLA_QK, gq_ref[0], rm_ref, MLA_ROPE // 2)):
        qa_ref[0, h] = o

    sk = seg(C_SK, C_SV)
    vb_ref[0] = seg(C_SV, C_U).astype(BF16)
    yield
    slots = [kvf[:, h * LANES:(h + 1) * LANES] + kr for h in range(MLA_HEADS)]
    for h, o in enumerate(_norm_rope(slots, [None] * MLA_HEADS, MLA_QK, gk_ref[0], rm_ref, MLA_ROPE // 2)):
        ka_ref[0, h] = o
    va_ref[0] = kvf[:, MLA_HEADS * LANES:].astype(BF16)
    u_ref[0] = seg(C_U, C_GATE)
    yield
    slots = [sq[:, p * LANES:(p + 1) * LANES] for p in range(SWA_HEADS // 2)]
    for p, o in enumerate(_norm_rope(slots, [lo] * len(slots), SWA_DIM, gsq_ref[0], rs_ref, SWA_DIM // 2)):
        qb_ref[0, :, p * LANES:(p + 1) * LANES] = o
    slots = [sk[:, j * LANES:(j + 1) * LANES] for j in range(SWA_KV_HEADS)]
    for j, o in enumerate(_norm_rope(slots, [lo] * len(slots), SWA_DIM, gsk_ref[0], rs_ref, SWA_DIM // 2, dup=True)):
        kb_ref[0, :, j * LANES:(j + 1) * LANES] = o
    g = seg(C_GATE, C_END)
    yield
    gt_ref[0] = (g * _sigmoid(g)).astype(BF16)


def _proj_call(l, x, ctx, mod, P, rope_mla, rope_swa):
    B, L, D = x.shape
    C = ctx.shape[1]
    S = C + L
    nt = S // TOK

    def wspec(arr):
        shp = arr.shape
        return pl.BlockSpec((1,) + shp[1:], lambda b, t: (l,) + (0,) * (len(shp) - 1))

    weights = [P["norm_g"], P["w_in"], P["g_cq"], P["w_uq"], P["g_q"],
               P["g_ckv"], P["w_ukv"], P["g_k"], P["g_sq"], P["g_sk"]]
    nl = L // TOK
    nb = PROJ_NB
    assert len(weights) + 2 == N_PROJ_WEIGHTS and B % nb == 0
    in_specs = ([pl.BlockSpec((nb, TOK, D), lambda b, t: (b, jnp.minimum(t, nl - 1), 0)),
                 pl.BlockSpec((nb, TOK, D), lambda b, t: (b, 0, 0)),
                 pl.BlockSpec((1, nb, 3, D), lambda b, t: (l, jnp.where(t == nl, B // nb, b), 0, 0))]
                + [wspec(w) for w in weights]
                + [pl.BlockSpec((3, TOK, LANES), lambda b, t: (0, t, 0)),
                   pl.BlockSpec((3, TOK, LANES), lambda b, t: (0, t, 0))])
    out_shape = [jax.ShapeDtypeStruct((B, MLA_HEADS, S, LANES), BF16),
                 jax.ShapeDtypeStruct((B, MLA_HEADS, S, LANES), BF16),
                 jax.ShapeDtypeStruct((B, S, 512), BF16),
                 jax.ShapeDtypeStruct((B, S, 512), BF16),
                 jax.ShapeDtypeStruct((B, S, 256), BF16),
                 jax.ShapeDtypeStruct((B, S, 256), BF16),
                 jax.ShapeDtypeStruct((B, S, 512), F32),
                 jax.ShapeDtypeStruct((B, S, 1536), BF16)]
    out_specs = [pl.BlockSpec((nb, MLA_HEADS, TOK, LANES), lambda b, t: (b, 0, t, 0)),
                 pl.BlockSpec((nb, MLA_HEADS, TOK, LANES), lambda b, t: (b, 0, t, 0))]
    out_specs += [pl.BlockSpec((nb, TOK, s.shape[2]), lambda b, t: (b, t, 0)) for s in out_shape[2:]]
    return pl.pallas_call(
        functools.partial(_proj_kernel, n_lat_tiles=nl), out_shape=out_shape, grid=(B // nb, nt),
        in_specs=in_specs, out_specs=out_specs,
        compiler_params=_cparams(2), name="proj",
    )(x, ctx, mod, *weights, rope_mla, rope_swa)


MLA_SUBTILES = 2
MLA_VT_ROWS = 80


def _mla_kernel(q_ref, k_ref, v_ref, o_ref, vt_ref, *, n_chunks, tq):
    ones = jnp.ones((MLA_VT_ROWS - MLA_V, KV_CHUNK), BF16)

    @pl.when(pl.program_id(2) == 0)
    def _():
        for c in range(n_chunks):
            vt = v_ref[0, c].astype(F32).T.astype(BF16)
            vt_ref[0, c] = jnp.concatenate([vt[:MLA_V], ones], axis=0)
            vt_ref[1, c] = jnp.concatenate([vt[MLA_V:], ones], axis=0)

    for r0 in range(0, q_ref.shape[2], tq):
        rows = slice(r0, r0 + tq)
        qts = [q_ref[0, e, rows, :].astype(F32).T.astype(BF16) for e in range(2)]
        score = lambda c, e: jnp.dot(k_ref[0, e, c], qts[e], preferred_element_type=F32)
        ms = [None, None]
        accs = [None, None]
        sts = [score(0, e) for e in range(2)]
        for c in range(n_chunks):
            nxt = [score(c + 1, e) for e in range(2)] if c + 1 < n_chunks else None
            for e in range(2):
                mc = jnp.max(sts[e], axis=0, keepdims=True)
                m_new = mc if c == 0 else jnp.maximum(ms[e], mc)
                pt = jnp.exp2(sts[e] - m_new).astype(BF16)
                pv = jnp.dot(vt_ref[e, c], pt, preferred_element_type=F32)
                accs[e] = pv if c == 0 else accs[e] * jnp.exp2(ms[e] - m_new) + pv
                ms[e] = m_new
            sts = nxt
        ot = jnp.concatenate([accs[e][:MLA_V] / accs[e][MLA_V:MLA_V + 1] for e in range(2)], axis=0)
        o_ref[0, rows, :] = ot.T.astype(BF16)


def _mla_call(qa, ka, va, n_lat):
    B, H, S, _ = qa.shape
    nc = S // KV_CHUNK
    tq = 2 * TOK
    bq = MLA_SUBTILES * tq if n_lat % (MLA_SUBTILES * tq) == 0 else tq
    k5 = ka.reshape(B, H, nc, KV_CHUNK, LANES)
    v4 = va.reshape(B, nc, KV_CHUNK, 512)
    o_lat = pl.pallas_call(
        functools.partial(_mla_kernel, n_chunks=nc, tq=tq),
        out_shape=jax.ShapeDtypeStruct((B, n_lat, 512), BF16),
        grid=(B, H // 2, n_lat // bq),
        in_specs=[pl.BlockSpec((1, 2, bq, LANES), lambda b, p, t: (b, p, t, 0)),
                  pl.BlockSpec((1, 2, nc, KV_CHUNK, LANES), lambda b, p, t: (b, p, 0, 0, 0)),
                  pl.BlockSpec((1, nc, KV_CHUNK, LANES), lambda b, p, t: (b, 0, 0, p))],
        out_specs=pl.BlockSpec((1, bq, LANES), lambda b, p, t: (b, t, p)),
        scratch_shapes=[pltpu.VMEM((2, nc, MLA_VT_ROWS, KV_CHUNK), BF16)],
        compiler_params=_cparams(3), name="mla_attn",
    )(qa, k5, v4)
    cblk = n_lat // KV_CHUNK
    o_ctx = pl.pallas_call(
        functools.partial(_mla_kernel, n_chunks=1, tq=TOK),
        out_shape=jax.ShapeDtypeStruct((B, S - n_lat, 512), BF16),
        grid=(B, H // 2, 1),
        in_specs=[pl.BlockSpec((1, 2, TOK, LANES), lambda b, p, t: (b, p, cblk, 0)),
                  pl.BlockSpec((1, 2, 1, KV_CHUNK, LANES), lambda b, p, t: (b, p, cblk, 0, 0)),
                  pl.BlockSpec((1, 1, KV_CHUNK, LANES), lambda b, p, t: (b, cblk, 0, p))],
        out_specs=pl.BlockSpec((1, TOK, LANES), lambda b, p, t: (b, 0, p)),
        scratch_shapes=[pltpu.VMEM((2, 1, MLA_VT_ROWS, KV_CHUNK), BF16)],
        compiler_params=_cparams(3), name="mla_attn_ctx",
    )(qa, k5, v4)
    return o_lat, o_ctx


SWA_VT_ROWS = 80


def _swa_kernel(q_ref, k_ref, v_ref, sink_ref, o_ref, vt_ref, bias_ref, *, n_ctx, n_lat):
    blk = SWA_BLOCK
    heads = SWA_HEADS // SWA_KV_HEADS
    win = 3 * blk
    n_blocks = n_lat // blk
    ctx_blk = n_blocks
    group = min(SWA_UNROLL, n_blocks)
    sink = sink_ref[0, 0:1, :]

    ones = jnp.ones((SWA_VT_ROWS - SWA_DIM, blk), BF16)
    for i in range((n_lat + n_ctx) // blk):
        vt = v_ref[0, i * blk:(i + 1) * blk, :].astype(F32).T
        vt_ref[i] = jnp.concatenate([vt[:SWA_DIM].astype(BF16), ones], axis=0)
    rel0 = (lax.broadcasted_iota(jnp.int32, (win, heads * blk), 0)
            - (lax.broadcasted_iota(jnp.int32, (win, heads * blk), 1) & (blk - 1)))
    for kind in range(3):
        bias_ref[kind] = jnp.where(jnp.abs(rel0 - kind * blk) <= SWA_WINDOW, 0.0, NEG)

    kc = k_ref[0, n_lat:n_lat + n_ctx, :]
    vtc = jnp.concatenate([vt_ref[ctx_blk + i] for i in range(n_ctx // blk)], axis=1)
    zrows = jnp.zeros((LANES - SWA_DIM, heads * blk), BF16)

    def qmat(r0):
        qt = q_ref[0, pl.ds(r0, blk), :].astype(F32).T
        w = jnp.concatenate([qt[h * SWA_DIM:(h + 1) * SWA_DIM] for h in range(heads)], axis=1)
        return jnp.concatenate([w.astype(BF16), zrows], axis=0)

    def finish(r0, m, acc):
        ot = acc[:SWA_DIM] / (acc[SWA_DIM:SWA_DIM + 1] + jnp.exp2(sink - m))
        o4 = jnp.concatenate([ot[:, h * blk:(h + 1) * blk] for h in range(heads)], axis=0)
        o_ref[0, pl.ds(r0, blk), :] = o4.T.astype(BF16)

    for n in range(n_ctx // blk):
        r0 = n_lat + n * blk
        s_c = jnp.dot(kc, qmat(r0), preferred_element_type=F32)
        m = jnp.maximum(jnp.max(s_c, axis=0, keepdims=True), sink)
        finish(r0, m, jnp.dot(vtc, jnp.exp2(s_c - m).astype(BF16), preferred_element_type=F32))

    def window_block(n):
        return jnp.clip(n - 1, 0, n_blocks - 3)

    def scores(n):
        wb = window_block(n)
        w = qmat(pl.multiple_of(n * blk, blk))
        kw = k_ref[0, pl.ds(pl.multiple_of(wb * blk, blk), win), :]
        return (jnp.dot(kc, w, preferred_element_type=F32),
                jnp.dot(kw, w, preferred_element_type=F32) + bias_ref[n - wb])

    def blocks(gi, carry):
        n0 = gi * group
        cur = scores(n0)
        for i in range(group):
            n = n0 + i
            nxt = scores(n + 1) if i + 1 < group else None
            s_c, s_w = cur
            m = jnp.maximum(jnp.maximum(jnp.max(s_c, axis=0, keepdims=True),
                                        jnp.max(s_w, axis=0, keepdims=True)), sink)
            wb = window_block(n)
            vtw = jnp.concatenate([vt_ref[wb + j] for j in range(3)], axis=1)
            acc = (jnp.dot(vtc, jnp.exp2(s_c - m).astype(BF16), preferred_element_type=F32)
                   + jnp.dot(vtw, jnp.exp2(s_w - m).astype(BF16), preferred_element_type=F32))
            finish(pl.multiple_of(n * blk, blk), m, acc)
            cur = nxt
        return carry

    assert n_blocks % group == 0
    lax.fori_loop(0, n_blocks // group, blocks, 0)


def _swa_call(l, qb, kb, vb, sinkrow, n_ctx):
    B, S, _ = qb.shape
    nq = (SWA_HEADS // SWA_KV_HEADS) * SWA_BLOCK
    return pl.pallas_call(
        functools.partial(_swa_kernel, n_ctx=n_ctx, n_lat=S - n_ctx),
        out_shape=jax.ShapeDtypeStruct((B, S, 512), BF16),
        grid=(B, SWA_KV_HEADS),
        in_specs=[pl.BlockSpec((1, S, 256), lambda b, j: (b, 0, j)),
                  pl.BlockSpec((1, S, LANES), lambda b, j: (b, 0, j)),
                  pl.BlockSpec((1, S, LANES), lambda b, j: (b, 0, j)),
                  pl.BlockSpec((1, 8, nq), lambda b, j: (l * SWA_KV_HEADS + j, 0, 0))],
        out_specs=pl.BlockSpec((1, S, 256), lambda b, j: (b, 0, j)),
        scratch_shapes=[pltpu.VMEM((S // SWA_BLOCK, SWA_VT_ROWS, SWA_BLOCK), BF16),
                        pltpu.VMEM((3, 3 * SWA_BLOCK, nq), F32)],
        compiler_params=_cparams(2), name="swa_attn",
    )(qb, kb, vb, sinkrow)


def _gelu(y):
    return 0.5 * y * (1.0 + jnp.tanh(math.sqrt(2.0 / math.pi) * (y + 0.044715 * (y * y * y))))


def _s5_kernel(u_ref, wst_ref, wloc_ref, wcar_ref, at_ref, o_ref,
               ut_ref, utc_ref, yt_ref, ytc_ref, ere_ref, eim_ref, hfr_ref, hfi_ref, hbr_ref, hbi_ref,
               *, nb, n_lat_chunks, n_ctx_chunks):
    T, GP = S5_CHUNK, S5_GROUP
    n_chunks = n_lat_chunks + n_ctx_chunks
    n_lat = n_lat_chunks * T
    cw = nb * n_ctx_chunks
    tn_dims = (((0,), (0,)), ((), ()))
    nt_dims = (((1,), (1,)), ((), ()))
    gpl = LANES // GP

    for b in range(nb):
        for s in range(T):
            xs = u_ref[b, pl.ds(s, n_lat_chunks, stride=T), :]
            ut_ref[b, :, s] = xs.T.reshape(gpl, GP, n_lat_chunks).astype(BF16)
    zpad = jnp.zeros((LANES - cw, LANES), F32)
    for s in range(T):
        xs = jnp.concatenate([u_ref[b, pl.ds(n_lat + s, n_ctx_chunks, stride=T), :] for b in range(nb)]
                             + [zpad], axis=0)
        utc_ref[:, s] = xs.T.reshape(gpl, GP, LANES).astype(BF16)

    lane = lax.broadcasted_iota(jnp.int32, (2 * nb, LANES), 1)
    fwd = lane < S5_STATE
    lane_l = lax.broadcasted_iota(jnp.int32, (n_lat_chunks, LANES), 1) < S5_STATE
    lane_c = lax.broadcasted_iota(jnp.int32, (LANES, LANES), 1) < S5_STATE

    slab = 2 * nb
    lat_rows = lambda gb: pl.ds(gb, n_lat_chunks, stride=slab)
    ctx_rows = lambda gb: pl.ds(n_lat_chunks * slab + gb, n_ctx_chunks, stride=slab)

    def pair(gp, carry):
        for gl in range(2):
            g = gp * 2 + gl
            wst = wst_ref[g]
            for b in range(nb):
                e = lax.dot_general(ut_ref[b, g].reshape(T * GP, n_lat_chunks), wst, tn_dims,
                                    preferred_element_type=F32)
                ere_ref[lat_rows(gl * nb + b), :] = e[:, :LANES]
                eim_ref[lat_rows(gl * nb + b), :] = e[:, LANES:]
            ec = lax.dot_general(utc_ref[g].reshape(T * GP, LANES), wst, tn_dims, preferred_element_type=F32)
            for b in range(nb):
                ere_ref[ctx_rows(gl * nb + b), :] = ec[b * n_ctx_chunks:(b + 1) * n_ctx_chunks, :LANES]
                eim_ref[ctx_rows(gl * nb + b), :] = ec[b * n_ctx_chunks:(b + 1) * n_ctx_chunks, LANES:]

        a_re = jnp.concatenate([jnp.broadcast_to(at_ref[gp * 2 + gl, 0:1, :], (nb, LANES)) for gl in range(2)], 0)
        a_im = jnp.concatenate([jnp.broadcast_to(at_ref[gp * 2 + gl, 1:2, :], (nb, LANES)) for gl in range(2)], 0)

        def step(i, hc):
            h_re, h_im = hc
            cf = jnp.where(i < n_ctx_chunks, n_lat_chunks + i, i - n_ctx_chunks)
            cb = n_chunks - 1 - i
            sf = pl.ds(pl.multiple_of(cf * slab, slab), slab)
            sb = pl.ds(pl.multiple_of(cb * slab, slab), slab)
            hfr_ref[sf, :] = h_re
            hfi_ref[sf, :] = h_im
            hbr_ref[sb, :] = h_re
            hbi_ref[sb, :] = h_im
            e_re = jnp.where(fwd, ere_ref[sf, :], ere_ref[sb, :])
            e_im = jnp.where(fwd, eim_ref[sf, :], eim_ref[sb, :])
            return (a_re * h_re - a_im * h_im + e_re, a_re * h_im + a_im * h_re + e_im)

        zero = jnp.zeros((2 * nb, LANES), F32)
        lax.fori_loop(0, n_chunks, step, (zero, zero), unroll=8)

        for gl in range(2):
            g = gp * 2 + gl
            wloc, wcar = wloc_ref[g], wcar_ref[g]
            for b in range(nb):
                rows = lat_rows(gl * nb + b)
                h_cat = jnp.concatenate([jnp.where(lane_l, hfr_ref[rows, :], hbr_ref[rows, :]),
                                         jnp.where(lane_l, hfi_ref[rows, :], hbi_ref[rows, :])],
                                        axis=-1).astype(BF16)
                yt = (jnp.dot(wloc, ut_ref[b, g].reshape(T * GP, n_lat_chunks), preferred_element_type=F32)
                      + lax.dot_general(wcar, h_cat, nt_dims, preferred_element_type=F32))
                yt_ref[b, g] = _gelu(yt).astype(BF16).reshape(T, GP, n_lat_chunks)
            crow = lambda ref: jnp.concatenate(
                [ref[ctx_rows(gl * nb + b), :] for b in range(nb)]
                + [jnp.zeros((LANES - cw, LANES), F32)], axis=0)
            h_cat = jnp.concatenate([jnp.where(lane_c, crow(hfr_ref), crow(hbr_ref)),
                                     jnp.where(lane_c, crow(hfi_ref), crow(hbi_ref))], axis=-1).astype(BF16)
            ytc = (jnp.dot(wloc, utc_ref[g].reshape(T * GP, LANES), preferred_element_type=F32)
                   + lax.dot_general(wcar, h_cat, nt_dims, preferred_element_type=F32))
            ytc_ref[g] = _gelu(ytc).astype(BF16).reshape(T, GP, LANES)
        return carry

    lax.fori_loop(0, gpl // 2, pair, 0)

    for b in range(nb):
        for t in range(T):
            z = yt_ref[b, :, t].astype(F32).reshape(LANES, n_lat_chunks)
            o_ref[b, pl.ds(t, n_lat_chunks, stride=T), :] = z.T
    for t in range(T):
        z = ytc_ref[:, t].astype(F32).reshape(LANES, LANES).T
        for b in range(nb):
            o_ref[b, pl.ds(n_lat + t, n_ctx_chunks, stride=T), :] = z[b * n_ctx_chunks:(b + 1) * n_ctx_chunks]


def _s5_call(l, u, P, n_lat):
    B, S, W = u.shape
    nb = 4 if B % 4 == 0 else B
    T, GP = S5_CHUNK, S5_GROUP
    nlc, ncc = n_lat // T, (S - n_lat) // T
    gpl = LANES // GP
    nblk = W // LANES
    big = lambda n: pl.BlockSpec((nb, S, LANES), lambda j, hb: (hb, 0, j), pipeline_mode=pl.Buffered(n))
    wspec = lambda: pl.BlockSpec((gpl, T * GP, T * GP), lambda j, hb: (l * nblk + j, 0, 0))
    rows = 2 * nb * (nlc + ncc)
    return pl.pallas_call(
        functools.partial(_s5_kernel, nb=nb, n_lat_chunks=nlc, n_ctx_chunks=ncc),
        out_shape=jax.ShapeDtypeStruct((B, S, W), F32),
        grid=(nblk, B // nb),
        in_specs=[big(2), wspec(), wspec(), wspec(),
                  pl.BlockSpec((gpl, 2, LANES), lambda j, hb: (l * nblk + j, 0, 0))],
        out_specs=big(1),
        scratch_shapes=[pltpu.VMEM((nb, gpl, T, GP, nlc), BF16), pltpu.VMEM((gpl, T, GP, LANES), BF16),
                        pltpu.VMEM((nb, gpl, T, GP, nlc), BF16), pltpu.VMEM((gpl, T, GP, LANES), BF16)]
                       + [pltpu.VMEM((rows, LANES), F32)] * 6,
        compiler_params=_cparams(2), name="s5",
    )(u, P["s5_wst"], P["s5_wloc_t"], P["s5_wcar_t"], P["s5_at"])


def _out_kernel(x_ref, c_ref, mod_ref, oa_ref, oac_ref, ob_ref, gy_ref, gt_ref, wglu_ref, wout_ref,
                xo_ref, co_ref, *, n_lat_tiles):
    t = pl.program_id(1)
    cb = OUT_COLS
    nb, rows = x_ref.shape[0], x_ref.shape[1]
    stack = lambda ref: jnp.concatenate([ref[i] for i in range(nb)], axis=0)
    gyb = stack(gy_ref).astype(BF16)
    g = stack(gt_ref).astype(F32)
    oa = jnp.where(t == n_lat_tiles, stack(oac_ref), stack(oa_ref))
    m_a = (oa.astype(F32) * g[:, 0:512]).astype(BF16)
    m_b = (stack(ob_ref).astype(F32) * g[:, 512:1024]).astype(BF16)
    oc = []
    for j in range(512 // cb):
        za = jnp.dot(gyb, wglu_ref[0, :, j * cb:(j + 1) * cb], preferred_element_type=F32)
        zb = jnp.dot(gyb, wglu_ref[0, :, 512 + j * cb:512 + (j + 1) * cb], preferred_element_type=F32)
        oc.append((za * _sigmoid(zb) * g[:, 1024 + j * cb:1024 + (j + 1) * cb]).astype(BF16))
    m_c = jnp.concatenate(oc, axis=-1)
    gate = jnp.concatenate([jnp.broadcast_to(mod_ref[0, i][2:3], (rows, x_ref.shape[2])) for i in range(nb)],
                           axis=0)
    resid = jnp.where(t == n_lat_tiles, stack(c_ref), stack(x_ref))
    new = []
    for j in range(resid.shape[1] // cb):
        cols = slice(j * cb, (j + 1) * cb)
        upd = (jnp.dot(m_a, wout_ref[0, 0:512, cols], preferred_element_type=F32)
               + jnp.dot(m_b, wout_ref[0, 512:1024, cols], preferred_element_type=F32)
               + jnp.dot(m_c, wout_ref[0, 1024:1536, cols], preferred_element_type=F32))
        new.append(resid[:, cols] + gate[:, cols] * upd)

    @pl.when(t == n_lat_tiles)
    def _():
        for j, v in enumerate(new):
            for i in range(nb):
                co_ref[i, :, j * cb:(j + 1) * cb] = v[i * rows:(i + 1) * rows]

    @pl.when(t < n_lat_tiles)
    def _():
        for j, v in enumerate(new):
            for i in range(nb):
                xo_ref[i, :, j * cb:(j + 1) * cb] = v[i * rows:(i + 1) * rows]


def _out_call(l, x, ctx, mod, oa, oa_ctx, ob, gy, gt, P):
    B, L, D = x.shape
    C = ctx.shape[1]
    S = C + L
    nl = L // TOK
    nb = PROJ_NB
    xmap = lambda b, t: (b, jnp.minimum(t, nl - 1), 0)
    cmap = lambda b, t: (b, 0, 0)
    tmap = lambda b, t: (b, t, 0)
    return pl.pallas_call(
        functools.partial(_out_kernel, n_lat_tiles=nl),
        out_shape=[jax.ShapeDtypeStruct(x.shape, F32), jax.ShapeDtypeStruct(ctx.shape, F32)],
        grid=(B // nb, S // TOK),
        in_specs=[pl.BlockSpec((nb, TOK, D), xmap),
                  pl.BlockSpec((nb, TOK, D), cmap),
                  pl.BlockSpec((1, nb, 3, D), lambda b, t: (l, jnp.where(t == nl, B // nb, b), 0, 0)),
                  pl.BlockSpec((nb, TOK, 512), xmap),
                  pl.BlockSpec((nb, TOK, 512), cmap),
                  pl.BlockSpec((nb, TOK, 512), tmap),
                  pl.BlockSpec((nb, TOK, 512), tmap),
                  pl.BlockSpec((nb, TOK, 1536), tmap),
                  pl.BlockSpec((1, 512, 1024), lambda b, t: (l, 0, 0)),
                  pl.BlockSpec((1, 1536, D), lambda b, t: (l, 0, 0))],
        out_specs=[pl.BlockSpec((nb, TOK, D), xmap), pl.BlockSpec((nb, TOK, D), cmap)],
        compiler_params=_cparams(2), name="out",
    )(x, ctx, mod, oa, oa_ctx, ob, gy, gt, P["w_glu"], P["w_out"])


def _rope_tables(n_lat, n_ctx, rot_dim, lead, reps):
    rows = n_lat // GRID_W
    r_idx, c_idx = jnp.meshgrid(jnp.arange(rows), jnp.arange(GRID_W), indexing="ij")
    r_idx, c_idx = r_idx.reshape(-1), c_idx.reshape(-1)
    n_freq = rot_dim // 4
    freqs = ROPE_BASE ** (-jnp.arange(n_freq, dtype=F32) / n_freq)
    ang = jnp.concatenate([r_idx.astype(F32)[:, None] * freqs,
                           c_idx.astype(F32)[:, None] * freqs], axis=-1)
    ang = jnp.concatenate([ang, jnp.zeros((n_ctx, rot_dim // 2), F32)], axis=0)
    cos, sin, zero = jnp.cos(ang), jnp.sin(ang), jnp.zeros_like(ang)
    n = ang.shape[0]
    tail = LANES // reps - lead - rot_dim

    def pack(x1, x2, fill):
        unit = [jnp.full((n, lead), fill, F32), x1, x2, jnp.full((n, tail), fill, F32)]
        return jnp.concatenate(unit * reps, axis=-1)

    return jnp.stack([pack(cos, cos, 1.0), pack(zero, sin, 0.0), pack(-sin, zero, 0.0)])


def _prep_params(norm_g, w_in, w_out, mla_g_cq, mla_g_ckv, mla_w_uq, mla_w_ukv, mla_g_qn, mla_g_kn,
                 swa_g_qn, swa_g_kn, swa_sink, s5_a_re, s5_a_im, s5_log_dt, s5_b_re, s5_b_im,
                 s5_c_re, s5_c_im, s5_d, s5_w_glu):
    depth, D, _ = w_in.shape
    o_cq, o_ckv, o_kr, o_gm, o_sq, o_sk, o_sv, o_gs, o_u, o_g5, o_end = (
        0, 384, 640, 672, 1184, 1696, 1824, 1952, 2464, 2976, 3488)
    z = lambda n: jnp.zeros((depth, D, n), F32)
    sk0, sk1 = w_in[:, :, o_sk:o_sk + 64], w_in[:, :, o_sk + 64:o_sv]
    sv0, sv1 = w_in[:, :, o_sv:o_sv + 64], w_in[:, :, o_sv + 64:o_gs]
    w_in_p = jnp.concatenate([
        w_in[:, :, o_cq:o_kr],
        z(64), w_in[:, :, o_kr:o_gm], z(32),
        w_in[:, :, o_sq:o_sk],
        sk0, sk0, sk1, sk1, sv0, sv0, sv1, sv1,
        w_in[:, :, o_u:o_g5],
        w_in[:, :, o_gm:o_sq], w_in[:, :, o_gs:o_u], w_in[:, :, o_g5:o_end],
    ], axis=-1).astype(BF16)
    assert w_in_p.shape[-1] == C_END

    wq = mla_w_uq.reshape(depth, MLA_Q_RANK, MLA_HEADS, MLA_QK)
    wq = jnp.pad(wq, ((0, 0), (0, 0), (0, 0), (0, LANES - MLA_QK)))
    w_uq_p = wq.reshape(depth, MLA_Q_RANK, MLA_HEADS * LANES).astype(BF16)
    wkv = mla_w_ukv.reshape(depth, MLA_KV_RANK, MLA_HEADS, MLA_NOPE + MLA_V)
    wk = jnp.pad(wkv[..., :MLA_NOPE], ((0, 0), (0, 0), (0, 0), (0, LANES - MLA_NOPE)))
    w_ukv_p = jnp.concatenate([wk.reshape(depth, MLA_KV_RANK, MLA_HEADS * LANES),
                               wkv[..., MLA_NOPE:].reshape(depth, MLA_KV_RANK, MLA_HEADS * MLA_V)],
                              axis=-1).astype(BF16)

    pad_qk = lambda g: jnp.pad(g, ((0, 0), (0, LANES - MLA_QK)))[:, None, :]
    g_q = pad_qk(mla_g_qn * (MLA_QK ** -0.5 * LOG2E))
    g_k = pad_qk(mla_g_kn)
    g_sq = jnp.tile(swa_g_qn * (SWA_DIM ** -0.5 * LOG2E), (1, 2))[:, None, :]
    g_sk = jnp.tile(swa_g_kn, (1, 2))[:, None, :]
    sink = (swa_sink * LOG2E).reshape(depth * SWA_KV_HEADS, 1, SWA_HEADS // SWA_KV_HEADS, 1)
    sinkrow = jnp.broadcast_to(sink, (depth * SWA_KV_HEADS, 8, SWA_HEADS // SWA_KV_HEADS, SWA_BLOCK))
    sinkrow = sinkrow.reshape(depth * SWA_KV_HEADS, 8, -1).astype(F32)

    T = S5_CHUNK
    A = lax.complex(s5_a_re, s5_a_im)
    dt = jnp.exp(s5_log_dt)[..., None]
    a_bar = jnp.exp(dt * A)
    b_bar = ((a_bar - 1.0) / A)[..., None] * lax.complex(s5_b_re, s5_b_im)
    c_mat = lax.complex(s5_c_re, s5_c_im)
    k_idx = jnp.arange(T + 1, dtype=F32)
    pw = jnp.exp(k_idx[None, None, None, :, None] * (dt * A)[:, :, :, None, :])
    hi = lax.Precision.HIGHEST
    tt = jnp.arange(T)
    GP, NG = S5_GROUP, depth * S5_GROUPS
    pw_f, pw_b = pw[:, 0], pw[:, 1]
    tap_f = jnp.einsum("lgpn,lgkn,lgnq->lgpkq", c_mat[:, 0], pw_f[:, :, T - 1 - tt], b_bar[:, 0],
                       precision=hi).real
    tap_b = jnp.einsum("lgpn,lgkn,lgnq->lgpkq", c_mat[:, 1], pw_b[:, :, tt], b_bar[:, 1],
                       precision=hi).real
    d_diag = s5_d.reshape(depth, S5_GROUPS, GP)[..., None] * jnp.eye(GP, dtype=F32)
    centre = tap_f[:, :, :, T - 1:] + tap_b[:, :, :, :1] + d_diag[:, :, :, None, :]
    krev = jnp.concatenate([tap_f[:, :, :, :T - 1], centre, tap_b[:, :, :, 1:]], axis=3)
    krev = krev.reshape(depth, S5_GROUPS, GP, (2 * T - 1) * GP)
    wloc_t = jnp.stack([krev[..., (T - 1 - t) * GP:(T - 1 - t) * GP + T * GP] for t in range(T)], axis=2)
    wloc_t = wloc_t.reshape(NG, T * GP, T * GP)
    b_t = jnp.swapaxes(b_bar, -1, -2)
    inc_f = pw_f[:, :, T - 1 - tt][:, :, :, None, :] * b_t[:, 0][:, :, None]
    inc_b = pw_b[:, :, tt][:, :, :, None, :] * b_t[:, 1][:, :, None]
    wst = jnp.concatenate([inc_f.real, inc_b.real, inc_f.imag, inc_b.imag], axis=-1)
    wst = wst.reshape(NG, T * GP, 4 * S5_STATE)
    ro_f = c_mat[:, 0][:, :, None] * pw_f[:, :, tt + 1][:, :, :, None, :]
    ro_b = c_mat[:, 1][:, :, None] * pw_b[:, :, T - tt][:, :, :, None, :]
    wcar_t = jnp.concatenate([ro_f.real, ro_b.real, -ro_f.imag, -ro_b.imag], axis=-1)
    wcar_t = wcar_t.reshape(NG, T * GP, 4 * S5_STATE)
    a_t = pw[:, :, :, T]
    at = jnp.stack([jnp.concatenate([a_t[:, 0].real, a_t[:, 1].real], axis=-1),
                    jnp.concatenate([a_t[:, 0].imag, a_t[:, 1].imag], axis=-1)], axis=2)
    at = at.reshape(NG, 2, LANES).astype(F32)

    return dict(norm_g=norm_g[:, None, :], w_in=w_in_p, g_cq=mla_g_cq[:, None, :], w_uq=w_uq_p, g_q=g_q,
                g_ckv=mla_g_ckv[:, None, :], w_ukv=w_ukv_p, g_k=g_k, g_sq=g_sq, g_sk=g_sk,
                sinkrow=sinkrow, s5_wloc_t=wloc_t.astype(BF16), s5_wst=wst.astype(BF16),
                s5_wcar_t=wcar_t.astype(BF16), s5_at=at, w_glu=s5_w_glu.astype(BF16),
                w_out=w_out.astype(BF16))


def kernel(x, c, ctx, c_ctx, norm_g, w_ada, b_ada, w_in, w_out, mla_g_cq, mla_g_ckv, mla_w_uq, mla_w_ukv, mla_g_qn, mla_g_kn, swa_g_qn, swa_g_kn, swa_sink, s5_a_re, s5_a_im, s5_log_dt, s5_b_re, s5_b_im, s5_c_re, s5_c_im, s5_d, s5_w_glu):
    B, L, D = x.shape
    C = ctx.shape[1]
    S = C + L
    depth = w_in.shape[0]
    assert B + PROJ_NB <= MOD_ROWS and C == TOK and L % (2 * TOK) == 0 and L % GRID_W == 0 and L >= 3 * SWA_BLOCK

    P = _prep_params(norm_g, w_in, w_out, mla_g_cq, mla_g_ckv, mla_w_uq, mla_w_ukv, mla_g_qn, mla_g_kn,
                     swa_g_qn, swa_g_kn, swa_sink, s5_a_re, s5_a_im, s5_log_dt, s5_b_re, s5_b_im,
                     s5_c_re, s5_c_im, s5_d, s5_w_glu)
    rope_mla = _rope_tables(L, C, MLA_ROPE, MLA_NOPE, 1)
    rope_swa = _rope_tables(L, C, SWA_DIM, 0, 2)

    cc = jnp.concatenate([c, jnp.tile(c_ctx[None, :], (PROJ_NB, 1)),
                          jnp.zeros((MOD_ROWS - B - PROJ_NB, D), F32)], axis=0)
    mod = _ada_call(cc, w_ada, b_ada).reshape(depth, MOD_ROWS, 3, D)

    for l in range(depth):
        qa, ka, va, qb, kb, vb, u, gt = _proj_call(l, x, ctx, mod, P, rope_mla, rope_swa)
        oa, oa_ctx = _mla_call(qa, ka, va, L)
        ob = _swa_call(l, qb, kb, vb, P["sinkrow"], C)
        gy = _s5_call(l, u, P, L)
        x, ctx = _out_call(l, x, ctx, mod, oa, oa_ctx, ob, gy, gt, P)
    return x
```

```python
import functools
import math

import jax
import jax.numpy as jnp
from jax import lax
from jax.experimental import pallas as pl
from jax.experimental.pallas import tpu as pltpu

F32 = jnp.float32
BF16 = jnp.bfloat16

GRID_W = 64
EPS = 1e-6
ROPE_BASE = 10000.0
NEG = -1e30
LOG2E = math.log2(math.e)

MLA_HEADS = 8
MLA_NOPE = 64
MLA_ROPE = 32
MLA_V = 64
MLA_QK = MLA_NOPE + MLA_ROPE
MLA_Q_RANK = 384
MLA_KV_RANK = 256

SWA_HEADS = 8
SWA_KV_HEADS = 2
SWA_DIM = 64
SWA_WINDOW = 128

S5_GROUP = 16
S5_GROUPS = 32
S5_STATE = 64
S5_CHUNK = 16

LANES = 128
TOK = 256
KV_CHUNK = 256
SWA_BLOCK = 128
MOD_ROWS = 16
PROJ_NB = 2
SWA_UNROLL = 32
OUT_COLS = 256
PROJ_SKEW = 1

C_CQ = 0
C_CKV = 384
C_KR = 640
C_SQ = 768
C_SK = 1280
C_SV = 1536
C_U = 1792
C_GATE = 2304
C_END = 3840

VMEM_LIMIT = 56 * 1024 * 1024


def _cparams(n_axes):
    return pltpu.CompilerParams(dimension_semantics=("arbitrary",) * n_axes,
                                vmem_limit_bytes=VMEM_LIMIT)


def _ada_kernel(c_ref, w_ref, b_ref, o_ref):
    cc = c_ref[...]
    s = cc * jax.nn.sigmoid(cc)
    o_ref[0] = jnp.dot(s, w_ref[0], preferred_element_type=F32,
                       precision=lax.Precision.HIGHEST) + b_ref[0]


def _ada_call(cc, w_ada, b_ada):
    depth, d, n3 = w_ada.shape
    tn = 768
    return pl.pallas_call(
        _ada_kernel,
        out_shape=jax.ShapeDtypeStruct((depth, MOD_ROWS, n3), F32),
        grid=(depth, n3 // tn),
        in_specs=[pl.BlockSpec((MOD_ROWS, d), lambda l, j: (0, 0)),
                  pl.BlockSpec((1, d, tn), lambda l, j: (l, 0, j)),
                  pl.BlockSpec((1, 1, tn), lambda l, j: (l, 0, j))],
        out_specs=pl.BlockSpec((1, MOD_ROWS, tn), lambda l, j: (l, 0, j)),
        compiler_params=_cparams(2),
        name="ada",
    )(cc, w_ada, b_ada.reshape(depth, 1, n3))


def _sigmoid(v):
    return 0.5 * jnp.tanh(0.5 * v) + 0.5


def _norm_rope(slots, lo_masks, dim, gain, tab_ref, half, dup=False):
    sq = [s * s for s in slots]
    sums = []
    for s2, lo in zip(sq, lo_masks):
        if lo is None:
            sums.append((jnp.sum(s2, axis=-1, keepdims=True),))
        elif dup:
            sums.append((jnp.sum(jnp.where(lo, s2, 0.0), axis=-1, keepdims=True),))
        else:
            sums.append((jnp.sum(jnp.where(lo, s2, 0.0), axis=-1, keepdims=True),
                         jnp.sum(jnp.where(lo, 0.0, s2), axis=-1, keepdims=True)))
    ys = []
    for s, ss, lo in zip(slots, sums, lo_masks):
        rs = [lax.rsqrt(v * (1.0 / dim) + EPS) for v in ss]
        r = rs[0] if len(rs) == 1 else jnp.where(lo, rs[0], rs[1])
        ys.append(s * r * gain)
    up = [pltpu.roll(y, half, 1) for y in ys]
    dn = [pltpu.roll(y, LANES - half, 1) for y in ys]
    return [(y * tab_ref[0] + u * tab_ref[1] + d * tab_ref[2]).astype(BF16) for y, u, d in zip(ys, up, dn)]


N_PROJ_DATA = 3
N_PROJ_WEIGHTS = 11


def _proj_kernel(*refs, n_lat_tiles):
    data = refs[:N_PROJ_DATA]
    shared = refs[N_PROJ_DATA:N_PROJ_DATA + N_PROJ_WEIGHTS]
    outs = refs[N_PROJ_DATA + N_PROJ_WEIGHTS:]
    streams = []
    for i in range(data[0].shape[0]):
        one = pl.ds(i, 1)
        streams.append(_proj_tile(data[0].at[one], data[1].at[one], data[2].at[:, one], *shared,
                                  *[o.at[one] for o in outs], n_lat_tiles=n_lat_tiles))
    pending = list(enumerate(streams))
    step = 0
    while pending:
        for item in list(pending):
            if step >= item[0] * PROJ_SKEW and next(item[1], "done") == "done":
                pending.remove(item)
        step += 1


def _proj_tile(x_ref, c_ref, mod_ref, ng_ref, win_ref, gcq_ref, wuq_ref,
               gckv_ref, wukv_ref, gk_ref, gsq_ref, gsk_ref, rm_ref, rs_ref,
               qa_ref, ka_ref, va_ref, qb_ref, kb_ref, vb_ref, u_ref, gt_ref, *, n_lat_tiles):
    t = pl.program_id(1)
    x = jnp.where(t == n_lat_tiles, c_ref[0], x_ref[0])
    mod = mod_ref[0, 0]
    y = x * lax.rsqrt(jnp.mean(x * x, axis=-1, keepdims=True) + EPS) * ng_ref[0]
    xn = (y * (1.0 + mod[1:2]) + mod[0:1]).astype(BF16)

    def seg(a, b):
        return jnp.dot(xn, win_ref[0, :, a:b], preferred_element_type=F32)

    lane = lax.broadcasted_iota(jnp.int32, (TOK, LANES), 1)
    lo = lane < 64

    yield
    cq = seg(C_CQ, C_CKV)
    ckv = seg(C_CKV, C_KR)
    kr = seg(C_KR, C_SQ)
    yield
    cqn = (cq * lax.rsqrt(jnp.mean(cq * cq, axis=-1, keepdims=True) + EPS) * gcq_ref[0]).astype(BF16)
    qf = jnp.dot(cqn, wuq_ref[0], preferred_element_type=F32)
    ckvn = (ckv * lax.rsqrt(jnp.mean(ckv * ckv, axis=-1, keepdims=True) + EPS) * gckv_ref[0]).astype(BF16)
    kvf = jnp.dot(ckvn, wukv_ref[0], preferred_element_type=F32)
    sq = seg(C_SQ, C_SK)
    yield
    for h in range(MLA_HEADS):
        qa_ref[0, h] = qf[:, h * LANES:(h + 1) * LANES].astype(BF16)

    sk = seg(C_SK, C_SV)
    vb_ref[0] = seg(C_SV, C_U).astype(BF16)
    yield
    slots = [kvf[:, h * LANES:(h + 1) * LANES] + kr for h in range(MLA_HEADS)]
    for h, o in enumerate(_norm_rope(slots, [None] * MLA_HEADS, MLA_QK, gk_ref[0], rm_ref, MLA_ROPE // 2)):
        ka_ref[0, h] = o
    va_ref[0] = kvf[:, MLA_HEADS * LANES:].astype(BF16)
    u_ref[0] = seg(C_U, C_GATE)
    yield
    slots = [sq[:, p * LANES:(p + 1) * LANES] for p in range(SWA_HEADS // 2)]
    for p, o in enumerate(_norm_rope(slots, [lo] * len(slots), SWA_DIM, gsq_ref[0], rs_ref, SWA_DIM // 2)):
        qb_ref[0, :, p * LANES:(p + 1) * LANES] = o
    slots = [sk[:, j * LANES:(j + 1) * LANES] for j in range(SWA_KV_HEADS)]
    for j, o in enumerate(_norm_rope(slots, [lo] * len(slots), SWA_DIM, gsk_ref[0], rs_ref, SWA_DIM // 2, dup=True)):
        kb_ref[0, :, j * LANES:(j + 1) * LANES] = o
    g = seg(C_GATE, C_END)
    yield
    gt_ref[0] = (g * _sigmoid(g)).astype(BF16)


def _proj_call(l, x, ctx, mod, P, rope_mla, rope_swa):
    B, L, D = x.shape
    C = ctx.shape[1]
    S = C + L
    nt = S // TOK

    def wspec(arr):
        shp = arr.shape
        return pl.BlockSpec((1,) + shp[1:], lambda b, t: (l,) + (0,) * (len(shp) - 1))

    weights = [P["norm_g"], P["w_in"], P["g_cq"], P["w_uq"],
               P["g_ckv"], P["w_ukv"], P["g_k"], P["g_sq"], P["g_sk"]]
    nl = L // TOK
    nb = PROJ_NB
    assert len(weights) + 2 == N_PROJ_WEIGHTS and B % nb == 0
    in_specs = ([pl.BlockSpec((nb, TOK, D), lambda b, t: (b, jnp.minimum(t, nl - 1), 0)),
                 pl.BlockSpec((nb, TOK, D), lambda b, t: (b, 0, 0)),
                 pl.BlockSpec((1, nb, 3, D), lambda b, t: (l, jnp.where(t == nl, B // nb, b), 0, 0))]
                + [wspec(w) for w in weights]
                + [pl.BlockSpec((3, TOK, LANES), lambda b, t: (0, t, 0)),
                   pl.BlockSpec((3, TOK, LANES), lambda b, t: (0, t, 0))])
    out_shape = [jax.ShapeDtypeStruct((B, MLA_HEADS, S, LANES), BF16),
                 jax.ShapeDtypeStruct((B, MLA_HEADS, S, LANES), BF16),
                 jax.ShapeDtypeStruct((B, S, 512), BF16),
                 jax.ShapeDtypeStruct((B, S, 512), BF16),
                 jax.ShapeDtypeStruct((B, S, 256), BF16),
                 jax.ShapeDtypeStruct((B, S, 256), BF16),
                 jax.ShapeDtypeStruct((B, S, 512), F32),
                 jax.ShapeDtypeStruct((B, S, 1536), BF16)]
    out_specs = [pl.BlockSpec((nb, MLA_HEADS, TOK, LANES), lambda b, t: (b, 0, t, 0)),
                 pl.BlockSpec((nb, MLA_HEADS, TOK, LANES), lambda b, t: (b, 0, t, 0))]
    out_specs += [pl.BlockSpec((nb, TOK, s.shape[2]), lambda b, t: (b, t, 0)) for s in out_shape[2:]]
    return pl.pallas_call(
        functools.partial(_proj_kernel, n_lat_tiles=nl), out_shape=out_shape, grid=(B // nb, nt),
        in_specs=in_specs, out_specs=out_specs,
        compiler_params=_cparams(2), name="proj",
    )(x, ctx, mod, *weights, rope_mla, rope_swa)


MLA_SUBTILES = 4
MLA_VT_ROWS = 80


def _mla_kernel(q_ref, k_ref, v_ref, gq_ref, rt_ref, o_ref, vt_ref, *, n_chunks, tq):
    ones = jnp.ones((MLA_VT_ROWS - MLA_V, KV_CHUNK), BF16)

    @pl.when(pl.program_id(2) == 0)
    def _():
        for c in range(n_chunks):
            vt = v_ref[0, c].astype(F32).T.astype(BF16)
            vt_ref[0, c] = jnp.concatenate([vt[:MLA_V], ones], axis=0)
            vt_ref[1, c] = jnp.concatenate([vt[MLA_V:], ones], axis=0)

    for r0 in range(0, q_ref.shape[2], tq):
        rows = slice(r0, r0 + tq)
        cos, sin = rt_ref[0, :, rows], rt_ref[1, :, rows]
        gain = jnp.concatenate([gq_ref[0]] * (tq // LANES), axis=1)
        half = MLA_ROPE // 2

        def qmat(e):
            qt = q_ref[0, e, rows, :].astype(F32).T
            r = lax.rsqrt(jnp.sum(qt * qt, axis=0, keepdims=True) * (1.0 / MLA_QK) + EPS)
            y = qt * r * gain
            x1, x2 = y[MLA_NOPE:MLA_NOPE + half], y[MLA_NOPE + half:MLA_QK]
            return jnp.concatenate([y[:MLA_NOPE], x1 * cos - x2 * sin, x2 * cos + x1 * sin, y[MLA_QK:]],
                                   axis=0).astype(BF16)

        qts = [qmat(e) for e in range(2)]
        score = lambda c, e: jnp.dot(k_ref[0, e, c], qts[e], preferred_element_type=F32)
        ms = [None, None]
        accs = [None, None]
        sts = [score(0, e) for e in range(2)]
        for c in range(n_chunks):
            nxt = [score(c + 1, e) for e in range(2)] if c + 1 < n_chunks else None
            for e in range(2):
                mc = jnp.max(sts[e], axis=0, keepdims=True)
                m_new = mc if c == 0 else jnp.maximum(ms[e], mc)
                pt = jnp.exp2(sts[e] - m_new).astype(BF16)
                pv = jnp.dot(vt_ref[e, c], pt, preferred_element_type=F32)
                accs[e] = pv if c == 0 else accs[e] * jnp.exp2(ms[e] - m_new) + pv
                ms[e] = m_new
            sts = nxt
        ot = jnp.concatenate([accs[e][:MLA_V] / accs[e][MLA_V:MLA_V + 1] for e in range(2)], axis=0)
        o_ref[0, rows, :] = ot.T.astype(BF16)


def _mla_call(l, qa, ka, va, g_qt, rope_t, n_lat):
    B, H, S, _ = qa.shape
    half = MLA_ROPE // 2
    gspec = pl.BlockSpec((1, LANES, LANES), lambda b, p, t: (l, 0, 0))
    nc = S // KV_CHUNK
    tq = 2 * TOK
    bq = MLA_SUBTILES * tq if n_lat % (MLA_SUBTILES * tq) == 0 else tq
    k5 = ka.reshape(B, H, nc, KV_CHUNK, LANES)
    v4 = va.reshape(B, nc, KV_CHUNK, 512)
    o_lat = pl.pallas_call(
        functools.partial(_mla_kernel, n_chunks=nc, tq=tq),
        out_shape=jax.ShapeDtypeStruct((B, n_lat, 512), BF16),
        grid=(B, H // 2, n_lat // bq),
        in_specs=[pl.BlockSpec((1, 2, bq, LANES), lambda b, p, t: (b, p, t, 0)),
                  pl.BlockSpec((1, 2, nc, KV_CHUNK, LANES), lambda b, p, t: (b, p, 0, 0, 0)),
                  pl.BlockSpec((1, nc, KV_CHUNK, LANES), lambda b, p, t: (b, 0, 0, p)),
                  gspec,
                  pl.BlockSpec((2, half, bq), lambda b, p, t: (0, 0, t))],
        out_specs=pl.BlockSpec((1, bq, LANES), lambda b, p, t: (b, t, p)),
        scratch_shapes=[pltpu.VMEM((2, nc, MLA_VT_ROWS, KV_CHUNK), BF16)],
        compiler_params=_cparams(3), name="mla_attn",
    )(qa, k5, v4, g_qt, rope_t)
    cblk = n_lat // KV_CHUNK
    o_ctx = pl.pallas_call(
        functools.partial(_mla_kernel, n_chunks=1, tq=TOK),
        out_shape=jax.ShapeDtypeStruct((B, S - n_lat, 512), BF16),
        grid=(B, H // 2, 1),
        in_specs=[pl.BlockSpec((1, 2, TOK, LANES), lambda b, p, t: (b, p, cblk, 0)),
                  pl.BlockSpec((1, 2, 1, KV_CHUNK, LANES), lambda b, p, t: (b, p, cblk, 0, 0)),
                  pl.BlockSpec((1, 1, KV_CHUNK, LANES), lambda b, p, t: (b, cblk, 0, p)),
                  gspec,
                  pl.BlockSpec((2, half, TOK), lambda b, p, t: (0, 0, cblk))],
        out_specs=pl.BlockSpec((1, TOK, LANES), lambda b, p, t: (b, 0, p)),
        scratch_shapes=[pltpu.VMEM((2, 1, MLA_VT_ROWS, KV_CHUNK), BF16)],
        compiler_params=_cparams(3), name="mla_attn_ctx",
    )(qa, k5, v4, g_qt, rope_t)
    return o_lat, o_ctx


SWA_VT_ROWS = 80


def _swa_kernel(q_ref, k_ref, v_ref, sink_ref, o_ref, vt_ref, bias_ref, *, n_ctx, n_lat):
    blk = SWA_BLOCK
    heads = SWA_HEADS // SWA_KV_HEADS
    win = 3 * blk
    n_blocks = n_lat // blk
    ctx_blk = n_blocks
    group = min(SWA_UNROLL, n_blocks)
    sink = sink_ref[0, 0:1, :]

    ones = jnp.ones((SWA_VT_ROWS - SWA_DIM, blk), BF16)
    for i in range((n_lat + n_ctx) // blk):
        vt = v_ref[0, i * blk:(i + 1) * blk, :].astype(F32).T
        vt_ref[i] = jnp.concatenate([vt[:SWA_DIM].astype(BF16), ones], axis=0)
    rel0 = (lax.broadcasted_iota(jnp.int32, (win, heads * blk), 0)
            - (lax.broadcasted_iota(jnp.int32, (win, heads * blk), 1) & (blk - 1)))
    for kind in range(3):
        bias_ref[kind] = jnp.where(jnp.abs(rel0 - kind * blk) <= SWA_WINDOW, 0.0, NEG)

    kc = k_ref[0, n_lat:n_lat + n_ctx, :]
    vtc = jnp.concatenate([vt_ref[ctx_blk + i] for i in range(n_ctx // blk)], axis=1)
    zrows = jnp.zeros((LANES - SWA_DIM, heads * blk), BF16)

    def qmat(r0):
        qt = q_ref[0, pl.ds(r0, blk), :].astype(F32).T
        w = jnp.concatenate([qt[h * SWA_DIM:(h + 1) * SWA_DIM] for h in range(heads)], axis=1)
        return jnp.concatenate([w.astype(BF16), zrows], axis=0)

    def finish(r0, m, acc):
        ot = acc[:SWA_DIM] / (acc[SWA_DIM:SWA_DIM + 1] + jnp.exp2(sink - m))
        o4 = jnp.concatenate([ot[:, h * blk:(h + 1) * blk] for h in range(heads)], axis=0)
        o_ref[0, pl.ds(r0, blk), :] = o4.T.astype(BF16)

    for n in range(n_ctx // blk):
        r0 = n_lat + n * blk
        s_c = jnp.dot(kc, qmat(r0), preferred_element_type=F32)
        m = jnp.maximum(jnp.max(s_c, axis=0, keepdims=True), sink)
        finish(r0, m, jnp.dot(vtc, jnp.exp2(s_c - m).astype(BF16), preferred_element_type=F32))

    def window_block(n):
        return jnp.clip(n - 1, 0, n_blocks - 3)

    def scores(n):
        wb = window_block(n)
        w = qmat(pl.multiple_of(n * blk, blk))
        kw = k_ref[0, pl.ds(pl.multiple_of(wb * blk, blk), win), :]
        return (jnp.dot(kc, w, preferred_element_type=F32),
                jnp.dot(kw, w, preferred_element_type=F32) + bias_ref[n - wb])

    def blocks(gi, carry):
        n0 = gi * group
        cur = scores(n0)
        for i in range(group):
            n = n0 + i
            nxt = scores(n + 1) if i + 1 < group else None
            s_c, s_w = cur
            m = jnp.maximum(jnp.maximum(jnp.max(s_c, axis=0, keepdims=True),
                                        jnp.max(s_w, axis=0, keepdims=True)), sink)
            wb = window_block(n)
            vtw = jnp.concatenate([vt_ref[wb + j] for j in range(3)], axis=1)
            acc = (jnp.dot(vtc, jnp.exp2(s_c - m).astype(BF16), preferred_element_type=F32)
                   + jnp.dot(vtw, jnp.exp2(s_w - m).astype(BF16), preferred_element_type=F32))
            finish(pl.multiple_of(n * blk, blk), m, acc)
            cur = nxt
        return carry

    assert n_blocks % group == 0
    lax.fori_loop(0, n_blocks // group, blocks, 0)


def _swa_call(l, qb, kb, vb, sinkrow, n_ctx):
    B, S, _ = qb.shape
    nq = (SWA_HEADS // SWA_KV_HEADS) * SWA_BLOCK
    return pl.pallas_call(
        functools.partial(_swa_kernel, n_ctx=n_ctx, n_lat=S - n_ctx),
        out_shape=jax.ShapeDtypeStruct((B, S, 512), BF16),
        grid=(B, SWA_KV_HEADS),
        in_specs=[pl.BlockSpec((1, S, 256), lambda b, j: (b, 0, j)),
                  pl.BlockSpec((1, S, LANES), lambda b, j: (b, 0, j)),
                  pl.BlockSpec((1, S, LANES), lambda b, j: (b, 0, j)),
                  pl.BlockSpec((1, 8, nq), lambda b, j: (l * SWA_KV_HEADS + j, 0, 0))],
        out_specs=pl.BlockSpec((1, S, 256), lambda b, j: (b, 0, j)),
        scratch_shapes=[pltpu.VMEM((S // SWA_BLOCK, SWA_VT_ROWS, SWA_BLOCK), BF16),
                        pltpu.VMEM((3, 3 * SWA_BLOCK, nq), F32)],
        compiler_params=_cparams(2), name="swa_attn",
    )(qb, kb, vb, sinkrow)


def _gelu(y):
    return 0.5 * y * (1.0 + jnp.tanh(math.sqrt(2.0 / math.pi) * (y + 0.044715 * (y * y * y))))


def _s5_kernel(u_ref, wst_ref, wloc_ref, wcar_ref, at_ref, o_ref,
               ut_ref, utc_ref, yt_ref, ytc_ref, ere_ref, eim_ref, hfr_ref, hfi_ref, hbr_ref, hbi_ref,
               *, nb, n_lat_chunks, n_ctx_chunks):
    T, GP = S5_CHUNK, S5_GROUP
    n_chunks = n_lat_chunks + n_ctx_chunks
    n_lat = n_lat_chunks * T
    cw = nb * n_ctx_chunks
    tn_dims = (((0,), (0,)), ((), ()))
    nt_dims = (((1,), (1,)), ((), ()))
    gpl = LANES // GP

    for b in range(nb):
        for s in range(T):
            xs = u_ref[b, pl.ds(s, n_lat_chunks, stride=T), :]
            ut_ref[b, :, s] = xs.T.reshape(gpl, GP, n_lat_chunks).astype(BF16)
    zpad = jnp.zeros((LANES - cw, LANES), F32)
    for s in range(T):
        xs = jnp.concatenate([u_ref[b, pl.ds(n_lat + s, n_ctx_chunks, stride=T), :] for b in range(nb)]
                             + [zpad], axis=0)
        utc_ref[:, s] = xs.T.reshape(gpl, GP, LANES).astype(BF16)

    lane = lax.broadcasted_iota(jnp.int32, (2 * nb, LANES), 1)
    fwd = lane < S5_STATE
    lane_l = lax.broadcasted_iota(jnp.int32, (n_lat_chunks, LANES), 1) < S5_STATE
    lane_c = lax.broadcasted_iota(jnp.int32, (LANES, LANES), 1) < S5_STATE

    slab = 2 * nb
    lat_rows = lambda gb: pl.ds(gb, n_lat_chunks, stride=slab)
    ctx_rows = lambda gb: pl.ds(n_lat_chunks * slab + gb, n_ctx_chunks, stride=slab)

    def pair(gp, carry):
        for gl in range(2):
            g = gp * 2 + gl
            wst = wst_ref[g]
            for b in range(nb):
                e = lax.dot_general(ut_ref[b, g].reshape(T * GP, n_lat_chunks), wst, tn_dims,
                                    preferred_element_type=F32)
                ere_ref[lat_rows(gl * nb + b), :] = e[:, :LANES]
                eim_ref[lat_rows(gl * nb + b), :] = e[:, LANES:]
            ec = lax.dot_general(utc_ref[g].reshape(T * GP, LANES), wst, tn_dims, preferred_element_type=F32)
            for b in range(nb):
                ere_ref[ctx_rows(gl * nb + b), :] = ec[b * n_ctx_chunks:(b + 1) * n_ctx_chunks, :LANES]
                eim_ref[ctx_rows(gl * nb + b), :] = ec[b * n_ctx_chunks:(b + 1) * n_ctx_chunks, LANES:]

        a_re = jnp.concatenate([jnp.broadcast_to(at_ref[gp * 2 + gl, 0:1, :], (nb, LANES)) for gl in range(2)], 0)
        a_im = jnp.concatenate([jnp.broadcast_to(at_ref[gp * 2 + gl, 1:2, :], (nb, LANES)) for gl in range(2)], 0)

        def step(i, hc):
            h_re, h_im = hc
            cf = jnp.where(i < n_ctx_chunks, n_lat_chunks + i, i - n_ctx_chunks)
            cb = n_chunks - 1 - i
            sf = pl.ds(pl.multiple_of(cf * slab, slab), slab)
            sb = pl.ds(pl.multiple_of(cb * slab, slab), slab)
            hfr_ref[sf, :] = h_re
            hfi_ref[sf, :] = h_im
            hbr_ref[sb, :] = h_re
            hbi_ref[sb, :] = h_im
            e_re = jnp.where(fwd, ere_ref[sf, :], ere_ref[sb, :])
            e_im = jnp.where(fwd, eim_ref[sf, :], eim_ref[sb, :])
            return (a_re * h_re - a_im * h_im + e_re, a_re * h_im + a_im * h_re + e_im)

        zero = jnp.zeros((2 * nb, LANES), F32)
        lax.fori_loop(0, n_chunks, step, (zero, zero), unroll=8)

        for gl in range(2):
            g = gp * 2 + gl
            wloc, wcar = wloc_ref[g], wcar_ref[g]
            for b in range(nb):
                rows = lat_rows(gl * nb + b)
                h_cat = jnp.concatenate([jnp.where(lane_l, hfr_ref[rows, :], hbr_ref[rows, :]),
                                         jnp.where(lane_l, hfi_ref[rows, :], hbi_ref[rows, :])],
                                        axis=-1).astype(BF16)
                yt = (jnp.dot(wloc, ut_ref[b, g].reshape(T * GP, n_lat_chunks), preferred_element_type=F32)
                      + lax.dot_general(wcar, h_cat, nt_dims, preferred_element_type=F32))
                yt_ref[b, g] = _gelu(yt).astype(BF16).reshape(T, GP, n_lat_chunks)
            crow = lambda ref: jnp.concatenate(
                [ref[ctx_rows(gl * nb + b), :] for b in range(nb)]
                + [jnp.zeros((LANES - cw, LANES), F32)], axis=0)
            h_cat = jnp.concatenate([jnp.where(lane_c, crow(hfr_ref), crow(hbr_ref)),
                                     jnp.where(lane_c, crow(hfi_ref), crow(hbi_ref))], axis=-1).astype(BF16)
            ytc = (jnp.dot(wloc, utc_ref[g].reshape(T * GP, LANES), preferred_element_type=F32)
                   + lax.dot_general(wcar, h_cat, nt_dims, preferred_element_type=F32))
            ytc_ref[g] = _gelu(ytc).astype(BF16).reshape(T, GP, LANES)
        return carry

    lax.fori_loop(0, gpl // 2, pair, 0)

    for b in range(nb):
        for t in range(T):
            z = yt_ref[b, :, t].astype(F32).reshape(LANES, n_lat_chunks)
            o_ref[b, pl.ds(t, n_lat_chunks, stride=T), :] = z.T
    for t in range(T):
        z = ytc_ref[:, t].astype(F32).reshape(LANES, LANES).T
        for b in range(nb):
            o_ref[b, pl.ds(n_lat + t, n_ctx_chunks, stride=T), :] = z[b * n_ctx_chunks:(b + 1) * n_ctx_chunks]


def _s5_call(l, u, P, n_lat):
    B, S, W = u.shape
    nb = 4 if B % 4 == 0 else B
    T, GP = S5_CHUNK, S5_GROUP
    nlc, ncc = n_lat // T, (S - n_lat) // T
    gpl = LANES // GP
    nblk = W // LANES
    big = lambda n: pl.BlockSpec((nb, S, LANES), lambda j, hb: (hb, 0, j), pipeline_mode=pl.Buffered(n))
    wspec = lambda: pl.BlockSpec((gpl, T * GP, T * GP), lambda j, hb: (l * nblk + j, 0, 0))
    rows = 2 * nb * (nlc + ncc)
    return pl.pallas_call(
        functools.partial(_s5_kernel, nb=nb, n_lat_chunks=nlc, n_ctx_chunks=ncc),
        out_shape=jax.ShapeDtypeStruct((B, S, W), F32),
        grid=(nblk, B // nb),
        in_specs=[big(2), wspec(), wspec(), wspec(),
                  pl.BlockSpec((gpl, 2, LANES), lambda j, hb: (l * nblk + j, 0, 0))],
        out_specs=big(1),
        scratch_shapes=[pltpu.VMEM((nb, gpl, T, GP, nlc), BF16), pltpu.VMEM((gpl, T, GP, LANES), BF16),
                        pltpu.VMEM((nb, gpl, T, GP, nlc), BF16), pltpu.VMEM((gpl, T, GP, LANES), BF16)]
                       + [pltpu.VMEM((rows, LANES), F32)] * 6,
        compiler_params=_cparams(2), name="s5",
    )(u, P["s5_wst"], P["s5_wloc_t"], P["s5_wcar_t"], P["s5_at"])


def _out_kernel(x_ref, c_ref, mod_ref, oa_ref, oac_ref, ob_ref, gy_ref, gt_ref, wglu_ref, wout_ref,
                xo_ref, co_ref, *, n_lat_tiles):
    t = pl.program_id(1)
    cb = OUT_COLS
    nb, rows = x_ref.shape[0], x_ref.shape[1]
    stack = lambda ref: jnp.concatenate([ref[i] for i in range(nb)], axis=0)
    gyb = stack(gy_ref).astype(BF16)
    g = stack(gt_ref).astype(F32)
    oa = jnp.where(t == n_lat_tiles, stack(oac_ref), stack(oa_ref))
    m_a = (oa.astype(F32) * g[:, 0:512]).astype(BF16)
    m_b = (stack(ob_ref).astype(F32) * g[:, 512:1024]).astype(BF16)
    oc = []
    for j in range(512 // cb):
        za = jnp.dot(gyb, wglu_ref[0, :, j * cb:(j + 1) * cb], preferred_element_type=F32)
        zb = jnp.dot(gyb, wglu_ref[0, :, 512 + j * cb:512 + (j + 1) * cb], preferred_element_type=F32)
        oc.append((za * _sigmoid(zb) * g[:, 1024 + j * cb:1024 + (j + 1) * cb]).astype(BF16))
    m_c = jnp.concatenate(oc, axis=-1)
    gate = jnp.concatenate([jnp.broadcast_to(mod_ref[0, i][2:3], (rows, x_ref.shape[2])) for i in range(nb)],
                           axis=0)
    resid = jnp.where(t == n_lat_tiles, stack(c_ref), stack(x_ref))
    new = []
    for j in range(resid.shape[1] // cb):
        cols = slice(j * cb, (j + 1) * cb)
        upd = (jnp.dot(m_a, wout_ref[0, 0:512, cols], preferred_element_type=F32)
               + jnp.dot(m_b, wout_ref[0, 512:1024, cols], preferred_element_type=F32)
               + jnp.dot(m_c, wout_ref[0, 1024:1536, cols], preferred_element_type=F32))
        new.append(resid[:, cols] + gate[:, cols] * upd)

    @pl.when(t == n_lat_tiles)
    def _():
        for j, v in enumerate(new):
            for i in range(nb):
                co_ref[i, :, j * cb:(j + 1) * cb] = v[i * rows:(i + 1) * rows]

    @pl.when(t < n_lat_tiles)
    def _():
        for j, v in enumerate(new):
            for i in range(nb):
                xo_ref[i, :, j * cb:(j + 1) * cb] = v[i * rows:(i + 1) * rows]


def _out_call(l, x, ctx, mod, oa, oa_ctx, ob, gy, gt, P):
    B, L, D = x.shape
    C = ctx.shape[1]
    S = C + L
    nl = L // TOK
    nb = PROJ_NB
    xmap = lambda b, t: (b, jnp.minimum(t, nl - 1), 0)
    cmap = lambda b, t: (b, 0, 0)
    tmap = lambda b, t: (b, t, 0)
    return pl.pallas_call(
        functools.partial(_out_kernel, n_lat_tiles=nl),
        out_shape=[jax.ShapeDtypeStruct(x.shape, F32), jax.ShapeDtypeStruct(ctx.shape, F32)],
        grid=(B // nb, S // TOK),
        in_specs=[pl.BlockSpec((nb, TOK, D), xmap),
                  pl.BlockSpec((nb, TOK, D), cmap),
                  pl.BlockSpec((1, nb, 3, D), lambda b, t: (l, jnp.where(t == nl, B // nb, b), 0, 0)),
                  pl.BlockSpec((nb, TOK, 512), xmap),
                  pl.BlockSpec((nb, TOK, 512), cmap),
                  pl.BlockSpec((nb, TOK, 512), tmap),
                  pl.BlockSpec((nb, TOK, 512), tmap),
                  pl.BlockSpec((nb, TOK, 1536), tmap),
                  pl.BlockSpec((1, 512, 1024), lambda b, t: (l, 0, 0)),
                  pl.BlockSpec((1, 1536, D), lambda b, t: (l, 0, 0))],
        out_specs=[pl.BlockSpec((nb, TOK, D), xmap), pl.BlockSpec((nb, TOK, D), cmap)],
        compiler_params=_cparams(2), name="out",
    )(x, ctx, mod, oa, oa_ctx, ob, gy, gt, P["w_glu"], P["w_out"])


def _rope_angles(n_lat, n_ctx, rot_dim):
    rows = n_lat // GRID_W
    r_idx, c_idx = jnp.meshgrid(jnp.arange(rows), jnp.arange(GRID_W), indexing="ij")
    r_idx, c_idx = r_idx.reshape(-1), c_idx.reshape(-1)
    n_freq = rot_dim // 4
    freqs = ROPE_BASE ** (-jnp.arange(n_freq, dtype=F32) / n_freq)
    ang = jnp.concatenate([r_idx.astype(F32)[:, None] * freqs,
                           c_idx.astype(F32)[:, None] * freqs], axis=-1)
    return jnp.concatenate([ang, jnp.zeros((n_ctx, rot_dim // 2), F32)], axis=0)


def _rope_tables_t(n_lat, n_ctx, rot_dim):
    ang = _rope_angles(n_lat, n_ctx, rot_dim).T
    return jnp.stack([jnp.cos(ang), jnp.sin(ang)])


def _rope_tables(n_lat, n_ctx, rot_dim, lead, reps):
    ang = _rope_angles(n_lat, n_ctx, rot_dim)
    cos, sin, zero = jnp.cos(ang), jnp.sin(ang), jnp.zeros_like(ang)
    n = ang.shape[0]
    tail = LANES // reps - lead - rot_dim

    def pack(x1, x2, fill):
        unit = [jnp.full((n, lead), fill, F32), x1, x2, jnp.full((n, tail), fill, F32)]
        return jnp.concatenate(unit * reps, axis=-1)

    return jnp.stack([pack(cos, cos, 1.0), pack(zero, sin, 0.0), pack(-sin, zero, 0.0)])


def _prep_params(norm_g, w_in, w_out, mla_g_cq, mla_g_ckv, mla_w_uq, mla_w_ukv, mla_g_qn, mla_g_kn,
                 swa_g_qn, swa_g_kn, swa_sink, s5_a_re, s5_a_im, s5_log_dt, s5_b_re, s5_b_im,
                 s5_c_re, s5_c_im, s5_d, s5_w_glu):
    depth, D, _ = w_in.shape
    o_cq, o_ckv, o_kr, o_gm, o_sq, o_sk, o_sv, o_gs, o_u, o_g5, o_end = (
        0, 384, 640, 672, 1184, 1696, 1824, 1952, 2464, 2976, 3488)
    z = lambda n: jnp.zeros((depth, D, n), F32)
    sk0, sk1 = w_in[:, :, o_sk:o_sk + 64], w_in[:, :, o_sk + 64:o_sv]
    sv0, sv1 = w_in[:, :, o_sv:o_sv + 64], w_in[:, :, o_sv + 64:o_gs]
    w_in_p = jnp.concatenate([
        w_in[:, :, o_cq:o_kr],
        z(64), w_in[:, :, o_kr:o_gm], z(32),
        w_in[:, :, o_sq:o_sk],
        sk0, sk0, sk1, sk1, sv0, sv0, sv1, sv1,
        w_in[:, :, o_u:o_g5],
        w_in[:, :, o_gm:o_sq], w_in[:, :, o_gs:o_u], w_in[:, :, o_g5:o_end],
    ], axis=-1).astype(BF16)
    assert w_in_p.shape[-1] == C_END

    wq = mla_w_uq.reshape(depth, MLA_Q_RANK, MLA_HEADS, MLA_QK)
    wq = jnp.pad(wq, ((0, 0), (0, 0), (0, 0), (0, LANES - MLA_QK)))
    w_uq_p = wq.reshape(depth, MLA_Q_RANK, MLA_HEADS * LANES).astype(BF16)
    wkv = mla_w_ukv.reshape(depth, MLA_KV_RANK, MLA_HEADS, MLA_NOPE + MLA_V)
    wk = jnp.pad(wkv[..., :MLA_NOPE], ((0, 0), (0, 0), (0, 0), (0, LANES - MLA_NOPE)))
    w_ukv_p = jnp.concatenate([wk.reshape(depth, MLA_KV_RANK, MLA_HEADS * LANES),
                               wkv[..., MLA_NOPE:].reshape(depth, MLA_KV_RANK, MLA_HEADS * MLA_V)],
                              axis=-1).astype(BF16)

    pad_qk = lambda g: jnp.pad(g, ((0, 0), (0, LANES - MLA_QK)))[:, None, :]
    g_q = pad_qk(mla_g_qn * (MLA_QK ** -0.5 * LOG2E))
    g_qt = jnp.broadcast_to(jnp.swapaxes(g_q, 1, 2), (depth, LANES, LANES))
    g_k = pad_qk(mla_g_kn)
    g_sq = jnp.tile(swa_g_qn * (SWA_DIM ** -0.5 * LOG2E), (1, 2))[:, None, :]
    g_sk = jnp.tile(swa_g_kn, (1, 2))[:, None, :]
    sink = (swa_sink * LOG2E).reshape(depth * SWA_KV_HEADS, 1, SWA_HEADS // SWA_KV_HEADS, 1)
    sinkrow = jnp.broadcast_to(sink, (depth * SWA_KV_HEADS, 8, SWA_HEADS // SWA_KV_HEADS, SWA_BLOCK))
    sinkrow = sinkrow.reshape(depth * SWA_KV_HEADS, 8, -1).astype(F32)

    T = S5_CHUNK
    A = lax.complex(s5_a_re, s5_a_im)
    dt = jnp.exp(s5_log_dt)[..., None]
    a_bar = jnp.exp(dt * A)
    b_bar = ((a_bar - 1.0) / A)[..., None] * lax.complex(s5_b_re, s5_b_im)
    c_mat = lax.complex(s5_c_re, s5_c_im)
    k_idx = jnp.arange(T + 1, dtype=F32)
    pw = jnp.exp(k_idx[None, None, None, :, None] * (dt * A)[:, :, :, None, :])
    hi = lax.Precision.HIGHEST
    tt = jnp.arange(T)
    GP, NG = S5_GROUP, depth * S5_GROUPS
    pw_f, pw_b = pw[:, 0], pw[:, 1]
    tap_f = jnp.einsum("lgpn,lgkn,lgnq->lgpkq", c_mat[:, 0], pw_f[:, :, T - 1 - tt], b_bar[:, 0],
                       precision=hi).real
    tap_b = jnp.einsum("lgpn,lgkn,lgnq->lgpkq", c_mat[:, 1], pw_b[:, :, tt], b_bar[:, 1],
                       precision=hi).real
    d_diag = s5_d.reshape(depth, S5_GROUPS, GP)[..., None] * jnp.eye(GP, dtype=F32)
    centre = tap_f[:, :, :, T - 1:] + tap_b[:, :, :, :1] + d_diag[:, :, :, None, :]
    krev = jnp.concatenate([tap_f[:, :, :, :T - 1], centre, tap_b[:, :, :, 1:]], axis=3)
    krev = krev.reshape(depth, S5_GROUPS, GP, (2 * T - 1) * GP)
    wloc_t = jnp.stack([krev[..., (T - 1 - t) * GP:(T - 1 - t) * GP + T * GP] for t in range(T)], axis=2)
    wloc_t = wloc_t.reshape(NG, T * GP, T * GP)
    b_t = jnp.swapaxes(b_bar, -1, -2)
    inc_f = pw_f[:, :, T - 1 - tt][:, :, :, None, :] * b_t[:, 0][:, :, None]
    inc_b = pw_b[:, :, tt][:, :, :, None, :] * b_t[:, 1][:, :, None]
    wst = jnp.concatenate([inc_f.real, inc_b.real, inc_f.imag, inc_b.imag], axis=-1)
    wst = wst.reshape(NG, T * GP, 4 * S5_STATE)
    ro_f = c_mat[:, 0][:, :, None] * pw_f[:, :, tt + 1][:, :, :, None, :]
    ro_b = c_mat[:, 1][:, :, None] * pw_b[:, :, T - tt][:, :, :, None, :]
    wcar_t = jnp.concatenate([ro_f.real, ro_b.real, -ro_f.imag, -ro_b.imag], axis=-1)
    wcar_t = wcar_t.reshape(NG, T * GP, 4 * S5_STATE)
    a_t = pw[:, :, :, T]
    at = jnp.stack([jnp.concatenate([a_t[:, 0].real, a_t[:, 1].real], axis=-1),
                    jnp.concatenate([a_t[:, 0].imag, a_t[:, 1].imag], axis=-1)], axis=2)
    at = at.reshape(NG, 2, LANES).astype(F32)

    return dict(norm_g=norm_g[:, None, :], w_in=w_in_p, g_cq=mla_g_cq[:, None, :], w_uq=w_uq_p, g_qt=g_qt,
                g_ckv=mla_g_ckv[:, None, :], w_ukv=w_ukv_p, g_k=g_k, g_sq=g_sq, g_sk=g_sk,
                sinkrow=sinkrow, s5_wloc_t=wloc_t.astype(BF16), s5_wst=wst.astype(BF16),
                s5_wcar_t=wcar_t.astype(BF16), s5_at=at, w_glu=s5_w_glu.astype(BF16),
                w_out=w_out.astype(BF16))


def kernel(x, c, ctx, c_ctx, norm_g, w_ada, b_ada, w_in, w_out, mla_g_cq, mla_g_ckv, mla_w_uq, mla_w_ukv, mla_g_qn, mla_g_kn, swa_g_qn, swa_g_kn, swa_sink, s5_a_re, s5_a_im, s5_log_dt, s5_b_re, s5_b_im, s5_c_re, s5_c_im, s5_d, s5_w_glu):
    B, L, D = x.shape
    C = ctx.shape[1]
    S = C + L
    depth = w_in.shape[0]
    assert B + PROJ_NB <= MOD_ROWS and C == TOK and L % (2 * TOK) == 0 and L % GRID_W == 0 and L >= 3 * SWA_BLOCK

    P = _prep_params(norm_g, w_in, w_out, mla_g_cq, mla_g_ckv, mla_w_uq, mla_w_ukv, mla_g_qn, mla_g_kn,
                     swa_g_qn, swa_g_kn, swa_sink, s5_a_re, s5_a_im, s5_log_dt, s5_b_re, s5_b_im,
                     s5_c_re, s5_c_im, s5_d, s5_w_glu)
    rope_mla = _rope_tables(L, C, MLA_ROPE, MLA_NOPE, 1)
    rope_swa = _rope_tables(L, C, SWA_DIM, 0, 2)
    rope_mla_t = _rope_tables_t(L, C, MLA_ROPE)

    cc = jnp.concatenate([c, jnp.tile(c_ctx[None, :], (PROJ_NB, 1)),
                          jnp.zeros((MOD_ROWS - B - PROJ_NB, D), F32)], axis=0)
    mod = _ada_call(cc, w_ada, b_ada).reshape(depth, MOD_ROWS, 3, D)

    for l in range(depth):
        qa, ka, va, qb, kb, vb, u, gt = _proj_call(l, x, ctx, mod, P, rope_mla, rope_swa)
        oa, oa_ctx = _mla_call(l, qa, ka, va, P["g_qt"], rope_mla_t, L)
        ob = _swa_call(l, qb, kb, vb, P["sinkrow"], C)
        gy = _s5_call(l, u, P, L)
        x, ctx = _out_call(l, x, ctx, mod, oa, oa_ctx, ob, gy, gt, P)
    return x
```

```python
import functools
import math

import jax
import jax.numpy as jnp
from jax import lax
from jax.experimental import pallas as pl
from jax.experimental.pallas import tpu as pltpu

F32 = jnp.float32
BF16 = jnp.bfloat16

GRID_W = 64
EPS = 1e-6
ROPE_BASE = 10000.0
NEG = -1e30
LOG2E = math.log2(math.e)

MLA_HEADS = 8
MLA_NOPE = 64
MLA_ROPE = 32
MLA_V = 64
MLA_QK = MLA_NOPE + MLA_ROPE
MLA_Q_RANK = 384
MLA_KV_RANK = 256

SWA_HEADS = 8
SWA_KV_HEADS = 2
SWA_DIM = 64
SWA_WINDOW = 128

S5_GROUP = 16
S5_GROUPS = 32
S5_STATE = 64
S5_CHUNK = 16

LANES = 128
TOK = 256
KV_CHUNK = 256
SWA_BLOCK = 128
MOD_ROWS = 16
PROJ_NB = 2
SWA_UNROLL = 32
OUT_COLS = 256
PROJ_SKEW = 1

C_CQ = 0
C_CKV = 384
C_KR = 640
C_SQ = 768
C_SK = 1280
C_SV = 1408
C_U = 1536
C_GATE = 2048
C_END = 3584

VMEM_LIMIT = 56 * 1024 * 1024


def _cparams(n_axes):
    return pltpu.CompilerParams(dimension_semantics=("arbitrary",) * n_axes,
                                vmem_limit_bytes=VMEM_LIMIT)


def _ada_kernel(c_ref, w_ref, b_ref, o_ref):
    cc = c_ref[...]
    s = cc * jax.nn.sigmoid(cc)
    o_ref[0] = jnp.dot(s, w_ref[0], preferred_element_type=F32,
                       precision=lax.Precision.HIGHEST) + b_ref[0]


def _ada_call(cc, w_ada, b_ada):
    depth, d, n3 = w_ada.shape
    tn = 768
    return pl.pallas_call(
        _ada_kernel,
        out_shape=jax.ShapeDtypeStruct((depth, MOD_ROWS, n3), F32),
        grid=(depth, n3 // tn),
        in_specs=[pl.BlockSpec((MOD_ROWS, d), lambda l, j: (0, 0)),
                  pl.BlockSpec((1, d, tn), lambda l, j: (l, 0, j)),
                  pl.BlockSpec((1, 1, tn), lambda l, j: (l, 0, j))],
        out_specs=pl.BlockSpec((1, MOD_ROWS, tn), lambda l, j: (l, 0, j)),
        compiler_params=_cparams(2),
        name="ada",
    )(cc, w_ada, b_ada.reshape(depth, 1, n3))


def _sigmoid(v):
    return 0.5 * jnp.tanh(0.5 * v) + 0.5


def _norm_rope(slots, lo_masks, dim, gain, tab_ref, half):
    sq = [s * s for s in slots]
    sums = []
    for s2, lo in zip(sq, lo_masks):
        if lo is None:
            sums.append((jnp.sum(s2, axis=-1, keepdims=True),))
        else:
            sums.append((jnp.sum(jnp.where(lo, s2, 0.0), axis=-1, keepdims=True),
                         jnp.sum(jnp.where(lo, 0.0, s2), axis=-1, keepdims=True)))
    ys = []
    for s, ss, lo in zip(slots, sums, lo_masks):
        rs = [lax.rsqrt(v * (1.0 / dim) + EPS) for v in ss]
        r = rs[0] if len(rs) == 1 else jnp.where(lo, rs[0], rs[1])
        ys.append(s * r * gain)
    up = [pltpu.roll(y, half, 1) for y in ys]
    dn = [pltpu.roll(y, LANES - half, 1) for y in ys]
    return [(y * tab_ref[0] + u * tab_ref[1] + d * tab_ref[2]).astype(BF16) for y, u, d in zip(ys, up, dn)]


N_PROJ_DATA = 3
N_PROJ_WEIGHTS = 10


def _proj_kernel(*refs, n_lat_tiles):
    data = refs[:N_PROJ_DATA]
    shared = refs[N_PROJ_DATA:N_PROJ_DATA + N_PROJ_WEIGHTS]
    outs = refs[N_PROJ_DATA + N_PROJ_WEIGHTS:]
    streams = []
    for i in range(data[0].shape[0]):
        one = pl.ds(i, 1)
        streams.append(_proj_tile(data[0].at[one], data[1].at[one], data[2].at[:, one], *shared,
                                  *[o.at[one] for o in outs], n_lat_tiles=n_lat_tiles))
    pending = list(enumerate(streams))
    step = 0
    while pending:
        for item in list(pending):
            if step >= item[0] * PROJ_SKEW and next(item[1], "done") == "done":
                pending.remove(item)
        step += 1


def _proj_tile(x_ref, c_ref, mod_ref, ng_ref, win_ref, gcq_ref, wuq_ref,
               gckv_ref, wukv_ref, gk_ref, gsk_ref, rm_ref, rs_ref,
               qa_ref, ka_ref, va_ref, qb_ref, kb_ref, vb_ref, u_ref, gt_ref, *, n_lat_tiles):
    t = pl.program_id(1)
    x = jnp.where(t == n_lat_tiles, c_ref[0], x_ref[0])
    mod = mod_ref[0, 0]
    y = x * lax.rsqrt(jnp.mean(x * x, axis=-1, keepdims=True) + EPS) * ng_ref[0]
    xn = (y * (1.0 + mod[1:2]) + mod[0:1]).astype(BF16)

    def seg(a, b):
        return jnp.dot(xn, win_ref[0, :, a:b], preferred_element_type=F32)

    lane = lax.broadcasted_iota(jnp.int32, (TOK, LANES), 1)
    lo = lane < 64

    yield
    cq = seg(C_CQ, C_CKV)
    ckv = seg(C_CKV, C_KR)
    kr = seg(C_KR, C_SQ)
    yield
    cqn = (cq * lax.rsqrt(jnp.mean(cq * cq, axis=-1, keepdims=True) + EPS) * gcq_ref[0]).astype(BF16)
    qf = jnp.dot(cqn, wuq_ref[0], preferred_element_type=F32)
    ckvn = (ckv * lax.rsqrt(jnp.mean(ckv * ckv, axis=-1, keepdims=True) + EPS) * gckv_ref[0]).astype(BF16)
    kvf = jnp.dot(ckvn, wukv_ref[0], preferred_element_type=F32)
    sq = seg(C_SQ, C_SK)
    yield
    for h in range(MLA_HEADS):
        qa_ref[0, h] = qf[:, h * LANES:(h + 1) * LANES].astype(BF16)

    sk = seg(C_SK, C_SV)
    vb_ref[0] = seg(C_SV, C_U).astype(BF16)
    yield
    slots = [kvf[:, h * LANES:(h + 1) * LANES] + kr for h in range(MLA_HEADS)]
    for h, o in enumerate(_norm_rope(slots, [None] * MLA_HEADS, MLA_QK, gk_ref[0], rm_ref, MLA_ROPE // 2)):
        ka_ref[0, h] = o
    va_ref[0] = kvf[:, MLA_HEADS * LANES:].astype(BF16)
    u_ref[0] = seg(C_U, C_GATE)
    yield
    qb_ref[0] = sq.astype(BF16)
    kb_ref[0] = _norm_rope([sk], [lo], SWA_DIM, gsk_ref[0], rs_ref, SWA_DIM // 2)[0]
    g = seg(C_GATE, C_END)
    yield
    gt_ref[0] = (g * _sigmoid(g)).astype(BF16)


def _proj_call(l, x, ctx, mod, P, rope_mla, rope_swa):
    B, L, D = x.shape
    C = ctx.shape[1]
    S = C + L
    nt = S // TOK

    def wspec(arr):
        shp = arr.shape
        return pl.BlockSpec((1,) + shp[1:], lambda b, t: (l,) + (0,) * (len(shp) - 1))

    weights = [P["norm_g"], P["w_in"], P["g_cq"], P["w_uq"],
               P["g_ckv"], P["w_ukv"], P["g_k"], P["g_sk"]]
    nl = L // TOK
    nb = PROJ_NB
    assert len(weights) + 2 == N_PROJ_WEIGHTS and B % nb == 0
    in_specs = ([pl.BlockSpec((nb, TOK, D), lambda b, t: (b, jnp.minimum(t, nl - 1), 0)),
                 pl.BlockSpec((nb, TOK, D), lambda b, t: (b, 0, 0)),
                 pl.BlockSpec((1, nb, 3, D), lambda b, t: (l, jnp.where(t == nl, B // nb, b), 0, 0))]
                + [wspec(w) for w in weights]
                + [pl.BlockSpec((3, TOK, LANES), lambda b, t: (0, t, 0)),
                   pl.BlockSpec((3, TOK, LANES), lambda b, t: (0, t, 0))])
    out_shape = [jax.ShapeDtypeStruct((B, MLA_HEADS, S, LANES), BF16),
                 jax.ShapeDtypeStruct((B, MLA_HEADS, S, LANES), BF16),
                 jax.ShapeDtypeStruct((B, S, 512), BF16),
                 jax.ShapeDtypeStruct((B, S, 512), BF16),
                 jax.ShapeDtypeStruct((B, S, LANES), BF16),
                 jax.ShapeDtypeStruct((B, S, LANES), BF16),
                 jax.ShapeDtypeStruct((B, S, 512), F32),
                 jax.ShapeDtypeStruct((B, S, 1536), BF16)]
    out_specs = [pl.BlockSpec((nb, MLA_HEADS, TOK, LANES), lambda b, t: (b, 0, t, 0)),
                 pl.BlockSpec((nb, MLA_HEADS, TOK, LANES), lambda b, t: (b, 0, t, 0))]
    out_specs += [pl.BlockSpec((nb, TOK, s.shape[2]), lambda b, t: (b, t, 0)) for s in out_shape[2:]]
    return pl.pallas_call(
        functools.partial(_proj_kernel, n_lat_tiles=nl), out_shape=out_shape, grid=(B // nb, nt),
        in_specs=in_specs, out_specs=out_specs,
        compiler_params=_cparams(2), name="proj",
    )(x, ctx, mod, *weights, rope_mla, rope_swa)


MLA_SUBTILES = 4
MLA_VT_ROWS = 80


def _mla_kernel(q_ref, k_ref, v_ref, gq_ref, rt_ref, o_ref, vt_ref, *, n_chunks, tq):
    ones = jnp.ones((MLA_VT_ROWS - MLA_V, KV_CHUNK), BF16)

    @pl.when(pl.program_id(2) == 0)
    def _():
        for c in range(n_chunks):
            vt = v_ref[0, c].astype(F32).T.astype(BF16)
            vt_ref[0, c] = jnp.concatenate([vt[:MLA_V], ones], axis=0)
            vt_ref[1, c] = jnp.concatenate([vt[MLA_V:], ones], axis=0)

    for r0 in range(0, q_ref.shape[2], tq):
        rows = slice(r0, r0 + tq)
        cos, sin = rt_ref[0, :, rows], rt_ref[1, :, rows]
        gain = jnp.concatenate([gq_ref[0]] * (tq // LANES), axis=1)
        half = MLA_ROPE // 2

        def qmat(e):
            qt = q_ref[0, e, rows, :].astype(F32).T
            r = lax.rsqrt(jnp.sum(qt * qt, axis=0, keepdims=True) * (1.0 / MLA_QK) + EPS)
            y = qt * r * gain
            x1, x2 = y[MLA_NOPE:MLA_NOPE + half], y[MLA_NOPE + half:MLA_QK]
            return jnp.concatenate([y[:MLA_NOPE], x1 * cos - x2 * sin, x2 * cos + x1 * sin, y[MLA_QK:]],
                                   axis=0).astype(BF16)

        qts = [qmat(e) for e in range(2)]
        score = lambda c, e: jnp.dot(k_ref[0, e, c], qts[e], preferred_element_type=F32)
        ms = [None, None]
        accs = [None, None]
        sts = [score(0, e) for e in range(2)]
        for c in range(n_chunks):
            nxt = [score(c + 1, e) for e in range(2)] if c + 1 < n_chunks else None
            for e in range(2):
                mc = jnp.max(sts[e], axis=0, keepdims=True)
                m_new = mc if c == 0 else jnp.maximum(ms[e], mc)
                pt = jnp.exp2(sts[e] - m_new).astype(BF16)
                pv = jnp.dot(vt_ref[e, c], pt, preferred_element_type=F32)
                accs[e] = pv if c == 0 else accs[e] * jnp.exp2(ms[e] - m_new) + pv
                ms[e] = m_new
            sts = nxt
        ot = jnp.concatenate([accs[e][:MLA_V] / accs[e][MLA_V:MLA_V + 1] for e in range(2)], axis=0)
        o_ref[0, rows, :] = ot.T.astype(BF16)


def _mla_call(l, qa, ka, va, g_qt, rope_t, n_lat):
    B, H, S, _ = qa.shape
    half = MLA_ROPE // 2
    gspec = pl.BlockSpec((1, LANES, LANES), lambda b, p, t: (l, 0, 0))
    nc = S // KV_CHUNK
    tq = 2 * TOK
    bq = MLA_SUBTILES * tq if n_lat % (MLA_SUBTILES * tq) == 0 else tq
    k5 = ka.reshape(B, H, nc, KV_CHUNK, LANES)
    v4 = va.reshape(B, nc, KV_CHUNK, 512)
    o_lat = pl.pallas_call(
        functools.partial(_mla_kernel, n_chunks=nc, tq=tq),
        out_shape=jax.ShapeDtypeStruct((B, n_lat, 512), BF16),
        grid=(B, H // 2, n_lat // bq),
        in_specs=[pl.BlockSpec((1, 2, bq, LANES), lambda b, p, t: (b, p, t, 0)),
                  pl.BlockSpec((1, 2, nc, KV_CHUNK, LANES), lambda b, p, t: (b, p, 0, 0, 0)),
                  pl.BlockSpec((1, nc, KV_CHUNK, LANES), lambda b, p, t: (b, 0, 0, p)),
                  gspec,
                  pl.BlockSpec((2, half, bq), lambda b, p, t: (0, 0, t))],
        out_specs=pl.BlockSpec((1, bq, LANES), lambda b, p, t: (b, t, p)),
        scratch_shapes=[pltpu.VMEM((2, nc, MLA_VT_ROWS, KV_CHUNK), BF16)],
        compiler_params=_cparams(3), name="mla_attn",
    )(qa, k5, v4, g_qt, rope_t)
    cblk = n_lat // KV_CHUNK
    o_ctx = pl.pallas_call(
        functools.partial(_mla_kernel, n_chunks=1, tq=TOK),
        out_shape=jax.ShapeDtypeStruct((B, S - n_lat, 512), BF16),
        grid=(B, H // 2, 1),
        in_specs=[pl.BlockSpec((1, 2, TOK, LANES), lambda b, p, t: (b, p, cblk, 0)),
                  pl.BlockSpec((1, 2, 1, KV_CHUNK, LANES), lambda b, p, t: (b, p, cblk, 0, 0)),
                  pl.BlockSpec((1, 1, KV_CHUNK, LANES), lambda b, p, t: (b, cblk, 0, p)),
                  gspec,
                  pl.BlockSpec((2, half, TOK), lambda b, p, t: (0, 0, cblk))],
        out_specs=pl.BlockSpec((1, TOK, LANES), lambda b, p, t: (b, 0, p)),
        scratch_shapes=[pltpu.VMEM((2, 1, MLA_VT_ROWS, KV_CHUNK), BF16)],
        compiler_params=_cparams(3), name="mla_attn_ctx",
    )(qa, k5, v4, g_qt, rope_t)
    return o_lat, o_ctx


SWA_VT_ROWS = 80


def _swa_kernel(q_ref, k_ref, v_ref, sink_ref, gq_ref, rt_ref, o_ref, vt_ref, bias_ref, *, n_ctx, n_lat):
    blk = SWA_BLOCK
    heads = SWA_HEADS // SWA_KV_HEADS
    win = 3 * blk
    n_blocks = n_lat // blk
    ctx_blk = n_blocks
    group = min(SWA_UNROLL, n_blocks)
    first = pl.program_id(1) == 0
    sink = sink_ref[0, 0:1, :]

    ones = jnp.ones((SWA_VT_ROWS - SWA_DIM, blk), BF16)
    for i in range((n_lat + n_ctx) // blk):
        vt = v_ref[0, i * blk:(i + 1) * blk, :].astype(F32).T
        vt_ref[i] = jnp.concatenate([jnp.where(first, vt[:SWA_DIM], vt[SWA_DIM:]).astype(BF16), ones], axis=0)
    rel0 = (lax.broadcasted_iota(jnp.int32, (win, heads * blk), 0)
            - (lax.broadcasted_iota(jnp.int32, (win, heads * blk), 1) & (blk - 1)))
    for kind in range(3):
        bias_ref[kind] = jnp.where(jnp.abs(rel0 - kind * blk) <= SWA_WINDOW, 0.0, NEG)

    kc = k_ref[0, n_lat:n_lat + n_ctx, :]
    vtc = jnp.concatenate([vt_ref[ctx_blk + i] for i in range(n_ctx // blk)], axis=1)

    gain = gq_ref[0]
    half = SWA_DIM // 2

    def qmat(n):
        qt = q_ref[0, pl.ds(pl.multiple_of(n * blk, blk), blk), :].astype(F32).T
        cos, sin = rt_ref[n, 0], rt_ref[n, 1]
        cols = []
        for h in range(heads):
            x = qt[h * SWA_DIM:(h + 1) * SWA_DIM]
            y = x * lax.rsqrt(jnp.sum(x * x, axis=0, keepdims=True) * (1.0 / SWA_DIM) + EPS) * gain
            x1, x2 = y[:half], y[half:]
            cols.append(jnp.concatenate([x1 * cos - x2 * sin, x2 * cos + x1 * sin], axis=0))
        w = jnp.concatenate(cols, axis=1).astype(BF16)
        z = jnp.zeros_like(w)
        return jnp.where(first, jnp.concatenate([w, z], axis=0), jnp.concatenate([z, w], axis=0))

    def finish(r0, m, acc):
        ot = acc[:SWA_DIM] / (acc[SWA_DIM:SWA_DIM + 1] + jnp.exp2(sink - m))
        o4 = jnp.concatenate([ot[:, h * blk:(h + 1) * blk] for h in range(heads)], axis=0)
        o_ref[0, pl.ds(r0, blk), :] = o4.T.astype(BF16)

    for n in range(n_ctx // blk):
        r0 = n_lat + n * blk
        s_c = jnp.dot(kc, qmat(ctx_blk + n), preferred_element_type=F32)
        m = jnp.maximum(jnp.max(s_c, axis=0, keepdims=True), sink)
        finish(r0, m, jnp.dot(vtc, jnp.exp2(s_c - m).astype(BF16), preferred_element_type=F32))

    def window_block(n):
        return jnp.clip(n - 1, 0, n_blocks - 3)

    def scores(n):
        wb = window_block(n)
        w = qmat(n)
        kw = k_ref[0, pl.ds(pl.multiple_of(wb * blk, blk), win), :]
        return (jnp.dot(kc, w, preferred_element_type=F32),
                jnp.dot(kw, w, preferred_element_type=F32) + bias_ref[n - wb])

    def blocks(gi, carry):
        n0 = gi * group
        cur = scores(n0)
        for i in range(group):
            n = n0 + i
            nxt = scores(n + 1) if i + 1 < group else None
            s_c, s_w = cur
            m = jnp.maximum(jnp.maximum(jnp.max(s_c, axis=0, keepdims=True),
                                        jnp.max(s_w, axis=0, keepdims=True)), sink)
            wb = window_block(n)
            vtw = jnp.concatenate([vt_ref[wb + j] for j in range(3)], axis=1)
            acc = (jnp.dot(vtc, jnp.exp2(s_c - m).astype(BF16), preferred_element_type=F32)
                   + jnp.dot(vtw, jnp.exp2(s_w - m).astype(BF16), preferred_element_type=F32))
            finish(pl.multiple_of(n * blk, blk), m, acc)
            cur = nxt
        return carry

    assert n_blocks % group == 0
    lax.fori_loop(0, n_blocks // group, blocks, 0)


def _swa_call(l, qb, kb, vb, sinkrow, g_sqt, rope_t, n_ctx):
    B, S, _ = qb.shape
    nq = (SWA_HEADS // SWA_KV_HEADS) * SWA_BLOCK
    return pl.pallas_call(
        functools.partial(_swa_kernel, n_ctx=n_ctx, n_lat=S - n_ctx),
        out_shape=jax.ShapeDtypeStruct((B, S, 512), BF16),
        grid=(B, SWA_KV_HEADS),
        in_specs=[pl.BlockSpec((1, S, 256), lambda b, j: (b, 0, j)),
                  pl.BlockSpec((1, S, LANES), lambda b, j: (b, 0, 0)),
                  pl.BlockSpec((1, S, LANES), lambda b, j: (b, 0, 0)),
                  pl.BlockSpec((1, 8, nq), lambda b, j: (l * SWA_KV_HEADS + j, 0, 0)),
                  pl.BlockSpec((1, SWA_DIM, SWA_BLOCK), lambda b, j: (l, 0, 0)),
                  pl.BlockSpec(rope_t.shape, lambda b, j: (0, 0, 0, 0))],
        out_specs=pl.BlockSpec((1, S, 256), lambda b, j: (b, 0, j)),
        scratch_shapes=[pltpu.VMEM((S // SWA_BLOCK, SWA_VT_ROWS, SWA_BLOCK), BF16),
                        pltpu.VMEM((3, 3 * SWA_BLOCK, nq), F32)],
        compiler_params=_cparams(2), name="swa_attn",
    )(qb, kb, vb, sinkrow, g_sqt, rope_t)


def _gelu(y):
    return 0.5 * y * (1.0 + jnp.tanh(math.sqrt(2.0 / math.pi) * (y + 0.044715 * (y * y * y))))


def _s5_kernel(u_ref, wst_ref, wloc_ref, wcar_ref, at_ref, o_ref,
               ut_ref, utc_ref, yt_ref, ytc_ref, ere_ref, eim_ref, hfr_ref, hfi_ref, hbr_ref, hbi_ref,
               *, nb, n_lat_chunks, n_ctx_chunks):
    T, GP = S5_CHUNK, S5_GROUP
    n_chunks = n_lat_chunks + n_ctx_chunks
    n_lat = n_lat_chunks * T
    cw = nb * n_ctx_chunks
    tn_dims = (((0,), (0,)), ((), ()))
    nt_dims = (((1,), (1,)), ((), ()))
    gpl = LANES // GP

    for b in range(nb):
        for s in range(T):
            xs = u_ref[b, pl.ds(s, n_lat_chunks, stride=T), :]
            ut_ref[b, :, s] = xs.T.reshape(gpl, GP, n_lat_chunks).astype(BF16)
    zpad = jnp.zeros((LANES - cw, LANES), F32)
    for s in range(T):
        xs = jnp.concatenate([u_ref[b, pl.ds(n_lat + s, n_ctx_chunks, stride=T), :] for b in range(nb)]
                             + [zpad], axis=0)
        utc_ref[:, s] = xs.T.reshape(gpl, GP, LANES).astype(BF16)

    lane = lax.broadcasted_iota(jnp.int32, (2 * nb, LANES), 1)
    fwd = lane < S5_STATE
    lane_l = lax.broadcasted_iota(jnp.int32, (n_lat_chunks, LANES), 1) < S5_STATE
    lane_c = lax.broadcasted_iota(jnp.int32, (LANES, LANES), 1) < S5_STATE

    slab = 2 * nb
    lat_rows = lambda gb: pl.ds(gb, n_lat_chunks, stride=slab)
    ctx_rows = lambda gb: pl.ds(n_lat_chunks * slab + gb, n_ctx_chunks, stride=slab)

    def pair(gp, carry):
        for gl in range(2):
            g = gp * 2 + gl
            wst = wst_ref[g]
            for b in range(nb):
                e = lax.dot_general(ut_ref[b, g].reshape(T * GP, n_lat_chunks), wst, tn_dims,
                                    preferred_element_type=F32)
                ere_ref[lat_rows(gl * nb + b), :] = e[:, :LANES]
                eim_ref[lat_rows(gl * nb + b), :] = e[:, LANES:]
            ec = lax.dot_general(utc_ref[g].reshape(T * GP, LANES), wst, tn_dims, preferred_element_type=F32)
            for b in range(nb):
                ere_ref[ctx_rows(gl * nb + b), :] = ec[b * n_ctx_chunks:(b + 1) * n_ctx_chunks, :LANES]
                eim_ref[ctx_rows(gl * nb + b), :] = ec[b * n_ctx_chunks:(b + 1) * n_ctx_chunks, LANES:]

        a_re = jnp.concatenate([jnp.broadcast_to(at_ref[gp * 2 + gl, 0:1, :], (nb, LANES)) for gl in range(2)], 0)
        a_im = jnp.concatenate([jnp.broadcast_to(at_ref[gp * 2 + gl, 1:2, :], (nb, LANES)) for gl in range(2)], 0)

        def step(i, hc):
            h_re, h_im = hc
            cf = jnp.where(i < n_ctx_chunks, n_lat_chunks + i, i - n_ctx_chunks)
            cb = n_chunks - 1 - i
            sf = pl.ds(pl.multiple_of(cf * slab, slab), slab)
            sb = pl.ds(pl.multiple_of(cb * slab, slab), slab)
            hfr_ref[sf, :] = h_re
            hfi_ref[sf, :] = h_im
            hbr_ref[sb, :] = h_re
            hbi_ref[sb, :] = h_im
            e_re = jnp.where(fwd, ere_ref[sf, :], ere_ref[sb, :])
            e_im = jnp.where(fwd, eim_ref[sf, :], eim_ref[sb, :])
            return (a_re * h_re - a_im * h_im + e_re, a_re * h_im + a_im * h_re + e_im)

        zero = jnp.zeros((2 * nb, LANES), F32)
        lax.fori_loop(0, n_chunks, step, (zero, zero), unroll=8)

        for gl in range(2):
            g = gp * 2 + gl
            wloc, wcar = wloc_ref[g], wcar_ref[g]
            for b in range(nb):
                rows = lat_rows(gl * nb + b)
                h_cat = jnp.concatenate([jnp.where(lane_l, hfr_ref[rows, :], hbr_ref[rows, :]),
                                         jnp.where(lane_l, hfi_ref[rows, :], hbi_ref[rows, :])],
                                        axis=-1).astype(BF16)
                yt = (jnp.dot(wloc, ut_ref[b, g].reshape(T * GP, n_lat_chunks), preferred_element_type=F32)
                      + lax.dot_general(wcar, h_cat, nt_dims, preferred_element_type=F32))
                yt_ref[b, g] = _gelu(yt).astype(BF16).reshape(T, GP, n_lat_chunks)
            crow = lambda ref: jnp.concatenate(
                [ref[ctx_rows(gl * nb + b), :] for b in range(nb)]
                + [jnp.zeros((LANES - cw, LANES), F32)], axis=0)
            h_cat = jnp.concatenate([jnp.where(lane_c, crow(hfr_ref), crow(hbr_ref)),
                                     jnp.where(lane_c, crow(hfi_ref), crow(hbi_ref))], axis=-1).astype(BF16)
            ytc = (jnp.dot(wloc, utc_ref[g].reshape(T * GP, LANES), preferred_element_type=F32)
                   + lax.dot_general(wcar, h_cat, nt_dims, preferred_element_type=F32))
            ytc_ref[g] = _gelu(ytc).astype(BF16).reshape(T, GP, LANES)
        return carry

    lax.fori_loop(0, gpl // 2, pair, 0)

    for b in range(nb):
        for t in range(T):
            z = yt_ref[b, :, t].astype(F32).reshape(LANES, n_lat_chunks)
            o_ref[b, pl.ds(t, n_lat_chunks, stride=T), :] = z.T
    for t in range(T):
        z = ytc_ref[:, t].astype(F32).reshape(LANES, LANES).T
        for b in range(nb):
            o_ref[b, pl.ds(n_lat + t, n_ctx_chunks, stride=T), :] = z[b * n_ctx_chunks:(b + 1) * n_ctx_chunks]


def _s5_call(l, u, P, n_lat):
    B, S, W = u.shape
    nb = 4 if B % 4 == 0 else B
    T, GP = S5_CHUNK, S5_GROUP
    nlc, ncc = n_lat // T, (S - n_lat) // T
    gpl = LANES // GP
    nblk = W // LANES
    big = lambda n: pl.BlockSpec((nb, S, LANES), lambda j, hb: (hb, 0, j), pipeline_mode=pl.Buffered(n))
    wspec = lambda: pl.BlockSpec((gpl, T * GP, T * GP), lambda j, hb: (l * nblk + j, 0, 0))
    rows = 2 * nb * (nlc + ncc)
    return pl.pallas_call(
        functools.partial(_s5_kernel, nb=nb, n_lat_chunks=nlc, n_ctx_chunks=ncc),
        out_shape=jax.ShapeDtypeStruct((B, S, W), F32),
        grid=(nblk, B // nb),
        in_specs=[big(2), wspec(), wspec(), wspec(),
                  pl.BlockSpec((gpl, 2, LANES), lambda j, hb: (l * nblk + j, 0, 0))],
        out_specs=big(1),
        scratch_shapes=[pltpu.VMEM((nb, gpl, T, GP, nlc), BF16), pltpu.VMEM((gpl, T, GP, LANES), BF16),
                        pltpu.VMEM((nb, gpl, T, GP, nlc), BF16), pltpu.VMEM((gpl, T, GP, LANES), BF16)]
                       + [pltpu.VMEM((rows, LANES), F32)] * 6,
        compiler_params=_cparams(2), name="s5",
    )(u, P["s5_wst"], P["s5_wloc_t"], P["s5_wcar_t"], P["s5_at"])


def _out_kernel(x_ref, c_ref, mod_ref, oa_ref, oac_ref, ob_ref, gy_ref, gt_ref, wglu_ref, wout_ref,
                xo_ref, co_ref, *, n_lat_tiles):
    t = pl.program_id(1)
    cb = OUT_COLS
    nb, rows = x_ref.shape[0], x_ref.shape[1]
    stack = lambda ref: jnp.concatenate([ref[i] for i in range(nb)], axis=0)
    gyb = stack(gy_ref).astype(BF16)
    g = stack(gt_ref).astype(F32)
    oa = jnp.where(t == n_lat_tiles, stack(oac_ref), stack(oa_ref))
    m_a = (oa.astype(F32) * g[:, 0:512]).astype(BF16)
    m_b = (stack(ob_ref).astype(F32) * g[:, 512:1024]).astype(BF16)
    oc = []
    for j in range(512 // cb):
        za = jnp.dot(gyb, wglu_ref[0, :, j * cb:(j + 1) * cb], preferred_element_type=F32)
        zb = jnp.dot(gyb, wglu_ref[0, :, 512 + j * cb:512 + (j + 1) * cb], preferred_element_type=F32)
        oc.append((za * _sigmoid(zb) * g[:, 1024 + j * cb:1024 + (j + 1) * cb]).astype(BF16))
    m_c = jnp.concatenate(oc, axis=-1)
    gate = jnp.concatenate([jnp.broadcast_to(mod_ref[0, i][2:3], (rows, x_ref.shape[2])) for i in range(nb)],
                           axis=0)
    resid = jnp.where(t == n_lat_tiles, stack(c_ref), stack(x_ref))
    new = []
    for j in range(resid.shape[1] // cb):
        cols = slice(j * cb, (j + 1) * cb)
        upd = (jnp.dot(m_a, wout_ref[0, 0:512, cols], preferred_element_type=F32)
               + jnp.dot(m_b, wout_ref[0, 512:1024, cols], preferred_element_type=F32)
               + jnp.dot(m_c, wout_ref[0, 1024:1536, cols], preferred_element_type=F32))
        new.append(resid[:, cols] + gate[:, cols] * upd)

    @pl.when(t == n_lat_tiles)
    def _():
        for j, v in enumerate(new):
            for i in range(nb):
                co_ref[i, :, j * cb:(j + 1) * cb] = v[i * rows:(i + 1) * rows]

    @pl.when(t < n_lat_tiles)
    def _():
        for j, v in enumerate(new):
            for i in range(nb):
                xo_ref[i, :, j * cb:(j + 1) * cb] = v[i * rows:(i + 1) * rows]


def _out_call(l, x, ctx, mod, oa, oa_ctx, ob, gy, gt, P):
    B, L, D = x.shape
    C = ctx.shape[1]
    S = C + L
    nl = L // TOK
    nb = PROJ_NB
    xmap = lambda b, t: (b, jnp.minimum(t, nl - 1), 0)
    cmap = lambda b, t: (b, 0, 0)
    tmap = lambda b, t: (b, t, 0)
    return pl.pallas_call(
        functools.partial(_out_kernel, n_lat_tiles=nl),
        out_shape=[jax.ShapeDtypeStruct(x.shape, F32), jax.ShapeDtypeStruct(ctx.shape, F32)],
        grid=(B // nb, S // TOK),
        in_specs=[pl.BlockSpec((nb, TOK, D), xmap),
                  pl.BlockSpec((nb, TOK, D), cmap),
                  pl.BlockSpec((1, nb, 3, D), lambda b, t: (l, jnp.where(t == nl, B // nb, b), 0, 0)),
                  pl.BlockSpec((nb, TOK, 512), xmap),
                  pl.BlockSpec((nb, TOK, 512), cmap),
                  pl.BlockSpec((nb, TOK, 512), tmap),
                  pl.BlockSpec((nb, TOK, 512), tmap),
                  pl.BlockSpec((nb, TOK, 1536), tmap),
                  pl.BlockSpec((1, 512, 1024), lambda b, t: (l, 0, 0)),
                  pl.BlockSpec((1, 1536, D), lambda b, t: (l, 0, 0))],
        out_specs=[pl.BlockSpec((nb, TOK, D), xmap), pl.BlockSpec((nb, TOK, D), cmap)],
        compiler_params=_cparams(2), name="out",
    )(x, ctx, mod, oa, oa_ctx, ob, gy, gt, P["w_glu"], P["w_out"])


def _rope_angles(n_lat, n_ctx, rot_dim):
    rows = n_lat // GRID_W
    r_idx, c_idx = jnp.meshgrid(jnp.arange(rows), jnp.arange(GRID_W), indexing="ij")
    r_idx, c_idx = r_idx.reshape(-1), c_idx.reshape(-1)
    n_freq = rot_dim // 4
    freqs = ROPE_BASE ** (-jnp.arange(n_freq, dtype=F32) / n_freq)
    ang = jnp.concatenate([r_idx.astype(F32)[:, None] * freqs,
                           c_idx.astype(F32)[:, None] * freqs], axis=-1)
    return jnp.concatenate([ang, jnp.zeros((n_ctx, rot_dim // 2), F32)], axis=0)


def _rope_tables_t(n_lat, n_ctx, rot_dim):
    ang = _rope_angles(n_lat, n_ctx, rot_dim).T
    return jnp.stack([jnp.cos(ang), jnp.sin(ang)])


def _rope_tables(n_lat, n_ctx, rot_dim, lead, reps):
    ang = _rope_angles(n_lat, n_ctx, rot_dim)
    cos, sin, zero = jnp.cos(ang), jnp.sin(ang), jnp.zeros_like(ang)
    n = ang.shape[0]
    tail = LANES // reps - lead - rot_dim

    def pack(x1, x2, fill):
        unit = [jnp.full((n, lead), fill, F32), x1, x2, jnp.full((n, tail), fill, F32)]
        return jnp.concatenate(unit * reps, axis=-1)

    return jnp.stack([pack(cos, cos, 1.0), pack(zero, sin, 0.0), pack(-sin, zero, 0.0)])


def _prep_params(norm_g, w_in, w_out, mla_g_cq, mla_g_ckv, mla_w_uq, mla_w_ukv, mla_g_qn, mla_g_kn,
                 swa_g_qn, swa_g_kn, swa_sink, s5_a_re, s5_a_im, s5_log_dt, s5_b_re, s5_b_im,
                 s5_c_re, s5_c_im, s5_d, s5_w_glu):
    depth, D, _ = w_in.shape
    o_cq, o_ckv, o_kr, o_gm, o_sq, o_sk, o_sv, o_gs, o_u, o_g5, o_end = (
        0, 384, 640, 672, 1184, 1696, 1824, 1952, 2464, 2976, 3488)
    z = lambda n: jnp.zeros((depth, D, n), F32)
    w_in_p = jnp.concatenate([
        w_in[:, :, o_cq:o_kr],
        z(64), w_in[:, :, o_kr:o_gm], z(32),
        w_in[:, :, o_sq:o_gs],
        w_in[:, :, o_u:o_g5],
        w_in[:, :, o_gm:o_sq], w_in[:, :, o_gs:o_u], w_in[:, :, o_g5:o_end],
    ], axis=-1).astype(BF16)
    assert w_in_p.shape[-1] == C_END

    wq = mla_w_uq.reshape(depth, MLA_Q_RANK, MLA_HEADS, MLA_QK)
    wq = jnp.pad(wq, ((0, 0), (0, 0), (0, 0), (0, LANES - MLA_QK)))
    w_uq_p = wq.reshape(depth, MLA_Q_RANK, MLA_HEADS * LANES).astype(BF16)
    wkv = mla_w_ukv.reshape(depth, MLA_KV_RANK, MLA_HEADS, MLA_NOPE + MLA_V)
    wk = jnp.pad(wkv[..., :MLA_NOPE], ((0, 0), (0, 0), (0, 0), (0, LANES - MLA_NOPE)))
    w_ukv_p = jnp.concatenate([wk.reshape(depth, MLA_KV_RANK, MLA_HEADS * LANES),
                               wkv[..., MLA_NOPE:].reshape(depth, MLA_KV_RANK, MLA_HEADS * MLA_V)],
                              axis=-1).astype(BF16)

    pad_qk = lambda g: jnp.pad(g, ((0, 0), (0, LANES - MLA_QK)))[:, None, :]
    g_q = pad_qk(mla_g_qn * (MLA_QK ** -0.5 * LOG2E))
    g_qt = jnp.broadcast_to(jnp.swapaxes(g_q, 1, 2), (depth, LANES, LANES))
    g_k = pad_qk(mla_g_kn)
    g_sqt = jnp.broadcast_to((swa_g_qn * (SWA_DIM ** -0.5 * LOG2E))[:, :, None], (depth, SWA_DIM, SWA_BLOCK))
    g_sk = jnp.tile(swa_g_kn, (1, 2))[:, None, :]
    sink = (swa_sink * LOG2E).reshape(depth * SWA_KV_HEADS, 1, SWA_HEADS // SWA_KV_HEADS, 1)
    sinkrow = jnp.broadcast_to(sink, (depth * SWA_KV_HEADS, 8, SWA_HEADS // SWA_KV_HEADS, SWA_BLOCK))
    sinkrow = sinkrow.reshape(depth * SWA_KV_HEADS, 8, -1).astype(F32)

    T = S5_CHUNK
    A = lax.complex(s5_a_re, s5_a_im)
    dt = jnp.exp(s5_log_dt)[..., None]
    a_bar = jnp.exp(dt * A)
    b_bar = ((a_bar - 1.0) / A)[..., None] * lax.complex(s5_b_re, s5_b_im)
    c_mat = lax.complex(s5_c_re, s5_c_im)
    k_idx = jnp.arange(T + 1, dtype=F32)
    pw = jnp.exp(k_idx[None, None, None, :, None] * (dt * A)[:, :, :, None, :])
    hi = lax.Precision.HIGHEST
    tt = jnp.arange(T)
    GP, NG = S5_GROUP, depth * S5_GROUPS
    pw_f, pw_b = pw[:, 0], pw[:, 1]
    tap_f = jnp.einsum("lgpn,lgkn,lgnq->lgpkq", c_mat[:, 0], pw_f[:, :, T - 1 - tt], b_bar[:, 0],
                       precision=hi).real
    tap_b = jnp.einsum("lgpn,lgkn,lgnq->lgpkq", c_mat[:, 1], pw_b[:, :, tt], b_bar[:, 1],
                       precision=hi).real
    d_diag = s5_d.reshape(depth, S5_GROUPS, GP)[..., None] * jnp.eye(GP, dtype=F32)
    centre = tap_f[:, :, :, T - 1:] + tap_b[:, :, :, :1] + d_diag[:, :, :, None, :]
    krev = jnp.concatenate([tap_f[:, :, :, :T - 1], centre, tap_b[:, :, :, 1:]], axis=3)
    krev = krev.reshape(depth, S5_GROUPS, GP, (2 * T - 1) * GP)
    wloc_t = jnp.stack([krev[..., (T - 1 - t) * GP:(T - 1 - t) * GP + T * GP] for t in range(T)], axis=2)
    wloc_t = wloc_t.reshape(NG, T * GP, T * GP)
    b_t = jnp.swapaxes(b_bar, -1, -2)
    inc_f = pw_f[:, :, T - 1 - tt][:, :, :, None, :] * b_t[:, 0][:, :, None]
    inc_b = pw_b[:, :, tt][:, :, :, None, :] * b_t[:, 1][:, :, None]
    wst = jnp.concatenate([inc_f.real, inc_b.real, inc_f.imag, inc_b.imag], axis=-1)
    wst = wst.reshape(NG, T * GP, 4 * S5_STATE)
    ro_f = c_mat[:, 0][:, :, None] * pw_f[:, :, tt + 1][:, :, :, None, :]
    ro_b = c_mat[:, 1][:, :, None] * pw_b[:, :, T - tt][:, :, :, None, :]
    wcar_t = jnp.concatenate([ro_f.real, ro_b.real, -ro_f.imag, -ro_b.imag], axis=-1)
    wcar_t = wcar_t.reshape(NG, T * GP, 4 * S5_STATE)
    a_t = pw[:, :, :, T]
    at = jnp.stack([jnp.concatenate([a_t[:, 0].real, a_t[:, 1].real], axis=-1),
                    jnp.concatenate([a_t[:, 0].imag, a_t[:, 1].imag], axis=-1)], axis=2)
    at = at.reshape(NG, 2, LANES).astype(F32)

    return dict(norm_g=norm_g[:, None, :], w_in=w_in_p, g_cq=mla_g_cq[:, None, :], w_uq=w_uq_p, g_qt=g_qt,
                g_ckv=mla_g_ckv[:, None, :], w_ukv=w_ukv_p, g_k=g_k, g_sqt=g_sqt, g_sk=g_sk,
                sinkrow=sinkrow, s5_wloc_t=wloc_t.astype(BF16), s5_wst=wst.astype(BF16),
                s5_wcar_t=wcar_t.astype(BF16), s5_at=at, w_glu=s5_w_glu.astype(BF16),
                w_out=w_out.astype(BF16))


def kernel(x, c, ctx, c_ctx, norm_g, w_ada, b_ada, w_in, w_out, mla_g_cq, mla_g_ckv, mla_w_uq, mla_w_ukv, mla_g_qn, mla_g_kn, swa_g_qn, swa_g_kn, swa_sink, s5_a_re, s5_a_im, s5_log_dt, s5_b_re, s5_b_im, s5_c_re, s5_c_im, s5_d, s5_w_glu):
    B, L, D = x.shape
    C = ctx.shape[1]
    S = C + L
    depth = w_in.shape[0]
    assert B + PROJ_NB <= MOD_ROWS and C == TOK and L % (2 * TOK) == 0 and L % GRID_W == 0 and L >= 3 * SWA_BLOCK

    P = _prep_params(norm_g, w_in, w_out, mla_g_cq, mla_g_ckv, mla_w_uq, mla_w_ukv, mla_g_qn, mla_g_kn,
                     swa_g_qn, swa_g_kn, swa_sink, s5_a_re, s5_a_im, s5_log_dt, s5_b_re, s5_b_im,
                     s5_c_re, s5_c_im, s5_d, s5_w_glu)
    rope_mla = _rope_tables(L, C, MLA_ROPE, MLA_NOPE, 1)
    rope_swa = _rope_tables(L, C, SWA_DIM, 0, 2)
    rope_mla_t = _rope_tables_t(L, C, MLA_ROPE)
    rope_swa_t = jnp.transpose(_rope_tables_t(L, C, SWA_DIM).reshape(2, SWA_DIM // 2, S // SWA_BLOCK, SWA_BLOCK),
                               (2, 0, 1, 3))

    cc = jnp.concatenate([c, jnp.tile(c_ctx[None, :], (PROJ_NB, 1)),
                          jnp.zeros((MOD_ROWS - B - PROJ_NB, D), F32)], axis=0)
    mod = _ada_call(cc, w_ada, b_ada).reshape(depth, MOD_ROWS, 3, D)

    for l in range(depth):
        qa, ka, va, qb, kb, vb, u, gt = _proj_call(l, x, ctx, mod, P, rope_mla, rope_swa)
        oa, oa_ctx = _mla_call(l, qa, ka, va, P["g_qt"], rope_mla_t, L)
        ob = _swa_call(l, qb, kb, vb, P["sinkrow"], P["g_sqt"], rope_swa_t, C)
        gy = _s5_call(l, u, P, L)
        x, ctx = _out_call(l, x, ctx, mod, oa, oa_ctx, ob, gy, gt, P)
    return x
```

```python
import functools
import math

import jax
import jax.numpy as jnp
from jax import lax
from jax.experimental import pallas as pl
from jax.experimental.pallas import tpu as pltpu

F32 = jnp.float32
BF16 = jnp.bfloat16

GRID_W = 64
EPS = 1e-6
ROPE_BASE = 10000.0
NEG = -1e30
LOG2E = math.log2(math.e)

MLA_HEADS = 8
MLA_NOPE = 64
MLA_ROPE = 32
MLA_V = 64
MLA_QK = MLA_NOPE + MLA_ROPE
MLA_Q_RANK = 384
MLA_KV_RANK = 256

SWA_HEADS = 8
SWA_KV_HEADS = 2
SWA_DIM = 64
SWA_WINDOW = 128

S5_GROUP = 16
S5_GROUPS = 32
S5_STATE = 64
S5_CHUNK = 16

LANES = 128
TOK = 256
KV_CHUNK = 256
SWA_BLOCK = 128
MOD_ROWS = 16
PROJ_NB = 2
SWA_UNROLL = 32
OUT_COLS = 256
PROJ_SKEW = 1

C_CQ = 0
C_CKV = 384
C_KR = 640
C_SQ = 768
C_SK = 1280
C_SV = 1408
C_U = 1536
C_GATE = 2048
C_END = 3584

VMEM_LIMIT = 56 * 1024 * 1024


def _cparams(n_axes):
    return pltpu.CompilerParams(dimension_semantics=("arbitrary",) * n_axes,
                                vmem_limit_bytes=VMEM_LIMIT)


def _ada_kernel(c_ref, w_ref, b_ref, o_ref):
    cc = c_ref[...]
    s = cc * jax.nn.sigmoid(cc)
    o_ref[0] = jnp.dot(s, w_ref[0], preferred_element_type=F32,
                       precision=lax.Precision.HIGHEST) + b_ref[0]


def _ada_call(cc, w_ada, b_ada):
    depth, d, n3 = w_ada.shape
    tn = 768
    return pl.pallas_call(
        _ada_kernel,
        out_shape=jax.ShapeDtypeStruct((depth, MOD_ROWS, n3), F32),
        grid=(depth, n3 // tn),
        in_specs=[pl.BlockSpec((MOD_ROWS, d), lambda l, j: (0, 0)),
                  pl.BlockSpec((1, d, tn), lambda l, j: (l, 0, j)),
                  pl.BlockSpec((1, 1, tn), lambda l, j: (l, 0, j))],
        out_specs=pl.BlockSpec((1, MOD_ROWS, tn), lambda l, j: (l, 0, j)),
        compiler_params=_cparams(2),
        name="ada",
    )(cc, w_ada, b_ada.reshape(depth, 1, n3))


def _sigmoid(v):
    return 0.5 * jnp.tanh(0.5 * v) + 0.5


def _norm_rope(slots, lo_masks, dim, gain, tab_ref, half):
    sq = [s * s for s in slots]
    sums = []
    for s2, lo in zip(sq, lo_masks):
        if lo is None:
            sums.append((jnp.sum(s2, axis=-1, keepdims=True),))
        else:
            sums.append((jnp.sum(jnp.where(lo, s2, 0.0), axis=-1, keepdims=True),
                         jnp.sum(jnp.where(lo, 0.0, s2), axis=-1, keepdims=True)))
    ys = []
    for s, ss, lo in zip(slots, sums, lo_masks):
        rs = [lax.rsqrt(v * (1.0 / dim) + EPS) for v in ss]
        r = rs[0] if len(rs) == 1 else jnp.where(lo, rs[0], rs[1])
        ys.append(s * r * gain)
    up = [pltpu.roll(y, half, 1) for y in ys]
    dn = [pltpu.roll(y, LANES - half, 1) for y in ys]
    return [(y * tab_ref[0] + u * tab_ref[1] + d * tab_ref[2]).astype(BF16) for y, u, d in zip(ys, up, dn)]


N_PROJ_DATA = 3
N_PROJ_WEIGHTS = 10


def _proj_kernel(*refs, n_lat_tiles):
    data = refs[:N_PROJ_DATA]
    shared = refs[N_PROJ_DATA:N_PROJ_DATA + N_PROJ_WEIGHTS]
    outs = refs[N_PROJ_DATA + N_PROJ_WEIGHTS:]
    streams = []
    for i in range(data[0].shape[0]):
        one = pl.ds(i, 1)
        streams.append(_proj_tile(data[0].at[one], data[1].at[one], data[2].at[:, one], *shared,
                                  *[o.at[one] for o in outs], n_lat_tiles=n_lat_tiles))
    pending = list(enumerate(streams))
    step = 0
    while pending:
        for item in list(pending):
            if step >= item[0] * PROJ_SKEW and next(item[1], "done") == "done":
                pending.remove(item)
        step += 1


def _proj_tile(x_ref, c_ref, mod_ref, ng_ref, win_ref, gcq_ref, wuq_ref,
               gckv_ref, wukv_ref, gk_ref, gsk_ref, rm_ref, rs_ref,
               qa_ref, ka_ref, va_ref, qb_ref, kb_ref, vb_ref, u_ref, gt_ref, *, n_lat_tiles):
    t = pl.program_id(1)
    x = jnp.where(t == n_lat_tiles, c_ref[0], x_ref[0])
    mod = mod_ref[0, 0]
    y = x * lax.rsqrt(jnp.mean(x * x, axis=-1, keepdims=True) + EPS) * ng_ref[0]
    xn = (y * (1.0 + mod[1:2]) + mod[0:1]).astype(BF16)

    def seg(a, b):
        return jnp.dot(xn, win_ref[0, :, a:b], preferred_element_type=F32)

    lane = lax.broadcasted_iota(jnp.int32, (TOK, LANES), 1)
    lo = lane < 64

    yield
    cq = seg(C_CQ, C_CKV)
    ckv = seg(C_CKV, C_KR)
    kr = seg(C_KR, C_SQ)
    yield
    cqn = (cq * lax.rsqrt(jnp.mean(cq * cq, axis=-1, keepdims=True) + EPS) * gcq_ref[0]).astype(BF16)
    qf = jnp.dot(cqn, wuq_ref[0], preferred_element_type=F32)
    ckvn = (ckv * lax.rsqrt(jnp.mean(ckv * ckv, axis=-1, keepdims=True) + EPS) * gckv_ref[0]).astype(BF16)
    kvf = jnp.dot(ckvn, wukv_ref[0], preferred_element_type=F32)
    sq = seg(C_SQ, C_SK)
    yield
    for h in range(MLA_HEADS):
        qa_ref[0, h] = qf[:, h * LANES:(h + 1) * LANES].astype(BF16)

    sk = seg(C_SK, C_SV)
    vb_ref[0] = seg(C_SV, C_U).astype(BF16)
    yield
    slots = [kvf[:, h * LANES:(h + 1) * LANES] + kr for h in range(MLA_HEADS)]
    for h, o in enumerate(_norm_rope(slots, [None] * MLA_HEADS, MLA_QK, gk_ref[0], rm_ref, MLA_ROPE // 2)):
        ka_ref[0, h] = jnp.where(lane == MLA_QK, jnp.ones_like(o), o)
    va_ref[0] = kvf[:, MLA_HEADS * LANES:].astype(BF16)
    u_ref[0] = seg(C_U, C_GATE)
    yield
    qb_ref[0] = sq.astype(BF16)
    kb_ref[0] = _norm_rope([sk], [lo], SWA_DIM, gsk_ref[0], rs_ref, SWA_DIM // 2)[0]
    g = seg(C_GATE, C_END)
    yield
    gt_ref[0] = (g * _sigmoid(g)).astype(BF16)


def _proj_call(l, x, ctx, mod, P, rope_mla, rope_swa):
    B, L, D = x.shape
    C = ctx.shape[1]
    S = C + L
    nt = S // TOK

    def wspec(arr):
        shp = arr.shape
        return pl.BlockSpec((1,) + shp[1:], lambda b, t: (l,) + (0,) * (len(shp) - 1))

    weights = [P["norm_g"], P["w_in"], P["g_cq"], P["w_uq"],
               P["g_ckv"], P["w_ukv"], P["g_k"], P["g_sk"]]
    nl = L // TOK
    nb = PROJ_NB
    assert len(weights) + 2 == N_PROJ_WEIGHTS and B % nb == 0
    in_specs = ([pl.BlockSpec((nb, TOK, D), lambda b, t: (b, jnp.minimum(t, nl - 1), 0)),
                 pl.BlockSpec((nb, TOK, D), lambda b, t: (b, 0, 0)),
                 pl.BlockSpec((1, nb, 3, D), lambda b, t: (l, jnp.where(t == nl, B // nb, b), 0, 0))]
                + [wspec(w) for w in weights]
                + [pl.BlockSpec((3, TOK, LANES), lambda b, t: (0, t, 0)),
                   pl.BlockSpec((3, TOK, LANES), lambda b, t: (0, t, 0))])
    out_shape = [jax.ShapeDtypeStruct((B, MLA_HEADS, S, LANES), BF16),
                 jax.ShapeDtypeStruct((B, MLA_HEADS, S, LANES), BF16),
                 jax.ShapeDtypeStruct((B, S, 512), BF16),
                 jax.ShapeDtypeStruct((B, S, 512), BF16),
                 jax.ShapeDtypeStruct((B, S, LANES), BF16),
                 jax.ShapeDtypeStruct((B, S, LANES), BF16),
                 jax.ShapeDtypeStruct((B, S, 512), F32),
                 jax.ShapeDtypeStruct((B, S, 1536), BF16)]
    out_specs = [pl.BlockSpec((nb, MLA_HEADS, TOK, LANES), lambda b, t: (b, 0, t, 0)),
                 pl.BlockSpec((nb, MLA_HEADS, TOK, LANES), lambda b, t: (b, 0, t, 0))]
    out_specs += [pl.BlockSpec((nb, TOK, s.shape[2]), lambda b, t: (b, t, 0)) for s in out_shape[2:]]
    return pl.pallas_call(
        functools.partial(_proj_kernel, n_lat_tiles=nl), out_shape=out_shape, grid=(B // nb, nt),
        in_specs=in_specs, out_specs=out_specs,
        compiler_params=_cparams(2), name="proj",
    )(x, ctx, mod, *weights, rope_mla, rope_swa)


MLA_SUBTILES = 4
MLA_VT_ROWS = 80


MLA_SAFE_RANGE = 60.0


def _mla_kernel(rb_ref, q_ref, k_ref, v_ref, gq_ref, rt_ref, o_ref, vt_ref, *, n_chunks, tq, layer):
    ones = jnp.ones((MLA_VT_ROWS - MLA_V, KV_CHUNK), BF16)

    @pl.when(pl.program_id(2) == 0)
    def _():
        for c in range(n_chunks):
            vt = v_ref[0, c].astype(F32).T.astype(BF16)
            vt_ref[0, c] = jnp.concatenate([vt[:MLA_V], ones], axis=0)
            vt_ref[1, c] = jnp.concatenate([vt[MLA_V:], ones], axis=0)

    bound = rb_ref[layer]
    half = MLA_ROPE // 2

    def sweep(fixed_shift):
        for r0 in range(0, q_ref.shape[2], tq):
            rows = slice(r0, r0 + tq)
            cos, sin = rt_ref[0, :, rows], rt_ref[1, :, rows]
            gain = jnp.concatenate([gq_ref[0]] * (tq // LANES), axis=1)
            pad = jnp.zeros((LANES - MLA_QK, tq), F32)
            if fixed_shift:
                row = lax.broadcasted_iota(jnp.int32, pad.shape, 0)
                pad = jnp.where(row == 0, -bound, pad)

            def qmat(e):
                qt = q_ref[0, e, rows, :].astype(F32).T
                r = lax.rsqrt(jnp.sum(qt * qt, axis=0, keepdims=True) * (1.0 / MLA_QK) + EPS)
                y = qt * r * gain
                x1, x2 = y[MLA_NOPE:MLA_NOPE + half], y[MLA_NOPE + half:MLA_QK]
                return jnp.concatenate([y[:MLA_NOPE], x1 * cos - x2 * sin, x2 * cos + x1 * sin, pad],
                                       axis=0).astype(BF16)

            qts = [qmat(e) for e in range(2)]
            score = lambda c, e: jnp.dot(k_ref[0, e, c], qts[e], preferred_element_type=F32)
            ms = [None, None]
            accs = [None, None]
            sts = [score(0, e) for e in range(2)]
            for c in range(n_chunks):
                nxt = [score(c + 1, e) for e in range(2)] if c + 1 < n_chunks else None
                for e in range(2):
                    if fixed_shift:
                        pv = jnp.dot(vt_ref[e, c], jnp.exp2(sts[e]).astype(BF16), preferred_element_type=F32)
                        accs[e] = pv if c == 0 else accs[e] + pv
                    else:
                        mc = jnp.max(sts[e], axis=0, keepdims=True)
                        m_new = mc if c == 0 else jnp.maximum(ms[e], mc)
                        pt = jnp.exp2(sts[e] - m_new).astype(BF16)
                        pv = jnp.dot(vt_ref[e, c], pt, preferred_element_type=F32)
                        accs[e] = pv if c == 0 else accs[e] * jnp.exp2(ms[e] - m_new) + pv
                        ms[e] = m_new
                sts = nxt
            ot = jnp.concatenate([accs[e][:MLA_V] / accs[e][MLA_V:MLA_V + 1] for e in range(2)], axis=0)
            o_ref[0, rows, :] = ot.T.astype(BF16)

    @pl.when(bound < MLA_SAFE_RANGE)
    def _():
        sweep(True)

    @pl.when(jnp.logical_not(bound < MLA_SAFE_RANGE))
    def _():
        sweep(False)


def _mla_call(l, qa, ka, va, g_qt, rope_t, r_bound, n_lat):
    B, H, S, _ = qa.shape
    half = MLA_ROPE // 2
    gspec = pl.BlockSpec((1, LANES, LANES), lambda b, p, t: (l, 0, 0))
    sspec = pl.BlockSpec(memory_space=pltpu.SMEM)
    nc = S // KV_CHUNK
    tq = 2 * TOK
    bq = MLA_SUBTILES * tq if n_lat % (MLA_SUBTILES * tq) == 0 else tq
    k5 = ka.reshape(B, H, nc, KV_CHUNK, LANES)
    v4 = va.reshape(B, nc, KV_CHUNK, 512)
    o_lat = pl.pallas_call(
        functools.partial(_mla_kernel, n_chunks=nc, tq=tq, layer=l),
        out_shape=jax.ShapeDtypeStruct((B, n_lat, 512), BF16),
        grid=(B, H // 2, n_lat // bq),
        in_specs=[sspec,
                  pl.BlockSpec((1, 2, bq, LANES), lambda b, p, t: (b, p, t, 0)),
                  pl.BlockSpec((1, 2, nc, KV_CHUNK, LANES), lambda b, p, t: (b, p, 0, 0, 0)),
                  pl.BlockSpec((1, nc, KV_CHUNK, LANES), lambda b, p, t: (b, 0, 0, p)),
                  gspec,
                  pl.BlockSpec((2, half, bq), lambda b, p, t: (0, 0, t))],
        out_specs=pl.BlockSpec((1, bq, LANES), lambda b, p, t: (b, t, p)),
        scratch_shapes=[pltpu.VMEM((2, nc, MLA_VT_ROWS, KV_CHUNK), BF16)],
        compiler_params=_cparams(3), name="mla_attn",
    )(r_bound, qa, k5, v4, g_qt, rope_t)
    cblk = n_lat // KV_CHUNK
    o_ctx = pl.pallas_call(
        functools.partial(_mla_kernel, n_chunks=1, tq=TOK, layer=l),
        out_shape=jax.ShapeDtypeStruct((B, S - n_lat, 512), BF16),
        grid=(B, H // 2, 1),
        in_specs=[sspec,
                  pl.BlockSpec((1, 2, TOK, LANES), lambda b, p, t: (b, p, cblk, 0)),
                  pl.BlockSpec((1, 2, 1, KV_CHUNK, LANES), lambda b, p, t: (b, p, cblk, 0, 0)),
                  pl.BlockSpec((1, 1, KV_CHUNK, LANES), lambda b, p, t: (b, cblk, 0, p)),
                  gspec,
                  pl.BlockSpec((2, half, TOK), lambda b, p, t: (0, 0, cblk))],
        out_specs=pl.BlockSpec((1, TOK, LANES), lambda b, p, t: (b, 0, p)),
        scratch_shapes=[pltpu.VMEM((2, 1, MLA_VT_ROWS, KV_CHUNK), BF16)],
        compiler_params=_cparams(3), name="mla_attn_ctx",
    )(r_bound, qa, k5, v4, g_qt, rope_t)
    return o_lat, o_ctx


SWA_VT_ROWS = 80


def _swa_kernel(q_ref, k_ref, v_ref, sink_ref, gq_ref, rt_ref, o_ref, vt_ref, bias_ref, *, n_ctx, n_lat):
    blk = SWA_BLOCK
    heads = SWA_HEADS // SWA_KV_HEADS
    win = 3 * blk
    n_blocks = n_lat // blk
    ctx_blk = n_blocks
    group = min(SWA_UNROLL, n_blocks)
    first = pl.program_id(1) == 0
    sink = sink_ref[0, 0:1, :]

    ones = jnp.ones((SWA_VT_ROWS - SWA_DIM, blk), BF16)
    for i in range((n_lat + n_ctx) // blk):
        vt = v_ref[0, i * blk:(i + 1) * blk, :].astype(F32).T
        vt_ref[i] = jnp.concatenate([jnp.where(first, vt[:SWA_DIM], vt[SWA_DIM:]).astype(BF16), ones], axis=0)
    rel0 = (lax.broadcasted_iota(jnp.int32, (win, heads * blk), 0)
            - (lax.broadcasted_iota(jnp.int32, (win, heads * blk), 1) & (blk - 1)))
    for kind in range(3):
        bias_ref[kind] = jnp.where(jnp.abs(rel0 - kind * blk) <= SWA_WINDOW, 0.0, NEG)

    kc = k_ref[0, n_lat:n_lat + n_ctx, :]
    vtc = jnp.concatenate([vt_ref[ctx_blk + i] for i in range(n_ctx // blk)], axis=1)

    gain = gq_ref[0]
    half = SWA_DIM // 2

    def qmat(n):
        qt = q_ref[0, pl.ds(pl.multiple_of(n * blk, blk), blk), :].astype(F32).T
        cos, sin = rt_ref[n, 0], rt_ref[n, 1]
        cols = []
        for h in range(heads):
            x = qt[h * SWA_DIM:(h + 1) * SWA_DIM]
            y = x * lax.rsqrt(jnp.sum(x * x, axis=0, keepdims=True) * (1.0 / SWA_DIM) + EPS) * gain
            x1, x2 = y[:half], y[half:]
            cols.append(jnp.concatenate([x1 * cos - x2 * sin, x2 * cos + x1 * sin], axis=0))
        w = jnp.concatenate(cols, axis=1).astype(BF16)
        z = jnp.zeros_like(w)
        return jnp.where(first, jnp.concatenate([w, z], axis=0), jnp.concatenate([z, w], axis=0))

    def finish(r0, m, acc):
        ot = acc[:SWA_DIM] / (acc[SWA_DIM:SWA_DIM + 1] + jnp.exp2(sink - m))
        o4 = jnp.concatenate([ot[:, h * blk:(h + 1) * blk] for h in range(heads)], axis=0)
        o_ref[0, pl.ds(r0, blk), :] = o4.T.astype(BF16)

    for n in range(n_ctx // blk):
        r0 = n_lat + n * blk
        s_c = jnp.dot(kc, qmat(ctx_blk + n), preferred_element_type=F32)
        m = jnp.maximum(jnp.max(s_c, axis=0, keepdims=True), sink)
        finish(r0, m, jnp.dot(vtc, jnp.exp2(s_c - m).astype(BF16), preferred_element_type=F32))

    def window_block(n):
        return jnp.clip(n - 1, 0, n_blocks - 3)

    def scores(n):
        wb = window_block(n)
        w = qmat(n)
        kw = k_ref[0, pl.ds(pl.multiple_of(wb * blk, blk), win), :]
        return (jnp.dot(kc, w, preferred_element_type=F32),
                jnp.dot(kw, w, preferred_element_type=F32) + bias_ref[n - wb])

    def blocks(gi, carry):
        n0 = gi * group
        cur = scores(n0)
        for i in range(group):
            n = n0 + i
            nxt = scores(n + 1) if i + 1 < group else None
            s_c, s_w = cur
            m = jnp.maximum(jnp.maximum(jnp.max(s_c, axis=0, keepdims=True),
                                        jnp.max(s_w, axis=0, keepdims=True)), sink)
            wb = window_block(n)
            vtw = jnp.concatenate([vt_ref[wb + j] for j in range(3)], axis=1)
            acc = (jnp.dot(vtc, jnp.exp2(s_c - m).astype(BF16), preferred_element_type=F32)
                   + jnp.dot(vtw, jnp.exp2(s_w - m).astype(BF16), preferred_element_type=F32))
            finish(pl.multiple_of(n * blk, blk), m, acc)
            cur = nxt
        return carry

    assert n_blocks % group == 0
    lax.fori_loop(0, n_blocks // group, blocks, 0)


def _swa_call(l, qb, kb, vb, sinkrow, g_sqt, rope_t, n_ctx):
    B, S, _ = qb.shape
    nq = (SWA_HEADS // SWA_KV_HEADS) * SWA_BLOCK
    return pl.pallas_call(
        functools.partial(_swa_kernel, n_ctx=n_ctx, n_lat=S - n_ctx),
        out_shape=jax.ShapeDtypeStruct((B, S, 512), BF16),
        grid=(B, SWA_KV_HEADS),
        in_specs=[pl.BlockSpec((1, S, 256), lambda b, j: (b, 0, j)),
                  pl.BlockSpec((1, S, LANES), lambda b, j: (b, 0, 0)),
                  pl.BlockSpec((1, S, LANES), lambda b, j: (b, 0, 0)),
                  pl.BlockSpec((1, 8, nq), lambda b, j: (l * SWA_KV_HEADS + j, 0, 0)),
                  pl.BlockSpec((1, SWA_DIM, SWA_BLOCK), lambda b, j: (l, 0, 0)),
                  pl.BlockSpec(rope_t.shape, lambda b, j: (0, 0, 0, 0))],
        out_specs=pl.BlockSpec((1, S, 256), lambda b, j: (b, 0, j)),
        scratch_shapes=[pltpu.VMEM((S // SWA_BLOCK, SWA_VT_ROWS, SWA_BLOCK), BF16),
                        pltpu.VMEM((3, 3 * SWA_BLOCK, nq), F32)],
        compiler_params=_cparams(2), name="swa_attn",
    )(qb, kb, vb, sinkrow, g_sqt, rope_t)


def _gelu(y):
    return 0.5 * y * (1.0 + jnp.tanh(math.sqrt(2.0 / math.pi) * (y + 0.044715 * (y * y * y))))


def _s5_kernel(u_ref, wst_ref, wloc_ref, wcar_ref, at_ref, o_ref,
               ut_ref, utc_ref, yt_ref, ytc_ref, ere_ref, eim_ref, hfr_ref, hfi_ref, hbr_ref, hbi_ref,
               *, nb, n_lat_chunks, n_ctx_chunks):
    T, GP = S5_CHUNK, S5_GROUP
    n_chunks = n_lat_chunks + n_ctx_chunks
    n_lat = n_lat_chunks * T
    cw = nb * n_ctx_chunks
    tn_dims = (((0,), (0,)), ((), ()))
    nt_dims = (((1,), (1,)), ((), ()))
    gpl = LANES // GP

    for b in range(nb):
        for s in range(T):
            xs = u_ref[b, pl.ds(s, n_lat_chunks, stride=T), :]
            ut_ref[b, :, s] = xs.T.reshape(gpl, GP, n_lat_chunks).astype(BF16)
    zpad = jnp.zeros((LANES - cw, LANES), F32)
    for s in range(T):
        xs = jnp.concatenate([u_ref[b, pl.ds(n_lat + s, n_ctx_chunks, stride=T), :] for b in range(nb)]
                             + [zpad], axis=0)
        utc_ref[:, s] = xs.T.reshape(gpl, GP, LANES).astype(BF16)

    lane = lax.broadcasted_iota(jnp.int32, (2 * nb, LANES), 1)
    fwd = lane < S5_STATE
    lane_l = lax.broadcasted_iota(jnp.int32, (n_lat_chunks, LANES), 1) < S5_STATE
    lane_c = lax.broadcasted_iota(jnp.int32, (LANES, LANES), 1) < S5_STATE

    slab = 2 * nb
    lat_rows = lambda gb: pl.ds(gb, n_lat_chunks, stride=slab)
    ctx_rows = lambda gb: pl.ds(n_lat_chunks * slab + gb, n_ctx_chunks, stride=slab)

    def pair(gp, carry):
        for gl in range(2):
            g = gp * 2 + gl
            wst = wst_ref[g]
            for b in range(nb):
                e = lax.dot_general(ut_ref[b, g].reshape(T * GP, n_lat_chunks), wst, tn_dims,
                                    preferred_element_type=F32)
                ere_ref[lat_rows(gl * nb + b), :] = e[:, :LANES]
                eim_ref[lat_rows(gl * nb + b), :] = e[:, LANES:]
            ec = lax.dot_general(utc_ref[g].reshape(T * GP, LANES), wst, tn_dims, preferred_element_type=F32)
            for b in range(nb):
                ere_ref[ctx_rows(gl * nb + b), :] = ec[b * n_ctx_chunks:(b + 1) * n_ctx_chunks, :LANES]
                eim_ref[ctx_rows(gl * nb + b), :] = ec[b * n_ctx_chunks:(b + 1) * n_ctx_chunks, LANES:]

        a_re = jnp.concatenate([jnp.broadcast_to(at_ref[gp * 2 + gl, 0:1, :], (nb, LANES)) for gl in range(2)], 0)
        a_im = jnp.concatenate([jnp.broadcast_to(at_ref[gp * 2 + gl, 1:2, :], (nb, LANES)) for gl in range(2)], 0)

        def step(i, hc):
            h_re, h_im = hc
            cf = jnp.where(i < n_ctx_chunks, n_lat_chunks + i, i - n_ctx_chunks)
            cb = n_chunks - 1 - i
            sf = pl.ds(pl.multiple_of(cf * slab, slab), slab)
            sb = pl.ds(pl.multiple_of(cb * slab, slab), slab)
            hfr_ref[sf, :] = h_re
            hfi_ref[sf, :] = h_im
            hbr_ref[sb, :] = h_re
            hbi_ref[sb, :] = h_im
            e_re = jnp.where(fwd, ere_ref[sf, :], ere_ref[sb, :])
            e_im = jnp.where(fwd, eim_ref[sf, :], eim_ref[sb, :])
            return (a_re * h_re - a_im * h_im + e_re, a_re * h_im + a_im * h_re + e_im)

        zero = jnp.zeros((2 * nb, LANES), F32)
        lax.fori_loop(0, n_chunks, step, (zero, zero), unroll=8)

        for gl in range(2):
            g = gp * 2 + gl
            wloc, wcar = wloc_ref[g], wcar_ref[g]
            for b in range(nb):
                rows = lat_rows(gl * nb + b)
                h_cat = jnp.concatenate([jnp.where(lane_l, hfr_ref[rows, :], hbr_ref[rows, :]),
                                         jnp.where(lane_l, hfi_ref[rows, :], hbi_ref[rows, :])],
                                        axis=-1).astype(BF16)
                yt = (jnp.dot(wloc, ut_ref[b, g].reshape(T * GP, n_lat_chunks), preferred_element_type=F32)
                      + lax.dot_general(wcar, h_cat, nt_dims, preferred_element_type=F32))
                yt_ref[b, g] = _gelu(yt).astype(BF16).reshape(T, GP, n_lat_chunks)
            crow = lambda ref: jnp.concatenate(
                [ref[ctx_rows(gl * nb + b), :] for b in range(nb)]
                + [jnp.zeros((LANES - cw, LANES), F32)], axis=0)
            h_cat = jnp.concatenate([jnp.where(lane_c, crow(hfr_ref), crow(hbr_ref)),
                                     jnp.where(lane_c, crow(hfi_ref), crow(hbi_ref))], axis=-1).astype(BF16)
            ytc = (jnp.dot(wloc, utc_ref[g].reshape(T * GP, LANES), preferred_element_type=F32)
                   + lax.dot_general(wcar, h_cat, nt_dims, preferred_element_type=F32))
            ytc_ref[g] = _gelu(ytc).astype(BF16).reshape(T, GP, LANES)
        return carry

    lax.fori_loop(0, gpl // 2, pair, 0)

    for b in range(nb):
        for t in range(T):
            z = yt_ref[b, :, t].astype(F32).reshape(LANES, n_lat_chunks)
            o_ref[b, pl.ds(t, n_lat_chunks, stride=T), :] = z.T
    for t in range(T):
        z = ytc_ref[:, t].astype(F32).reshape(LANES, LANES).T
        for b in range(nb):
            o_ref[b, pl.ds(n_lat + t, n_ctx_chunks, stride=T), :] = z[b * n_ctx_chunks:(b + 1) * n_ctx_chunks]


def _s5_call(l, u, P, n_lat):
    B, S, W = u.shape
    nb = 4 if B % 4 == 0 else B
    T, GP = S5_CHUNK, S5_GROUP
    nlc, ncc = n_lat // T, (S - n_lat) // T
    gpl = LANES // GP
    nblk = W // LANES
    big = lambda n: pl.BlockSpec((nb, S, LANES), lambda j, hb: (hb, 0, j), pipeline_mode=pl.Buffered(n))
    wspec = lambda: pl.BlockSpec((gpl, T * GP, T * GP), lambda j, hb: (l * nblk + j, 0, 0))
    rows = 2 * nb * (nlc + ncc)
    return pl.pallas_call(
        functools.partial(_s5_kernel, nb=nb, n_lat_chunks=nlc, n_ctx_chunks=ncc),
        out_shape=jax.ShapeDtypeStruct((B, S, W), F32),
        grid=(nblk, B // nb),
        in_specs=[big(2), wspec(), wspec(), wspec(),
                  pl.BlockSpec((gpl, 2, LANES), lambda j, hb: (l * nblk + j, 0, 0))],
        out_specs=big(1),
        scratch_shapes=[pltpu.VMEM((nb, gpl, T, GP, nlc), BF16), pltpu.VMEM((gpl, T, GP, LANES), BF16),
                        pltpu.VMEM((nb, gpl, T, GP, nlc), BF16), pltpu.VMEM((gpl, T, GP, LANES), BF16)]
                       + [pltpu.VMEM((rows, LANES), F32)] * 6,
        compiler_params=_cparams(2), name="s5",
    )(u, P["s5_wst"], P["s5_wloc_t"], P["s5_wcar_t"], P["s5_at"])


def _out_kernel(x_ref, c_ref, mod_ref, oa_ref, oac_ref, ob_ref, gy_ref, gt_ref, wglu_ref, wout_ref,
                xo_ref, co_ref, *, n_lat_tiles):
    t = pl.program_id(1)
    cb = OUT_COLS
    nb, rows = x_ref.shape[0], x_ref.shape[1]
    stack = lambda ref: jnp.concatenate([ref[i] for i in range(nb)], axis=0)
    gyb = stack(gy_ref).astype(BF16)
    g = stack(gt_ref).astype(F32)
    oa = jnp.where(t == n_lat_tiles, stack(oac_ref), stack(oa_ref))
    m_a = (oa.astype(F32) * g[:, 0:512]).astype(BF16)
    m_b = (stack(ob_ref).astype(F32) * g[:, 512:1024]).astype(BF16)
    oc = []
    for j in range(512 // cb):
        za = jnp.dot(gyb, wglu_ref[0, :, j * cb:(j + 1) * cb], preferred_element_type=F32)
        zb = jnp.dot(gyb, wglu_ref[0, :, 512 + j * cb:512 + (j + 1) * cb], preferred_element_type=F32)
        oc.append((za * _sigmoid(zb) * g[:, 1024 + j * cb:1024 + (j + 1) * cb]).astype(BF16))
    m_c = jnp.concatenate(oc, axis=-1)
    gate = jnp.concatenate([jnp.broadcast_to(mod_ref[0, i][2:3], (rows, x_ref.shape[2])) for i in range(nb)],
                           axis=0)
    resid = jnp.where(t == n_lat_tiles, stack(c_ref), stack(x_ref))
    new = []
    for j in range(resid.shape[1] // cb):
        cols = slice(j * cb, (j + 1) * cb)
        upd = (jnp.dot(m_a, wout_ref[0, 0:512, cols], preferred_element_type=F32)
               + jnp.dot(m_b, wout_ref[0, 512:1024, cols], preferred_element_type=F32)
               + jnp.dot(m_c, wout_ref[0, 1024:1536, cols], preferred_element_type=F32))
        new.append(resid[:, cols] + gate[:, cols] * upd)

    @pl.when(t == n_lat_tiles)
    def _():
        for j, v in enumerate(new):
            for i in range(nb):
                co_ref[i, :, j * cb:(j + 1) * cb] = v[i * rows:(i + 1) * rows]

    @pl.when(t < n_lat_tiles)
    def _():
        for j, v in enumerate(new):
            for i in range(nb):
                xo_ref[i, :, j * cb:(j + 1) * cb] = v[i * rows:(i + 1) * rows]


def _out_call(l, x, ctx, mod, oa, oa_ctx, ob, gy, gt, P):
    B, L, D = x.shape
    C = ctx.shape[1]
    S = C + L
    nl = L // TOK
    nb = PROJ_NB
    xmap = lambda b, t: (b, jnp.minimum(t, nl - 1), 0)
    cmap = lambda b, t: (b, 0, 0)
    tmap = lambda b, t: (b, t, 0)
    return pl.pallas_call(
        functools.partial(_out_kernel, n_lat_tiles=nl),
        out_shape=[jax.ShapeDtypeStruct(x.shape, F32), jax.ShapeDtypeStruct(ctx.shape, F32)],
        grid=(B // nb, S // TOK),
        in_specs=[pl.BlockSpec((nb, TOK, D), xmap),
                  pl.BlockSpec((nb, TOK, D), cmap),
                  pl.BlockSpec((1, nb, 3, D), lambda b, t: (l, jnp.where(t == nl, B // nb, b), 0, 0)),
                  pl.BlockSpec((nb, TOK, 512), xmap),
                  pl.BlockSpec((nb, TOK, 512), cmap),
                  pl.BlockSpec((nb, TOK, 512), tmap),
                  pl.BlockSpec((nb, TOK, 512), tmap),
                  pl.BlockSpec((nb, TOK, 1536), tmap),
                  pl.BlockSpec((1, 512, 1024), lambda b, t: (l, 0, 0)),
                  pl.BlockSpec((1, 1536, D), lambda b, t: (l, 0, 0))],
        out_specs=[pl.BlockSpec((nb, TOK, D), xmap), pl.BlockSpec((nb, TOK, D), cmap)],
        compiler_params=_cparams(2), name="out",
    )(x, ctx, mod, oa, oa_ctx, ob, gy, gt, P["w_glu"], P["w_out"])


def _rope_angles(n_lat, n_ctx, rot_dim):
    rows = n_lat // GRID_W
    r_idx, c_idx = jnp.meshgrid(jnp.arange(rows), jnp.arange(GRID_W), indexing="ij")
    r_idx, c_idx = r_idx.reshape(-1), c_idx.reshape(-1)
    n_freq = rot_dim // 4
    freqs = ROPE_BASE ** (-jnp.arange(n_freq, dtype=F32) / n_freq)
    ang = jnp.concatenate([r_idx.astype(F32)[:, None] * freqs,
                           c_idx.astype(F32)[:, None] * freqs], axis=-1)
    return jnp.concatenate([ang, jnp.zeros((n_ctx, rot_dim // 2), F32)], axis=0)


def _rope_tables_t(n_lat, n_ctx, rot_dim):
    ang = _rope_angles(n_lat, n_ctx, rot_dim).T
    return jnp.stack([jnp.cos(ang), jnp.sin(ang)])


def _rope_tables(n_lat, n_ctx, rot_dim, lead, reps):
    ang = _rope_angles(n_lat, n_ctx, rot_dim)
    cos, sin, zero = jnp.cos(ang), jnp.sin(ang), jnp.zeros_like(ang)
    n = ang.shape[0]
    tail = LANES // reps - lead - rot_dim

    def pack(x1, x2, fill):
        unit = [jnp.full((n, lead), fill, F32), x1, x2, jnp.full((n, tail), fill, F32)]
        return jnp.concatenate(unit * reps, axis=-1)

    return jnp.stack([pack(cos, cos, 1.0), pack(zero, sin, 0.0), pack(-sin, zero, 0.0)])


def _prep_params(norm_g, w_in, w_out, mla_g_cq, mla_g_ckv, mla_w_uq, mla_w_ukv, mla_g_qn, mla_g_kn,
                 swa_g_qn, swa_g_kn, swa_sink, s5_a_re, s5_a_im, s5_log_dt, s5_b_re, s5_b_im,
                 s5_c_re, s5_c_im, s5_d, s5_w_glu):
    depth, D, _ = w_in.shape
    o_cq, o_ckv, o_kr, o_gm, o_sq, o_sk, o_sv, o_gs, o_u, o_g5, o_end = (
        0, 384, 640, 672, 1184, 1696, 1824, 1952, 2464, 2976, 3488)
    z = lambda n: jnp.zeros((depth, D, n), F32)
    w_in_p = jnp.concatenate([
        w_in[:, :, o_cq:o_kr],
        z(64), w_in[:, :, o_kr:o_gm], z(32),
        w_in[:, :, o_sq:o_gs],
        w_in[:, :, o_u:o_g5],
        w_in[:, :, o_gm:o_sq], w_in[:, :, o_gs:o_u], w_in[:, :, o_g5:o_end],
    ], axis=-1).astype(BF16)
    assert w_in_p.shape[-1] == C_END

    wq = mla_w_uq.reshape(depth, MLA_Q_RANK, MLA_HEADS, MLA_QK)
    wq = jnp.pad(wq, ((0, 0), (0, 0), (0, 0), (0, LANES - MLA_QK)))
    w_uq_p = wq.reshape(depth, MLA_Q_RANK, MLA_HEADS * LANES).astype(BF16)
    wkv = mla_w_ukv.reshape(depth, MLA_KV_RANK, MLA_HEADS, MLA_NOPE + MLA_V)
    wk = jnp.pad(wkv[..., :MLA_NOPE], ((0, 0), (0, 0), (0, 0), (0, LANES - MLA_NOPE)))
    w_ukv_p = jnp.concatenate([wk.reshape(depth, MLA_KV_RANK, MLA_HEADS * LANES),
                               wkv[..., MLA_NOPE:].reshape(depth, MLA_KV_RANK, MLA_HEADS * MLA_V)],
                              axis=-1).astype(BF16)

    pad_qk = lambda g: jnp.pad(g, ((0, 0), (0, LANES - MLA_QK)))[:, None, :]
    g_q = pad_qk(mla_g_qn * (MLA_QK ** -0.5 * LOG2E))
    g_qt = jnp.broadcast_to(jnp.swapaxes(g_q, 1, 2), (depth, LANES, LANES))
    g_k = pad_qk(mla_g_kn)
    mla_bound = 1.02 * MLA_QK * jnp.max(jnp.abs(g_q), axis=(1, 2)) * jnp.max(jnp.abs(g_k), axis=(1, 2))
    g_sqt = jnp.broadcast_to((swa_g_qn * (SWA_DIM ** -0.5 * LOG2E))[:, :, None], (depth, SWA_DIM, SWA_BLOCK))
    g_sk = jnp.tile(swa_g_kn, (1, 2))[:, None, :]
    sink = (swa_sink * LOG2E).reshape(depth * SWA_KV_HEADS, 1, SWA_HEADS // SWA_KV_HEADS, 1)
    sinkrow = jnp.broadcast_to(sink, (depth * SWA_KV_HEADS, 8, SWA_HEADS // SWA_KV_HEADS, SWA_BLOCK))
    sinkrow = sinkrow.reshape(depth * SWA_KV_HEADS, 8, -1).astype(F32)

    T = S5_CHUNK
    A = lax.complex(s5_a_re, s5_a_im)
    dt = jnp.exp(s5_log_dt)[..., None]
    a_bar = jnp.exp(dt * A)
    b_bar = ((a_bar - 1.0) / A)[..., None] * lax.complex(s5_b_re, s5_b_im)
    c_mat = lax.complex(s5_c_re, s5_c_im)
    k_idx = jnp.arange(T + 1, dtype=F32)
    pw = jnp.exp(k_idx[None, None, None, :, None] * (dt * A)[:, :, :, None, :])
    hi = lax.Precision.HIGHEST
    tt = jnp.arange(T)
    GP, NG = S5_GROUP, depth * S5_GROUPS
    pw_f, pw_b = pw[:, 0], pw[:, 1]
    tap_f = jnp.einsum("lgpn,lgkn,lgnq->lgpkq", c_mat[:, 0], pw_f[:, :, T - 1 - tt], b_bar[:, 0],
                       precision=hi).real
    tap_b = jnp.einsum("lgpn,lgkn,lgnq->lgpkq", c_mat[:, 1], pw_b[:, :, tt], b_bar[:, 1],
                       precision=hi).real
    d_diag = s5_d.reshape(depth, S5_GROUPS, GP)[..., None] * jnp.eye(GP, dtype=F32)
    centre = tap_f[:, :, :, T - 1:] + tap_b[:, :, :, :1] + d_diag[:, :, :, None, :]
    krev = jnp.concatenate([tap_f[:, :, :, :T - 1], centre, tap_b[:, :, :, 1:]], axis=3)
    krev = krev.reshape(depth, S5_GROUPS, GP, (2 * T - 1) * GP)
    wloc_t = jnp.stack([krev[..., (T - 1 - t) * GP:(T - 1 - t) * GP + T * GP] for t in range(T)], axis=2)
    wloc_t = wloc_t.reshape(NG, T * GP, T * GP)
    b_t = jnp.swapaxes(b_bar, -1, -2)
    inc_f = pw_f[:, :, T - 1 - tt][:, :, :, None, :] * b_t[:, 0][:, :, None]
    inc_b = pw_b[:, :, tt][:, :, :, None, :] * b_t[:, 1][:, :, None]
    wst = jnp.concatenate([inc_f.real, inc_b.real, inc_f.imag, inc_b.imag], axis=-1)
    wst = wst.reshape(NG, T * GP, 4 * S5_STATE)
    ro_f = c_mat[:, 0][:, :, None] * pw_f[:, :, tt + 1][:, :, :, None, :]
    ro_b = c_mat[:, 1][:, :, None] * pw_b[:, :, T - tt][:, :, :, None, :]
    wcar_t = jnp.concatenate([ro_f.real, ro_b.real, -ro_f.imag, -ro_b.imag], axis=-1)
    wcar_t = wcar_t.reshape(NG, T * GP, 4 * S5_STATE)
    a_t = pw[:, :, :, T]
    at = jnp.stack([jnp.concatenate([a_t[:, 0].real, a_t[:, 1].real], axis=-1),
                    jnp.concatenate([a_t[:, 0].imag, a_t[:, 1].imag], axis=-1)], axis=2)
    at = at.reshape(NG, 2, LANES).astype(F32)

    return dict(norm_g=norm_g[:, None, :], w_in=w_in_p, g_cq=mla_g_cq[:, None, :], w_uq=w_uq_p, g_qt=g_qt, mla_bound=mla_bound.astype(F32),
                g_ckv=mla_g_ckv[:, None, :], w_ukv=w_ukv_p, g_k=g_k, g_sqt=g_sqt, g_sk=g_sk,
                sinkrow=sinkrow, s5_wloc_t=wloc_t.astype(BF16), s5_wst=wst.astype(BF16),
                s5_wcar_t=wcar_t.astype(BF16), s5_at=at, w_glu=s5_w_glu.astype(BF16),
                w_out=w_out.astype(BF16))


def kernel(x, c, ctx, c_ctx, norm_g, w_ada, b_ada, w_in, w_out, mla_g_cq, mla_g_ckv, mla_w_uq, mla_w_ukv, mla_g_qn, mla_g_kn, swa_g_qn, swa_g_kn, swa_sink, s5_a_re, s5_a_im, s5_log_dt, s5_b_re, s5_b_im, s5_c_re, s5_c_im, s5_d, s5_w_glu):
    B, L, D = x.shape
    C = ctx.shape[1]
    S = C + L
    depth = w_in.shape[0]
    assert B + PROJ_NB <= MOD_ROWS and C == TOK and L % (2 * TOK) == 0 and L % GRID_W == 0 and L >= 3 * SWA_BLOCK

    P = _prep_params(norm_g, w_in, w_out, mla_g_cq, mla_g_ckv, mla_w_uq, mla_w_ukv, mla_g_qn, mla_g_kn,
                     swa_g_qn, swa_g_kn, swa_sink, s5_a_re, s5_a_im, s5_log_dt, s5_b_re, s5_b_im,
                     s5_c_re, s5_c_im, s5_d, s5_w_glu)
    rope_mla = _rope_tables(L, C, MLA_ROPE, MLA_NOPE, 1)
    rope_swa = _rope_tables(L, C, SWA_DIM, 0, 2)
    rope_mla_t = _rope_tables_t(L, C, MLA_ROPE)
    rope_swa_t = jnp.transpose(_rope_tables_t(L, C, SWA_DIM).reshape(2, SWA_DIM // 2, S // SWA_BLOCK, SWA_BLOCK),
                               (2, 0, 1, 3))

    cc = jnp.concatenate([c, jnp.tile(c_ctx[None, :], (PROJ_NB, 1)),
                          jnp.zeros((MOD_ROWS - B - PROJ_NB, D), F32)], axis=0)
    mod = _ada_call(cc, w_ada, b_ada).reshape(depth, MOD_ROWS, 3, D)

    for l in range(depth):
        qa, ka, va, qb, kb, vb, u, gt = _proj_call(l, x, ctx, mod, P, rope_mla, rope_swa)
        oa, oa_ctx = _mla_call(l, qa, ka, va, P["g_qt"], rope_mla_t, P["mla_bound"], L)
        ob = _swa_call(l, qb, kb, vb, P["sinkrow"], P["g_sqt"], rope_swa_t, C)
        gy = _s5_call(l, u, P, L)
        x, ctx = _out_call(l, x, ctx, mod, oa, oa_ctx, ob, gy, gt, P)
    return x
```

```python
import functools
import math

import jax
import jax.numpy as jnp
from jax import lax
from jax.experimental import pallas as pl
from jax.experimental.pallas import tpu as pltpu

F32 = jnp.float32
BF16 = jnp.bfloat16

GRID_W = 64
EPS = 1e-6
ROPE_BASE = 10000.0
NEG = -1e30
LOG2E = math.log2(math.e)

MLA_HEADS = 8
MLA_NOPE = 64
MLA_ROPE = 32
MLA_V = 64
MLA_QK = MLA_NOPE + MLA_ROPE
MLA_Q_RANK = 384
MLA_KV_RANK = 256

SWA_HEADS = 8
SWA_KV_HEADS = 2
SWA_DIM = 64
SWA_WINDOW = 128

S5_GROUP = 16
S5_GROUPS = 32
S5_STATE = 64
S5_CHUNK = 16

LANES = 128
TOK = 256
KV_CHUNK = 256
SWA_BLOCK = 128
MOD_ROWS = 16
PROJ_NB = 2
SWA_UNROLL = 32
OUT_COLS = 256
PROJ_SKEW = 1

C_CQ = 0
C_CKV = 384
C_KR = 640
C_SQ = 768
C_SK = 1280
C_SV = 1408
C_U = 1536
C_GATE = 2048
C_END = 3584

VMEM_LIMIT = 56 * 1024 * 1024


def _cparams(n_axes):
    return pltpu.CompilerParams(dimension_semantics=("arbitrary",) * n_axes,
                                vmem_limit_bytes=VMEM_LIMIT)


def _ada_kernel(c_ref, w_ref, b_ref, o_ref):
    cc = c_ref[...]
    s = cc * jax.nn.sigmoid(cc)
    o_ref[0] = jnp.dot(s, w_ref[0], preferred_element_type=F32,
                       precision=lax.Precision.HIGHEST) + b_ref[0]


def _ada_call(cc, w_ada, b_ada):
    depth, d, n3 = w_ada.shape
    tn = 768
    return pl.pallas_call(
        _ada_kernel,
        out_shape=jax.ShapeDtypeStruct((depth, MOD_ROWS, n3), F32),
        grid=(depth, n3 // tn),
        in_specs=[pl.BlockSpec((MOD_ROWS, d), lambda l, j: (0, 0)),
                  pl.BlockSpec((1, d, tn), lambda l, j: (l, 0, j)),
                  pl.BlockSpec((1, 1, tn), lambda l, j: (l, 0, j))],
        out_specs=pl.BlockSpec((1, MOD_ROWS, tn), lambda l, j: (l, 0, j)),
        compiler_params=_cparams(2),
        name="ada",
    )(cc, w_ada, b_ada.reshape(depth, 1, n3))


def _sigmoid(v):
    return 0.5 * jnp.tanh(0.5 * v) + 0.5


def _norm_rope(slots, lo_masks, dim, gain, tab_ref, half):
    sq = [s * s for s in slots]
    sums = []
    for s2, lo in zip(sq, lo_masks):
        if lo is None:
            sums.append((jnp.sum(s2, axis=-1, keepdims=True),))
        else:
            sums.append((jnp.sum(jnp.where(lo, s2, 0.0), axis=-1, keepdims=True),
                         jnp.sum(jnp.where(lo, 0.0, s2), axis=-1, keepdims=True)))
    ys = []
    for s, ss, lo in zip(slots, sums, lo_masks):
        rs = [lax.rsqrt(v * (1.0 / dim) + EPS) for v in ss]
        r = rs[0] if len(rs) == 1 else jnp.where(lo, rs[0], rs[1])
        ys.append(s * r * gain)
    up = [pltpu.roll(y, half, 1) for y in ys]
    dn = [pltpu.roll(y, LANES - half, 1) for y in ys]
    return [(y * tab_ref[0] + u * tab_ref[1] + d * tab_ref[2]).astype(BF16) for y, u, d in zip(ys, up, dn)]


N_PROJ_DATA = 3
N_PROJ_WEIGHTS = 10


def _proj_kernel(*refs, n_lat_tiles):
    data = refs[:N_PROJ_DATA]
    shared = refs[N_PROJ_DATA:N_PROJ_DATA + N_PROJ_WEIGHTS]
    outs = refs[N_PROJ_DATA + N_PROJ_WEIGHTS:]
    streams = []
    for i in range(data[0].shape[0]):
        one = pl.ds(i, 1)
        streams.append(_proj_tile(data[0].at[one], data[1].at[one], data[2].at[:, one], *shared,
                                  *[o.at[one] for o in outs], n_lat_tiles=n_lat_tiles))
    pending = list(enumerate(streams))
    step = 0
    while pending:
        for item in list(pending):
            if step >= item[0] * PROJ_SKEW and next(item[1], "done") == "done":
                pending.remove(item)
        step += 1


def _proj_tile(x_ref, c_ref, mod_ref, ng_ref, win_ref, gcq_ref, wuq_ref,
               gckv_ref, wukv_ref, gk_ref, gsk_ref, rm_ref, rs_ref,
               qa_ref, ka_ref, va_ref, qb_ref, kb_ref, vb_ref, u_ref, gt_ref, *, n_lat_tiles):
    t = pl.program_id(1)
    x = jnp.where(t == n_lat_tiles, c_ref[0], x_ref[0])
    mod = mod_ref[0, 0]
    y = x * lax.rsqrt(jnp.mean(x * x, axis=-1, keepdims=True) + EPS) * ng_ref[0]
    xn = (y * (1.0 + mod[1:2]) + mod[0:1]).astype(BF16)

    def seg(a, b):
        return jnp.dot(xn, win_ref[0, :, a:b], preferred_element_type=F32)

    lane = lax.broadcasted_iota(jnp.int32, (TOK, LANES), 1)
    lo = lane < 64

    yield
    cq = seg(C_CQ, C_CKV)
    ckv = seg(C_CKV, C_KR)
    kr = seg(C_KR, C_SQ)
    yield
    cqn = (cq * lax.rsqrt(jnp.mean(cq * cq, axis=-1, keepdims=True) + EPS) * gcq_ref[0]).astype(BF16)
    qf = jnp.dot(cqn, wuq_ref[0], preferred_element_type=F32)
    ckvn = (ckv * lax.rsqrt(jnp.mean(ckv * ckv, axis=-1, keepdims=True) + EPS) * gckv_ref[0]).astype(BF16)
    kvf = jnp.dot(ckvn, wukv_ref[0], preferred_element_type=F32)
    sq = seg(C_SQ, C_SK)
    yield
    for h in range(MLA_HEADS):
        qa_ref[0, h] = qf[:, h * LANES:(h + 1) * LANES].astype(BF16)

    sk = seg(C_SK, C_SV)
    vb_ref[0] = seg(C_SV, C_U).astype(BF16)
    yield
    slots = [kvf[:, h * LANES:(h + 1) * LANES] + kr for h in range(MLA_HEADS)]
    for h, o in enumerate(_norm_rope(slots, [None] * MLA_HEADS, MLA_QK, gk_ref[0], rm_ref, MLA_ROPE // 2)):
        ka_ref[0, h] = jnp.where(lane == MLA_QK, jnp.ones_like(o), o)
    va_ref[0] = kvf[:, MLA_HEADS * LANES:].astype(BF16)
    u_ref[0] = seg(C_U, C_GATE)
    yield
    qb_ref[0] = sq.astype(BF16)
    kb_ref[0] = _norm_rope([sk], [lo], SWA_DIM, gsk_ref[0], rs_ref, SWA_DIM // 2)[0]
    g = seg(C_GATE, C_END)
    yield
    gt_ref[0] = (g * _sigmoid(g)).astype(BF16)


def _proj_call(l, x, ctx, mod, P, rope_mla, rope_swa):
    B, L, D = x.shape
    C = ctx.shape[1]
    S = C + L
    nt = S // TOK

    def wspec(arr):
        shp = arr.shape
        return pl.BlockSpec((1,) + shp[1:], lambda b, t: (l,) + (0,) * (len(shp) - 1))

    weights = [P["norm_g"], P["w_in"], P["g_cq"], P["w_uq"],
               P["g_ckv"], P["w_ukv"], P["g_k"], P["g_sk"]]
    nl = L // TOK
    nb = PROJ_NB
    assert len(weights) + 2 == N_PROJ_WEIGHTS and B % nb == 0
    in_specs = ([pl.BlockSpec((nb, TOK, D), lambda b, t: (b, jnp.minimum(t, nl - 1), 0)),
                 pl.BlockSpec((nb, TOK, D), lambda b, t: (b, 0, 0)),
                 pl.BlockSpec((1, nb, 3, D), lambda b, t: (l, jnp.where(t == nl, B // nb, b), 0, 0))]
                + [wspec(w) for w in weights]
                + [pl.BlockSpec((3, TOK, LANES), lambda b, t: (0, t, 0)),
                   pl.BlockSpec((3, TOK, LANES), lambda b, t: (0, t, 0))])
    out_shape = [jax.ShapeDtypeStruct((B, MLA_HEADS, S, LANES), BF16),
                 jax.ShapeDtypeStruct((B, MLA_HEADS, S, LANES), BF16),
                 jax.ShapeDtypeStruct((B, S, 512), BF16),
                 jax.ShapeDtypeStruct((B, S, 512), BF16),
                 jax.ShapeDtypeStruct((B, S, LANES), BF16),
                 jax.ShapeDtypeStruct((B, S, LANES), BF16),
                 jax.ShapeDtypeStruct((B, S, 512), F32),
                 jax.ShapeDtypeStruct((B, S, 1536), BF16)]
    out_specs = [pl.BlockSpec((nb, MLA_HEADS, TOK, LANES), lambda b, t: (b, 0, t, 0)),
                 pl.BlockSpec((nb, MLA_HEADS, TOK, LANES), lambda b, t: (b, 0, t, 0))]
    out_specs += [pl.BlockSpec((nb, TOK, s.shape[2]), lambda b, t: (b, t, 0)) for s in out_shape[2:]]
    return pl.pallas_call(
        functools.partial(_proj_kernel, n_lat_tiles=nl), out_shape=out_shape, grid=(B // nb, nt),
        in_specs=in_specs, out_specs=out_specs,
        compiler_params=_cparams(2), name="proj",
    )(x, ctx, mod, *weights, rope_mla, rope_swa)


MLA_SUBTILES = 4
MLA_VT_ROWS = 80


MLA_SAFE_RANGE = 60.0


def _mla_kernel(rb_ref, q_ref, k_ref, v_ref, gq_ref, rt_ref, o_ref, vt_ref, *, n_chunks, tq, layer):
    ones = jnp.ones((MLA_VT_ROWS - MLA_V, KV_CHUNK), BF16)

    @pl.when(pl.program_id(2) == 0)
    def _():
        for c in range(n_chunks):
            vt = v_ref[0, c].astype(F32).T.astype(BF16)
            vt_ref[0, c] = jnp.concatenate([vt[:MLA_V], ones], axis=0)
            vt_ref[1, c] = jnp.concatenate([vt[MLA_V:], ones], axis=0)

    bound = rb_ref[layer]
    half = MLA_ROPE // 2

    def sweep(fixed_shift):
        for r0 in range(0, q_ref.shape[2], tq):
            rows = slice(r0, r0 + tq)
            cos, sin = rt_ref[0, :, rows], rt_ref[1, :, rows]
            gain = jnp.concatenate([gq_ref[0]] * (tq // LANES), axis=1)
            pad = jnp.zeros((LANES - MLA_QK, tq), F32)
            if fixed_shift:
                row = lax.broadcasted_iota(jnp.int32, pad.shape, 0)
                pad = jnp.where(row == 0, -bound, pad)

            def qmat(e):
                qt = q_ref[0, e, rows, :].astype(F32).T
                r = lax.rsqrt(jnp.sum(qt * qt, axis=0, keepdims=True) * (1.0 / MLA_QK) + EPS)
                y = qt * r * gain
                x1, x2 = y[MLA_NOPE:MLA_NOPE + half], y[MLA_NOPE + half:MLA_QK]
                return jnp.concatenate([y[:MLA_NOPE], x1 * cos - x2 * sin, x2 * cos + x1 * sin, pad],
                                       axis=0).astype(BF16)

            qts = [qmat(e) for e in range(2)]
            score = lambda c, e: jnp.dot(k_ref[0, e, c], qts[e], preferred_element_type=F32)
            ms = [None, None]
            accs = [None, None]
            sts = [score(0, e) for e in range(2)]
            for c in range(n_chunks):
                nxt = [score(c + 1, e) for e in range(2)] if c + 1 < n_chunks else None
                for e in range(2):
                    if fixed_shift:
                        pv = jnp.dot(vt_ref[e, c], jnp.exp2(sts[e]).astype(BF16), preferred_element_type=F32)
                        accs[e] = pv if c == 0 else accs[e] + pv
                    else:
                        mc = jnp.max(sts[e], axis=0, keepdims=True)
                        m_new = mc if c == 0 else jnp.maximum(ms[e], mc)
                        pt = jnp.exp2(sts[e] - m_new).astype(BF16)
                        pv = jnp.dot(vt_ref[e, c], pt, preferred_element_type=F32)
                        accs[e] = pv if c == 0 else accs[e] * jnp.exp2(ms[e] - m_new) + pv
                        ms[e] = m_new
                sts = nxt
            ot = jnp.concatenate([accs[e][:MLA_V] / accs[e][MLA_V:MLA_V + 1] for e in range(2)], axis=0)
            o_ref[0, rows, :] = ot.T.astype(BF16)

    @pl.when(bound < MLA_SAFE_RANGE)
    def _():
        sweep(True)

    @pl.when(jnp.logical_not(bound < MLA_SAFE_RANGE))
    def _():
        sweep(False)


def _mla_call(l, qa, ka, va, g_qt, rope_t, r_bound, n_lat):
    B, H, S, _ = qa.shape
    half = MLA_ROPE // 2
    gspec = pl.BlockSpec((1, LANES, LANES), lambda b, p, t: (l, 0, 0))
    sspec = pl.BlockSpec(memory_space=pltpu.SMEM)
    nc = S // KV_CHUNK
    tq = 2 * TOK
    bq = MLA_SUBTILES * tq if n_lat % (MLA_SUBTILES * tq) == 0 else tq
    k5 = ka.reshape(B, H, nc, KV_CHUNK, LANES)
    v4 = va.reshape(B, nc, KV_CHUNK, 512)
    o_lat = pl.pallas_call(
        functools.partial(_mla_kernel, n_chunks=nc, tq=tq, layer=l),
        out_shape=jax.ShapeDtypeStruct((B, n_lat, 512), BF16),
        grid=(B, H // 2, n_lat // bq),
        in_specs=[sspec,
                  pl.BlockSpec((1, 2, bq, LANES), lambda b, p, t: (b, p, t, 0)),
                  pl.BlockSpec((1, 2, nc, KV_CHUNK, LANES), lambda b, p, t: (b, p, 0, 0, 0)),
                  pl.BlockSpec((1, nc, KV_CHUNK, LANES), lambda b, p, t: (b, 0, 0, p)),
                  gspec,
                  pl.BlockSpec((2, half, bq), lambda b, p, t: (0, 0, t))],
        out_specs=pl.BlockSpec((1, bq, LANES), lambda b, p, t: (b, t, p)),
        scratch_shapes=[pltpu.VMEM((2, nc, MLA_VT_ROWS, KV_CHUNK), BF16)],
        compiler_params=_cparams(3), name="mla_attn",
    )(r_bound, qa, k5, v4, g_qt, rope_t)
    cblk = n_lat // KV_CHUNK
    o_ctx = pl.pallas_call(
        functools.partial(_mla_kernel, n_chunks=1, tq=TOK, layer=l),
        out_shape=jax.ShapeDtypeStruct((B, S - n_lat, 512), BF16),
        grid=(B, H // 2, 1),
        in_specs=[sspec,
                  pl.BlockSpec((1, 2, TOK, LANES), lambda b, p, t: (b, p, cblk, 0)),
                  pl.BlockSpec((1, 2, 1, KV_CHUNK, LANES), lambda b, p, t: (b, p, cblk, 0, 0)),
                  pl.BlockSpec((1, 1, KV_CHUNK, LANES), lambda b, p, t: (b, cblk, 0, p)),
                  gspec,
                  pl.BlockSpec((2, half, TOK), lambda b, p, t: (0, 0, cblk))],
        out_specs=pl.BlockSpec((1, TOK, LANES), lambda b, p, t: (b, 0, p)),
        scratch_shapes=[pltpu.VMEM((2, 1, MLA_VT_ROWS, KV_CHUNK), BF16)],
        compiler_params=_cparams(3), name="mla_attn_ctx",
    )(r_bound, qa, k5, v4, g_qt, rope_t)
    return o_lat, o_ctx


SWA_SAFE_RANGE = 120.0
SWA_VT_ROWS = 80


def _swa_kernel(rng_ref, q_ref, k_ref, v_ref, sink_ref, shift_ref, gq_ref, rt_ref, o_ref, vt_ref, bias_ref,
                *, n_ctx, n_lat, layer):
    blk = SWA_BLOCK
    heads = SWA_HEADS // SWA_KV_HEADS
    win = 3 * blk
    n_blocks = n_lat // blk
    ctx_blk = n_blocks
    group = min(SWA_UNROLL, n_blocks)
    first = pl.program_id(1) == 0
    sink = sink_ref[0, 0:1, :]

    ones = jnp.ones((SWA_VT_ROWS - SWA_DIM, blk), BF16)
    for i in range((n_lat + n_ctx) // blk):
        vt = v_ref[0, i * blk:(i + 1) * blk, :].astype(F32).T
        vt_ref[i] = jnp.concatenate([jnp.where(first, vt[:SWA_DIM], vt[SWA_DIM:]).astype(BF16), ones], axis=0)
    rel0 = (lax.broadcasted_iota(jnp.int32, (win, heads * blk), 0)
            - (lax.broadcasted_iota(jnp.int32, (win, heads * blk), 1) & (blk - 1)))
    for kind in range(3):
        bias_ref[kind] = jnp.where(jnp.abs(rel0 - kind * blk) <= SWA_WINDOW, 0.0, NEG)

    kc = k_ref[0, n_lat:n_lat + n_ctx, :]
    vtc = jnp.concatenate([vt_ref[ctx_blk + i] for i in range(n_ctx // blk)], axis=1)

    gain = gq_ref[0]
    half = SWA_DIM // 2

    def qmat(n):
        qt = q_ref[0, pl.ds(pl.multiple_of(n * blk, blk), blk), :].astype(F32).T
        cos, sin = rt_ref[n, 0], rt_ref[n, 1]
        cols = []
        for h in range(heads):
            x = qt[h * SWA_DIM:(h + 1) * SWA_DIM]
            y = x * lax.rsqrt(jnp.sum(x * x, axis=0, keepdims=True) * (1.0 / SWA_DIM) + EPS) * gain
            x1, x2 = y[:half], y[half:]
            cols.append(jnp.concatenate([x1 * cos - x2 * sin, x2 * cos + x1 * sin], axis=0))
        w = jnp.concatenate(cols, axis=1).astype(BF16)
        z = jnp.zeros_like(w)
        return jnp.where(first, jnp.concatenate([w, z], axis=0), jnp.concatenate([z, w], axis=0))

    def finish(r0, m, acc):
        ot = acc[:SWA_DIM] / (acc[SWA_DIM:SWA_DIM + 1] + jnp.exp2(sink - m))
        o4 = jnp.concatenate([ot[:, h * blk:(h + 1) * blk] for h in range(heads)], axis=0)
        o_ref[0, pl.ds(r0, blk), :] = o4.T.astype(BF16)

    shift = shift_ref[0, 0:1, :]

    def row_max(fixed, *scores):
        if fixed:
            return shift
        m = sink
        for sc in scores:
            m = jnp.maximum(m, jnp.max(sc, axis=0, keepdims=True))
        return m

    def ctx_queries(fixed):
        for n in range(n_ctx // blk):
            r0 = n_lat + n * blk
            s_c = jnp.dot(kc, qmat(ctx_blk + n), preferred_element_type=F32)
            m = row_max(fixed, s_c)
            finish(r0, m, jnp.dot(vtc, jnp.exp2(s_c - m).astype(BF16), preferred_element_type=F32))

    def window_block(n):
        return jnp.clip(n - 1, 0, n_blocks - 3)

    def scores(n):
        wb = window_block(n)
        w = qmat(n)
        kw = k_ref[0, pl.ds(pl.multiple_of(wb * blk, blk), win), :]
        return (jnp.dot(kc, w, preferred_element_type=F32),
                jnp.dot(kw, w, preferred_element_type=F32) + bias_ref[n - wb])

    def latent_queries(fixed):
        def blocks(gi, carry):
            n0 = gi * group
            cur = scores(n0)
            for i in range(group):
                n = n0 + i
                nxt = scores(n + 1) if i + 1 < group else None
                s_c, s_w = cur
                m = row_max(fixed, s_c, s_w)
                wb = window_block(n)
                vtw = jnp.concatenate([vt_ref[wb + j] for j in range(3)], axis=1)
                acc = (jnp.dot(vtc, jnp.exp2(s_c - m).astype(BF16), preferred_element_type=F32)
                       + jnp.dot(vtw, jnp.exp2(s_w - m).astype(BF16), preferred_element_type=F32))
                finish(pl.multiple_of(n * blk, blk), m, acc)
                cur = nxt
            return carry

        lax.fori_loop(0, n_blocks // group, blocks, 0)

    assert n_blocks % group == 0
    safe = rng_ref[layer] < SWA_SAFE_RANGE

    @pl.when(safe)
    def _():
        ctx_queries(True)
        latent_queries(True)

    @pl.when(jnp.logical_not(safe))
    def _():
        ctx_queries(False)
        latent_queries(False)


def _swa_call(l, qb, kb, vb, sinkrow, shiftrow, swa_range, g_sqt, rope_t, n_ctx):
    B, S, _ = qb.shape
    nq = (SWA_HEADS // SWA_KV_HEADS) * SWA_BLOCK
    return pl.pallas_call(
        functools.partial(_swa_kernel, n_ctx=n_ctx, n_lat=S - n_ctx, layer=l),
        out_shape=jax.ShapeDtypeStruct((B, S, 512), BF16),
        grid=(B, SWA_KV_HEADS),
        in_specs=[pl.BlockSpec(memory_space=pltpu.SMEM),
                  pl.BlockSpec((1, S, 256), lambda b, j: (b, 0, j)),
                  pl.BlockSpec((1, S, LANES), lambda b, j: (b, 0, 0)),
                  pl.BlockSpec((1, S, LANES), lambda b, j: (b, 0, 0)),
                  pl.BlockSpec((1, 8, nq), lambda b, j: (l * SWA_KV_HEADS + j, 0, 0)),
                  pl.BlockSpec((1, 8, nq), lambda b, j: (l * SWA_KV_HEADS + j, 0, 0)),
                  pl.BlockSpec((1, SWA_DIM, SWA_BLOCK), lambda b, j: (l, 0, 0)),
                  pl.BlockSpec(rope_t.shape, lambda b, j: (0, 0, 0, 0))],
        out_specs=pl.BlockSpec((1, S, 256), lambda b, j: (b, 0, j)),
        scratch_shapes=[pltpu.VMEM((S // SWA_BLOCK, SWA_VT_ROWS, SWA_BLOCK), BF16),
                        pltpu.VMEM((3, 3 * SWA_BLOCK, nq), F32)],
        compiler_params=_cparams(2), name="swa_attn",
    )(swa_range, qb, kb, vb, sinkrow, shiftrow, g_sqt, rope_t)


def _gelu(y):
    return 0.5 * y * (1.0 + jnp.tanh(math.sqrt(2.0 / math.pi) * (y + 0.044715 * (y * y * y))))


def _s5_kernel(u_ref, wst_ref, wloc_ref, wcar_ref, at_ref, o_ref,
               ut_ref, utc_ref, yt_ref, ytc_ref, ere_ref, eim_ref, hfr_ref, hfi_ref, hbr_ref, hbi_ref,
               *, nb, n_lat_chunks, n_ctx_chunks):
    T, GP = S5_CHUNK, S5_GROUP
    n_chunks = n_lat_chunks + n_ctx_chunks
    n_lat = n_lat_chunks * T
    cw = nb * n_ctx_chunks
    tn_dims = (((0,), (0,)), ((), ()))
    nt_dims = (((1,), (1,)), ((), ()))
    gpl = LANES // GP

    for b in range(nb):
        for s in range(T):
            xs = u_ref[b, pl.ds(s, n_lat_chunks, stride=T), :]
            ut_ref[b, :, s] = xs.T.reshape(gpl, GP, n_lat_chunks).astype(BF16)
    zpad = jnp.zeros((LANES - cw, LANES), F32)
    for s in range(T):
        xs = jnp.concatenate([u_ref[b, pl.ds(n_lat + s, n_ctx_chunks, stride=T), :] for b in range(nb)]
                             + [zpad], axis=0)
        utc_ref[:, s] = xs.T.reshape(gpl, GP, LANES).astype(BF16)

    lane = lax.broadcasted_iota(jnp.int32, (2 * nb, LANES), 1)
    fwd = lane < S5_STATE
    lane_l = lax.broadcasted_iota(jnp.int32, (n_lat_chunks, LANES), 1) < S5_STATE
    lane_c = lax.broadcasted_iota(jnp.int32, (LANES, LANES), 1) < S5_STATE

    slab = 2 * nb
    lat_rows = lambda gb: pl.ds(gb, n_lat_chunks, stride=slab)
    ctx_rows = lambda gb: pl.ds(n_lat_chunks * slab + gb, n_ctx_chunks, stride=slab)

    def pair(gp, carry):
        for gl in range(2):
            g = gp * 2 + gl
            wst = wst_ref[g]
            for b in range(nb):
                e = lax.dot_general(ut_ref[b, g].reshape(T * GP, n_lat_chunks), wst, tn_dims,
                                    preferred_element_type=F32)
                ere_ref[lat_rows(gl * nb + b), :] = e[:, :LANES]
                eim_ref[lat_rows(gl * nb + b), :] = e[:, LANES:]
            ec = lax.dot_general(utc_ref[g].reshape(T * GP, LANES), wst, tn_dims, preferred_element_type=F32)
            for b in range(nb):
                ere_ref[ctx_rows(gl * nb + b), :] = ec[b * n_ctx_chunks:(b + 1) * n_ctx_chunks, :LANES]
                eim_ref[ctx_rows(gl * nb + b), :] = ec[b * n_ctx_chunks:(b + 1) * n_ctx_chunks, LANES:]

        a_re = jnp.concatenate([jnp.broadcast_to(at_ref[gp * 2 + gl, 0:1, :], (nb, LANES)) for gl in range(2)], 0)
        a_im = jnp.concatenate([jnp.broadcast_to(at_ref[gp * 2 + gl, 1:2, :], (nb, LANES)) for gl in range(2)], 0)

        def step(i, hc):
            h_re, h_im = hc
            cf = jnp.where(i < n_ctx_chunks, n_lat_chunks + i, i - n_ctx_chunks)
            cb = n_chunks - 1 - i
            sf = pl.ds(pl.multiple_of(cf * slab, slab), slab)
            sb = pl.ds(pl.multiple_of(cb * slab, slab), slab)
            hfr_ref[sf, :] = h_re
            hfi_ref[sf, :] = h_im
            hbr_ref[sb, :] = h_re
            hbi_ref[sb, :] = h_im
            e_re = jnp.where(fwd, ere_ref[sf, :], ere_ref[sb, :])
            e_im = jnp.where(fwd, eim_ref[sf, :], eim_ref[sb, :])
            return (a_re * h_re - a_im * h_im + e_re, a_re * h_im + a_im * h_re + e_im)

        zero = jnp.zeros((2 * nb, LANES), F32)
        lax.fori_loop(0, n_chunks, step, (zero, zero), unroll=8)

        for gl in range(2):
            g = gp * 2 + gl
            wloc, wcar = wloc_ref[g], wcar_ref[g]
            for b in range(nb):
                rows = lat_rows(gl * nb + b)
                h_cat = jnp.concatenate([jnp.where(lane_l, hfr_ref[rows, :], hbr_ref[rows, :]),
                                         jnp.where(lane_l, hfi_ref[rows, :], hbi_ref[rows, :])],
                                        axis=-1).astype(BF16)
                yt = (jnp.dot(wloc, ut_ref[b, g].reshape(T * GP, n_lat_chunks), preferred_element_type=F32)
                      + lax.dot_general(wcar, h_cat, nt_dims, preferred_element_type=F32))
                yt_ref[b, g] = _gelu(yt).astype(BF16).reshape(T, GP, n_lat_chunks)
            crow = lambda ref: jnp.concatenate(
                [ref[ctx_rows(gl * nb + b), :] for b in range(nb)]
                + [jnp.zeros((LANES - cw, LANES), F32)], axis=0)
            h_cat = jnp.concatenate([jnp.where(lane_c, crow(hfr_ref), crow(hbr_ref)),
                                     jnp.where(lane_c, crow(hfi_ref), crow(hbi_ref))], axis=-1).astype(BF16)
            ytc = (jnp.dot(wloc, utc_ref[g].reshape(T * GP, LANES), preferred_element_type=F32)
                   + lax.dot_general(wcar, h_cat, nt_dims, preferred_element_type=F32))
            ytc_ref[g] = _gelu(ytc).astype(BF16).reshape(T, GP, LANES)
        return carry

    lax.fori_loop(0, gpl // 2, pair, 0)

    for b in range(nb):
        for t in range(T):
            z = yt_ref[b, :, t].astype(F32).reshape(LANES, n_lat_chunks)
            o_ref[b, pl.ds(t, n_lat_chunks, stride=T), :] = z.T
    for t in range(T):
        z = ytc_ref[:, t].astype(F32).reshape(LANES, LANES).T
        for b in range(nb):
            o_ref[b, pl.ds(n_lat + t, n_ctx_chunks, stride=T), :] = z[b * n_ctx_chunks:(b + 1) * n_ctx_chunks]


def _s5_call(l, u, P, n_lat):
    B, S, W = u.shape
    nb = 4 if B % 4 == 0 else B
    T, GP = S5_CHUNK, S5_GROUP
    nlc, ncc = n_lat // T, (S - n_lat) // T
    gpl = LANES // GP
    nblk = W // LANES
    big = lambda n: pl.BlockSpec((nb, S, LANES), lambda j, hb: (hb, 0, j), pipeline_mode=pl.Buffered(n))
    wspec = lambda: pl.BlockSpec((gpl, T * GP, T * GP), lambda j, hb: (l * nblk + j, 0, 0))
    rows = 2 * nb * (nlc + ncc)
    return pl.pallas_call(
        functools.partial(_s5_kernel, nb=nb, n_lat_chunks=nlc, n_ctx_chunks=ncc),
        out_shape=jax.ShapeDtypeStruct((B, S, W), F32),
        grid=(nblk, B // nb),
        in_specs=[big(2), wspec(), wspec(), wspec(),
                  pl.BlockSpec((gpl, 2, LANES), lambda j, hb: (l * nblk + j, 0, 0))],
        out_specs=big(1),
        scratch_shapes=[pltpu.VMEM((nb, gpl, T, GP, nlc), BF16), pltpu.VMEM((gpl, T, GP, LANES), BF16),
                        pltpu.VMEM((nb, gpl, T, GP, nlc), BF16), pltpu.VMEM((gpl, T, GP, LANES), BF16)]
                       + [pltpu.VMEM((rows, LANES), F32)] * 6,
        compiler_params=_cparams(2), name="s5",
    )(u, P["s5_wst"], P["s5_wloc_t"], P["s5_wcar_t"], P["s5_at"])


def _out_kernel(x_ref, c_ref, mod_ref, oa_ref, oac_ref, ob_ref, gy_ref, gt_ref, wglu_ref, wout_ref,
                xo_ref, co_ref, *, n_lat_tiles):
    t = pl.program_id(1)
    cb = OUT_COLS
    nb, rows = x_ref.shape[0], x_ref.shape[1]
    stack = lambda ref: jnp.concatenate([ref[i] for i in range(nb)], axis=0)
    gyb = stack(gy_ref).astype(BF16)
    g = stack(gt_ref).astype(F32)
    oa = jnp.where(t == n_lat_tiles, stack(oac_ref), stack(oa_ref))
    m_a = (oa.astype(F32) * g[:, 0:512]).astype(BF16)
    m_b = (stack(ob_ref).astype(F32) * g[:, 512:1024]).astype(BF16)
    oc = []
    for j in range(512 // cb):
        za = jnp.dot(gyb, wglu_ref[0, :, j * cb:(j + 1) * cb], preferred_element_type=F32)
        zb = jnp.dot(gyb, wglu_ref[0, :, 512 + j * cb:512 + (j + 1) * cb], preferred_element_type=F32)
        oc.append((za * _sigmoid(zb) * g[:, 1024 + j * cb:1024 + (j + 1) * cb]).astype(BF16))
    m_c = jnp.concatenate(oc, axis=-1)
    gate = jnp.concatenate([jnp.broadcast_to(mod_ref[0, i][2:3], (rows, x_ref.shape[2])) for i in range(nb)],
                           axis=0)
    resid = jnp.where(t == n_lat_tiles, stack(c_ref), stack(x_ref))
    new = []
    for j in range(resid.shape[1] // cb):
        cols = slice(j * cb, (j + 1) * cb)
        upd = (jnp.dot(m_a, wout_ref[0, 0:512, cols], preferred_element_type=F32)
               + jnp.dot(m_b, wout_ref[0, 512:1024, cols], preferred_element_type=F32)
               + jnp.dot(m_c, wout_ref[0, 1024:1536, cols], preferred_element_type=F32))
        new.append(resid[:, cols] + gate[:, cols] * upd)

    @pl.when(t == n_lat_tiles)
    def _():
        for j, v in enumerate(new):
            for i in range(nb):
                co_ref[i, :, j * cb:(j + 1) * cb] = v[i * rows:(i + 1) * rows]

    @pl.when(t < n_lat_tiles)
    def _():
        for j, v in enumerate(new):
            for i in range(nb):
                xo_ref[i, :, j * cb:(j + 1) * cb] = v[i * rows:(i + 1) * rows]


def _out_call(l, x, ctx, mod, oa, oa_ctx, ob, gy, gt, P):
    B, L, D = x.shape
    C = ctx.shape[1]
    S = C + L
    nl = L // TOK
    nb = PROJ_NB
    xmap = lambda b, t: (b, jnp.minimum(t, nl - 1), 0)
    cmap = lambda b, t: (b, 0, 0)
    tmap = lambda b, t: (b, t, 0)
    return pl.pallas_call(
        functools.partial(_out_kernel, n_lat_tiles=nl),
        out_shape=[jax.ShapeDtypeStruct(x.shape, F32), jax.ShapeDtypeStruct(ctx.shape, F32)],
        grid=(B // nb, S // TOK),
        in_specs=[pl.BlockSpec((nb, TOK, D), xmap),
                  pl.BlockSpec((nb, TOK, D), cmap),
                  pl.BlockSpec((1, nb, 3, D), lambda b, t: (l, jnp.where(t == nl, B // nb, b), 0, 0)),
                  pl.BlockSpec((nb, TOK, 512), xmap),
                  pl.BlockSpec((nb, TOK, 512), cmap),
                  pl.BlockSpec((nb, TOK, 512), tmap),
                  pl.BlockSpec((nb, TOK, 512), tmap),
                  pl.BlockSpec((nb, TOK, 1536), tmap),
                  pl.BlockSpec((1, 512, 1024), lambda b, t: (l, 0, 0)),
                  pl.BlockSpec((1, 1536, D), lambda b, t: (l, 0, 0))],
        out_specs=[pl.BlockSpec((nb, TOK, D), xmap), pl.BlockSpec((nb, TOK, D), cmap)],
        compiler_params=_cparams(2), name="out",
    )(x, ctx, mod, oa, oa_ctx, ob, gy, gt, P["w_glu"], P["w_out"])


def _rope_angles(n_lat, n_ctx, rot_dim):
    rows = n_lat // GRID_W
    r_idx, c_idx = jnp.meshgrid(jnp.arange(rows), jnp.arange(GRID_W), indexing="ij")
    r_idx, c_idx = r_idx.reshape(-1), c_idx.reshape(-1)
    n_freq = rot_dim // 4
    freqs = ROPE_BASE ** (-jnp.arange(n_freq, dtype=F32) / n_freq)
    ang = jnp.concatenate([r_idx.astype(F32)[:, None] * freqs,
                           c_idx.astype(F32)[:, None] * freqs], axis=-1)
    return jnp.concatenate([ang, jnp.zeros((n_ctx, rot_dim // 2), F32)], axis=0)


def _rope_tables_t(n_lat, n_ctx, rot_dim):
    ang = _rope_angles(n_lat, n_ctx, rot_dim).T
    return jnp.stack([jnp.cos(ang), jnp.sin(ang)])


def _rope_tables(n_lat, n_ctx, rot_dim, lead, reps):
    ang = _rope_angles(n_lat, n_ctx, rot_dim)
    cos, sin, zero = jnp.cos(ang), jnp.sin(ang), jnp.zeros_like(ang)
    n = ang.shape[0]
    tail = LANES // reps - lead - rot_dim

    def pack(x1, x2, fill):
        unit = [jnp.full((n, lead), fill, F32), x1, x2, jnp.full((n, tail), fill, F32)]
        return jnp.concatenate(unit * reps, axis=-1)

    return jnp.stack([pack(cos, cos, 1.0), pack(zero, sin, 0.0), pack(-sin, zero, 0.0)])


def _prep_params(norm_g, w_in, w_out, mla_g_cq, mla_g_ckv, mla_w_uq, mla_w_ukv, mla_g_qn, mla_g_kn,
                 swa_g_qn, swa_g_kn, swa_sink, s5_a_re, s5_a_im, s5_log_dt, s5_b_re, s5_b_im,
                 s5_c_re, s5_c_im, s5_d, s5_w_glu):
    depth, D, _ = w_in.shape
    o_cq, o_ckv, o_kr, o_gm, o_sq, o_sk, o_sv, o_gs, o_u, o_g5, o_end = (
        0, 384, 640, 672, 1184, 1696, 1824, 1952, 2464, 2976, 3488)
    z = lambda n: jnp.zeros((depth, D, n), F32)
    w_in_p = jnp.concatenate([
        w_in[:, :, o_cq:o_kr],
        z(64), w_in[:, :, o_kr:o_gm], z(32),
        w_in[:, :, o_sq:o_gs],
        w_in[:, :, o_u:o_g5],
        w_in[:, :, o_gm:o_sq], w_in[:, :, o_gs:o_u], w_in[:, :, o_g5:o_end],
    ], axis=-1).astype(BF16)
    assert w_in_p.shape[-1] == C_END

    wq = mla_w_uq.reshape(depth, MLA_Q_RANK, MLA_HEADS, MLA_QK)
    wq = jnp.pad(wq, ((0, 0), (0, 0), (0, 0), (0, LANES - MLA_QK)))
    w_uq_p = wq.reshape(depth, MLA_Q_RANK, MLA_HEADS * LANES).astype(BF16)
    wkv = mla_w_ukv.reshape(depth, MLA_KV_RANK, MLA_HEADS, MLA_NOPE + MLA_V)
    wk = jnp.pad(wkv[..., :MLA_NOPE], ((0, 0), (0, 0), (0, 0), (0, LANES - MLA_NOPE)))
    w_ukv_p = jnp.concatenate([wk.reshape(depth, MLA_KV_RANK, MLA_HEADS * LANES),
                               wkv[..., MLA_NOPE:].reshape(depth, MLA_KV_RANK, MLA_HEADS * MLA_V)],
                              axis=-1).astype(BF16)

    pad_qk = lambda g: jnp.pad(g, ((0, 0), (0, LANES - MLA_QK)))[:, None, :]
    g_q = pad_qk(mla_g_qn * (MLA_QK ** -0.5 * LOG2E))
    g_qt = jnp.broadcast_to(jnp.swapaxes(g_q, 1, 2), (depth, LANES, LANES))
    g_k = pad_qk(mla_g_kn)
    mla_bound = 1.02 * MLA_QK * jnp.max(jnp.abs(g_q), axis=(1, 2)) * jnp.max(jnp.abs(g_k), axis=(1, 2))
    g_sqt = jnp.broadcast_to((swa_g_qn * (SWA_DIM ** -0.5 * LOG2E))[:, :, None], (depth, SWA_DIM, SWA_BLOCK))
    g_sk = jnp.tile(swa_g_kn, (1, 2))[:, None, :]
    per_kv = SWA_HEADS // SWA_KV_HEADS
    lanes_of = lambda v: jnp.broadcast_to(v.reshape(depth * SWA_KV_HEADS, 1, per_kv, 1),
                                          (depth * SWA_KV_HEADS, 8, per_kv, SWA_BLOCK)
                                          ).reshape(depth * SWA_KV_HEADS, 8, -1).astype(F32)
    sink2 = swa_sink * LOG2E
    sinkrow = lanes_of(sink2)
    swa_bound = 1.02 * SWA_DIM * jnp.max(jnp.abs(g_sqt), axis=(1, 2)) * jnp.max(jnp.abs(g_sk), axis=(1, 2))
    shift = jnp.maximum(swa_bound[:, None], sink2)
    shiftrow = lanes_of(shift)
    swa_range = (swa_bound + jnp.max(shift, axis=1)).astype(F32)

    T = S5_CHUNK
    A = lax.complex(s5_a_re, s5_a_im)
    dt = jnp.exp(s5_log_dt)[..., None]
    a_bar = jnp.exp(dt * A)
    b_bar = ((a_bar - 1.0) / A)[..., None] * lax.complex(s5_b_re, s5_b_im)
    c_mat = lax.complex(s5_c_re, s5_c_im)
    k_idx = jnp.arange(T + 1, dtype=F32)
    pw = jnp.exp(k_idx[None, None, None, :, None] * (dt * A)[:, :, :, None, :])
    hi = lax.Precision.HIGHEST
    tt = jnp.arange(T)
    GP, NG = S5_GROUP, depth * S5_GROUPS
    pw_f, pw_b = pw[:, 0], pw[:, 1]
    tap_f = jnp.einsum("lgpn,lgkn,lgnq->lgpkq", c_mat[:, 0], pw_f[:, :, T - 1 - tt], b_bar[:, 0],
                       precision=hi).real
    tap_b = jnp.einsum("lgpn,lgkn,lgnq->lgpkq", c_mat[:, 1], pw_b[:, :, tt], b_bar[:, 1],
                       precision=hi).real
    d_diag = s5_d.reshape(depth, S5_GROUPS, GP)[..., None] * jnp.eye(GP, dtype=F32)
    centre = tap_f[:, :, :, T - 1:] + tap_b[:, :, :, :1] + d_diag[:, :, :, None, :]
    krev = jnp.concatenate([tap_f[:, :, :, :T - 1], centre, tap_b[:, :, :, 1:]], axis=3)
    krev = krev.reshape(depth, S5_GROUPS, GP, (2 * T - 1) * GP)
    wloc_t = jnp.stack([krev[..., (T - 1 - t) * GP:(T - 1 - t) * GP + T * GP] for t in range(T)], axis=2)
    wloc_t = wloc_t.reshape(NG, T * GP, T * GP)
    b_t = jnp.swapaxes(b_bar, -1, -2)
    inc_f = pw_f[:, :, T - 1 - tt][:, :, :, None, :] * b_t[:, 0][:, :, None]
    inc_b = pw_b[:, :, tt][:, :, :, None, :] * b_t[:, 1][:, :, None]
    wst = jnp.concatenate([inc_f.real, inc_b.real, inc_f.imag, inc_b.imag], axis=-1)
    wst = wst.reshape(NG, T * GP, 4 * S5_STATE)
    ro_f = c_mat[:, 0][:, :, None] * pw_f[:, :, tt + 1][:, :, :, None, :]
    ro_b = c_mat[:, 1][:, :, None] * pw_b[:, :, T - tt][:, :, :, None, :]
    wcar_t = jnp.concatenate([ro_f.real, ro_b.real, -ro_f.imag, -ro_b.imag], axis=-1)
    wcar_t = wcar_t.reshape(NG, T * GP, 4 * S5_STATE)
    a_t = pw[:, :, :, T]
    at = jnp.stack([jnp.concatenate([a_t[:, 0].real, a_t[:, 1].real], axis=-1),
                    jnp.concatenate([a_t[:, 0].imag, a_t[:, 1].imag], axis=-1)], axis=2)
    at = at.reshape(NG, 2, LANES).astype(F32)

    return dict(norm_g=norm_g[:, None, :], w_in=w_in_p, g_cq=mla_g_cq[:, None, :], w_uq=w_uq_p, g_qt=g_qt, mla_bound=mla_bound.astype(F32),
                g_ckv=mla_g_ckv[:, None, :], w_ukv=w_ukv_p, g_k=g_k, g_sqt=g_sqt, g_sk=g_sk,
                sinkrow=sinkrow, shiftrow=shiftrow, swa_range=swa_range, s5_wloc_t=wloc_t.astype(BF16), s5_wst=wst.astype(BF16),
                s5_wcar_t=wcar_t.astype(BF16), s5_at=at, w_glu=s5_w_glu.astype(BF16),
                w_out=w_out.astype(BF16))


def kernel(x, c, ctx, c_ctx, norm_g, w_ada, b_ada, w_in, w_out, mla_g_cq, mla_g_ckv, mla_w_uq, mla_w_ukv, mla_g_qn, mla_g_kn, swa_g_qn, swa_g_kn, swa_sink, s5_a_re, s5_a_im, s5_log_dt, s5_b_re, s5_b_im, s5_c_re, s5_c_im, s5_d, s5_w_glu):
    B, L, D = x.shape
    C = ctx.shape[1]
    S = C + L
    depth = w_in.shape[0]
    assert B + PROJ_NB <= MOD_ROWS and C == TOK and L % (2 * TOK) == 0 and L % GRID_W == 0 and L >= 3 * SWA_BLOCK

    P = _prep_params(norm_g, w_in, w_out, mla_g_cq, mla_g_ckv, mla_w_uq, mla_w_ukv, mla_g_qn, mla_g_kn,
                     swa_g_qn, swa_g_kn, swa_sink, s5_a_re, s5_a_im, s5_log_dt, s5_b_re, s5_b_im,
                     s5_c_re, s5_c_im, s5_d, s5_w_glu)
    rope_mla = _rope_tables(L, C, MLA_ROPE, MLA_NOPE, 1)
    rope_swa = _rope_tables(L, C, SWA_DIM, 0, 2)
    rope_mla_t = _rope_tables_t(L, C, MLA_ROPE)
    rope_swa_t = jnp.transpose(_rope_tables_t(L, C, SWA_DIM).reshape(2, SWA_DIM // 2, S // SWA_BLOCK, SWA_BLOCK),
                               (2, 0, 1, 3))

    cc = jnp.concatenate([c, jnp.tile(c_ctx[None, :], (PROJ_NB, 1)),
                          jnp.zeros((MOD_ROWS - B - PROJ_NB, D), F32)], axis=0)
    mod = _ada_call(cc, w_ada, b_ada).reshape(depth, MOD_ROWS, 3, D)

    for l in range(depth):
        qa, ka, va, qb, kb, vb, u, gt = _proj_call(l, x, ctx, mod, P, rope_mla, rope_swa)
        oa, oa_ctx = _mla_call(l, qa, ka, va, P["g_qt"], rope_mla_t, P["mla_bound"], L)
        ob = _swa_call(l, qb, kb, vb, P["sinkrow"], P["shiftrow"], P["swa_range"], P["g_sqt"], rope_swa_t, C)
        gy = _s5_call(l, u, P, L)
        x, ctx = _out_call(l, x, ctx, mod, oa, oa_ctx, ob, gy, gt, P)
    return x
```

```python
import functools
import math

import jax
import jax.numpy as jnp
from jax import lax
from jax.experimental import pallas as pl
from jax.experimental.pallas import tpu as pltpu

F32 = jnp.float32
BF16 = jnp.bfloat16

GRID_W = 64
EPS = 1e-6
ROPE_BASE = 10000.0
NEG = -1e30
LOG2E = math.log2(math.e)

MLA_HEADS = 8
MLA_NOPE = 64
MLA_ROPE = 32
MLA_V = 64
MLA_QK = MLA_NOPE + MLA_ROPE
MLA_Q_RANK = 384
MLA_KV_RANK = 256

SWA_HEADS = 8
SWA_KV_HEADS = 2
SWA_DIM = 64
SWA_WINDOW = 128

S5_GROUP = 16
S5_GROUPS = 32
S5_STATE = 64
S5_CHUNK = 16

LANES = 128
TOK = 256
KV_CHUNK = 256
SWA_BLOCK = 128
MOD_ROWS = 16
PROJ_NB = 2
SWA_UNROLL = 32
OUT_COLS = 256
PROJ_SKEW = 1

C_CQ = 0
C_CKV = 384
C_KR = 640
C_SQ = 768
C_SK = 1280
C_SV = 1408
C_U = 1536
C_GATE = 2048
C_END = 3584

VMEM_LIMIT = 56 * 1024 * 1024


def _cparams(n_axes):
    return pltpu.CompilerParams(dimension_semantics=("arbitrary",) * n_axes,
                                vmem_limit_bytes=VMEM_LIMIT)


def _ada_kernel(c_ref, w_ref, b_ref, o_ref):
    cc = c_ref[...]
    s = cc * jax.nn.sigmoid(cc)
    o_ref[0] = jnp.dot(s, w_ref[0], preferred_element_type=F32,
                       precision=lax.Precision.HIGHEST) + b_ref[0]


def _ada_call(cc, w_ada, b_ada):
    depth, d, n3 = w_ada.shape
    tn = 768
    return pl.pallas_call(
        _ada_kernel,
        out_shape=jax.ShapeDtypeStruct((depth, MOD_ROWS, n3), F32),
        grid=(depth, n3 // tn),
        in_specs=[pl.BlockSpec((MOD_ROWS, d), lambda l, j: (0, 0)),
                  pl.BlockSpec((1, d, tn), lambda l, j: (l, 0, j)),
                  pl.BlockSpec((1, 1, tn), lambda l, j: (l, 0, j))],
        out_specs=pl.BlockSpec((1, MOD_ROWS, tn), lambda l, j: (l, 0, j)),
        compiler_params=_cparams(2),
        name="ada",
    )(cc, w_ada, b_ada.reshape(depth, 1, n3))


def _sigmoid(v):
    return 0.5 * jnp.tanh(0.5 * v) + 0.5


def _norm_rope(slots, lo_masks, dim, gain, tab_ref, half):
    sq = [s * s for s in slots]
    sums = []
    for s2, lo in zip(sq, lo_masks):
        if lo is None:
            sums.append((jnp.sum(s2, axis=-1, keepdims=True),))
        else:
            sums.append((jnp.sum(jnp.where(lo, s2, 0.0), axis=-1, keepdims=True),
                         jnp.sum(jnp.where(lo, 0.0, s2), axis=-1, keepdims=True)))
    ys = []
    for s, ss, lo in zip(slots, sums, lo_masks):
        rs = [lax.rsqrt(v * (1.0 / dim) + EPS) for v in ss]
        r = rs[0] if len(rs) == 1 else jnp.where(lo, rs[0], rs[1])
        ys.append(s * r * gain)
    up = [pltpu.roll(y, half, 1) for y in ys]
    dn = [pltpu.roll(y, LANES - half, 1) for y in ys]
    return [(y * tab_ref[0] + u * tab_ref[1] + d * tab_ref[2]).astype(BF16) for y, u, d in zip(ys, up, dn)]


N_PROJ_DATA = 3
N_PROJ_WEIGHTS = 14


def _proj_kernel(*refs, n_lat_tiles):
    data = refs[:N_PROJ_DATA]
    shared = refs[N_PROJ_DATA:N_PROJ_DATA + N_PROJ_WEIGHTS]
    outs = refs[N_PROJ_DATA + N_PROJ_WEIGHTS:]
    streams = []
    for i in range(data[0].shape[0]):
        one = pl.ds(i, 1)
        streams.append(_proj_tile(data[0].at[one], data[1].at[one], data[2].at[:, one], *shared,
                                  *[o.at[one] for o in outs], n_lat_tiles=n_lat_tiles))
    pending = list(enumerate(streams))
    step = 0
    while pending:
        for item in list(pending):
            if step >= item[0] * PROJ_SKEW and next(item[1], "done") == "done":
                pending.remove(item)
        step += 1


def _proj_tile(x_ref, c_ref, mod_ref, ng_ref, wa_ref, wkr_ref, wb_ref, wu_ref, wg_ref, gcq_ref, wuq_ref,
               gckv_ref, wukv_ref, gk_ref, gsk_ref, rm_ref, rs_ref,
               qa_ref, ka_ref, va_ref, qb_ref, kb_ref, vb_ref, u_ref, gt_ref, *, n_lat_tiles):
    t = pl.program_id(1)
    x = jnp.where(t == n_lat_tiles, c_ref[0], x_ref[0])
    mod = mod_ref[0, 0]
    y = x * lax.rsqrt(jnp.mean(x * x, axis=-1, keepdims=True) + EPS) * ng_ref[0]
    xn = (y * (1.0 + mod[1:2]) + mod[0:1]).astype(BF16)

    pieces = ((C_CQ, wa_ref), (C_KR, wkr_ref), (C_SQ, wb_ref), (C_U, wu_ref), (C_GATE, wg_ref))

    def seg(a, b):
        start, ref = [p for p in pieces if p[0] <= a][-1]
        return jnp.dot(xn, ref[0, :, a - start:b - start], preferred_element_type=F32)

    lane = lax.broadcasted_iota(jnp.int32, (TOK, LANES), 1)
    lo = lane < 64

    yield
    cq = seg(C_CQ, C_CKV)
    ckv = seg(C_CKV, C_KR)
    kr = seg(C_KR, C_SQ)
    yield
    cqn = (cq * lax.rsqrt(jnp.mean(cq * cq, axis=-1, keepdims=True) + EPS) * gcq_ref[0]).astype(BF16)
    qf = jnp.dot(cqn, wuq_ref[0], preferred_element_type=F32)
    ckvn = (ckv * lax.rsqrt(jnp.mean(ckv * ckv, axis=-1, keepdims=True) + EPS) * gckv_ref[0]).astype(BF16)
    kvf = jnp.dot(ckvn, wukv_ref[0], preferred_element_type=F32)
    sq = seg(C_SQ, C_SK)
    yield
    for h in range(MLA_HEADS):
        qa_ref[0, h] = qf[:, h * LANES:(h + 1) * LANES].astype(BF16)

    sk = seg(C_SK, C_SV)
    vb_ref[0] = seg(C_SV, C_U).astype(BF16)
    yield
    slots = [kvf[:, h * LANES:(h + 1) * LANES] + kr for h in range(MLA_HEADS)]
    for h, o in enumerate(_norm_rope(slots, [None] * MLA_HEADS, MLA_QK, gk_ref[0], rm_ref, MLA_ROPE // 2)):
        ka_ref[0, h] = jnp.where(lane == MLA_QK, jnp.ones_like(o), o)
    va_ref[0] = kvf[:, MLA_HEADS * LANES:].astype(BF16)
    u_ref[0] = seg(C_U, C_GATE)
    yield
    qb_ref[0] = sq.astype(BF16)
    kb_ref[0] = _norm_rope([sk], [lo], SWA_DIM, gsk_ref[0], rs_ref, SWA_DIM // 2)[0]
    g = seg(C_GATE, C_END)
    yield
    gt_ref[0] = (g * _sigmoid(g)).astype(BF16)


def _proj_call(l, x, ctx, mod, P, rope_mla, rope_swa):
    B, L, D = x.shape
    C = ctx.shape[1]
    S = C + L
    nt = S // TOK

    def wspec(arr):
        shp = arr.shape
        return pl.BlockSpec((1,) + shp[1:], lambda b, t: (l,) + (0,) * (len(shp) - 1))

    weights = [P["norm_g"], *P["w_in"], P["g_cq"], P["w_uq"],
               P["g_ckv"], P["w_ukv"], P["g_k"], P["g_sk"]]
    nl = L // TOK
    nb = PROJ_NB
    assert len(weights) + 2 == N_PROJ_WEIGHTS and B % nb == 0
    in_specs = ([pl.BlockSpec((nb, TOK, D), lambda b, t: (b, jnp.minimum(t, nl - 1), 0)),
                 pl.BlockSpec((nb, TOK, D), lambda b, t: (b, 0, 0)),
                 pl.BlockSpec((1, nb, 3, D), lambda b, t: (l, jnp.where(t == nl, B // nb, b), 0, 0))]
                + [wspec(w) for w in weights]
                + [pl.BlockSpec((3, TOK, LANES), lambda b, t: (0, t, 0)),
                   pl.BlockSpec((3, TOK, LANES), lambda b, t: (0, t, 0))])
    out_shape = [jax.ShapeDtypeStruct((B, MLA_HEADS, S, LANES), BF16),
                 jax.ShapeDtypeStruct((B, MLA_HEADS, S, LANES), BF16),
                 jax.ShapeDtypeStruct((B, S, 512), BF16),
                 jax.ShapeDtypeStruct((B, S, 512), BF16),
                 jax.ShapeDtypeStruct((B, S, LANES), BF16),
                 jax.ShapeDtypeStruct((B, S, LANES), BF16),
                 jax.ShapeDtypeStruct((B, S, 512), F32),
                 jax.ShapeDtypeStruct((B, S, 1536), BF16)]
    out_specs = [pl.BlockSpec((nb, MLA_HEADS, TOK, LANES), lambda b, t: (b, 0, t, 0)),
                 pl.BlockSpec((nb, MLA_HEADS, TOK, LANES), lambda b, t: (b, 0, t, 0))]
    out_specs += [pl.BlockSpec((nb, TOK, s.shape[2]), lambda b, t: (b, t, 0)) for s in out_shape[2:]]
    return pl.pallas_call(
        functools.partial(_proj_kernel, n_lat_tiles=nl), out_shape=out_shape, grid=(B // nb, nt),
        in_specs=in_specs, out_specs=out_specs,
        compiler_params=_cparams(2), name="proj",
    )(x, ctx, mod, *weights, rope_mla, rope_swa)


MLA_SUBTILES = 4
MLA_VT_ROWS = 80


MLA_SAFE_RANGE = 60.0


def _mla_kernel(rb_ref, q_ref, k_ref, v_ref, gq_ref, rt_ref, o_ref, vt_ref, *, n_chunks, tq, layer):
    ones = jnp.ones((MLA_VT_ROWS - MLA_V, KV_CHUNK), BF16)

    @pl.when(pl.program_id(2) == 0)
    def _():
        for c in range(n_chunks):
            vt = v_ref[0, c].astype(F32).T.astype(BF16)
            vt_ref[0, c] = jnp.concatenate([vt[:MLA_V], ones], axis=0)
            vt_ref[1, c] = jnp.concatenate([vt[MLA_V:], ones], axis=0)

    bound = rb_ref[layer]
    half = MLA_ROPE // 2

    def sweep(fixed_shift):
        for r0 in range(0, q_ref.shape[2], tq):
            rows = slice(r0, r0 + tq)
            cos, sin = rt_ref[0, :, rows], rt_ref[1, :, rows]
            gain = jnp.concatenate([gq_ref[0]] * (tq // LANES), axis=1)
            pad = jnp.zeros((LANES - MLA_QK, tq), F32)
            if fixed_shift:
                row = lax.broadcasted_iota(jnp.int32, pad.shape, 0)
                pad = jnp.where(row == 0, -bound, pad)

            def qmat(e):
                qt = q_ref[0, e, rows, :].astype(F32).T
                r = lax.rsqrt(jnp.sum(qt * qt, axis=0, keepdims=True) * (1.0 / MLA_QK) + EPS)
                y = qt * r * gain
                x1, x2 = y[MLA_NOPE:MLA_NOPE + half], y[MLA_NOPE + half:MLA_QK]
                return jnp.concatenate([y[:MLA_NOPE], x1 * cos - x2 * sin, x2 * cos + x1 * sin, pad],
                                       axis=0).astype(BF16)

            qts = [qmat(e) for e in range(2)]
            score = lambda c, e: jnp.dot(k_ref[0, e, c], qts[e], preferred_element_type=F32)
            ms = [None, None]
            accs = [None, None]
            sts = [score(0, e) for e in range(2)]
            for c in range(n_chunks):
                nxt = [None, None]
                for e in range(2):
                    if c + 1 < n_chunks:
                        nxt[e] = score(c + 1, e)
                    if fixed_shift:
                        pv = jnp.dot(vt_ref[e, c], jnp.exp2(sts[e]).astype(BF16), preferred_element_type=F32)
                        accs[e] = pv if c == 0 else accs[e] + pv
                    else:
                        mc = jnp.max(sts[e], axis=0, keepdims=True)
                        m_new = mc if c == 0 else jnp.maximum(ms[e], mc)
                        pt = jnp.exp2(sts[e] - m_new).astype(BF16)
                        pv = jnp.dot(vt_ref[e, c], pt, preferred_element_type=F32)
                        accs[e] = pv if c == 0 else accs[e] * jnp.exp2(ms[e] - m_new) + pv
                        ms[e] = m_new
                sts = nxt
            ot = jnp.concatenate([accs[e][:MLA_V] / accs[e][MLA_V:MLA_V + 1] for e in range(2)], axis=0)
            o_ref[0, rows, :] = ot.T.astype(BF16)

    @pl.when(bound < MLA_SAFE_RANGE)
    def _():
        sweep(True)

    @pl.when(jnp.logical_not(bound < MLA_SAFE_RANGE))
    def _():
        sweep(False)


def _mla_call(l, qa, ka, va, g_qt, rope_t, r_bound, n_lat):
    B, H, S, _ = qa.shape
    half = MLA_ROPE // 2
    gspec = pl.BlockSpec((1, LANES, LANES), lambda b, p, t: (l, 0, 0))
    sspec = pl.BlockSpec(memory_space=pltpu.SMEM)
    nc = S // KV_CHUNK
    tq = 2 * TOK
    bq = MLA_SUBTILES * tq if n_lat % (MLA_SUBTILES * tq) == 0 else tq
    k5 = ka.reshape(B, H, nc, KV_CHUNK, LANES)
    v4 = va.reshape(B, nc, KV_CHUNK, 512)
    o_lat = pl.pallas_call(
        functools.partial(_mla_kernel, n_chunks=nc, tq=tq, layer=l),
        out_shape=jax.ShapeDtypeStruct((B, n_lat, 512), BF16),
        grid=(B, H // 2, n_lat // bq),
        in_specs=[sspec,
                  pl.BlockSpec((1, 2, bq, LANES), lambda b, p, t: (b, p, t, 0)),
                  pl.BlockSpec((1, 2, nc, KV_CHUNK, LANES), lambda b, p, t: (b, p, 0, 0, 0)),
                  pl.BlockSpec((1, nc, KV_CHUNK, LANES), lambda b, p, t: (b, 0, 0, p)),
                  gspec,
                  pl.BlockSpec((2, half, bq), lambda b, p, t: (0, 0, t))],
        out_specs=pl.BlockSpec((1, bq, LANES), lambda b, p, t: (b, t, p)),
        scratch_shapes=[pltpu.VMEM((2, nc, MLA_VT_ROWS, KV_CHUNK), BF16)],
        compiler_params=_cparams(3), name="mla_attn",
    )(r_bound, qa, k5, v4, g_qt, rope_t)
    cblk = n_lat // KV_CHUNK
    o_ctx = pl.pallas_call(
        functools.partial(_mla_kernel, n_chunks=1, tq=TOK, layer=l),
        out_shape=jax.ShapeDtypeStruct((B, S - n_lat, 512), BF16),
        grid=(B, H // 2, 1),
        in_specs=[sspec,
                  pl.BlockSpec((1, 2, TOK, LANES), lambda b, p, t: (b, p, cblk, 0)),
                  pl.BlockSpec((1, 2, 1, KV_CHUNK, LANES), lambda b, p, t: (b, p, cblk, 0, 0)),
                  pl.BlockSpec((1, 1, KV_CHUNK, LANES), lambda b, p, t: (b, cblk, 0, p)),
                  gspec,
                  pl.BlockSpec((2, half, TOK), lambda b, p, t: (0, 0, cblk))],
        out_specs=pl.BlockSpec((1, TOK, LANES), lambda b, p, t: (b, 0, p)),
        scratch_shapes=[pltpu.VMEM((2, 1, MLA_VT_ROWS, KV_CHUNK), BF16)],
        compiler_params=_cparams(3), name="mla_attn_ctx",
    )(r_bound, qa, k5, v4, g_qt, rope_t)
    return o_lat, o_ctx


SWA_SAFE_RANGE = 120.0
SWA_VT_ROWS = 80


def _swa_kernel(rng_ref, q_ref, k_ref, v_ref, sink_ref, shift_ref, gq_ref, rt_ref, o_ref, vt_ref, bias_ref,
                *, n_ctx, n_lat, layer):
    blk = SWA_BLOCK
    heads = SWA_HEADS // SWA_KV_HEADS
    win = 3 * blk
    n_blocks = n_lat // blk
    ctx_blk = n_blocks
    group = min(SWA_UNROLL, n_blocks)
    first = pl.program_id(1) == 0
    sink = sink_ref[0, 0:1, :]

    ones = jnp.ones((SWA_VT_ROWS - SWA_DIM, blk), BF16)
    for i in range((n_lat + n_ctx) // blk):
        vt = v_ref[0, i * blk:(i + 1) * blk, :].astype(F32).T
        vt_ref[i] = jnp.concatenate([jnp.where(first, vt[:SWA_DIM], vt[SWA_DIM:]).astype(BF16), ones], axis=0)
    rel0 = (lax.broadcasted_iota(jnp.int32, (win, heads * blk), 0)
            - (lax.broadcasted_iota(jnp.int32, (win, heads * blk), 1) & (blk - 1)))
    for kind in range(3):
        bias_ref[kind] = jnp.where(jnp.abs(rel0 - kind * blk) <= SWA_WINDOW, 0.0, NEG)

    kc = k_ref[0, n_lat:n_lat + n_ctx, :]
    vtc = jnp.concatenate([vt_ref[ctx_blk + i] for i in range(n_ctx // blk)], axis=1)

    gain = gq_ref[0]
    half = SWA_DIM // 2

    def qmat(n):
        qt = q_ref[0, pl.ds(pl.multiple_of(n * blk, blk), blk), :].astype(F32).T
        cos, sin = rt_ref[n, 0], rt_ref[n, 1]
        cols = []
        for h in range(heads):
            x = qt[h * SWA_DIM:(h + 1) * SWA_DIM]
            y = x * lax.rsqrt(jnp.sum(x * x, axis=0, keepdims=True) * (1.0 / SWA_DIM) + EPS) * gain
            x1, x2 = y[:half], y[half:]
            cols.append(jnp.concatenate([x1 * cos - x2 * sin, x2 * cos + x1 * sin], axis=0))
        w = jnp.concatenate(cols, axis=1).astype(BF16)
        z = jnp.zeros_like(w)
        return jnp.where(first, jnp.concatenate([w, z], axis=0), jnp.concatenate([z, w], axis=0))

    def finish(r0, m, acc):
        ot = acc[:SWA_DIM] / (acc[SWA_DIM:SWA_DIM + 1] + jnp.exp2(sink - m))
        o4 = jnp.concatenate([ot[:, h * blk:(h + 1) * blk] for h in range(heads)], axis=0)
        o_ref[0, pl.ds(r0, blk), :] = o4.T.astype(BF16)

    shift = shift_ref[0, 0:1, :]

    def row_max(fixed, *scores):
        if fixed:
            return shift
        m = sink
        for sc in scores:
            m = jnp.maximum(m, jnp.max(sc, axis=0, keepdims=True))
        return m

    def ctx_queries(fixed):
        for n in range(n_ctx // blk):
            r0 = n_lat + n * blk
            s_c = jnp.dot(kc, qmat(ctx_blk + n), preferred_element_type=F32)
            m = row_max(fixed, s_c)
            finish(r0, m, jnp.dot(vtc, jnp.exp2(s_c - m).astype(BF16), preferred_element_type=F32))

    def window_block(n):
        return jnp.clip(n - 1, 0, n_blocks - 3)

    def scores(n):
        wb = window_block(n)
        w = qmat(n)
        kw = k_ref[0, pl.ds(pl.multiple_of(wb * blk, blk), win), :]
        return (jnp.dot(kc, w, preferred_element_type=F32),
                jnp.dot(kw, w, preferred_element_type=F32) + bias_ref[n - wb])

    def latent_queries(fixed):
        def blocks(gi, carry):
            n0 = gi * group
            cur = scores(n0)
            for i in range(group):
                n = n0 + i
                nxt = scores(n + 1) if i + 1 < group else None
                s_c, s_w = cur
                m = row_max(fixed, s_c, s_w)
                wb = window_block(n)
                vtw = jnp.concatenate([vt_ref[wb + j] for j in range(3)], axis=1)
                acc = (jnp.dot(vtc, jnp.exp2(s_c - m).astype(BF16), preferred_element_type=F32)
                       + jnp.dot(vtw, jnp.exp2(s_w - m).astype(BF16), preferred_element_type=F32))
                finish(pl.multiple_of(n * blk, blk), m, acc)
                cur = nxt
            return carry

        lax.fori_loop(0, n_blocks // group, blocks, 0)

    assert n_blocks % group == 0
    safe = rng_ref[layer] < SWA_SAFE_RANGE

    @pl.when(safe)
    def _():
        ctx_queries(True)
        latent_queries(True)

    @pl.when(jnp.logical_not(safe))
    def _():
        ctx_queries(False)
        latent_queries(False)


def _swa_call(l, qb, kb, vb, sinkrow, shiftrow, swa_range, g_sqt, rope_t, n_ctx):
    B, S, _ = qb.shape
    nq = (SWA_HEADS // SWA_KV_HEADS) * SWA_BLOCK
    return pl.pallas_call(
        functools.partial(_swa_kernel, n_ctx=n_ctx, n_lat=S - n_ctx, layer=l),
        out_shape=jax.ShapeDtypeStruct((B, S, 512), BF16),
        grid=(B, SWA_KV_HEADS),
        in_specs=[pl.BlockSpec(memory_space=pltpu.SMEM),
                  pl.BlockSpec((1, S, 256), lambda b, j: (b, 0, j)),
                  pl.BlockSpec((1, S, LANES), lambda b, j: (b, 0, 0)),
                  pl.BlockSpec((1, S, LANES), lambda b, j: (b, 0, 0)),
                  pl.BlockSpec((1, 8, nq), lambda b, j: (l * SWA_KV_HEADS + j, 0, 0)),
                  pl.BlockSpec((1, 8, nq), lambda b, j: (l * SWA_KV_HEADS + j, 0, 0)),
                  pl.BlockSpec((1, SWA_DIM, SWA_BLOCK), lambda b, j: (l, 0, 0)),
                  pl.BlockSpec(rope_t.shape, lambda b, j: (0, 0, 0, 0))],
        out_specs=pl.BlockSpec((1, S, 256), lambda b, j: (b, 0, j)),
        scratch_shapes=[pltpu.VMEM((S // SWA_BLOCK, SWA_VT_ROWS, SWA_BLOCK), BF16),
                        pltpu.VMEM((3, 3 * SWA_BLOCK, nq), F32)],
        compiler_params=_cparams(2), name="swa_attn",
    )(swa_range, qb, kb, vb, sinkrow, shiftrow, g_sqt, rope_t)


def _gelu(y):
    return 0.5 * y * (1.0 + jnp.tanh(math.sqrt(2.0 / math.pi) * (y + 0.044715 * (y * y * y))))


def _s5_kernel(u_ref, wst_ref, wloc_ref, wcar_ref, at_ref, o_ref,
               ut_ref, utc_ref, yt_ref, ytc_ref, ere_ref, eim_ref, hfr_ref, hfi_ref, hbr_ref, hbi_ref,
               *, nb, n_lat_chunks, n_ctx_chunks):
    T, GP = S5_CHUNK, S5_GROUP
    n_chunks = n_lat_chunks + n_ctx_chunks
    n_lat = n_lat_chunks * T
    cw = nb * n_ctx_chunks
    tn_dims = (((0,), (0,)), ((), ()))
    nt_dims = (((1,), (1,)), ((), ()))
    gpl = LANES // GP

    for b in range(nb):
        for s in range(T):
            xs = u_ref[b, pl.ds(s, n_lat_chunks, stride=T), :]
            ut_ref[b, :, s] = xs.T.reshape(gpl, GP, n_lat_chunks).astype(BF16)
    zpad = jnp.zeros((LANES - cw, LANES), F32)
    for s in range(T):
        xs = jnp.concatenate([u_ref[b, pl.ds(n_lat + s, n_ctx_chunks, stride=T), :] for b in range(nb)]
                             + [zpad], axis=0)
        utc_ref[:, s] = xs.T.reshape(gpl, GP, LANES).astype(BF16)

    lane = lax.broadcasted_iota(jnp.int32, (2 * nb, LANES), 1)
    fwd = lane < S5_STATE
    lane_l = lax.broadcasted_iota(jnp.int32, (n_lat_chunks, LANES), 1) < S5_STATE
    lane_c = lax.broadcasted_iota(jnp.int32, (LANES, LANES), 1) < S5_STATE

    slab = 2 * nb
    lat_rows = lambda gb: pl.ds(gb, n_lat_chunks, stride=slab)
    ctx_rows = lambda gb: pl.ds(n_lat_chunks * slab + gb, n_ctx_chunks, stride=slab)

    def pair(gp, carry):
        for gl in range(2):
            g = gp * 2 + gl
            wst = wst_ref[g]
            for b in range(nb):
                e = lax.dot_general(ut_ref[b, g].reshape(T * GP, n_lat_chunks), wst, tn_dims,
                                    preferred_element_type=F32)
                ere_ref[lat_rows(gl * nb + b), :] = e[:, :LANES]
                eim_ref[lat_rows(gl * nb + b), :] = e[:, LANES:]
            ec = lax.dot_general(utc_ref[g].reshape(T * GP, LANES), wst, tn_dims, preferred_element_type=F32)
            for b in range(nb):
                ere_ref[ctx_rows(gl * nb + b), :] = ec[b * n_ctx_chunks:(b + 1) * n_ctx_chunks, :LANES]
                eim_ref[ctx_rows(gl * nb + b), :] = ec[b * n_ctx_chunks:(b + 1) * n_ctx_chunks, LANES:]

        a_re = jnp.concatenate([jnp.broadcast_to(at_ref[gp * 2 + gl, 0:1, :], (nb, LANES)) for gl in range(2)], 0)
        a_im = jnp.concatenate([jnp.broadcast_to(at_ref[gp * 2 + gl, 1:2, :], (nb, LANES)) for gl in range(2)], 0)

        def step(i, hc):
            h_re, h_im = hc
            cf = jnp.where(i < n_ctx_chunks, n_lat_chunks + i, i - n_ctx_chunks)
            cb = n_chunks - 1 - i
            sf = pl.ds(pl.multiple_of(cf * slab, slab), slab)
            sb = pl.ds(pl.multiple_of(cb * slab, slab), slab)
            hfr_ref[sf, :] = h_re
            hfi_ref[sf, :] = h_im
            hbr_ref[sb, :] = h_re
            hbi_ref[sb, :] = h_im
            e_re = jnp.where(fwd, ere_ref[sf, :], ere_ref[sb, :])
            e_im = jnp.where(fwd, eim_ref[sf, :], eim_ref[sb, :])
            return (a_re * h_re - a_im * h_im + e_re, a_re * h_im + a_im * h_re + e_im)

        zero = jnp.zeros((2 * nb, LANES), F32)
        lax.fori_loop(0, n_chunks, step, (zero, zero), unroll=8)

        for gl in range(2):
            g = gp * 2 + gl
            wloc, wcar = wloc_ref[g], wcar_ref[g]
            for b in range(nb):
                rows = lat_rows(gl * nb + b)
                h_cat = jnp.concatenate([jnp.where(lane_l, hfr_ref[rows, :], hbr_ref[rows, :]),
                                         jnp.where(lane_l, hfi_ref[rows, :], hbi_ref[rows, :])],
                                        axis=-1).astype(BF16)
                yt = (jnp.dot(wloc, ut_ref[b, g].reshape(T * GP, n_lat_chunks), preferred_element_type=F32)
                      + lax.dot_general(wcar, h_cat, nt_dims, preferred_element_type=F32))
                yt_ref[b, g] = _gelu(yt).astype(BF16).reshape(T, GP, n_lat_chunks)
            crow = lambda ref: jnp.concatenate(
                [ref[ctx_rows(gl * nb + b), :] for b in range(nb)]
                + [jnp.zeros((LANES - cw, LANES), F32)], axis=0)
            h_cat = jnp.concatenate([jnp.where(lane_c, crow(hfr_ref), crow(hbr_ref)),
                                     jnp.where(lane_c, crow(hfi_ref), crow(hbi_ref))], axis=-1).astype(BF16)
            ytc = (jnp.dot(wloc, utc_ref[g].reshape(T * GP, LANES), preferred_element_type=F32)
                   + lax.dot_general(wcar, h_cat, nt_dims, preferred_element_type=F32))
            ytc_ref[g] = _gelu(ytc).astype(BF16).reshape(T, GP, LANES)
        return carry

    lax.fori_loop(0, gpl // 2, pair, 0)

    for b in range(nb):
        for t in range(T):
            z = yt_ref[b, :, t].astype(F32).reshape(LANES, n_lat_chunks)
            o_ref[b, pl.ds(t, n_lat_chunks, stride=T), :] = z.T
    for t in range(T):
        z = ytc_ref[:, t].astype(F32).reshape(LANES, LANES).T
        for b in range(nb):
            o_ref[b, pl.ds(n_lat + t, n_ctx_chunks, stride=T), :] = z[b * n_ctx_chunks:(b + 1) * n_ctx_chunks]


def _s5_call(l, u, P, n_lat):
    B, S, W = u.shape
    nb = 4 if B % 4 == 0 else B
    T, GP = S5_CHUNK, S5_GROUP
    nlc, ncc = n_lat // T, (S - n_lat) // T
    gpl = LANES // GP
    nblk = W // LANES
    big = lambda n: pl.BlockSpec((nb, S, LANES), lambda j, hb: (hb, 0, j), pipeline_mode=pl.Buffered(n))
    wspec = lambda: pl.BlockSpec((gpl, T * GP, T * GP), lambda j, hb: (l * nblk + j, 0, 0))
    rows = 2 * nb * (nlc + ncc)
    return pl.pallas_call(
        functools.partial(_s5_kernel, nb=nb, n_lat_chunks=nlc, n_ctx_chunks=ncc),
        out_shape=jax.ShapeDtypeStruct((B, S, W), F32),
        grid=(nblk, B // nb),
        in_specs=[big(2), wspec(), wspec(), wspec(),
                  pl.BlockSpec((gpl, 2, LANES), lambda j, hb: (l * nblk + j, 0, 0))],
        out_specs=big(1),
        scratch_shapes=[pltpu.VMEM((nb, gpl, T, GP, nlc), BF16), pltpu.VMEM((gpl, T, GP, LANES), BF16),
                        pltpu.VMEM((nb, gpl, T, GP, nlc), BF16), pltpu.VMEM((gpl, T, GP, LANES), BF16)]
                       + [pltpu.VMEM((rows, LANES), F32)] * 6,
        compiler_params=_cparams(2), name="s5",
    )(u, P["s5_wst"], P["s5_wloc_t"], P["s5_wcar_t"], P["s5_at"])


def _out_kernel(x_ref, c_ref, mod_ref, oa_ref, oac_ref, ob_ref, gy_ref, gt_ref, wglu_ref, wout_ref,
                xo_ref, co_ref, *, n_lat_tiles):
    t = pl.program_id(1)
    cb = OUT_COLS
    nb, rows = x_ref.shape[0], x_ref.shape[1]
    stack = lambda ref: jnp.concatenate([ref[i] for i in range(nb)], axis=0)
    gyb = stack(gy_ref).astype(BF16)
    g = stack(gt_ref).astype(F32)
    oa = jnp.where(t == n_lat_tiles, stack(oac_ref), stack(oa_ref))
    m_a = (oa.astype(F32) * g[:, 0:512]).astype(BF16)
    m_b = (stack(ob_ref).astype(F32) * g[:, 512:1024]).astype(BF16)
    oc = []
    for j in range(512 // cb):
        za = jnp.dot(gyb, wglu_ref[0, :, j * cb:(j + 1) * cb], preferred_element_type=F32)
        zb = jnp.dot(gyb, wglu_ref[0, :, 512 + j * cb:512 + (j + 1) * cb], preferred_element_type=F32)
        oc.append((za * _sigmoid(zb) * g[:, 1024 + j * cb:1024 + (j + 1) * cb]).astype(BF16))
    m_c = jnp.concatenate(oc, axis=-1)
    gate = jnp.concatenate([jnp.broadcast_to(mod_ref[0, i][2:3], (rows, x_ref.shape[2])) for i in range(nb)],
                           axis=0)
    resid = jnp.where(t == n_lat_tiles, stack(c_ref), stack(x_ref))
    new = []
    for j in range(resid.shape[1] // cb):
        cols = slice(j * cb, (j + 1) * cb)
        upd = (jnp.dot(m_a, wout_ref[0, 0:512, cols], preferred_element_type=F32)
               + jnp.dot(m_b, wout_ref[0, 512:1024, cols], preferred_element_type=F32)
               + jnp.dot(m_c, wout_ref[0, 1024:1536, cols], preferred_element_type=F32))
        new.append(resid[:, cols] + gate[:, cols] * upd)

    @pl.when(t == n_lat_tiles)
    def _():
        for j, v in enumerate(new):
            for i in range(nb):
                co_ref[i, :, j * cb:(j + 1) * cb] = v[i * rows:(i + 1) * rows]

    @pl.when(t < n_lat_tiles)
    def _():
        for j, v in enumerate(new):
            for i in range(nb):
                xo_ref[i, :, j * cb:(j + 1) * cb] = v[i * rows:(i + 1) * rows]


def _out_call(l, x, ctx, mod, oa, oa_ctx, ob, gy, gt, P):
    B, L, D = x.shape
    C = ctx.shape[1]
    S = C + L
    nl = L // TOK
    nb = PROJ_NB
    xmap = lambda b, t: (b, jnp.minimum(t, nl - 1), 0)
    cmap = lambda b, t: (b, 0, 0)
    tmap = lambda b, t: (b, t, 0)
    return pl.pallas_call(
        functools.partial(_out_kernel, n_lat_tiles=nl),
        out_shape=[jax.ShapeDtypeStruct(x.shape, F32), jax.ShapeDtypeStruct(ctx.shape, F32)],
        grid=(B // nb, S // TOK),
        in_specs=[pl.BlockSpec((nb, TOK, D), xmap),
                  pl.BlockSpec((nb, TOK, D), cmap),
                  pl.BlockSpec((1, nb, 3, D), lambda b, t: (l, jnp.where(t == nl, B // nb, b), 0, 0)),
                  pl.BlockSpec((nb, TOK, 512), xmap),
                  pl.BlockSpec((nb, TOK, 512), cmap),
                  pl.BlockSpec((nb, TOK, 512), tmap),
                  pl.BlockSpec((nb, TOK, 512), tmap),
                  pl.BlockSpec((nb, TOK, 1536), tmap),
                  pl.BlockSpec((1, 512, 1024), lambda b, t: (l, 0, 0)),
                  pl.BlockSpec((1, 1536, D), lambda b, t: (l, 0, 0))],
        out_specs=[pl.BlockSpec((nb, TOK, D), xmap), pl.BlockSpec((nb, TOK, D), cmap)],
        compiler_params=_cparams(2), name="out",
    )(x, ctx, mod, oa, oa_ctx, ob, gy, gt, P["w_glu"], P["w_out"])


def _rope_angles(n_lat, n_ctx, rot_dim):
    rows = n_lat // GRID_W
    r_idx, c_idx = jnp.meshgrid(jnp.arange(rows), jnp.arange(GRID_W), indexing="ij")
    r_idx, c_idx = r_idx.reshape(-1), c_idx.reshape(-1)
    n_freq = rot_dim // 4
    freqs = ROPE_BASE ** (-jnp.arange(n_freq, dtype=F32) / n_freq)
    ang = jnp.concatenate([r_idx.astype(F32)[:, None] * freqs,
                           c_idx.astype(F32)[:, None] * freqs], axis=-1)
    return jnp.concatenate([ang, jnp.zeros((n_ctx, rot_dim // 2), F32)], axis=0)


def _rope_tables_t(n_lat, n_ctx, rot_dim):
    ang = _rope_angles(n_lat, n_ctx, rot_dim).T
    return jnp.stack([jnp.cos(ang), jnp.sin(ang)])


def _rope_tables(n_lat, n_ctx, rot_dim, lead, reps):
    ang = _rope_angles(n_lat, n_ctx, rot_dim)
    cos, sin, zero = jnp.cos(ang), jnp.sin(ang), jnp.zeros_like(ang)
    n = ang.shape[0]
    tail = LANES // reps - lead - rot_dim

    def pack(x1, x2, fill):
        unit = [jnp.full((n, lead), fill, F32), x1, x2, jnp.full((n, tail), fill, F32)]
        return jnp.concatenate(unit * reps, axis=-1)

    return jnp.stack([pack(cos, cos, 1.0), pack(zero, sin, 0.0), pack(-sin, zero, 0.0)])


def _prep_params(norm_g, w_in, w_out, mla_g_cq, mla_g_ckv, mla_w_uq, mla_w_ukv, mla_g_qn, mla_g_kn,
                 swa_g_qn, swa_g_kn, swa_sink, s5_a_re, s5_a_im, s5_log_dt, s5_b_re, s5_b_im,
                 s5_c_re, s5_c_im, s5_d, s5_w_glu):
    depth, D, _ = w_in.shape
    o_cq, o_ckv, o_kr, o_gm, o_sq, o_sk, o_sv, o_gs, o_u, o_g5, o_end = (
        0, 384, 640, 672, 1184, 1696, 1824, 1952, 2464, 2976, 3488)
    z = lambda n: jnp.zeros((depth, D, n), F32)
    w_in_p = (
        w_in[:, :, o_cq:o_kr].astype(BF16),
        jnp.concatenate([z(64), w_in[:, :, o_kr:o_gm], z(32)], axis=-1).astype(BF16),
        w_in[:, :, o_sq:o_gs].astype(BF16),
        w_in[:, :, o_u:o_g5].astype(BF16),
        jnp.concatenate([w_in[:, :, o_gm:o_sq], w_in[:, :, o_gs:o_u], w_in[:, :, o_g5:o_end]],
                        axis=-1).astype(BF16),
    )
    assert sum(w.shape[-1] for w in w_in_p) == C_END

    wq = mla_w_uq.reshape(depth, MLA_Q_RANK, MLA_HEADS, MLA_QK)
    wq = jnp.pad(wq, ((0, 0), (0, 0), (0, 0), (0, LANES - MLA_QK)))
    w_uq_p = wq.reshape(depth, MLA_Q_RANK, MLA_HEADS * LANES).astype(BF16)
    wkv = mla_w_ukv.reshape(depth, MLA_KV_RANK, MLA_HEADS, MLA_NOPE + MLA_V)
    wk = jnp.pad(wkv[..., :MLA_NOPE], ((0, 0), (0, 0), (0, 0), (0, LANES - MLA_NOPE)))
    w_ukv_p = jnp.concatenate([wk.reshape(depth, MLA_KV_RANK, MLA_HEADS * LANES),
                               wkv[..., MLA_NOPE:].reshape(depth, MLA_KV_RANK, MLA_HEADS * MLA_V)],
                              axis=-1).astype(BF16)

    pad_qk = lambda g: jnp.pad(g, ((0, 0), (0, LANES - MLA_QK)))[:, None, :]
    g_q = pad_qk(mla_g_qn * (MLA_QK ** -0.5 * LOG2E))
    g_qt = jnp.broadcast_to(jnp.swapaxes(g_q, 1, 2), (depth, LANES, LANES))
    g_k = pad_qk(mla_g_kn)
    mla_bound = 1.02 * MLA_QK * jnp.max(jnp.abs(g_q), axis=(1, 2)) * jnp.max(jnp.abs(g_k), axis=(1, 2))
    g_sqt = jnp.broadcast_to((swa_g_qn * (SWA_DIM ** -0.5 * LOG2E))[:, :, None], (depth, SWA_DIM, SWA_BLOCK))
    g_sk = jnp.tile(swa_g_kn, (1, 2))[:, None, :]
    per_kv = SWA_HEADS // SWA_KV_HEADS
    lanes_of = lambda v: jnp.broadcast_to(v.reshape(depth * SWA_KV_HEADS, 1, per_kv, 1),
                                          (depth * SWA_KV_HEADS, 8, per_kv, SWA_BLOCK)
                                          ).reshape(depth * SWA_KV_HEADS, 8, -1).astype(F32)
    sink2 = swa_sink * LOG2E
    sinkrow = lanes_of(sink2)
    swa_bound = 1.02 * SWA_DIM * jnp.max(jnp.abs(g_sqt), axis=(1, 2)) * jnp.max(jnp.abs(g_sk), axis=(1, 2))
    shift = jnp.maximum(swa_bound[:, None], sink2)
    shiftrow = lanes_of(shift)
    swa_range = (swa_bound + jnp.max(shift, axis=1)).astype(F32)

    T = S5_CHUNK
    A = lax.complex(s5_a_re, s5_a_im)
    dt = jnp.exp(s5_log_dt)[..., None]
    a_bar = jnp.exp(dt * A)
    b_bar = ((a_bar - 1.0) / A)[..., None] * lax.complex(s5_b_re, s5_b_im)
    c_mat = lax.complex(s5_c_re, s5_c_im)
    k_idx = jnp.arange(T + 1, dtype=F32)
    pw = jnp.exp(k_idx[None, None, None, :, None] * (dt * A)[:, :, :, None, :])
    hi = lax.Precision.HIGHEST
    tt = jnp.arange(T)
    GP, NG = S5_GROUP, depth * S5_GROUPS
    pw_f, pw_b = pw[:, 0], pw[:, 1]
    tap_f = jnp.einsum("lgpn,lgkn,lgnq->lgpkq", c_mat[:, 0], pw_f[:, :, T - 1 - tt], b_bar[:, 0],
                       precision=hi).real
    tap_b = jnp.einsum("lgpn,lgkn,lgnq->lgpkq", c_mat[:, 1], pw_b[:, :, tt], b_bar[:, 1],
                       precision=hi).real
    d_diag = s5_d.reshape(depth, S5_GROUPS, GP)[..., None] * jnp.eye(GP, dtype=F32)
    centre = tap_f[:, :, :, T - 1:] + tap_b[:, :, :, :1] + d_diag[:, :, :, None, :]
    krev = jnp.concatenate([tap_f[:, :, :, :T - 1], centre, tap_b[:, :, :, 1:]], axis=3)
    krev = krev.reshape(depth, S5_GROUPS, GP, (2 * T - 1) * GP)
    wloc_t = jnp.stack([krev[..., (T - 1 - t) * GP:(T - 1 - t) * GP + T * GP] for t in range(T)], axis=2)
    wloc_t = wloc_t.reshape(NG, T * GP, T * GP)
    b_t = jnp.swapaxes(b_bar, -1, -2)
    inc_f = pw_f[:, :, T - 1 - tt][:, :, :, None, :] * b_t[:, 0][:, :, None]
    inc_b = pw_b[:, :, tt][:, :, :, None, :] * b_t[:, 1][:, :, None]
    wst = jnp.concatenate([inc_f.real, inc_b.real, inc_f.imag, inc_b.imag], axis=-1)
    wst = wst.reshape(NG, T * GP, 4 * S5_STATE)
    ro_f = c_mat[:, 0][:, :, None] * pw_f[:, :, tt + 1][:, :, :, None, :]
    ro_b = c_mat[:, 1][:, :, None] * pw_b[:, :, T - tt][:, :, :, None, :]
    wcar_t = jnp.concatenate([ro_f.real, ro_b.real, -ro_f.imag, -ro_b.imag], axis=-1)
    wcar_t = wcar_t.reshape(NG, T * GP, 4 * S5_STATE)
    a_t = pw[:, :, :, T]
    at = jnp.stack([jnp.concatenate([a_t[:, 0].real, a_t[:, 1].real], axis=-1),
                    jnp.concatenate([a_t[:, 0].imag, a_t[:, 1].imag], axis=-1)], axis=2)
    at = at.reshape(NG, 2, LANES).astype(F32)

    return dict(norm_g=norm_g[:, None, :], w_in=w_in_p, g_cq=mla_g_cq[:, None, :], w_uq=w_uq_p, g_qt=g_qt, mla_bound=mla_bound.astype(F32),
                g_ckv=mla_g_ckv[:, None, :], w_ukv=w_ukv_p, g_k=g_k, g_sqt=g_sqt, g_sk=g_sk,
                sinkrow=sinkrow, shiftrow=shiftrow, swa_range=swa_range, s5_wloc_t=wloc_t.astype(BF16), s5_wst=wst.astype(BF16),
                s5_wcar_t=wcar_t.astype(BF16), s5_at=at, w_glu=s5_w_glu.astype(BF16),
                w_out=w_out.astype(BF16))


def kernel(x, c, ctx, c_ctx, norm_g, w_ada, b_ada, w_in, w_out, mla_g_cq, mla_g_ckv, mla_w_uq, mla_w_ukv, mla_g_qn, mla_g_kn, swa_g_qn, swa_g_kn, swa_sink, s5_a_re, s5_a_im, s5_log_dt, s5_b_re, s5_b_im, s5_c_re, s5_c_im, s5_d, s5_w_glu):
    B, L, D = x.shape
    C = ctx.shape[1]
    S = C + L
    depth = w_in.shape[0]
    assert B + PROJ_NB <= MOD_ROWS and C == TOK and L % (2 * TOK) == 0 and L % GRID_W == 0 and L >= 3 * SWA_BLOCK

    P = _prep_params(norm_g, w_in, w_out, mla_g_cq, mla_g_ckv, mla_w_uq, mla_w_ukv, mla_g_qn, mla_g_kn,
                     swa_g_qn, swa_g_kn, swa_sink, s5_a_re, s5_a_im, s5_log_dt, s5_b_re, s5_b_im,
                     s5_c_re, s5_c_im, s5_d, s5_w_glu)
    rope_mla = _rope_tables(L, C, MLA_ROPE, MLA_NOPE, 1)
    rope_swa = _rope_tables(L, C, SWA_DIM, 0, 2)
    rope_mla_t = _rope_tables_t(L, C, MLA_ROPE)
    rope_swa_t = jnp.transpose(_rope_tables_t(L, C, SWA_DIM).reshape(2, SWA_DIM // 2, S // SWA_BLOCK, SWA_BLOCK),
                               (2, 0, 1, 3))

    cc = jnp.concatenate([c, jnp.tile(c_ctx[None, :], (PROJ_NB, 1)),
                          jnp.zeros((MOD_ROWS - B - PROJ_NB, D), F32)], axis=0)
    mod = _ada_call(cc, w_ada, b_ada).reshape(depth, MOD_ROWS, 3, D)

    for l in range(depth):
        qa, ka, va, qb, kb, vb, u, gt = _proj_call(l, x, ctx, mod, P, rope_mla, rope_swa)
        oa, oa_ctx = _mla_call(l, qa, ka, va, P["g_qt"], rope_mla_t, P["mla_bound"], L)
        ob = _swa_call(l, qb, kb, vb, P["sinkrow"], P["shiftrow"], P["swa_range"], P["g_sqt"], rope_swa_t, C)
        gy = _s5_call(l, u, P, L)
        x, ctx = _out_call(l, x, ctx, mod, oa, oa_ctx, ob, gy, gt, P)
    return x
```

```python
import functools
import math

import jax
import jax.numpy as jnp
from jax import lax
from jax.experimental import pallas as pl
from jax.experimental.pallas import tpu as pltpu

F32 = jnp.float32
BF16 = jnp.bfloat16

GRID_W = 64
EPS = 1e-6
ROPE_BASE = 10000.0
NEG = -1e30
LOG2E = math.log2(math.e)

MLA_HEADS = 8
MLA_NOPE = 64
MLA_ROPE = 32
MLA_V = 64
MLA_QK = MLA_NOPE + MLA_ROPE
MLA_Q_RANK = 384
MLA_KV_RANK = 256

SWA_HEADS = 8
SWA_KV_HEADS = 2
SWA_DIM = 64
SWA_WINDOW = 128

S5_GROUP = 16
S5_GROUPS = 32
S5_STATE = 64
S5_CHUNK = 16

LANES = 128
TOK = 256
KV_CHUNK = 256
SWA_BLOCK = 128
MOD_ROWS = 16
PROJ_NB = 2
SWA_UNROLL = 32
OUT_COLS = 256
PROJ_SKEW = 1

C_CQ = 0
C_CKV = 384
C_KR = 640
C_SQ = 768
C_SK = 1280
C_SV = 1408
C_U = 1536
C_GATE = 2048
C_END = 3584

VMEM_LIMIT = 56 * 1024 * 1024
S5_VMEM_LIMIT = 60 * 1024 * 1024


def _cparams(n_axes):
    return pltpu.CompilerParams(dimension_semantics=("arbitrary",) * n_axes,
                                vmem_limit_bytes=VMEM_LIMIT)


def _ada_kernel(c_ref, w_ref, b_ref, o_ref):
    cc = c_ref[...]
    s = cc * jax.nn.sigmoid(cc)
    o_ref[0] = jnp.dot(s, w_ref[0], preferred_element_type=F32,
                       precision=lax.Precision.HIGHEST) + b_ref[0]


def _ada_call(cc, w_ada, b_ada):
    depth, d, n3 = w_ada.shape
    tn = 768
    return pl.pallas_call(
        _ada_kernel,
        out_shape=jax.ShapeDtypeStruct((depth, MOD_ROWS, n3), F32),
        grid=(depth, n3 // tn),
        in_specs=[pl.BlockSpec((MOD_ROWS, d), lambda l, j: (0, 0)),
                  pl.BlockSpec((1, d, tn), lambda l, j: (l, 0, j)),
                  pl.BlockSpec((1, 1, tn), lambda l, j: (l, 0, j))],
        out_specs=pl.BlockSpec((1, MOD_ROWS, tn), lambda l, j: (l, 0, j)),
        compiler_params=_cparams(2),
        name="ada",
    )(cc, w_ada, b_ada.reshape(depth, 1, n3))


def _sigmoid(v):
    return 0.5 * jnp.tanh(0.5 * v) + 0.5


def _norm_rope(slots, lo_masks, dim, gain, tab_ref, half):
    sq = [s * s for s in slots]
    sums = []
    for s2, lo in zip(sq, lo_masks):
        if lo is None:
            sums.append((jnp.sum(s2, axis=-1, keepdims=True),))
        else:
            sums.append((jnp.sum(jnp.where(lo, s2, 0.0), axis=-1, keepdims=True),
                         jnp.sum(jnp.where(lo, 0.0, s2), axis=-1, keepdims=True)))
    ys = []
    for s, ss, lo in zip(slots, sums, lo_masks):
        rs = [lax.rsqrt(v * (1.0 / dim) + EPS) for v in ss]
        r = rs[0] if len(rs) == 1 else jnp.where(lo, rs[0], rs[1])
        ys.append(s * r * gain)
    up = [pltpu.roll(y, half, 1) for y in ys]
    dn = [pltpu.roll(y, LANES - half, 1) for y in ys]
    return [(y * tab_ref[0] + u * tab_ref[1] + d * tab_ref[2]).astype(BF16) for y, u, d in zip(ys, up, dn)]


N_PROJ_DATA = 3
N_PROJ_WEIGHTS = 14


def _proj_kernel(*refs, n_lat_tiles):
    data = refs[:N_PROJ_DATA]
    shared = refs[N_PROJ_DATA:N_PROJ_DATA + N_PROJ_WEIGHTS]
    outs = refs[N_PROJ_DATA + N_PROJ_WEIGHTS:]
    streams = []
    for i in range(data[0].shape[0]):
        one = pl.ds(i, 1)
        streams.append(_proj_tile(data[0].at[one], data[1].at[one], data[2].at[:, one], *shared,
                                  *[o.at[one] for o in outs], n_lat_tiles=n_lat_tiles))
    pending = list(enumerate(streams))
    step = 0
    while pending:
        for item in list(pending):
            if step >= item[0] * PROJ_SKEW and next(item[1], "done") == "done":
                pending.remove(item)
        step += 1


def _proj_tile(x_ref, c_ref, mod_ref, ng_ref, wa_ref, wkr_ref, wb_ref, wu_ref, wg_ref, gcq_ref, wuq_ref,
               gckv_ref, wukv_ref, gk_ref, gsk_ref, rm_ref, rs_ref,
               qa_ref, ka_ref, va_ref, qb_ref, kb_ref, vb_ref, u_ref, gt_ref, *, n_lat_tiles):
    t = pl.program_id(1)
    x = jnp.where(t == n_lat_tiles, c_ref[0], x_ref[0])
    mod = mod_ref[0, 0]
    y = x * lax.rsqrt(jnp.mean(x * x, axis=-1, keepdims=True) + EPS) * ng_ref[0]
    xn = (y * (1.0 + mod[1:2]) + mod[0:1]).astype(BF16)

    pieces = ((C_CQ, wa_ref), (C_KR, wkr_ref), (C_SQ, wb_ref), (C_U, wu_ref), (C_GATE, wg_ref))

    def seg(a, b):
        start, ref = [p for p in pieces if p[0] <= a][-1]
        return jnp.dot(xn, ref[0, :, a - start:b - start], preferred_element_type=F32)

    lane = lax.broadcasted_iota(jnp.int32, (TOK, LANES), 1)
    lo = lane < 64

    yield
    cq = seg(C_CQ, C_CKV)
    ckv = seg(C_CKV, C_KR)
    kr = seg(C_KR, C_SQ)
    yield
    cqn = (cq * lax.rsqrt(jnp.mean(cq * cq, axis=-1, keepdims=True) + EPS) * gcq_ref[0]).astype(BF16)
    qf = jnp.dot(cqn, wuq_ref[0], preferred_element_type=F32)
    ckvn = (ckv * lax.rsqrt(jnp.mean(ckv * ckv, axis=-1, keepdims=True) + EPS) * gckv_ref[0]).astype(BF16)
    kvf = jnp.dot(ckvn, wukv_ref[0], preferred_element_type=F32)
    sq = seg(C_SQ, C_SK)
    yield
    for h in range(MLA_HEADS):
        qa_ref[0, h] = qf[:, h * LANES:(h + 1) * LANES].astype(BF16)

    sk = seg(C_SK, C_SV)
    vb_ref[0] = seg(C_SV, C_U).astype(BF16)
    yield
    slots = [kvf[:, h * LANES:(h + 1) * LANES] + kr for h in range(MLA_HEADS)]
    for h, o in enumerate(_norm_rope(slots, [None] * MLA_HEADS, MLA_QK, gk_ref[0], rm_ref, MLA_ROPE // 2)):
        ka_ref[0, h] = jnp.where(lane == MLA_QK, jnp.ones_like(o), o)
    va_ref[0] = kvf[:, MLA_HEADS * LANES:].astype(BF16)
    u_ref[0] = seg(C_U, C_GATE)
    yield
    qb_ref[0] = sq.astype(BF16)
    kb_ref[0] = _norm_rope([sk], [lo], SWA_DIM, gsk_ref[0], rs_ref, SWA_DIM // 2)[0]
    g = seg(C_GATE, C_END)
    yield
    gt_ref[0] = (g * _sigmoid(g)).astype(BF16)


def _proj_call(l, x, ctx, mod, P, rope_mla, rope_swa):
    B, L, D = x.shape
    C = ctx.shape[1]
    S = C + L
    nt = S // TOK

    def wspec(arr):
        shp = arr.shape
        return pl.BlockSpec((1,) + shp[1:], lambda b, t: (l,) + (0,) * (len(shp) - 1))

    weights = [P["norm_g"], *P["w_in"], P["g_cq"], P["w_uq"],
               P["g_ckv"], P["w_ukv"], P["g_k"], P["g_sk"]]
    nl = L // TOK
    nb = PROJ_NB
    assert len(weights) + 2 == N_PROJ_WEIGHTS and B % nb == 0
    in_specs = ([pl.BlockSpec((nb, TOK, D), lambda b, t: (b, jnp.minimum(t, nl - 1), 0)),
                 pl.BlockSpec((nb, TOK, D), lambda b, t: (b, 0, 0)),
                 pl.BlockSpec((1, nb, 3, D), lambda b, t: (l, jnp.where(t == nl, B // nb, b), 0, 0))]
                + [wspec(w) for w in weights]
                + [pl.BlockSpec((3, TOK, LANES), lambda b, t: (0, t, 0)),
                   pl.BlockSpec((3, TOK, LANES), lambda b, t: (0, t, 0))])
    out_shape = [jax.ShapeDtypeStruct((B, MLA_HEADS, S, LANES), BF16),
                 jax.ShapeDtypeStruct((B, MLA_HEADS, S, LANES), BF16),
                 jax.ShapeDtypeStruct((B, S, 512), BF16),
                 jax.ShapeDtypeStruct((B, S, 512), BF16),
                 jax.ShapeDtypeStruct((B, S, LANES), BF16),
                 jax.ShapeDtypeStruct((B, S, LANES), BF16),
                 jax.ShapeDtypeStruct((B, S, 512), F32),
                 jax.ShapeDtypeStruct((B, S, 1536), BF16)]
    out_specs = [pl.BlockSpec((nb, MLA_HEADS, TOK, LANES), lambda b, t: (b, 0, t, 0)),
                 pl.BlockSpec((nb, MLA_HEADS, TOK, LANES), lambda b, t: (b, 0, t, 0))]
    out_specs += [pl.BlockSpec((nb, TOK, s.shape[2]), lambda b, t: (b, t, 0)) for s in out_shape[2:]]
    return pl.pallas_call(
        functools.partial(_proj_kernel, n_lat_tiles=nl), out_shape=out_shape, grid=(B // nb, nt),
        in_specs=in_specs, out_specs=out_specs,
        compiler_params=_cparams(2), name="proj",
    )(x, ctx, mod, *weights, rope_mla, rope_swa)


MLA_SUBTILES = 4
MLA_VT_ROWS = 80


MLA_SAFE_RANGE = 60.0


def _mla_kernel(rb_ref, q_ref, k_ref, v_ref, gq_ref, rt_ref, o_ref, vt_ref, *, n_chunks, tq, layer):
    ones = jnp.ones((MLA_VT_ROWS - MLA_V, KV_CHUNK), BF16)

    @pl.when(pl.program_id(2) == 0)
    def _():
        for c in range(n_chunks):
            vt = v_ref[0, c].astype(F32).T.astype(BF16)
            vt_ref[0, c] = jnp.concatenate([vt[:MLA_V], ones], axis=0)
            vt_ref[1, c] = jnp.concatenate([vt[MLA_V:], ones], axis=0)

    bound = rb_ref[layer]
    half = MLA_ROPE // 2

    def sweep(fixed_shift):
        for r0 in range(0, q_ref.shape[2], tq):
            rows = slice(r0, r0 + tq)
            cos, sin = rt_ref[0, :, rows], rt_ref[1, :, rows]
            gain = jnp.concatenate([gq_ref[0]] * (tq // LANES), axis=1)
            pad = jnp.zeros((LANES - MLA_QK, tq), F32)
            if fixed_shift:
                row = lax.broadcasted_iota(jnp.int32, pad.shape, 0)
                pad = jnp.where(row == 0, -bound, pad)

            def qmat(e):
                qt = q_ref[0, e, rows, :].astype(F32).T
                r = lax.rsqrt(jnp.sum(qt * qt, axis=0, keepdims=True) * (1.0 / MLA_QK) + EPS)
                y = qt * r * gain
                x1, x2 = y[MLA_NOPE:MLA_NOPE + half], y[MLA_NOPE + half:MLA_QK]
                return jnp.concatenate([y[:MLA_NOPE], x1 * cos - x2 * sin, x2 * cos + x1 * sin, pad],
                                       axis=0).astype(BF16)

            qts = [qmat(e) for e in range(2)]
            score = lambda c, e: jnp.dot(k_ref[0, e, c], qts[e], preferred_element_type=F32)
            ms = [None, None]
            accs = [None, None]
            sts = [score(0, e) for e in range(2)]
            for c in range(n_chunks):
                nxt = [None, None]
                for e in range(2):
                    if c + 1 < n_chunks:
                        nxt[e] = score(c + 1, e)
                    if fixed_shift:
                        pv = jnp.dot(vt_ref[e, c], jnp.exp2(sts[e]).astype(BF16), preferred_element_type=F32)
                        accs[e] = pv if c == 0 else accs[e] + pv
                    else:
                        mc = jnp.max(sts[e], axis=0, keepdims=True)
                        m_new = mc if c == 0 else jnp.maximum(ms[e], mc)
                        pt = jnp.exp2(sts[e] - m_new).astype(BF16)
                        pv = jnp.dot(vt_ref[e, c], pt, preferred_element_type=F32)
                        accs[e] = pv if c == 0 else accs[e] * jnp.exp2(ms[e] - m_new) + pv
                        ms[e] = m_new
                sts = nxt
            ot = jnp.concatenate([accs[e][:MLA_V] / accs[e][MLA_V:MLA_V + 1] for e in range(2)], axis=0)
            o_ref[0, rows, :] = ot.T.astype(BF16)

    @pl.when(bound < MLA_SAFE_RANGE)
    def _():
        sweep(True)

    @pl.when(jnp.logical_not(bound < MLA_SAFE_RANGE))
    def _():
        sweep(False)


def _mla_call(l, qa, ka, va, g_qt, rope_t, r_bound, n_lat):
    B, H, S, _ = qa.shape
    half = MLA_ROPE // 2
    gspec = pl.BlockSpec((1, LANES, LANES), lambda b, p, t: (l, 0, 0))
    sspec = pl.BlockSpec(memory_space=pltpu.SMEM)
    nc = S // KV_CHUNK
    tq = 2 * TOK
    bq = MLA_SUBTILES * tq if n_lat % (MLA_SUBTILES * tq) == 0 else tq
    k5 = ka.reshape(B, H, nc, KV_CHUNK, LANES)
    v4 = va.reshape(B, nc, KV_CHUNK, 512)
    o_lat = pl.pallas_call(
        functools.partial(_mla_kernel, n_chunks=nc, tq=tq, layer=l),
        out_shape=jax.ShapeDtypeStruct((B, n_lat, 512), BF16),
        grid=(B, H // 2, n_lat // bq),
        in_specs=[sspec,
                  pl.BlockSpec((1, 2, bq, LANES), lambda b, p, t: (b, p, t, 0)),
                  pl.BlockSpec((1, 2, nc, KV_CHUNK, LANES), lambda b, p, t: (b, p, 0, 0, 0)),
                  pl.BlockSpec((1, nc, KV_CHUNK, LANES), lambda b, p, t: (b, 0, 0, p)),
                  gspec,
                  pl.BlockSpec((2, half, bq), lambda b, p, t: (0, 0, t))],
        out_specs=pl.BlockSpec((1, bq, LANES), lambda b, p, t: (b, t, p)),
        scratch_shapes=[pltpu.VMEM((2, nc, MLA_VT_ROWS, KV_CHUNK), BF16)],
        compiler_params=_cparams(3), name="mla_attn",
    )(r_bound, qa, k5, v4, g_qt, rope_t)
    cblk = n_lat // KV_CHUNK
    o_ctx = pl.pallas_call(
        functools.partial(_mla_kernel, n_chunks=1, tq=TOK, layer=l),
        out_shape=jax.ShapeDtypeStruct((B, S - n_lat, 512), BF16),
        grid=(B, H // 2, 1),
        in_specs=[sspec,
                  pl.BlockSpec((1, 2, TOK, LANES), lambda b, p, t: (b, p, cblk, 0)),
                  pl.BlockSpec((1, 2, 1, KV_CHUNK, LANES), lambda b, p, t: (b, p, cblk, 0, 0)),
                  pl.BlockSpec((1, 1, KV_CHUNK, LANES), lambda b, p, t: (b, cblk, 0, p)),
                  gspec,
                  pl.BlockSpec((2, half, TOK), lambda b, p, t: (0, 0, cblk))],
        out_specs=pl.BlockSpec((1, TOK, LANES), lambda b, p, t: (b, 0, p)),
        scratch_shapes=[pltpu.VMEM((2, 1, MLA_VT_ROWS, KV_CHUNK), BF16)],
        compiler_params=_cparams(3), name="mla_attn_ctx",
    )(r_bound, qa, k5, v4, g_qt, rope_t)
    return o_lat, o_ctx


SWA_SAFE_RANGE = 120.0
SWA_VT_ROWS = 80


def _swa_kernel(rng_ref, q_ref, k_ref, v_ref, sink_ref, shift_ref, gq_ref, rt_ref, o_ref, vt_ref, bias_ref,
                *, n_ctx, n_lat, layer):
    blk = SWA_BLOCK
    heads = SWA_HEADS // SWA_KV_HEADS
    win = 3 * blk
    n_blocks = n_lat // blk
    ctx_blk = n_blocks
    group = min(SWA_UNROLL, n_blocks)
    first = pl.program_id(1) == 0
    sink = sink_ref[0, 0:1, :]

    ones = jnp.ones((SWA_VT_ROWS - SWA_DIM, blk), BF16)
    for i in range((n_lat + n_ctx) // blk):
        vt = v_ref[0, i * blk:(i + 1) * blk, :].astype(F32).T
        vt_ref[i] = jnp.concatenate([jnp.where(first, vt[:SWA_DIM], vt[SWA_DIM:]).astype(BF16), ones], axis=0)
    rel0 = (lax.broadcasted_iota(jnp.int32, (win, heads * blk), 0)
            - (lax.broadcasted_iota(jnp.int32, (win, heads * blk), 1) & (blk - 1)))
    for kind in range(3):
        bias_ref[kind] = jnp.where(jnp.abs(rel0 - kind * blk) <= SWA_WINDOW, 0.0, NEG)

    kc = k_ref[0, n_lat:n_lat + n_ctx, :]
    vtc = jnp.concatenate([vt_ref[ctx_blk + i] for i in range(n_ctx // blk)], axis=1)

    gain = gq_ref[0]
    half = SWA_DIM // 2

    def qmat(n):
        qt = q_ref[0, pl.ds(pl.multiple_of(n * blk, blk), blk), :].astype(F32).T
        cos, sin = rt_ref[n, 0], rt_ref[n, 1]
        cols = []
        for h in range(heads):
            x = qt[h * SWA_DIM:(h + 1) * SWA_DIM]
            y = x * lax.rsqrt(jnp.sum(x * x, axis=0, keepdims=True) * (1.0 / SWA_DIM) + EPS) * gain
            x1, x2 = y[:half], y[half:]
            cols.append(jnp.concatenate([x1 * cos - x2 * sin, x2 * cos + x1 * sin], axis=0))
        w = jnp.concatenate(cols, axis=1).astype(BF16)
        z = jnp.zeros_like(w)
        return jnp.where(first, jnp.concatenate([w, z], axis=0), jnp.concatenate([z, w], axis=0))

    def finish(r0, m, acc):
        ot = acc[:SWA_DIM] / (acc[SWA_DIM:SWA_DIM + 1] + jnp.exp2(sink - m))
        o4 = jnp.concatenate([ot[:, h * blk:(h + 1) * blk] for h in range(heads)], axis=0)
        o_ref[0, pl.ds(r0, blk), :] = o4.T.astype(BF16)

    shift = shift_ref[0, 0:1, :]

    def row_max(fixed, *scores):
        if fixed:
            return shift
        m = sink
        for sc in scores:
            m = jnp.maximum(m, jnp.max(sc, axis=0, keepdims=True))
        return m

    def ctx_queries(fixed):
        for n in range(n_ctx // blk):
            r0 = n_lat + n * blk
            s_c = jnp.dot(kc, qmat(ctx_blk + n), preferred_element_type=F32)
            m = row_max(fixed, s_c)
            finish(r0, m, jnp.dot(vtc, jnp.exp2(s_c - m).astype(BF16), preferred_element_type=F32))

    def window_block(n):
        return jnp.clip(n - 1, 0, n_blocks - 3)

    def scores(n):
        wb = window_block(n)
        w = qmat(n)
        kw = k_ref[0, pl.ds(pl.multiple_of(wb * blk, blk), win), :]
        return (jnp.dot(kc, w, preferred_element_type=F32),
                jnp.dot(kw, w, preferred_element_type=F32) + bias_ref[n - wb])

    def latent_queries(fixed):
        def blocks(gi, carry):
            n0 = gi * group
            cur = scores(n0)
            for i in range(group):
                n = n0 + i
                nxt = scores(n + 1) if i + 1 < group else None
                s_c, s_w = cur
                m = row_max(fixed, s_c, s_w)
                wb = window_block(n)
                vtw = jnp.concatenate([vt_ref[wb + j] for j in range(3)], axis=1)
                acc = (jnp.dot(vtc, jnp.exp2(s_c - m).astype(BF16), preferred_element_type=F32)
                       + jnp.dot(vtw, jnp.exp2(s_w - m).astype(BF16), preferred_element_type=F32))
                finish(pl.multiple_of(n * blk, blk), m, acc)
                cur = nxt
            return carry

        lax.fori_loop(0, n_blocks // group, blocks, 0)

    assert n_blocks % group == 0
    safe = rng_ref[layer] < SWA_SAFE_RANGE

    @pl.when(safe)
    def _():
        ctx_queries(True)
        latent_queries(True)

    @pl.when(jnp.logical_not(safe))
    def _():
        ctx_queries(False)
        latent_queries(False)


def _swa_call(l, qb, kb, vb, sinkrow, shiftrow, swa_range, g_sqt, rope_t, n_ctx):
    B, S, _ = qb.shape
    nq = (SWA_HEADS // SWA_KV_HEADS) * SWA_BLOCK
    return pl.pallas_call(
        functools.partial(_swa_kernel, n_ctx=n_ctx, n_lat=S - n_ctx, layer=l),
        out_shape=jax.ShapeDtypeStruct((B, S, 512), BF16),
        grid=(B, SWA_KV_HEADS),
        in_specs=[pl.BlockSpec(memory_space=pltpu.SMEM),
                  pl.BlockSpec((1, S, 256), lambda b, j: (b, 0, j)),
                  pl.BlockSpec((1, S, LANES), lambda b, j: (b, 0, 0)),
                  pl.BlockSpec((1, S, LANES), lambda b, j: (b, 0, 0)),
                  pl.BlockSpec((1, 8, nq), lambda b, j: (l * SWA_KV_HEADS + j, 0, 0)),
                  pl.BlockSpec((1, 8, nq), lambda b, j: (l * SWA_KV_HEADS + j, 0, 0)),
                  pl.BlockSpec((1, SWA_DIM, SWA_BLOCK), lambda b, j: (l, 0, 0)),
                  pl.BlockSpec(rope_t.shape, lambda b, j: (0, 0, 0, 0))],
        out_specs=pl.BlockSpec((1, S, 256), lambda b, j: (b, 0, j)),
        scratch_shapes=[pltpu.VMEM((S // SWA_BLOCK, SWA_VT_ROWS, SWA_BLOCK), BF16),
                        pltpu.VMEM((3, 3 * SWA_BLOCK, nq), F32)],
        compiler_params=_cparams(2), name="swa_attn",
    )(swa_range, qb, kb, vb, sinkrow, shiftrow, g_sqt, rope_t)


def _gelu(y):
    return 0.5 * y * (1.0 + jnp.tanh(math.sqrt(2.0 / math.pi) * (y + 0.044715 * (y * y * y))))


def _s5_kernel(u_ref, wst_ref, wloc_ref, wcar_ref, at_ref, o_ref,
               ut_ref, utc_ref, yt_ref, ytc_ref, ere_ref, eim_ref, hfr_ref, hfi_ref, hbr_ref, hbi_ref,
               *, nb, n_lat_chunks, n_ctx_chunks):
    T, GP = S5_CHUNK, S5_GROUP
    n_chunks = n_lat_chunks + n_ctx_chunks
    n_lat = n_lat_chunks * T
    cw = nb * n_ctx_chunks
    tn_dims = (((0,), (0,)), ((), ()))
    nt_dims = (((1,), (1,)), ((), ()))
    gpl = LANES // GP

    for b in range(nb):
        for s in range(T):
            xs = u_ref[b, pl.ds(s, n_lat_chunks, stride=T), :]
            ut_ref[b, :, s] = xs.T.reshape(gpl, GP, n_lat_chunks).astype(BF16)
    zpad = jnp.zeros((LANES - cw, LANES), F32)
    for s in range(T):
        xs = jnp.concatenate([u_ref[b, pl.ds(n_lat + s, n_ctx_chunks, stride=T), :] for b in range(nb)]
                             + [zpad], axis=0)
        utc_ref[:, s] = xs.T.reshape(gpl, GP, LANES).astype(BF16)

    lane = lax.broadcasted_iota(jnp.int32, (2 * nb, LANES), 1)
    fwd = lane < S5_STATE
    lane_l = lax.broadcasted_iota(jnp.int32, (n_lat_chunks, LANES), 1) < S5_STATE
    lane_c = lax.broadcasted_iota(jnp.int32, (LANES, LANES), 1) < S5_STATE

    slab = 2 * nb
    lat_rows = lambda gb: pl.ds(gb, n_lat_chunks, stride=slab)
    ctx_rows = lambda gb: pl.ds(n_lat_chunks * slab + gb, n_ctx_chunks, stride=slab)

    def pair(gp, carry):
        for gl in range(2):
            g = gp * 2 + gl
            wst = wst_ref[g]
            for b in range(nb):
                e = lax.dot_general(ut_ref[b, g].reshape(T * GP, n_lat_chunks), wst, tn_dims,
                                    preferred_element_type=F32)
                ere_ref[lat_rows(gl * nb + b), :] = e[:, :LANES]
                eim_ref[lat_rows(gl * nb + b), :] = e[:, LANES:]
            ec = lax.dot_general(utc_ref[g].reshape(T * GP, LANES), wst, tn_dims, preferred_element_type=F32)
            for b in range(nb):
                ere_ref[ctx_rows(gl * nb + b), :] = ec[b * n_ctx_chunks:(b + 1) * n_ctx_chunks, :LANES]
                eim_ref[ctx_rows(gl * nb + b), :] = ec[b * n_ctx_chunks:(b + 1) * n_ctx_chunks, LANES:]

        a_re = jnp.concatenate([jnp.broadcast_to(at_ref[gp * 2 + gl, 0:1, :], (nb, LANES)) for gl in range(2)], 0)
        a_im = jnp.concatenate([jnp.broadcast_to(at_ref[gp * 2 + gl, 1:2, :], (nb, LANES)) for gl in range(2)], 0)

        def step(i, hc):
            h_re, h_im = hc
            cf = jnp.where(i < n_ctx_chunks, n_lat_chunks + i, i - n_ctx_chunks)
            cb = n_chunks - 1 - i
            sf = pl.ds(pl.multiple_of(cf * slab, slab), slab)
            sb = pl.ds(pl.multiple_of(cb * slab, slab), slab)
            hfr_ref[sf, :] = h_re
            hfi_ref[sf, :] = h_im
            hbr_ref[sb, :] = h_re
            hbi_ref[sb, :] = h_im
            e_re = jnp.where(fwd, ere_ref[sf, :], ere_ref[sb, :])
            e_im = jnp.where(fwd, eim_ref[sf, :], eim_ref[sb, :])
            return (a_re * h_re - a_im * h_im + e_re, a_re * h_im + a_im * h_re + e_im)

        zero = jnp.zeros((2 * nb, LANES), F32)
        lax.fori_loop(0, n_chunks, step, (zero, zero), unroll=8)

        for gl in range(2):
            g = gp * 2 + gl
            wloc, wcar = wloc_ref[g], wcar_ref[g]
            for b in range(nb):
                rows = lat_rows(gl * nb + b)
                h_cat = jnp.concatenate([jnp.where(lane_l, hfr_ref[rows, :], hbr_ref[rows, :]),
                                         jnp.where(lane_l, hfi_ref[rows, :], hbi_ref[rows, :])],
                                        axis=-1).astype(BF16)
                yt = (jnp.dot(wloc, ut_ref[b, g].reshape(T * GP, n_lat_chunks), preferred_element_type=F32)
                      + lax.dot_general(wcar, h_cat, nt_dims, preferred_element_type=F32))
                yt_ref[b, g] = _gelu(yt).astype(BF16).reshape(T, GP, n_lat_chunks)
            crow = lambda ref: jnp.concatenate(
                [ref[ctx_rows(gl * nb + b), :] for b in range(nb)]
                + [jnp.zeros((LANES - cw, LANES), F32)], axis=0)
            h_cat = jnp.concatenate([jnp.where(lane_c, crow(hfr_ref), crow(hbr_ref)),
                                     jnp.where(lane_c, crow(hfi_ref), crow(hbi_ref))], axis=-1).astype(BF16)
            ytc = (jnp.dot(wloc, utc_ref[g].reshape(T * GP, LANES), preferred_element_type=F32)
                   + lax.dot_general(wcar, h_cat, nt_dims, preferred_element_type=F32))
            ytc_ref[g] = _gelu(ytc).astype(BF16).reshape(T, GP, LANES)
        return carry

    lax.fori_loop(0, gpl // 2, pair, 0)

    for b in range(nb):
        for t in range(T):
            z = yt_ref[b, :, t].astype(F32).reshape(LANES, n_lat_chunks)
            o_ref[b, pl.ds(t, n_lat_chunks, stride=T), :] = z.T
    for t in range(T):
        z = ytc_ref[:, t].astype(F32).reshape(LANES, LANES).T
        for b in range(nb):
            o_ref[b, pl.ds(n_lat + t, n_ctx_chunks, stride=T), :] = z[b * n_ctx_chunks:(b + 1) * n_ctx_chunks]


def _s5_call(l, u, P, n_lat):
    B, S, W = u.shape
    nb = 4 if B % 4 == 0 else B
    T, GP = S5_CHUNK, S5_GROUP
    nlc, ncc = n_lat // T, (S - n_lat) // T
    gpl = LANES // GP
    nblk = W // LANES
    big = lambda n: pl.BlockSpec((nb, S, LANES), lambda j, hb: (hb, 0, j), pipeline_mode=pl.Buffered(n))
    wspec = lambda: pl.BlockSpec((gpl, T * GP, T * GP), lambda j, hb: (l * nblk + j, 0, 0))
    rows = 2 * nb * (nlc + ncc)
    return pl.pallas_call(
        functools.partial(_s5_kernel, nb=nb, n_lat_chunks=nlc, n_ctx_chunks=ncc),
        out_shape=jax.ShapeDtypeStruct((B, S, W), F32),
        grid=(nblk, B // nb),
        in_specs=[big(2), wspec(), wspec(), wspec(),
                  pl.BlockSpec((gpl, 2, LANES), lambda j, hb: (l * nblk + j, 0, 0))],
        out_specs=big(2),
        scratch_shapes=[pltpu.VMEM((nb, gpl, T, GP, nlc), BF16), pltpu.VMEM((gpl, T, GP, LANES), BF16),
                        pltpu.VMEM((nb, gpl, T, GP, nlc), BF16), pltpu.VMEM((gpl, T, GP, LANES), BF16)]
                       + [pltpu.VMEM((rows, LANES), F32)] * 6,
        compiler_params=pltpu.CompilerParams(dimension_semantics=("arbitrary",) * 2,
                                             vmem_limit_bytes=S5_VMEM_LIMIT), name="s5",
    )(u, P["s5_wst"], P["s5_wloc_t"], P["s5_wcar_t"], P["s5_at"])


def _out_kernel(x_ref, c_ref, mod_ref, oa_ref, oac_ref, ob_ref, gy_ref, gt_ref, wglu_ref, wout_ref,
                xo_ref, co_ref, *, n_lat_tiles):
    t = pl.program_id(1)
    cb = OUT_COLS
    nb, rows = x_ref.shape[0], x_ref.shape[1]
    stack = lambda ref: jnp.concatenate([ref[i] for i in range(nb)], axis=0)
    gyb = stack(gy_ref).astype(BF16)
    g = stack(gt_ref).astype(F32)
    oa = jnp.where(t == n_lat_tiles, stack(oac_ref), stack(oa_ref))
    m_a = (oa.astype(F32) * g[:, 0:512]).astype(BF16)
    m_b = (stack(ob_ref).astype(F32) * g[:, 512:1024]).astype(BF16)
    oc = []
    for j in range(512 // cb):
        za = jnp.dot(gyb, wglu_ref[0, :, j * cb:(j + 1) * cb], preferred_element_type=F32)
        zb = jnp.dot(gyb, wglu_ref[0, :, 512 + j * cb:512 + (j + 1) * cb], preferred_element_type=F32)
        oc.append((za * _sigmoid(zb) * g[:, 1024 + j * cb:1024 + (j + 1) * cb]).astype(BF16))
    m_c = jnp.concatenate(oc, axis=-1)
    gate = jnp.concatenate([jnp.broadcast_to(mod_ref[0, i][2:3], (rows, x_ref.shape[2])) for i in range(nb)],
                           axis=0)
    resid = jnp.where(t == n_lat_tiles, stack(c_ref), stack(x_ref))
    new = []
    for j in range(resid.shape[1] // cb):
        cols = slice(j * cb, (j + 1) * cb)
        upd = (jnp.dot(m_a, wout_ref[0, 0:512, cols], preferred_element_type=F32)
               + jnp.dot(m_b, wout_ref[0, 512:1024, cols], preferred_element_type=F32)
               + jnp.dot(m_c, wout_ref[0, 1024:1536, cols], preferred_element_type=F32))
        new.append(resid[:, cols] + gate[:, cols] * upd)

    @pl.when(t == n_lat_tiles)
    def _():
        for j, v in enumerate(new):
            for i in range(nb):
                co_ref[i, :, j * cb:(j + 1) * cb] = v[i * rows:(i + 1) * rows]

    @pl.when(t < n_lat_tiles)
    def _():
        for j, v in enumerate(new):
            for i in range(nb):
                xo_ref[i, :, j * cb:(j + 1) * cb] = v[i * rows:(i + 1) * rows]


def _out_call(l, x, ctx, mod, oa, oa_ctx, ob, gy, gt, P):
    B, L, D = x.shape
    C = ctx.shape[1]
    S = C + L
    nl = L // TOK
    nb = PROJ_NB
    xmap = lambda b, t: (b, jnp.minimum(t, nl - 1), 0)
    cmap = lambda b, t: (b, 0, 0)
    tmap = lambda b, t: (b, t, 0)
    return pl.pallas_call(
        functools.partial(_out_kernel, n_lat_tiles=nl),
        out_shape=[jax.ShapeDtypeStruct(x.shape, F32), jax.ShapeDtypeStruct(ctx.shape, F32)],
        grid=(B // nb, S // TOK),
        in_specs=[pl.BlockSpec((nb, TOK, D), xmap),
                  pl.BlockSpec((nb, TOK, D), cmap),
                  pl.BlockSpec((1, nb, 3, D), lambda b, t: (l, jnp.where(t == nl, B // nb, b), 0, 0)),
                  pl.BlockSpec((nb, TOK, 512), xmap),
                  pl.BlockSpec((nb, TOK, 512), cmap),
                  pl.BlockSpec((nb, TOK, 512), tmap),
                  pl.BlockSpec((nb, TOK, 512), tmap),
                  pl.BlockSpec((nb, TOK, 1536), tmap),
                  pl.BlockSpec((1, 512, 1024), lambda b, t: (l, 0, 0)),
                  pl.BlockSpec((1, 1536, D), lambda b, t: (l, 0, 0))],
        out_specs=[pl.BlockSpec((nb, TOK, D), xmap), pl.BlockSpec((nb, TOK, D), cmap)],
        compiler_params=_cparams(2), name="out",
    )(x, ctx, mod, oa, oa_ctx, ob, gy, gt, P["w_glu"], P["w_out"])


def _rope_angles(n_lat, n_ctx, rot_dim):
    rows = n_lat // GRID_W
    r_idx, c_idx = jnp.meshgrid(jnp.arange(rows), jnp.arange(GRID_W), indexing="ij")
    r_idx, c_idx = r_idx.reshape(-1), c_idx.reshape(-1)
    n_freq = rot_dim // 4
    freqs = ROPE_BASE ** (-jnp.arange(n_freq, dtype=F32) / n_freq)
    ang = jnp.concatenate([r_idx.astype(F32)[:, None] * freqs,
                           c_idx.astype(F32)[:, None] * freqs], axis=-1)
    return jnp.concatenate([ang, jnp.zeros((n_ctx, rot_dim // 2), F32)], axis=0)


def _rope_tables_t(n_lat, n_ctx, rot_dim):
    ang = _rope_angles(n_lat, n_ctx, rot_dim).T
    return jnp.stack([jnp.cos(ang), jnp.sin(ang)])


def _rope_tables(n_lat, n_ctx, rot_dim, lead, reps):
    ang = _rope_angles(n_lat, n_ctx, rot_dim)
    cos, sin, zero = jnp.cos(ang), jnp.sin(ang), jnp.zeros_like(ang)
    n = ang.shape[0]
    tail = LANES // reps - lead - rot_dim

    def pack(x1, x2, fill):
        unit = [jnp.full((n, lead), fill, F32), x1, x2, jnp.full((n, tail), fill, F32)]
        return jnp.concatenate(unit * reps, axis=-1)

    return jnp.stack([pack(cos, cos, 1.0), pack(zero, sin, 0.0), pack(-sin, zero, 0.0)])


def _prep_params(norm_g, w_in, w_out, mla_g_cq, mla_g_ckv, mla_w_uq, mla_w_ukv, mla_g_qn, mla_g_kn,
                 swa_g_qn, swa_g_kn, swa_sink, s5_a_re, s5_a_im, s5_log_dt, s5_b_re, s5_b_im,
                 s5_c_re, s5_c_im, s5_d, s5_w_glu):
    depth, D, _ = w_in.shape
    o_cq, o_ckv, o_kr, o_gm, o_sq, o_sk, o_sv, o_gs, o_u, o_g5, o_end = (
        0, 384, 640, 672, 1184, 1696, 1824, 1952, 2464, 2976, 3488)
    z = lambda n: jnp.zeros((depth, D, n), F32)
    w_in_p = (
        w_in[:, :, o_cq:o_kr].astype(BF16),
        jnp.concatenate([z(64), w_in[:, :, o_kr:o_gm], z(32)], axis=-1).astype(BF16),
        w_in[:, :, o_sq:o_gs].astype(BF16),
        w_in[:, :, o_u:o_g5].astype(BF16),
        jnp.concatenate([w_in[:, :, o_gm:o_sq], w_in[:, :, o_gs:o_u], w_in[:, :, o_g5:o_end]],
                        axis=-1).astype(BF16),
    )
    assert sum(w.shape[-1] for w in w_in_p) == C_END

    wq = mla_w_uq.reshape(depth, MLA_Q_RANK, MLA_HEADS, MLA_QK)
    wq = jnp.pad(wq, ((0, 0), (0, 0), (0, 0), (0, LANES - MLA_QK)))
    w_uq_p = wq.reshape(depth, MLA_Q_RANK, MLA_HEADS * LANES).astype(BF16)
    wkv = mla_w_ukv.reshape(depth, MLA_KV_RANK, MLA_HEADS, MLA_NOPE + MLA_V)
    wk = jnp.pad(wkv[..., :MLA_NOPE], ((0, 0), (0, 0), (0, 0), (0, LANES - MLA_NOPE)))
    w_ukv_p = jnp.concatenate([wk.reshape(depth, MLA_KV_RANK, MLA_HEADS * LANES),
                               wkv[..., MLA_NOPE:].reshape(depth, MLA_KV_RANK, MLA_HEADS * MLA_V)],
                              axis=-1).astype(BF16)

    pad_qk = lambda g: jnp.pad(g, ((0, 0), (0, LANES - MLA_QK)))[:, None, :]
    g_q = pad_qk(mla_g_qn * (MLA_QK ** -0.5 * LOG2E))
    g_qt = jnp.broadcast_to(jnp.swapaxes(g_q, 1, 2), (depth, LANES, LANES))
    g_k = pad_qk(mla_g_kn)
    mla_bound = 1.02 * MLA_QK * jnp.max(jnp.abs(g_q), axis=(1, 2)) * jnp.max(jnp.abs(g_k), axis=(1, 2))
    g_sqt = jnp.broadcast_to((swa_g_qn * (SWA_DIM ** -0.5 * LOG2E))[:, :, None], (depth, SWA_DIM, SWA_BLOCK))
    g_sk = jnp.tile(swa_g_kn, (1, 2))[:, None, :]
    per_kv = SWA_HEADS // SWA_KV_HEADS
    lanes_of = lambda v: jnp.broadcast_to(v.reshape(depth * SWA_KV_HEADS, 1, per_kv, 1),
                                          (depth * SWA_KV_HEADS, 8, per_kv, SWA_BLOCK)
                                          ).reshape(depth * SWA_KV_HEADS, 8, -1).astype(F32)
    sink2 = swa_sink * LOG2E
    sinkrow = lanes_of(sink2)
    swa_bound = 1.02 * SWA_DIM * jnp.max(jnp.abs(g_sqt), axis=(1, 2)) * jnp.max(jnp.abs(g_sk), axis=(1, 2))
    shift = jnp.maximum(swa_bound[:, None], sink2)
    shiftrow = lanes_of(shift)
    swa_range = (swa_bound + jnp.max(shift, axis=1)).astype(F32)

    T = S5_CHUNK
    A = lax.complex(s5_a_re, s5_a_im)
    dt = jnp.exp(s5_log_dt)[..., None]
    a_bar = jnp.exp(dt * A)
    b_bar = ((a_bar - 1.0) / A)[..., None] * lax.complex(s5_b_re, s5_b_im)
    c_mat = lax.complex(s5_c_re, s5_c_im)
    k_idx = jnp.arange(T + 1, dtype=F32)
    pw = jnp.exp(k_idx[None, None, None, :, None] * (dt * A)[:, :, :, None, :])
    hi = lax.Precision.HIGHEST
    tt = jnp.arange(T)
    GP, NG = S5_GROUP, depth * S5_GROUPS
    pw_f, pw_b = pw[:, 0], pw[:, 1]
    tap_f = jnp.einsum("lgpn,lgkn,lgnq->lgpkq", c_mat[:, 0], pw_f[:, :, T - 1 - tt], b_bar[:, 0],
                       precision=hi).real
    tap_b = jnp.einsum("lgpn,lgkn,lgnq->lgpkq", c_mat[:, 1], pw_b[:, :, tt], b_bar[:, 1],
                       precision=hi).real
    d_diag = s5_d.reshape(depth, S5_GROUPS, GP)[..., None] * jnp.eye(GP, dtype=F32)
    centre = tap_f[:, :, :, T - 1:] + tap_b[:, :, :, :1] + d_diag[:, :, :, None, :]
    krev = jnp.concatenate([tap_f[:, :, :, :T - 1], centre, tap_b[:, :, :, 1:]], axis=3)
    krev = krev.reshape(depth, S5_GROUPS, GP, (2 * T - 1) * GP)
    wloc_t = jnp.stack([krev[..., (T - 1 - t) * GP:(T - 1 - t) * GP + T * GP] for t in range(T)], axis=2)
    wloc_t = wloc_t.reshape(NG, T * GP, T * GP)
    b_t = jnp.swapaxes(b_bar, -1, -2)
    inc_f = pw_f[:, :, T - 1 - tt][:, :, :, None, :] * b_t[:, 0][:, :, None]
    inc_b = pw_b[:, :, tt][:, :, :, None, :] * b_t[:, 1][:, :, None]
    wst = jnp.concatenate([inc_f.real, inc_b.real, inc_f.imag, inc_b.imag], axis=-1)
    wst = wst.reshape(NG, T * GP, 4 * S5_STATE)
    ro_f = c_mat[:, 0][:, :, None] * pw_f[:, :, tt + 1][:, :, :, None, :]
    ro_b = c_mat[:, 1][:, :, None] * pw_b[:, :, T - tt][:, :, :, None, :]
    wcar_t = jnp.concatenate([ro_f.real, ro_b.real, -ro_f.imag, -ro_b.imag], axis=-1)
    wcar_t = wcar_t.reshape(NG, T * GP, 4 * S5_STATE)
    a_t = pw[:, :, :, T]
    at = jnp.stack([jnp.concatenate([a_t[:, 0].real, a_t[:, 1].real], axis=-1),
                    jnp.concatenate([a_t[:, 0].imag, a_t[:, 1].imag], axis=-1)], axis=2)
    at = at.reshape(NG, 2, LANES).astype(F32)

    return dict(norm_g=norm_g[:, None, :], w_in=w_in_p, g_cq=mla_g_cq[:, None, :], w_uq=w_uq_p, g_qt=g_qt, mla_bound=mla_bound.astype(F32),
                g_ckv=mla_g_ckv[:, None, :], w_ukv=w_ukv_p, g_k=g_k, g_sqt=g_sqt, g_sk=g_sk,
                sinkrow=sinkrow, shiftrow=shiftrow, swa_range=swa_range, s5_wloc_t=wloc_t.astype(BF16), s5_wst=wst.astype(BF16),
                s5_wcar_t=wcar_t.astype(BF16), s5_at=at, w_glu=s5_w_glu.astype(BF16),
                w_out=w_out.astype(BF16))


def kernel(x, c, ctx, c_ctx, norm_g, w_ada, b_ada, w_in, w_out, mla_g_cq, mla_g_ckv, mla_w_uq, mla_w_ukv, mla_g_qn, mla_g_kn, swa_g_qn, swa_g_kn, swa_sink, s5_a_re, s5_a_im, s5_log_dt, s5_b_re, s5_b_im, s5_c_re, s5_c_im, s5_d, s5_w_glu):
    B, L, D = x.shape
    C = ctx.shape[1]
    S = C + L
    depth = w_in.shape[0]
    assert B + PROJ_NB <= MOD_ROWS and C == TOK and L % (2 * TOK) == 0 and L % GRID_W == 0 and L >= 3 * SWA_BLOCK

    P = _prep_params(norm_g, w_in, w_out, mla_g_cq, mla_g_ckv, mla_w_uq, mla_w_ukv, mla_g_qn, mla_g_kn,
                     swa_g_qn, swa_g_kn, swa_sink, s5_a_re, s5_a_im, s5_log_dt, s5_b_re, s5_b_im,
                     s5_c_re, s5_c_im, s5_d, s5_w_glu)
    rope_mla = _rope_tables(L, C, MLA_ROPE, MLA_NOPE, 1)
    rope_swa = _rope_tables(L, C, SWA_DIM, 0, 2)
    rope_mla_t = _rope_tables_t(L, C, MLA_ROPE)
    rope_swa_t = jnp.transpose(_rope_tables_t(L, C, SWA_DIM).reshape(2, SWA_DIM // 2, S // SWA_BLOCK, SWA_BLOCK),
                               (2, 0, 1, 3))

    cc = jnp.concatenate([c, jnp.tile(c_ctx[None, :], (PROJ_NB, 1)),
                          jnp.zeros((MOD_ROWS - B - PROJ_NB, D), F32)], axis=0)
    mod = _ada_call(cc, w_ada, b_ada).reshape(depth, MOD_ROWS, 3, D)

    for l in range(depth):
        qa, ka, va, qb, kb, vb, u, gt = _proj_call(l, x, ctx, mod, P, rope_mla, rope_swa)
        oa, oa_ctx = _mla_call(l, qa, ka, va, P["g_qt"], rope_mla_t, P["mla_bound"], L)
        ob = _swa_call(l, qb, kb, vb, P["sinkrow"], P["shiftrow"], P["swa_range"], P["g_sqt"], rope_swa_t, C)
        gy = _s5_call(l, u, P, L)
        x, ctx = _out_call(l, x, ctx, mod, oa, oa_ctx, ob, gy, gt, P)
    return x
```

```python
import functools
import math

import jax
import jax.numpy as jnp
from jax import lax
from jax.experimental import pallas as pl
from jax.experimental.pallas import tpu as pltpu

F32 = jnp.float32
BF16 = jnp.bfloat16

GRID_W = 64
EPS = 1e-6
ROPE_BASE = 10000.0
NEG = -1e30
LOG2E = math.log2(math.e)

MLA_HEADS = 8
MLA_NOPE = 64
MLA_ROPE = 32
MLA_V = 64
MLA_QK = MLA_NOPE + MLA_ROPE
MLA_Q_RANK = 384
MLA_KV_RANK = 256

SWA_HEADS = 8
SWA_KV_HEADS = 2
SWA_DIM = 64
SWA_WINDOW = 128

S5_GROUP = 16
S5_GROUPS = 32
S5_STATE = 64
S5_CHUNK = 16

LANES = 128
TOK = 256
KV_CHUNK = 256
SWA_BLOCK = 128
MOD_ROWS = 16
PROJ_NB = 2
SWA_UNROLL = 32
OUT_COLS = 256
PROJ_SKEW = 1

C_CQ = 0
C_CKV = 384
C_KR = 640
C_SQ = 768
C_SK = 1280
C_SV = 1408
C_U = 1536
C_GATE = 2048
C_END = 3584

VMEM_LIMIT = 56 * 1024 * 1024
S5_VMEM_LIMIT = 60 * 1024 * 1024


def _cparams(n_axes):
    return pltpu.CompilerParams(dimension_semantics=("arbitrary",) * n_axes,
                                vmem_limit_bytes=VMEM_LIMIT)


def _ada_kernel(c_ref, w_ref, b_ref, o_ref):
    cc = c_ref[...]
    s = cc * jax.nn.sigmoid(cc)
    o_ref[0] = jnp.dot(s, w_ref[0], preferred_element_type=F32,
                       precision=lax.Precision.HIGHEST) + b_ref[0]


def _ada_call(cc, w_ada, b_ada):
    depth, d, n3 = w_ada.shape
    tn = 768
    return pl.pallas_call(
        _ada_kernel,
        out_shape=jax.ShapeDtypeStruct((depth, MOD_ROWS, n3), F32),
        grid=(depth, n3 // tn),
        in_specs=[pl.BlockSpec((MOD_ROWS, d), lambda l, j: (0, 0)),
                  pl.BlockSpec((1, d, tn), lambda l, j: (l, 0, j)),
                  pl.BlockSpec((1, 1, tn), lambda l, j: (l, 0, j))],
        out_specs=pl.BlockSpec((1, MOD_ROWS, tn), lambda l, j: (l, 0, j)),
        compiler_params=_cparams(2),
        name="ada",
    )(cc, w_ada, b_ada.reshape(depth, 1, n3))


def _sigmoid(v):
    return 0.5 * jnp.tanh(0.5 * v) + 0.5


def _norm_rope(slots, lo_masks, dim, gain, tab_ref, half):
    sq = [s * s for s in slots]
    sums = []
    for s2, lo in zip(sq, lo_masks):
        if lo is None:
            sums.append((jnp.sum(s2, axis=-1, keepdims=True),))
        else:
            sums.append((jnp.sum(jnp.where(lo, s2, 0.0), axis=-1, keepdims=True),
                         jnp.sum(jnp.where(lo, 0.0, s2), axis=-1, keepdims=True)))
    ys = []
    for s, ss, lo in zip(slots, sums, lo_masks):
        rs = [lax.rsqrt(v * (1.0 / dim) + EPS) for v in ss]
        r = rs[0] if len(rs) == 1 else jnp.where(lo, rs[0], rs[1])
        ys.append(s * r * gain)
    up = [pltpu.roll(y, half, 1) for y in ys]
    dn = [pltpu.roll(y, LANES - half, 1) for y in ys]
    return [(y * tab_ref[0] + u * tab_ref[1] + d * tab_ref[2]).astype(BF16) for y, u, d in zip(ys, up, dn)]


N_PROJ_DATA = 3
N_PROJ_WEIGHTS = 14


def _proj_kernel(*refs, n_lat_tiles):
    data = refs[:N_PROJ_DATA]
    shared = refs[N_PROJ_DATA:N_PROJ_DATA + N_PROJ_WEIGHTS]
    outs = refs[N_PROJ_DATA + N_PROJ_WEIGHTS:]
    streams = []
    for i in range(data[0].shape[0]):
        one = pl.ds(i, 1)
        streams.append(_proj_tile(data[0].at[one], data[1].at[one], data[2].at[:, one], *shared,
                                  *[o.at[one] for o in outs], n_lat_tiles=n_lat_tiles))
    pending = list(enumerate(streams))
    step = 0
    while pending:
        for item in list(pending):
            if step >= item[0] * PROJ_SKEW and next(item[1], "done") == "done":
                pending.remove(item)
        step += 1


def _proj_tile(x_ref, c_ref, mod_ref, ng_ref, wa_ref, wkr_ref, wb_ref, wu_ref, wg_ref, gcq_ref, wuq_ref,
               gckv_ref, wukv_ref, gk_ref, gsk_ref, rm_ref, rs_ref,
               qa_ref, ka_ref, va_ref, qb_ref, kb_ref, vb_ref, u_ref, gt_ref, *, n_lat_tiles):
    t = pl.program_id(1)
    x = jnp.where(t == n_lat_tiles, c_ref[0], x_ref[0])
    mod = mod_ref[0, 0]
    y = x * lax.rsqrt(jnp.mean(x * x, axis=-1, keepdims=True) + EPS) * ng_ref[0]
    xn = (y * (1.0 + mod[1:2]) + mod[0:1]).astype(BF16)

    pieces = ((C_CQ, wa_ref), (C_KR, wkr_ref), (C_SQ, wb_ref), (C_U, wu_ref), (C_GATE, wg_ref))

    def seg(a, b):
        start, ref = [p for p in pieces if p[0] <= a][-1]
        return jnp.dot(xn, ref[0, :, a - start:b - start], preferred_element_type=F32)

    lane = lax.broadcasted_iota(jnp.int32, (TOK, LANES), 1)
    lo = lane < 64

    yield
    cq = seg(C_CQ, C_CKV)
    ckv = seg(C_CKV, C_KR)
    kr = seg(C_KR, C_SQ)
    yield
    cqn = (cq * lax.rsqrt(jnp.mean(cq * cq, axis=-1, keepdims=True) + EPS) * gcq_ref[0]).astype(BF16)
    qf = jnp.dot(cqn, wuq_ref[0], preferred_element_type=F32)
    ckvn = (ckv * lax.rsqrt(jnp.mean(ckv * ckv, axis=-1, keepdims=True) + EPS) * gckv_ref[0]).astype(BF16)
    kvf = jnp.dot(ckvn, wukv_ref[0], preferred_element_type=F32)
    sq = seg(C_SQ, C_SK)
    yield
    for h in range(MLA_HEADS):
        qa_ref[0, h] = qf[:, h * LANES:(h + 1) * LANES].astype(BF16)

    sk = seg(C_SK, C_SV)
    vb_ref[0] = seg(C_SV, C_U).astype(BF16)
    yield
    slots = [kvf[:, h * LANES:(h + 1) * LANES] + kr for h in range(MLA_HEADS)]
    for h, o in enumerate(_norm_rope(slots, [None] * MLA_HEADS, MLA_QK, gk_ref[0], rm_ref, MLA_ROPE // 2)):
        ka_ref[0, h] = jnp.where(lane == MLA_QK, jnp.ones_like(o), o)
    va_ref[0] = kvf[:, MLA_HEADS * LANES:].astype(BF16)
    u_ref[0] = seg(C_U, C_GATE)
    yield
    qb_ref[0] = sq.astype(BF16)
    kb_ref[0] = _norm_rope([sk], [lo], SWA_DIM, gsk_ref[0], rs_ref, SWA_DIM // 2)[0]
    g = seg(C_GATE, C_END)
    yield
    gt_ref[0] = (g * _sigmoid(g)).astype(BF16)


def _proj_call(l, x, ctx, mod, P, rope_mla, rope_swa):
    B, L, D = x.shape
    C = ctx.shape[1]
    S = C + L
    nt = S // TOK

    def wspec(arr):
        shp = arr.shape
        return pl.BlockSpec((1,) + shp[1:], lambda b, t: (l,) + (0,) * (len(shp) - 1))

    weights = [P["norm_g"], *P["w_in"], P["g_cq"], P["w_uq"],
               P["g_ckv"], P["w_ukv"], P["g_k"], P["g_sk"]]
    nl = L // TOK
    nb = PROJ_NB
    assert len(weights) + 2 == N_PROJ_WEIGHTS and B % nb == 0
    in_specs = ([pl.BlockSpec((nb, TOK, D), lambda b, t: (b, jnp.minimum(t, nl - 1), 0)),
                 pl.BlockSpec((nb, TOK, D), lambda b, t: (b, 0, 0)),
                 pl.BlockSpec((1, nb, 3, D), lambda b, t: (l, jnp.where(t == nl, B // nb, b), 0, 0))]
                + [wspec(w) for w in weights]
                + [pl.BlockSpec((3, TOK, LANES), lambda b, t: (0, t, 0)),
                   pl.BlockSpec((3, TOK, LANES), lambda b, t: (0, t, 0))])
    out_shape = [jax.ShapeDtypeStruct((B, MLA_HEADS, S, LANES), BF16),
                 jax.ShapeDtypeStruct((B, MLA_HEADS, S, LANES), BF16),
                 jax.ShapeDtypeStruct((B, S, 512), BF16),
                 jax.ShapeDtypeStruct((B, S, 512), BF16),
                 jax.ShapeDtypeStruct((B, S, LANES), BF16),
                 jax.ShapeDtypeStruct((B, S, LANES), BF16),
                 jax.ShapeDtypeStruct((B, S, 512), F32),
                 jax.ShapeDtypeStruct((B, S, 1536), BF16)]
    out_specs = [pl.BlockSpec((nb, MLA_HEADS, TOK, LANES), lambda b, t: (b, 0, t, 0)),
                 pl.BlockSpec((nb, MLA_HEADS, TOK, LANES), lambda b, t: (b, 0, t, 0))]
    out_specs += [pl.BlockSpec((nb, TOK, s.shape[2]), lambda b, t: (b, t, 0)) for s in out_shape[2:]]
    return pl.pallas_call(
        functools.partial(_proj_kernel, n_lat_tiles=nl), out_shape=out_shape, grid=(B // nb, nt),
        in_specs=in_specs, out_specs=out_specs,
        compiler_params=_cparams(2), name="proj",
    )(x, ctx, mod, *weights, rope_mla, rope_swa)


MLA_SUBTILES = 4
MLA_VT_ROWS = 80


MLA_SAFE_RANGE = 60.0


def _mla_kernel(rb_ref, q_ref, k_ref, v_ref, gq_ref, rt_ref, o_ref, vt_ref, *, n_chunks, tq, layer):
    ones = jnp.ones((MLA_VT_ROWS - MLA_V, KV_CHUNK), BF16)

    @pl.when(pl.program_id(2) == 0)
    def _():
        for c in range(n_chunks):
            vt = v_ref[0, c].astype(F32).T.astype(BF16)
            vt_ref[0, c] = jnp.concatenate([vt[:MLA_V], ones], axis=0)
            vt_ref[1, c] = jnp.concatenate([vt[MLA_V:], ones], axis=0)

    bound = rb_ref[layer]
    half = MLA_ROPE // 2

    def sweep(fixed_shift):
        for r0 in range(0, q_ref.shape[2], tq):
            rows = slice(r0, r0 + tq)
            cos, sin = rt_ref[0, :, rows], rt_ref[1, :, rows]
            gain = jnp.concatenate([gq_ref[0]] * (tq // LANES), axis=1)
            pad = jnp.zeros((LANES - MLA_QK, tq), F32)
            if fixed_shift:
                row = lax.broadcasted_iota(jnp.int32, pad.shape, 0)
                pad = jnp.where(row == 0, -bound, pad)

            def qmat(e):
                qt = q_ref[0, e, rows, :].astype(F32).T
                r = lax.rsqrt(jnp.sum(qt * qt, axis=0, keepdims=True) * (1.0 / MLA_QK) + EPS)
                y = qt * r * gain
                x1, x2 = y[MLA_NOPE:MLA_NOPE + half], y[MLA_NOPE + half:MLA_QK]
                return jnp.concatenate([y[:MLA_NOPE], x1 * cos - x2 * sin, x2 * cos + x1 * sin, pad],
                                       axis=0).astype(BF16)

            qts = [qmat(e) for e in range(2)]
            score = lambda c, e: jnp.dot(k_ref[0, e, c], qts[e], preferred_element_type=F32)
            ms = [None, None]
            accs = [None, None]
            sts = [score(0, e) for e in range(2)]
            for c in range(n_chunks):
                nxt = [None, None]
                for e in range(2):
                    if c + 1 < n_chunks:
                        nxt[e] = score(c + 1, e)
                    if fixed_shift:
                        pv = jnp.dot(vt_ref[e, c], jnp.exp2(sts[e]).astype(BF16), preferred_element_type=F32)
                        accs[e] = pv if c == 0 else accs[e] + pv
                    else:
                        mc = jnp.max(sts[e], axis=0, keepdims=True)
                        m_new = mc if c == 0 else jnp.maximum(ms[e], mc)
                        pt = jnp.exp2(sts[e] - m_new).astype(BF16)
                        pv = jnp.dot(vt_ref[e, c], pt, preferred_element_type=F32)
                        accs[e] = pv if c == 0 else accs[e] * jnp.exp2(ms[e] - m_new) + pv
                        ms[e] = m_new
                sts = nxt
            ot = jnp.concatenate([accs[e][:MLA_V] / accs[e][MLA_V:MLA_V + 1] for e in range(2)], axis=0)
            o_ref[0, rows, :] = ot.T.astype(BF16)

    @pl.when(bound < MLA_SAFE_RANGE)
    def _():
        sweep(True)

    @pl.when(jnp.logical_not(bound < MLA_SAFE_RANGE))
    def _():
        sweep(False)


def _mla_call(l, qa, ka, va, g_qt, rope_t, r_bound, n_lat):
    B, H, S, _ = qa.shape
    half = MLA_ROPE // 2
    gspec = pl.BlockSpec((1, LANES, LANES), lambda b, p, t: (l, 0, 0))
    sspec = pl.BlockSpec(memory_space=pltpu.SMEM)
    nc = S // KV_CHUNK
    tq = 2 * TOK
    bq = MLA_SUBTILES * tq if n_lat % (MLA_SUBTILES * tq) == 0 else tq
    k5 = ka.reshape(B, H, nc, KV_CHUNK, LANES)
    v4 = va.reshape(B, nc, KV_CHUNK, 512)
    o_lat = pl.pallas_call(
        functools.partial(_mla_kernel, n_chunks=nc, tq=tq, layer=l),
        out_shape=jax.ShapeDtypeStruct((B, n_lat, 512), BF16),
        grid=(B, H // 2, n_lat // bq),
        in_specs=[sspec,
                  pl.BlockSpec((1, 2, bq, LANES), lambda b, p, t: (b, p, t, 0)),
                  pl.BlockSpec((1, 2, nc, KV_CHUNK, LANES), lambda b, p, t: (b, p, 0, 0, 0)),
                  pl.BlockSpec((1, nc, KV_CHUNK, LANES), lambda b, p, t: (b, 0, 0, p)),
                  gspec,
                  pl.BlockSpec((2, half, bq), lambda b, p, t: (0, 0, t))],
        out_specs=pl.BlockSpec((1, bq, LANES), lambda b, p, t: (b, t, p)),
        scratch_shapes=[pltpu.VMEM((2, nc, MLA_VT_ROWS, KV_CHUNK), BF16)],
        compiler_params=_cparams(3), name="mla_attn",
    )(r_bound, qa, k5, v4, g_qt, rope_t)
    cblk = n_lat // KV_CHUNK
    o_ctx = pl.pallas_call(
        functools.partial(_mla_kernel, n_chunks=1, tq=TOK, layer=l),
        out_shape=jax.ShapeDtypeStruct((B, S - n_lat, 512), BF16),
        grid=(B, H // 2, 1),
        in_specs=[sspec,
                  pl.BlockSpec((1, 2, TOK, LANES), lambda b, p, t: (b, p, cblk, 0)),
                  pl.BlockSpec((1, 2, 1, KV_CHUNK, LANES), lambda b, p, t: (b, p, cblk, 0, 0)),
                  pl.BlockSpec((1, 1, KV_CHUNK, LANES), lambda b, p, t: (b, cblk, 0, p)),
                  gspec,
                  pl.BlockSpec((2, half, TOK), lambda b, p, t: (0, 0, cblk))],
        out_specs=pl.BlockSpec((1, TOK, LANES), lambda b, p, t: (b, 0, p)),
        scratch_shapes=[pltpu.VMEM((2, 1, MLA_VT_ROWS, KV_CHUNK), BF16)],
        compiler_params=_cparams(3), name="mla_attn_ctx",
    )(r_bound, qa, k5, v4, g_qt, rope_t)
    return o_lat, o_ctx


SWA_SAFE_RANGE = 120.0
SWA_VT_ROWS = 80


def _swa_kernel(rng_ref, q_ref, k_ref, v_ref, sink_ref, shift_ref, gq_ref, rt_ref, bias_ref, o_ref, vt_ref,
                *, n_ctx, n_lat, layer):
    blk = SWA_BLOCK
    heads = SWA_HEADS // SWA_KV_HEADS
    win = 3 * blk
    n_blocks = n_lat // blk
    ctx_blk = n_blocks
    group = min(SWA_UNROLL, n_blocks)
    first = pl.program_id(1) == 0
    sink = sink_ref[0, 0:1, :]

    ones = jnp.ones((SWA_VT_ROWS - SWA_DIM, blk), BF16)
    for i in range((n_lat + n_ctx) // blk):
        vt = v_ref[0, i * blk:(i + 1) * blk, :].astype(F32).T
        vt_ref[i] = jnp.concatenate([jnp.where(first, vt[:SWA_DIM], vt[SWA_DIM:]).astype(BF16), ones], axis=0)
    kc = k_ref[0, n_lat:n_lat + n_ctx, :]
    vtc = jnp.concatenate([vt_ref[ctx_blk + i] for i in range(n_ctx // blk)], axis=1)

    gain = gq_ref[0]
    half = SWA_DIM // 2

    def qmat(n):
        qt = q_ref[0, pl.ds(pl.multiple_of(n * blk, blk), blk), :].astype(F32).T
        cos, sin = rt_ref[n, 0], rt_ref[n, 1]
        cols = []
        for h in range(heads):
            x = qt[h * SWA_DIM:(h + 1) * SWA_DIM]
            y = x * lax.rsqrt(jnp.sum(x * x, axis=0, keepdims=True) * (1.0 / SWA_DIM) + EPS) * gain
            x1, x2 = y[:half], y[half:]
            cols.append(jnp.concatenate([x1 * cos - x2 * sin, x2 * cos + x1 * sin], axis=0))
        w = jnp.concatenate(cols, axis=1).astype(BF16)
        z = jnp.zeros_like(w)
        return jnp.where(first, jnp.concatenate([w, z], axis=0), jnp.concatenate([z, w], axis=0))

    def finish(r0, m, acc):
        ot = acc[:SWA_DIM] / (acc[SWA_DIM:SWA_DIM + 1] + jnp.exp2(sink - m))
        o4 = jnp.concatenate([ot[:, h * blk:(h + 1) * blk] for h in range(heads)], axis=0)
        o_ref[0, pl.ds(r0, blk), :] = o4.T.astype(BF16)

    shift = shift_ref[0, 0:1, :]

    def row_max(fixed, *scores):
        if fixed:
            return shift
        m = sink
        for sc in scores:
            m = jnp.maximum(m, jnp.max(sc, axis=0, keepdims=True))
        return m

    def ctx_queries(fixed):
        for n in range(n_ctx // blk):
            r0 = n_lat + n * blk
            s_c = jnp.dot(kc, qmat(ctx_blk + n), preferred_element_type=F32)
            m = row_max(fixed, s_c)
            finish(r0, m, jnp.dot(vtc, jnp.exp2(s_c - m).astype(BF16), preferred_element_type=F32))

    def window_block(n):
        return jnp.clip(n - 1, 0, n_blocks - 3)

    def scores(n):
        wb = window_block(n)
        w = qmat(n)
        kw = k_ref[0, pl.ds(pl.multiple_of(wb * blk, blk), win), :]
        return (jnp.dot(kc, w, preferred_element_type=F32),
                jnp.dot(kw, w, preferred_element_type=F32) + bias_ref[n - wb])

    def latent_queries(fixed):
        def blocks(gi, carry):
            n0 = gi * group
            cur = scores(n0)
            for i in range(group):
                n = n0 + i
                nxt = scores(n + 1) if i + 1 < group else None
                s_c, s_w = cur
                m = row_max(fixed, s_c, s_w)
                wb = window_block(n)
                vtw = jnp.concatenate([vt_ref[wb + j] for j in range(3)], axis=1)
                acc = (jnp.dot(vtc, jnp.exp2(s_c - m).astype(BF16), preferred_element_type=F32)
                       + jnp.dot(vtw, jnp.exp2(s_w - m).astype(BF16), preferred_element_type=F32))
                finish(pl.multiple_of(n * blk, blk), m, acc)
                cur = nxt
            return carry

        lax.fori_loop(0, n_blocks // group, blocks, 0)

    assert n_blocks % group == 0
    safe = rng_ref[layer] < SWA_SAFE_RANGE

    @pl.when(safe)
    def _():
        ctx_queries(True)
        latent_queries(True)

    @pl.when(jnp.logical_not(safe))
    def _():
        ctx_queries(False)
        latent_queries(False)


def _swa_call(l, qb, kb, vb, sinkrow, shiftrow, swa_range, g_sqt, rope_t, n_ctx):
    B, S, _ = qb.shape
    nq = (SWA_HEADS // SWA_KV_HEADS) * SWA_BLOCK
    win = 3 * SWA_BLOCK
    rel = jnp.arange(win)[:, None] - (jnp.arange(nq)[None, :] % SWA_BLOCK)
    bias = jnp.stack([jnp.where(jnp.abs(rel - kind * SWA_BLOCK) <= SWA_WINDOW, 0.0, NEG) for kind in range(3)])
    return pl.pallas_call(
        functools.partial(_swa_kernel, n_ctx=n_ctx, n_lat=S - n_ctx, layer=l),
        out_shape=jax.ShapeDtypeStruct((B, S, 512), BF16),
        grid=(B, SWA_KV_HEADS),
        in_specs=[pl.BlockSpec(memory_space=pltpu.SMEM),
                  pl.BlockSpec((1, S, 256), lambda b, j: (b, 0, j)),
                  pl.BlockSpec((1, S, LANES), lambda b, j: (b, 0, 0)),
                  pl.BlockSpec((1, S, LANES), lambda b, j: (b, 0, 0)),
                  pl.BlockSpec((1, 8, nq), lambda b, j: (l * SWA_KV_HEADS + j, 0, 0)),
                  pl.BlockSpec((1, 8, nq), lambda b, j: (l * SWA_KV_HEADS + j, 0, 0)),
                  pl.BlockSpec((1, SWA_DIM, SWA_BLOCK), lambda b, j: (l, 0, 0)),
                  pl.BlockSpec(rope_t.shape, lambda b, j: (0, 0, 0, 0)),
                  pl.BlockSpec((3, win, nq), lambda b, j: (0, 0, 0))],
        out_specs=pl.BlockSpec((1, S, 256), lambda b, j: (b, 0, j)),
        scratch_shapes=[pltpu.VMEM((S // SWA_BLOCK, SWA_VT_ROWS, SWA_BLOCK), BF16)],
        compiler_params=_cparams(2), name="swa_attn",
    )(swa_range, qb, kb, vb, sinkrow, shiftrow, g_sqt, rope_t, bias.astype(F32))


def _gelu(y):
    return 0.5 * y * (1.0 + jnp.tanh(math.sqrt(2.0 / math.pi) * (y + 0.044715 * (y * y * y))))


def _s5_kernel(u_ref, wst_ref, wloc_ref, wcar_ref, at_ref, o_ref,
               ut_ref, utc_ref, yt_ref, ytc_ref, ere_ref, eim_ref, hfr_ref, hfi_ref, hbr_ref, hbi_ref,
               *, nb, n_lat_chunks, n_ctx_chunks):
    T, GP = S5_CHUNK, S5_GROUP
    n_chunks = n_lat_chunks + n_ctx_chunks
    n_lat = n_lat_chunks * T
    cw = nb * n_ctx_chunks
    tn_dims = (((0,), (0,)), ((), ()))
    nt_dims = (((1,), (1,)), ((), ()))
    gpl = LANES // GP

    for b in range(nb):
        for s in range(T):
            xs = u_ref[b, pl.ds(s, n_lat_chunks, stride=T), :]
            ut_ref[b, :, s] = xs.T.reshape(gpl, GP, n_lat_chunks).astype(BF16)
    zpad = jnp.zeros((LANES - cw, LANES), F32)
    for s in range(T):
        xs = jnp.concatenate([u_ref[b, pl.ds(n_lat + s, n_ctx_chunks, stride=T), :] for b in range(nb)]
                             + [zpad], axis=0)
        utc_ref[:, s] = xs.T.reshape(gpl, GP, LANES).astype(BF16)

    lane = lax.broadcasted_iota(jnp.int32, (2 * nb, LANES), 1)
    fwd = lane < S5_STATE
    lane_l = lax.broadcasted_iota(jnp.int32, (n_lat_chunks, LANES), 1) < S5_STATE
    lane_c = lax.broadcasted_iota(jnp.int32, (LANES, LANES), 1) < S5_STATE

    slab = 2 * nb
    lat_rows = lambda gb: pl.ds(gb, n_lat_chunks, stride=slab)
    ctx_rows = lambda gb: pl.ds(n_lat_chunks * slab + gb, n_ctx_chunks, stride=slab)

    def pair(gp, carry):
        for gl in range(2):
            g = gp * 2 + gl
            wst = wst_ref[g]
            for b in range(nb):
                e = lax.dot_general(ut_ref[b, g].reshape(T * GP, n_lat_chunks), wst, tn_dims,
                                    preferred_element_type=F32)
                ere_ref[lat_rows(gl * nb + b), :] = e[:, :LANES]
                eim_ref[lat_rows(gl * nb + b), :] = e[:, LANES:]
            ec = lax.dot_general(utc_ref[g].reshape(T * GP, LANES), wst, tn_dims, preferred_element_type=F32)
            for b in range(nb):
                ere_ref[ctx_rows(gl * nb + b), :] = ec[b * n_ctx_chunks:(b + 1) * n_ctx_chunks, :LANES]
                eim_ref[ctx_rows(gl * nb + b), :] = ec[b * n_ctx_chunks:(b + 1) * n_ctx_chunks, LANES:]

        a_re = jnp.concatenate([jnp.broadcast_to(at_ref[gp * 2 + gl, 0:1, :], (nb, LANES)) for gl in range(2)], 0)
        a_im = jnp.concatenate([jnp.broadcast_to(at_ref[gp * 2 + gl, 1:2, :], (nb, LANES)) for gl in range(2)], 0)

        def step(i, hc):
            h_re, h_im = hc
            cf = jnp.where(i < n_ctx_chunks, n_lat_chunks + i, i - n_ctx_chunks)
            cb = n_chunks - 1 - i
            sf = pl.ds(pl.multiple_of(cf * slab, slab), slab)
            sb = pl.ds(pl.multiple_of(cb * slab, slab), slab)
            hfr_ref[sf, :] = h_re
            hfi_ref[sf, :] = h_im
            hbr_ref[sb, :] = h_re
            hbi_ref[sb, :] = h_im
            e_re = jnp.where(fwd, ere_ref[sf, :], ere_ref[sb, :])
            e_im = jnp.where(fwd, eim_ref[sf, :], eim_ref[sb, :])
            return (a_re * h_re - a_im * h_im + e_re, a_re * h_im + a_im * h_re + e_im)

        zero = jnp.zeros((2 * nb, LANES), F32)
        lax.fori_loop(0, n_chunks, step, (zero, zero), unroll=8)

        for gl in range(2):
            g = gp * 2 + gl
            wloc, wcar = wloc_ref[g], wcar_ref[g]
            for b in range(nb):
                rows = lat_rows(gl * nb + b)
                h_cat = jnp.concatenate([jnp.where(lane_l, hfr_ref[rows, :], hbr_ref[rows, :]),
                                         jnp.where(lane_l, hfi_ref[rows, :], hbi_ref[rows, :])],
                                        axis=-1).astype(BF16)
                yt = (jnp.dot(wloc, ut_ref[b, g].reshape(T * GP, n_lat_chunks), preferred_element_type=F32)
                      + lax.dot_general(wcar, h_cat, nt_dims, preferred_element_type=F32))
                yt_ref[b, g] = _gelu(yt).astype(BF16).reshape(T, GP, n_lat_chunks)
            crow = lambda ref: jnp.concatenate(
                [ref[ctx_rows(gl * nb + b), :] for b in range(nb)]
                + [jnp.zeros((LANES - cw, LANES), F32)], axis=0)
            h_cat = jnp.concatenate([jnp.where(lane_c, crow(hfr_ref), crow(hbr_ref)),
                                     jnp.where(lane_c, crow(hfi_ref), crow(hbi_ref))], axis=-1).astype(BF16)
            ytc = (jnp.dot(wloc, utc_ref[g].reshape(T * GP, LANES), preferred_element_type=F32)
                   + lax.dot_general(wcar, h_cat, nt_dims, preferred_element_type=F32))
            ytc_ref[g] = _gelu(ytc).astype(BF16).reshape(T, GP, LANES)
        return carry

    lax.fori_loop(0, gpl // 2, pair, 0)

    for b in range(nb):
        for t in range(T):
            z = yt_ref[b, :, t].astype(F32).reshape(LANES, n_lat_chunks)
            o_ref[b, pl.ds(t, n_lat_chunks, stride=T), :] = z.T
    for t in range(T):
        z = ytc_ref[:, t].astype(F32).reshape(LANES, LANES).T
        for b in range(nb):
            o_ref[b, pl.ds(n_lat + t, n_ctx_chunks, stride=T), :] = z[b * n_ctx_chunks:(b + 1) * n_ctx_chunks]


def _s5_call(l, u, P, n_lat):
    B, S, W = u.shape
    nb = 4 if B % 4 == 0 else B
    T, GP = S5_CHUNK, S5_GROUP
    nlc, ncc = n_lat // T, (S - n_lat) // T
    gpl = LANES // GP
    nblk = W // LANES
    big = lambda n: pl.BlockSpec((nb, S, LANES), lambda j, hb: (hb, 0, j), pipeline_mode=pl.Buffered(n))
    wspec = lambda: pl.BlockSpec((gpl, T * GP, T * GP), lambda j, hb: (l * nblk + j, 0, 0))
    rows = 2 * nb * (nlc + ncc)
    return pl.pallas_call(
        functools.partial(_s5_kernel, nb=nb, n_lat_chunks=nlc, n_ctx_chunks=ncc),
        out_shape=jax.ShapeDtypeStruct((B, S, W), F32),
        grid=(nblk, B // nb),
        in_specs=[big(2), wspec(), wspec(), wspec(),
                  pl.BlockSpec((gpl, 2, LANES), lambda j, hb: (l * nblk + j, 0, 0))],
        out_specs=big(2),
        scratch_shapes=[pltpu.VMEM((nb, gpl, T, GP, nlc), BF16), pltpu.VMEM((gpl, T, GP, LANES), BF16),
                        pltpu.VMEM((nb, gpl, T, GP, nlc), BF16), pltpu.VMEM((gpl, T, GP, LANES), BF16)]
                       + [pltpu.VMEM((rows, LANES), F32)] * 6,
        compiler_params=pltpu.CompilerParams(dimension_semantics=("arbitrary",) * 2,
                                             vmem_limit_bytes=S5_VMEM_LIMIT), name="s5",
    )(u, P["s5_wst"], P["s5_wloc_t"], P["s5_wcar_t"], P["s5_at"])


def _out_kernel(x_ref, c_ref, mod_ref, oa_ref, oac_ref, ob_ref, gy_ref, gt_ref, wglu_ref, wout_ref,
                xo_ref, co_ref, *, n_lat_tiles):
    t = pl.program_id(1)
    cb = OUT_COLS
    nb, rows = x_ref.shape[0], x_ref.shape[1]
    stack = lambda ref: jnp.concatenate([ref[i] for i in range(nb)], axis=0)
    gyb = stack(gy_ref).astype(BF16)
    g = stack(gt_ref).astype(F32)
    oa = jnp.where(t == n_lat_tiles, stack(oac_ref), stack(oa_ref))
    m_a = (oa.astype(F32) * g[:, 0:512]).astype(BF16)
    m_b = (stack(ob_ref).astype(F32) * g[:, 512:1024]).astype(BF16)
    oc = []
    for j in range(512 // cb):
        za = jnp.dot(gyb, wglu_ref[0, :, j * cb:(j + 1) * cb], preferred_element_type=F32)
        zb = jnp.dot(gyb, wglu_ref[0, :, 512 + j * cb:512 + (j + 1) * cb], preferred_element_type=F32)
        oc.append((za * _sigmoid(zb) * g[:, 1024 + j * cb:1024 + (j + 1) * cb]).astype(BF16))
    m_c = jnp.concatenate(oc, axis=-1)
    gate = jnp.concatenate([jnp.broadcast_to(mod_ref[0, i][2:3], (rows, x_ref.shape[2])) for i in range(nb)],
                           axis=0)
    resid = jnp.where(t == n_lat_tiles, stack(c_ref), stack(x_ref))
    new = []
    for j in range(resid.shape[1] // cb):
        cols = slice(j * cb, (j + 1) * cb)
        upd = (jnp.dot(m_a, wout_ref[0, 0:512, cols], preferred_element_type=F32)
               + jnp.dot(m_b, wout_ref[0, 512:1024, cols], preferred_element_type=F32)
               + jnp.dot(m_c, wout_ref[0, 1024:1536, cols], preferred_element_type=F32))
        new.append(resid[:, cols] + gate[:, cols] * upd)

    @pl.when(t == n_lat_tiles)
    def _():
        for j, v in enumerate(new):
            for i in range(nb):
                co_ref[i, :, j * cb:(j + 1) * cb] = v[i * rows:(i + 1) * rows]

    @pl.when(t < n_lat_tiles)
    def _():
        for j, v in enumerate(new):
            for i in range(nb):
                xo_ref[i, :, j * cb:(j + 1) * cb] = v[i * rows:(i + 1) * rows]


def _out_call(l, x, ctx, mod, oa, oa_ctx, ob, gy, gt, P):
    B, L, D = x.shape
    C = ctx.shape[1]
    S = C + L
    nl = L // TOK
    nb = PROJ_NB
    xmap = lambda b, t: (b, jnp.minimum(t, nl - 1), 0)
    cmap = lambda b, t: (b, 0, 0)
    tmap = lambda b, t: (b, t, 0)
    return pl.pallas_call(
        functools.partial(_out_kernel, n_lat_tiles=nl),
        out_shape=[jax.ShapeDtypeStruct(x.shape, F32), jax.ShapeDtypeStruct(ctx.shape, F32)],
        grid=(B // nb, S // TOK),
        in_specs=[pl.BlockSpec((nb, TOK, D), xmap),
                  pl.BlockSpec((nb, TOK, D), cmap),
                  pl.BlockSpec((1, nb, 3, D), lambda b, t: (l, jnp.where(t == nl, B // nb, b), 0, 0)),
                  pl.BlockSpec((nb, TOK, 512), xmap),
                  pl.BlockSpec((nb, TOK, 512), cmap),
                  pl.BlockSpec((nb, TOK, 512), tmap),
                  pl.BlockSpec((nb, TOK, 512), tmap),
                  pl.BlockSpec((nb, TOK, 1536), tmap),
                  pl.BlockSpec((1, 512, 1024), lambda b, t: (l, 0, 0)),
                  pl.BlockSpec((1, 1536, D), lambda b, t: (l, 0, 0))],
        out_specs=[pl.BlockSpec((nb, TOK, D), xmap), pl.BlockSpec((nb, TOK, D), cmap)],
        compiler_params=_cparams(2), name="out",
    )(x, ctx, mod, oa, oa_ctx, ob, gy, gt, P["w_glu"], P["w_out"])


def _rope_angles(n_lat, n_ctx, rot_dim):
    rows = n_lat // GRID_W
    r_idx, c_idx = jnp.meshgrid(jnp.arange(rows), jnp.arange(GRID_W), indexing="ij")
    r_idx, c_idx = r_idx.reshape(-1), c_idx.reshape(-1)
    n_freq = rot_dim // 4
    freqs = ROPE_BASE ** (-jnp.arange(n_freq, dtype=F32) / n_freq)
    ang = jnp.concatenate([r_idx.astype(F32)[:, None] * freqs,
                           c_idx.astype(F32)[:, None] * freqs], axis=-1)
    return jnp.concatenate([ang, jnp.zeros((n_ctx, rot_dim // 2), F32)], axis=0)


def _rope_tables_t(n_lat, n_ctx, rot_dim):
    ang = _rope_angles(n_lat, n_ctx, rot_dim).T
    return jnp.stack([jnp.cos(ang), jnp.sin(ang)])


def _rope_tables(n_lat, n_ctx, rot_dim, lead, reps):
    ang = _rope_angles(n_lat, n_ctx, rot_dim)
    cos, sin, zero = jnp.cos(ang), jnp.sin(ang), jnp.zeros_like(ang)
    n = ang.shape[0]
    tail = LANES // reps - lead - rot_dim

    def pack(x1, x2, fill):
        unit = [jnp.full((n, lead), fill, F32), x1, x2, jnp.full((n, tail), fill, F32)]
        return jnp.concatenate(unit * reps, axis=-1)

    return jnp.stack([pack(cos, cos, 1.0), pack(zero, sin, 0.0), pack(-sin, zero, 0.0)])


def _prep_params(norm_g, w_in, w_out, mla_g_cq, mla_g_ckv, mla_w_uq, mla_w_ukv, mla_g_qn, mla_g_kn,
                 swa_g_qn, swa_g_kn, swa_sink, s5_a_re, s5_a_im, s5_log_dt, s5_b_re, s5_b_im,
                 s5_c_re, s5_c_im, s5_d, s5_w_glu):
    depth, D, _ = w_in.shape
    o_cq, o_ckv, o_kr, o_gm, o_sq, o_sk, o_sv, o_gs, o_u, o_g5, o_end = (
        0, 384, 640, 672, 1184, 1696, 1824, 1952, 2464, 2976, 3488)
    z = lambda n: jnp.zeros((depth, D, n), F32)
    w_in_p = (
        w_in[:, :, o_cq:o_kr].astype(BF16),
        jnp.concatenate([z(64), w_in[:, :, o_kr:o_gm], z(32)], axis=-1).astype(BF16),
        w_in[:, :, o_sq:o_gs].astype(BF16),
        w_in[:, :, o_u:o_g5].astype(BF16),
        jnp.concatenate([w_in[:, :, o_gm:o_sq], w_in[:, :, o_gs:o_u], w_in[:, :, o_g5:o_end]],
                        axis=-1).astype(BF16),
    )
    assert sum(w.shape[-1] for w in w_in_p) == C_END

    wq = mla_w_uq.reshape(depth, MLA_Q_RANK, MLA_HEADS, MLA_QK)
    wq = jnp.pad(wq, ((0, 0), (0, 0), (0, 0), (0, LANES - MLA_QK)))
    w_uq_p = wq.reshape(depth, MLA_Q_RANK, MLA_HEADS * LANES).astype(BF16)
    wkv = mla_w_ukv.reshape(depth, MLA_KV_RANK, MLA_HEADS, MLA_NOPE + MLA_V)
    wk = jnp.pad(wkv[..., :MLA_NOPE], ((0, 0), (0, 0), (0, 0), (0, LANES - MLA_NOPE)))
    w_ukv_p = jnp.concatenate([wk.reshape(depth, MLA_KV_RANK, MLA_HEADS * LANES),
                               wkv[..., MLA_NOPE:].reshape(depth, MLA_KV_RANK, MLA_HEADS * MLA_V)],
                              axis=-1).astype(BF16)

    pad_qk = lambda g: jnp.pad(g, ((0, 0), (0, LANES - MLA_QK)))[:, None, :]
    g_q = pad_qk(mla_g_qn * (MLA_QK ** -0.5 * LOG2E))
    g_qt = jnp.broadcast_to(jnp.swapaxes(g_q, 1, 2), (depth, LANES, LANES))
    g_k = pad_qk(mla_g_kn)
    mla_bound = 1.02 * MLA_QK * jnp.max(jnp.abs(g_q), axis=(1, 2)) * jnp.max(jnp.abs(g_k), axis=(1, 2))
    g_sqt = jnp.broadcast_to((swa_g_qn * (SWA_DIM ** -0.5 * LOG2E))[:, :, None], (depth, SWA_DIM, SWA_BLOCK))
    g_sk = jnp.tile(swa_g_kn, (1, 2))[:, None, :]
    per_kv = SWA_HEADS // SWA_KV_HEADS
    lanes_of = lambda v: jnp.broadcast_to(v.reshape(depth * SWA_KV_HEADS, 1, per_kv, 1),
                                          (depth * SWA_KV_HEADS, 8, per_kv, SWA_BLOCK)
                                          ).reshape(depth * SWA_KV_HEADS, 8, -1).astype(F32)
    sink2 = swa_sink * LOG2E
    sinkrow = lanes_of(sink2)
    swa_bound = 1.02 * SWA_DIM * jnp.max(jnp.abs(g_sqt), axis=(1, 2)) * jnp.max(jnp.abs(g_sk), axis=(1, 2))
    shift = jnp.maximum(swa_bound[:, None], sink2)
    shiftrow = lanes_of(shift)
    swa_range = (swa_bound + jnp.max(shift, axis=1)).astype(F32)

    T = S5_CHUNK
    A = lax.complex(s5_a_re, s5_a_im)
    dt = jnp.exp(s5_log_dt)[..., None]
    a_bar = jnp.exp(dt * A)
    b_bar = ((a_bar - 1.0) / A)[..., None] * lax.complex(s5_b_re, s5_b_im)
    c_mat = lax.complex(s5_c_re, s5_c_im)
    k_idx = jnp.arange(T + 1, dtype=F32)
    pw = jnp.exp(k_idx[None, None, None, :, None] * (dt * A)[:, :, :, None, :])
    hi = lax.Precision.HIGHEST
    tt = jnp.arange(T)
    GP, NG = S5_GROUP, depth * S5_GROUPS
    pw_f, pw_b = pw[:, 0], pw[:, 1]
    tap_f = jnp.einsum("lgpn,lgkn,lgnq->lgpkq", c_mat[:, 0], pw_f[:, :, T - 1 - tt], b_bar[:, 0],
                       precision=hi).real
    tap_b = jnp.einsum("lgpn,lgkn,lgnq->lgpkq", c_mat[:, 1], pw_b[:, :, tt], b_bar[:, 1],
                       precision=hi).real
    d_diag = s5_d.reshape(depth, S5_GROUPS, GP)[..., None] * jnp.eye(GP, dtype=F32)
    centre = tap_f[:, :, :, T - 1:] + tap_b[:, :, :, :1] + d_diag[:, :, :, None, :]
    krev = jnp.concatenate([tap_f[:, :, :, :T - 1], centre, tap_b[:, :, :, 1:]], axis=3)
    krev = krev.reshape(depth, S5_GROUPS, GP, (2 * T - 1) * GP)
    wloc_t = jnp.stack([krev[..., (T - 1 - t) * GP:(T - 1 - t) * GP + T * GP] for t in range(T)], axis=2)
    wloc_t = wloc_t.reshape(NG, T * GP, T * GP)
    b_t = jnp.swapaxes(b_bar, -1, -2)
    inc_f = pw_f[:, :, T - 1 - tt][:, :, :, None, :] * b_t[:, 0][:, :, None]
    inc_b = pw_b[:, :, tt][:, :, :, None, :] * b_t[:, 1][:, :, None]
    wst = jnp.concatenate([inc_f.real, inc_b.real, inc_f.imag, inc_b.imag], axis=-1)
    wst = wst.reshape(NG, T * GP, 4 * S5_STATE)
    ro_f = c_mat[:, 0][:, :, None] * pw_f[:, :, tt + 1][:, :, :, None, :]
    ro_b = c_mat[:, 1][:, :, None] * pw_b[:, :, T - tt][:, :, :, None, :]
    wcar_t = jnp.concatenate([ro_f.real, ro_b.real, -ro_f.imag, -ro_b.imag], axis=-1)
    wcar_t = wcar_t.reshape(NG, T * GP, 4 * S5_STATE)
    a_t = pw[:, :, :, T]
    at = jnp.stack([jnp.concatenate([a_t[:, 0].real, a_t[:, 1].real], axis=-1),
                    jnp.concatenate([a_t[:, 0].imag, a_t[:, 1].imag], axis=-1)], axis=2)
    at = at.reshape(NG, 2, LANES).astype(F32)

    return dict(norm_g=norm_g[:, None, :], w_in=w_in_p, g_cq=mla_g_cq[:, None, :], w_uq=w_uq_p, g_qt=g_qt, mla_bound=mla_bound.astype(F32),
                g_ckv=mla_g_ckv[:, None, :], w_ukv=w_ukv_p, g_k=g_k, g_sqt=g_sqt, g_sk=g_sk,
                sinkrow=sinkrow, shiftrow=shiftrow, swa_range=swa_range, s5_wloc_t=wloc_t.astype(BF16), s5_wst=wst.astype(BF16),
                s5_wcar_t=wcar_t.astype(BF16), s5_at=at, w_glu=s5_w_glu.astype(BF16),
                w_out=w_out.astype(BF16))


def kernel(x, c, ctx, c_ctx, norm_g, w_ada, b_ada, w_in, w_out, mla_g_cq, mla_g_ckv, mla_w_uq, mla_w_ukv, mla_g_qn, mla_g_kn, swa_g_qn, swa_g_kn, swa_sink, s5_a_re, s5_a_im, s5_log_dt, s5_b_re, s5_b_im, s5_c_re, s5_c_im, s5_d, s5_w_glu):
    B, L, D = x.shape
    C = ctx.shape[1]
    S = C + L
    depth = w_in.shape[0]
    assert B + PROJ_NB <= MOD_ROWS and C == TOK and L % (2 * TOK) == 0 and L % GRID_W == 0 and L >= 3 * SWA_BLOCK

    P = _prep_params(norm_g, w_in, w_out, mla_g_cq, mla_g_ckv, mla_w_uq, mla_w_ukv, mla_g_qn, mla_g_kn,
                     swa_g_qn, swa_g_kn, swa_sink, s5_a_re, s5_a_im, s5_log_dt, s5_b_re, s5_b_im,
                     s5_c_re, s5_c_im, s5_d, s5_w_glu)
    rope_mla = _rope_tables(L, C, MLA_ROPE, MLA_NOPE, 1)
    rope_swa = _rope_tables(L, C, SWA_DIM, 0, 2)
    rope_mla_t = _rope_tables_t(L, C, MLA_ROPE)
    rope_swa_t = jnp.transpose(_rope_tables_t(L, C, SWA_DIM).reshape(2, SWA_DIM // 2, S // SWA_BLOCK, SWA_BLOCK),
                               (2, 0, 1, 3))

    cc = jnp.concatenate([c, jnp.tile(c_ctx[None, :], (PROJ_NB, 1)),
                          jnp.zeros((MOD_ROWS - B - PROJ_NB, D), F32)], axis=0)
    mod = _ada_call(cc, w_ada, b_ada).reshape(depth, MOD_ROWS, 3, D)

    for l in range(depth):
        qa, ka, va, qb, kb, vb, u, gt = _proj_call(l, x, ctx, mod, P, rope_mla, rope_swa)
        oa, oa_ctx = _mla_call(l, qa, ka, va, P["g_qt"], rope_mla_t, P["mla_bound"], L)
        ob = _swa_call(l, qb, kb, vb, P["sinkrow"], P["shiftrow"], P["swa_range"], P["g_sqt"], rope_swa_t, C)
        gy = _s5_call(l, u, P, L)
        x, ctx = _out_call(l, x, ctx, mod, oa, oa_ctx, ob, gy, gt, P)
    return x
```

```python
import functools
import math

import jax
import jax.numpy as jnp
from jax import lax
from jax.experimental import pallas as pl
from jax.experimental.pallas import tpu as pltpu

F32 = jnp.float32
BF16 = jnp.bfloat16

GRID_W = 64
EPS = 1e-6
ROPE_BASE = 10000.0
NEG = -1e30
LOG2E = math.log2(math.e)

MLA_HEADS = 8
MLA_NOPE = 64
MLA_ROPE = 32
MLA_V = 64
MLA_QK = MLA_NOPE + MLA_ROPE
MLA_Q_RANK = 384
MLA_KV_RANK = 256

SWA_HEADS = 8
SWA_KV_HEADS = 2
SWA_DIM = 64
SWA_WINDOW = 128

S5_GROUP = 16
S5_GROUPS = 32
S5_STATE = 64
S5_CHUNK = 16

LANES = 128
TOK = 256
KV_CHUNK = 256
SWA_BLOCK = 128
MOD_ROWS = 16
PROJ_NB = 2
SWA_UNROLL = 32
OUT_COLS = 256
PROJ_SKEW = 1

C_CQ = 0
C_CKV = 384
C_KR = 640
C_SQ = 768
C_SK = 1280
C_SV = 1408
C_U = 1536
C_GATE = 2048
C_END = 3584

VMEM_LIMIT = 56 * 1024 * 1024
S5_VMEM_LIMIT = 60 * 1024 * 1024


def _cparams(n_axes):
    return pltpu.CompilerParams(dimension_semantics=("arbitrary",) * n_axes,
                                vmem_limit_bytes=VMEM_LIMIT)


def _ada_kernel(c_ref, w_ref, b_ref, o_ref):
    cc = c_ref[...]
    s = cc * jax.nn.sigmoid(cc)
    o_ref[0] = jnp.dot(s, w_ref[0], preferred_element_type=F32,
                       precision=lax.Precision.HIGHEST) + b_ref[0]


def _ada_call(cc, w_ada, b_ada):
    depth, d, n3 = w_ada.shape
    tn = 768
    return pl.pallas_call(
        _ada_kernel,
        out_shape=jax.ShapeDtypeStruct((depth, MOD_ROWS, n3), F32),
        grid=(depth, n3 // tn),
        in_specs=[pl.BlockSpec((MOD_ROWS, d), lambda l, j: (0, 0)),
                  pl.BlockSpec((1, d, tn), lambda l, j: (l, 0, j)),
                  pl.BlockSpec((1, 1, tn), lambda l, j: (l, 0, j))],
        out_specs=pl.BlockSpec((1, MOD_ROWS, tn), lambda l, j: (l, 0, j)),
        compiler_params=_cparams(2),
        name="ada",
    )(cc, w_ada, b_ada.reshape(depth, 1, n3))


def _sigmoid(v):
    return 0.5 * jnp.tanh(0.5 * v) + 0.5


def _norm_rope(slots, lo_masks, dim, gain, tab_ref, half):
    sq = [s * s for s in slots]
    sums = []
    for s2, lo in zip(sq, lo_masks):
        if lo is None:
            sums.append((jnp.sum(s2, axis=-1, keepdims=True),))
        else:
            sums.append((jnp.sum(jnp.where(lo, s2, 0.0), axis=-1, keepdims=True),
                         jnp.sum(jnp.where(lo, 0.0, s2), axis=-1, keepdims=True)))
    ys = []
    for s, ss, lo in zip(slots, sums, lo_masks):
        rs = [lax.rsqrt(v * (1.0 / dim) + EPS) for v in ss]
        r = rs[0] if len(rs) == 1 else jnp.where(lo, rs[0], rs[1])
        ys.append(s * r * gain)
    up = [pltpu.roll(y, half, 1) for y in ys]
    dn = [pltpu.roll(y, LANES - half, 1) for y in ys]
    return [(y * tab_ref[0] + u * tab_ref[1] + d * tab_ref[2]).astype(BF16) for y, u, d in zip(ys, up, dn)]


N_PROJ_DATA = 3
N_PROJ_WEIGHTS = 14


def _proj_kernel(*refs, n_lat_tiles):
    data = refs[:N_PROJ_DATA]
    shared = refs[N_PROJ_DATA:N_PROJ_DATA + N_PROJ_WEIGHTS]
    outs = refs[N_PROJ_DATA + N_PROJ_WEIGHTS:]
    streams = []
    for i in range(data[0].shape[0]):
        one = pl.ds(i, 1)
        streams.append(_proj_tile(data[0].at[one], data[1].at[one], data[2].at[:, one], *shared,
                                  *[o.at[one] for o in outs], n_lat_tiles=n_lat_tiles))
    pending = list(enumerate(streams))
    step = 0
    while pending:
        for item in list(pending):
            if step >= item[0] * PROJ_SKEW and next(item[1], "done") == "done":
                pending.remove(item)
        step += 1


def _proj_tile(x_ref, c_ref, mod_ref, ng_ref, wa_ref, wkr_ref, wb_ref, wu_ref, wg_ref, gcq_ref, wuq_ref,
               gckv_ref, wukv_ref, gk_ref, gsk_ref, rm_ref, rs_ref,
               qa_ref, ka_ref, va_ref, qb_ref, kb_ref, vb_ref, u_ref, gt_ref, *, n_lat_tiles):
    t = pl.program_id(1)
    x = jnp.where(t == n_lat_tiles, c_ref[0], x_ref[0])
    mod = mod_ref[0, 0]
    y = x * lax.rsqrt(jnp.mean(x * x, axis=-1, keepdims=True) + EPS) * ng_ref[0]
    xn = (y * (1.0 + mod[1:2]) + mod[0:1]).astype(BF16)

    pieces = ((C_CQ, wa_ref), (C_KR, wkr_ref), (C_SQ, wb_ref), (C_U, wu_ref), (C_GATE, wg_ref))

    def seg(a, b):
        start, ref = [p for p in pieces if p[0] <= a][-1]
        return jnp.dot(xn, ref[0, :, a - start:b - start], preferred_element_type=F32)

    lane = lax.broadcasted_iota(jnp.int32, (TOK, LANES), 1)
    lo = lane < 64

    yield
    cq = seg(C_CQ, C_CKV)
    ckv = seg(C_CKV, C_KR)
    kr = seg(C_KR, C_SQ)
    yield
    cqn = (cq * lax.rsqrt(jnp.mean(cq * cq, axis=-1, keepdims=True) + EPS) * gcq_ref[0]).astype(BF16)
    qf = jnp.dot(cqn, wuq_ref[0], preferred_element_type=F32)
    ckvn = (ckv * lax.rsqrt(jnp.mean(ckv * ckv, axis=-1, keepdims=True) + EPS) * gckv_ref[0]).astype(BF16)
    kvf = jnp.dot(ckvn, wukv_ref[0], preferred_element_type=F32)
    sq = seg(C_SQ, C_SK)
    yield
    for h in range(MLA_HEADS):
        qa_ref[0, h] = qf[:, h * LANES:(h + 1) * LANES].astype(BF16)

    sk = seg(C_SK, C_SV)
    vb_ref[0] = seg(C_SV, C_U).astype(BF16)
    yield
    slots = [kvf[:, h * LANES:(h + 1) * LANES] + kr for h in range(MLA_HEADS)]
    for h, o in enumerate(_norm_rope(slots, [None] * MLA_HEADS, MLA_QK, gk_ref[0], rm_ref, MLA_ROPE // 2)):
        ka_ref[0, h] = jnp.where(lane == MLA_QK, jnp.ones_like(o), o)
    va_ref[0] = kvf[:, MLA_HEADS * LANES:].astype(BF16)
    u_ref[0] = seg(C_U, C_GATE)
    yield
    qb_ref[0] = sq.astype(BF16)
    kb_ref[0] = _norm_rope([sk], [lo], SWA_DIM, gsk_ref[0], rs_ref, SWA_DIM // 2)[0]
    g = seg(C_GATE, C_END)
    yield
    gt_ref[0] = (g * _sigmoid(g)).astype(BF16)


def _proj_call(l, x, ctx, mod, P, rope_mla, rope_swa):
    B, L, D = x.shape
    C = ctx.shape[1]
    S = C + L
    nt = S // TOK

    def wspec(arr):
        shp = arr.shape
        return pl.BlockSpec((1,) + shp[1:], lambda b, t: (l,) + (0,) * (len(shp) - 1))

    weights = [P["norm_g"], *P["w_in"], P["g_cq"], P["w_uq"],
               P["g_ckv"], P["w_ukv"], P["g_k"], P["g_sk"]]
    nl = L // TOK
    nb = PROJ_NB
    assert len(weights) + 2 == N_PROJ_WEIGHTS and B % nb == 0
    in_specs = ([pl.BlockSpec((nb, TOK, D), lambda b, t: (b, jnp.minimum(t, nl - 1), 0)),
                 pl.BlockSpec((nb, TOK, D), lambda b, t: (b, 0, 0)),
                 pl.BlockSpec((1, nb, 3, D), lambda b, t: (l, jnp.where(t == nl, B // nb, b), 0, 0))]
                + [wspec(w) for w in weights]
                + [pl.BlockSpec((3, TOK, LANES), lambda b, t: (0, t, 0)),
                   pl.BlockSpec((3, TOK, LANES), lambda b, t: (0, t, 0))])
    out_shape = [jax.ShapeDtypeStruct((B, MLA_HEADS, S, LANES), BF16),
                 jax.ShapeDtypeStruct((B, MLA_HEADS, S, LANES), BF16),
                 jax.ShapeDtypeStruct((B, S, 512), BF16),
                 jax.ShapeDtypeStruct((B, S, 512), BF16),
                 jax.ShapeDtypeStruct((B, S, LANES), BF16),
                 jax.ShapeDtypeStruct((B, S, LANES), BF16),
                 jax.ShapeDtypeStruct((B, S, 512), F32),
                 jax.ShapeDtypeStruct((B, S, 1536), BF16)]
    out_specs = [pl.BlockSpec((nb, MLA_HEADS, TOK, LANES), lambda b, t: (b, 0, t, 0)),
                 pl.BlockSpec((nb, MLA_HEADS, TOK, LANES), lambda b, t: (b, 0, t, 0))]
    out_specs += [pl.BlockSpec((nb, TOK, s.shape[2]), lambda b, t: (b, t, 0)) for s in out_shape[2:]]
    return pl.pallas_call(
        functools.partial(_proj_kernel, n_lat_tiles=nl), out_shape=out_shape, grid=(B // nb, nt),
        in_specs=in_specs, out_specs=out_specs,
        compiler_params=_cparams(2), name="proj",
    )(x, ctx, mod, *weights, rope_mla, rope_swa)


MLA_SUBTILES = 4
MLA_VT_ROWS = 80


MLA_SAFE_RANGE = 60.0


def _mla_kernel(rb_ref, q_ref, k_ref, v_ref, gq_ref, rt_ref, o_ref, vt_ref, *, n_chunks, tq, layer):
    ones = jnp.ones((MLA_VT_ROWS - MLA_V, KV_CHUNK), BF16)

    @pl.when(pl.program_id(2) == 0)
    def _():
        for c in range(n_chunks):
            vt = v_ref[0, c].astype(F32).T.astype(BF16)
            vt_ref[0, c] = jnp.concatenate([vt[:MLA_V], ones], axis=0)
            vt_ref[1, c] = jnp.concatenate([vt[MLA_V:], ones], axis=0)

    bound = rb_ref[layer]
    half = MLA_ROPE // 2

    def sweep(fixed_shift):
        for r0 in range(0, q_ref.shape[2], tq):
            rows = slice(r0, r0 + tq)
            cos, sin = rt_ref[0, :, rows], rt_ref[1, :, rows]
            gain = jnp.concatenate([gq_ref[0]] * (tq // LANES), axis=1)
            pad = jnp.zeros((LANES - MLA_QK, tq), F32)
            if fixed_shift:
                row = lax.broadcasted_iota(jnp.int32, pad.shape, 0)
                pad = jnp.where(row == 0, -bound, pad)

            def qmat(e):
                qt = q_ref[0, e, rows, :].astype(F32).T
                r = lax.rsqrt(jnp.sum(qt * qt, axis=0, keepdims=True) * (1.0 / MLA_QK) + EPS)
                y = qt * r * gain
                x1, x2 = y[MLA_NOPE:MLA_NOPE + half], y[MLA_NOPE + half:MLA_QK]
                return jnp.concatenate([y[:MLA_NOPE], x1 * cos - x2 * sin, x2 * cos + x1 * sin, pad],
                                       axis=0).astype(BF16)

            qts = [qmat(e) for e in range(2)]
            score = lambda c, e: jnp.dot(k_ref[0, e, c], qts[e], preferred_element_type=F32)
            ms = [None, None]
            accs = [None, None]
            sts = [score(0, e) for e in range(2)]
            for c in range(n_chunks):
                nxt = [None, None]
                for e in range(2):
                    if c + 1 < n_chunks:
                        nxt[e] = score(c + 1, e)
                    if fixed_shift:
                        pv = jnp.dot(vt_ref[e, c], jnp.exp2(sts[e]).astype(BF16), preferred_element_type=F32)
                        accs[e] = pv if c == 0 else accs[e] + pv
                    else:
                        mc = jnp.max(sts[e], axis=0, keepdims=True)
                        m_new = mc if c == 0 else jnp.maximum(ms[e], mc)
                        pt = jnp.exp2(sts[e] - m_new).astype(BF16)
                        pv = jnp.dot(vt_ref[e, c], pt, preferred_element_type=F32)
                        accs[e] = pv if c == 0 else accs[e] * jnp.exp2(ms[e] - m_new) + pv
                        ms[e] = m_new
                sts = nxt
            ot = jnp.concatenate([accs[e][:MLA_V] / accs[e][MLA_V:MLA_V + 1] for e in range(2)], axis=0)
            o_ref[0, rows, :] = ot.T.astype(BF16)

    @pl.when(bound < MLA_SAFE_RANGE)
    def _():
        sweep(True)

    @pl.when(jnp.logical_not(bound < MLA_SAFE_RANGE))
    def _():
        sweep(False)


def _mla_call(l, qa, ka, va, g_qt, rope_t, r_bound, n_lat):
    B, H, S, _ = qa.shape
    half = MLA_ROPE // 2
    gspec = pl.BlockSpec((1, LANES, LANES), lambda b, p, t: (l, 0, 0))
    sspec = pl.BlockSpec(memory_space=pltpu.SMEM)
    nc = S // KV_CHUNK
    tq = 2 * TOK
    bq = MLA_SUBTILES * tq if n_lat % (MLA_SUBTILES * tq) == 0 else tq
    k5 = ka.reshape(B, H, nc, KV_CHUNK, LANES)
    v4 = va.reshape(B, nc, KV_CHUNK, 512)
    o_lat = pl.pallas_call(
        functools.partial(_mla_kernel, n_chunks=nc, tq=tq, layer=l),
        out_shape=jax.ShapeDtypeStruct((B, n_lat, 512), BF16),
        grid=(B, H // 2, n_lat // bq),
        in_specs=[sspec,
                  pl.BlockSpec((1, 2, bq, LANES), lambda b, p, t: (b, p, t, 0)),
                  pl.BlockSpec((1, 2, nc, KV_CHUNK, LANES), lambda b, p, t: (b, p, 0, 0, 0)),
                  pl.BlockSpec((1, nc, KV_CHUNK, LANES), lambda b, p, t: (b, 0, 0, p)),
                  gspec,
                  pl.BlockSpec((2, half, bq), lambda b, p, t: (0, 0, t))],
        out_specs=pl.BlockSpec((1, bq, LANES), lambda b, p, t: (b, t, p)),
        scratch_shapes=[pltpu.VMEM((2, nc, MLA_VT_ROWS, KV_CHUNK), BF16)],
        compiler_params=_cparams(3), name="mla_attn",
    )(r_bound, qa, k5, v4, g_qt, rope_t)
    cblk = n_lat // KV_CHUNK
    o_ctx = pl.pallas_call(
        functools.partial(_mla_kernel, n_chunks=1, tq=TOK, layer=l),
        out_shape=jax.ShapeDtypeStruct((B, S - n_lat, 512), BF16),
        grid=(B, H // 2, 1),
        in_specs=[sspec,
                  pl.BlockSpec((1, 2, TOK, LANES), lambda b, p, t: (b, p, cblk, 0)),
                  pl.BlockSpec((1, 2, 1, KV_CHUNK, LANES), lambda b, p, t: (b, p, cblk, 0, 0)),
                  pl.BlockSpec((1, 1, KV_CHUNK, LANES), lambda b, p, t: (b, cblk, 0, p)),
                  gspec,
                  pl.BlockSpec((2, half, TOK), lambda b, p, t: (0, 0, cblk))],
        out_specs=pl.BlockSpec((1, TOK, LANES), lambda b, p, t: (b, 0, p)),
        scratch_shapes=[pltpu.VMEM((2, 1, MLA_VT_ROWS, KV_CHUNK), BF16)],
        compiler_params=_cparams(3), name="mla_attn_ctx",
    )(r_bound, qa, k5, v4, g_qt, rope_t)
    return o_lat, o_ctx


SWA_SAFE_RANGE = 120.0
SWA_VT_ROWS = 80


def _swa_kernel(rng_ref, q_ref, k_ref, v_ref, sink_ref, shift_ref, gq_ref, rt_ref, bias_ref, o_ref, vt_ref,
                *, n_ctx, n_lat, layer):
    blk = SWA_BLOCK
    heads = SWA_HEADS // SWA_KV_HEADS
    win = 3 * blk
    n_blocks = n_lat // blk
    ctx_blk = n_blocks
    group = min(SWA_UNROLL, n_blocks)
    first = pl.program_id(1) == 0
    sink = sink_ref[0, 0:1, :]

    kv = pl.program_id(1)

    @pl.when(first)
    def _():
        ones = jnp.ones((SWA_VT_ROWS - SWA_DIM, blk), BF16)
        for i in range((n_lat + n_ctx) // blk):
            vt = v_ref[0, i * blk:(i + 1) * blk, :].astype(F32).T.astype(BF16)
            vt_ref[0, i] = jnp.concatenate([vt[:SWA_DIM], ones], axis=0)
            vt_ref[1, i] = jnp.concatenate([vt[SWA_DIM:], ones], axis=0)
    kc = k_ref[0, n_lat:n_lat + n_ctx, :]
    vtc = jnp.concatenate([vt_ref[kv, ctx_blk + i] for i in range(n_ctx // blk)], axis=1)

    gain = gq_ref[0]
    half = SWA_DIM // 2

    def qmat(n):
        qt = q_ref[0, pl.ds(pl.multiple_of(n * blk, blk), blk), :].astype(F32).T
        cos, sin = rt_ref[n, 0], rt_ref[n, 1]
        cols = []
        for h in range(heads):
            x = qt[h * SWA_DIM:(h + 1) * SWA_DIM]
            y = x * lax.rsqrt(jnp.sum(x * x, axis=0, keepdims=True) * (1.0 / SWA_DIM) + EPS) * gain
            x1, x2 = y[:half], y[half:]
            cols.append(jnp.concatenate([x1 * cos - x2 * sin, x2 * cos + x1 * sin], axis=0))
        w = jnp.concatenate(cols, axis=1).astype(BF16)
        z = jnp.zeros_like(w)
        return jnp.where(first, jnp.concatenate([w, z], axis=0), jnp.concatenate([z, w], axis=0))

    def finish(r0, m, acc):
        ot = acc[:SWA_DIM] / (acc[SWA_DIM:SWA_DIM + 1] + jnp.exp2(sink - m))
        o4 = jnp.concatenate([ot[:, h * blk:(h + 1) * blk] for h in range(heads)], axis=0)
        o_ref[0, pl.ds(r0, blk), :] = o4.T.astype(BF16)

    shift = shift_ref[0, 0:1, :]

    def row_max(fixed, *scores):
        if fixed:
            return shift
        m = sink
        for sc in scores:
            m = jnp.maximum(m, jnp.max(sc, axis=0, keepdims=True))
        return m

    def ctx_queries(fixed):
        for n in range(n_ctx // blk):
            r0 = n_lat + n * blk
            s_c = jnp.dot(kc, qmat(ctx_blk + n), preferred_element_type=F32)
            m = row_max(fixed, s_c)
            finish(r0, m, jnp.dot(vtc, jnp.exp2(s_c - m).astype(BF16), preferred_element_type=F32))

    def window_block(n):
        return jnp.clip(n - 1, 0, n_blocks - 3)

    def scores(n):
        wb = window_block(n)
        w = qmat(n)
        kw = k_ref[0, pl.ds(pl.multiple_of(wb * blk, blk), win), :]
        return (jnp.dot(kc, w, preferred_element_type=F32),
                jnp.dot(kw, w, preferred_element_type=F32) + bias_ref[n - wb])

    def latent_queries(fixed):
        def blocks(gi, carry):
            n0 = gi * group
            cur = scores(n0)
            for i in range(group):
                n = n0 + i
                nxt = scores(n + 1) if i + 1 < group else None
                s_c, s_w = cur
                m = row_max(fixed, s_c, s_w)
                wb = window_block(n)
                vtw = jnp.concatenate([vt_ref[kv, wb + j] for j in range(3)], axis=1)
                acc = (jnp.dot(vtc, jnp.exp2(s_c - m).astype(BF16), preferred_element_type=F32)
                       + jnp.dot(vtw, jnp.exp2(s_w - m).astype(BF16), preferred_element_type=F32))
                finish(pl.multiple_of(n * blk, blk), m, acc)
                cur = nxt
            return carry

        lax.fori_loop(0, n_blocks // group, blocks, 0)

    assert n_blocks % group == 0
    safe = rng_ref[layer] < SWA_SAFE_RANGE

    @pl.when(safe)
    def _():
        ctx_queries(True)
        latent_queries(True)

    @pl.when(jnp.logical_not(safe))
    def _():
        ctx_queries(False)
        latent_queries(False)


def _swa_call(l, qb, kb, vb, sinkrow, shiftrow, swa_range, g_sqt, rope_t, n_ctx):
    B, S, _ = qb.shape
    nq = (SWA_HEADS // SWA_KV_HEADS) * SWA_BLOCK
    win = 3 * SWA_BLOCK
    rel = jnp.arange(win)[:, None] - (jnp.arange(nq)[None, :] % SWA_BLOCK)
    bias = jnp.stack([jnp.where(jnp.abs(rel - kind * SWA_BLOCK) <= SWA_WINDOW, 0.0, NEG) for kind in range(3)])
    return pl.pallas_call(
        functools.partial(_swa_kernel, n_ctx=n_ctx, n_lat=S - n_ctx, layer=l),
        out_shape=jax.ShapeDtypeStruct((B, S, 512), BF16),
        grid=(B, SWA_KV_HEADS),
        in_specs=[pl.BlockSpec(memory_space=pltpu.SMEM),
                  pl.BlockSpec((1, S, 256), lambda b, j: (b, 0, j)),
                  pl.BlockSpec((1, S, LANES), lambda b, j: (b, 0, 0)),
                  pl.BlockSpec((1, S, LANES), lambda b, j: (b, 0, 0)),
                  pl.BlockSpec((1, 8, nq), lambda b, j: (l * SWA_KV_HEADS + j, 0, 0)),
                  pl.BlockSpec((1, 8, nq), lambda b, j: (l * SWA_KV_HEADS + j, 0, 0)),
                  pl.BlockSpec((1, SWA_DIM, SWA_BLOCK), lambda b, j: (l, 0, 0)),
                  pl.BlockSpec(rope_t.shape, lambda b, j: (0, 0, 0, 0)),
                  pl.BlockSpec((3, win, nq), lambda b, j: (0, 0, 0))],
        out_specs=pl.BlockSpec((1, S, 256), lambda b, j: (b, 0, j)),
        scratch_shapes=[pltpu.VMEM((SWA_KV_HEADS, S // SWA_BLOCK, SWA_VT_ROWS, SWA_BLOCK), BF16)],
        compiler_params=_cparams(2), name="swa_attn",
    )(swa_range, qb, kb, vb, sinkrow, shiftrow, g_sqt, rope_t, bias.astype(F32))


def _gelu(y):
    return 0.5 * y * (1.0 + jnp.tanh(math.sqrt(2.0 / math.pi) * (y + 0.044715 * (y * y * y))))


def _s5_kernel(u_ref, wst_ref, wloc_ref, wcar_ref, at_ref, o_ref,
               ut_ref, utc_ref, yt_ref, ytc_ref, ere_ref, eim_ref, hfr_ref, hfi_ref, hbr_ref, hbi_ref,
               *, nb, n_lat_chunks, n_ctx_chunks):
    T, GP = S5_CHUNK, S5_GROUP
    n_chunks = n_lat_chunks + n_ctx_chunks
    n_lat = n_lat_chunks * T
    cw = nb * n_ctx_chunks
    tn_dims = (((0,), (0,)), ((), ()))
    nt_dims = (((1,), (1,)), ((), ()))
    gpl = LANES // GP

    for b in range(nb):
        for s in range(T):
            xs = u_ref[b, pl.ds(s, n_lat_chunks, stride=T), :]
            ut_ref[b, :, s] = xs.T.reshape(gpl, GP, n_lat_chunks).astype(BF16)
    zpad = jnp.zeros((LANES - cw, LANES), F32)
    for s in range(T):
        xs = jnp.concatenate([u_ref[b, pl.ds(n_lat + s, n_ctx_chunks, stride=T), :] for b in range(nb)]
                             + [zpad], axis=0)
        utc_ref[:, s] = xs.T.reshape(gpl, GP, LANES).astype(BF16)

    lane = lax.broadcasted_iota(jnp.int32, (2 * nb, LANES), 1)
    fwd = lane < S5_STATE
    lane_l = lax.broadcasted_iota(jnp.int32, (n_lat_chunks, LANES), 1) < S5_STATE
    lane_c = lax.broadcasted_iota(jnp.int32, (LANES, LANES), 1) < S5_STATE

    slab = 2 * nb
    lat_rows = lambda gb: pl.ds(gb, n_lat_chunks, stride=slab)
    ctx_rows = lambda gb: pl.ds(n_lat_chunks * slab + gb, n_ctx_chunks, stride=slab)

    def pair(gp, carry):
        for gl in range(2):
            g = gp * 2 + gl
            wst = wst_ref[g]
            for b in range(nb):
                e = lax.dot_general(ut_ref[b, g].reshape(T * GP, n_lat_chunks), wst, tn_dims,
                                    preferred_element_type=F32)
                ere_ref[lat_rows(gl * nb + b), :] = e[:, :LANES]
                eim_ref[lat_rows(gl * nb + b), :] = e[:, LANES:]
            ec = lax.dot_general(utc_ref[g].reshape(T * GP, LANES), wst, tn_dims, preferred_element_type=F32)
            for b in range(nb):
                ere_ref[ctx_rows(gl * nb + b), :] = ec[b * n_ctx_chunks:(b + 1) * n_ctx_chunks, :LANES]
                eim_ref[ctx_rows(gl * nb + b), :] = ec[b * n_ctx_chunks:(b + 1) * n_ctx_chunks, LANES:]

        a_re = jnp.concatenate([jnp.broadcast_to(at_ref[gp * 2 + gl, 0:1, :], (nb, LANES)) for gl in range(2)], 0)
        a_im = jnp.concatenate([jnp.broadcast_to(at_ref[gp * 2 + gl, 1:2, :], (nb, LANES)) for gl in range(2)], 0)

        def step(i, hc):
            h_re, h_im = hc
            cf = jnp.where(i < n_ctx_chunks, n_lat_chunks + i, i - n_ctx_chunks)
            cb = n_chunks - 1 - i
            sf = pl.ds(pl.multiple_of(cf * slab, slab), slab)
            sb = pl.ds(pl.multiple_of(cb * slab, slab), slab)
            hfr_ref[sf, :] = h_re
            hfi_ref[sf, :] = h_im
            hbr_ref[sb, :] = h_re
            hbi_ref[sb, :] = h_im
            e_re = jnp.where(fwd, ere_ref[sf, :], ere_ref[sb, :])
            e_im = jnp.where(fwd, eim_ref[sf, :], eim_ref[sb, :])
            return (a_re * h_re - a_im * h_im + e_re, a_re * h_im + a_im * h_re + e_im)

        zero = jnp.zeros((2 * nb, LANES), F32)
        lax.fori_loop(0, n_chunks, step, (zero, zero), unroll=8)

        for gl in range(2):
            g = gp * 2 + gl
            wloc, wcar = wloc_ref[g], wcar_ref[g]
            for b in range(nb):
                rows = lat_rows(gl * nb + b)
                h_cat = jnp.concatenate([jnp.where(lane_l, hfr_ref[rows, :], hbr_ref[rows, :]),
                                         jnp.where(lane_l, hfi_ref[rows, :], hbi_ref[rows, :])],
                                        axis=-1).astype(BF16)
                yt = (jnp.dot(wloc, ut_ref[b, g].reshape(T * GP, n_lat_chunks), preferred_element_type=F32)
                      + lax.dot_general(wcar, h_cat, nt_dims, preferred_element_type=F32))
                yt_ref[b, g] = _gelu(yt).astype(BF16).reshape(T, GP, n_lat_chunks)
            crow = lambda ref: jnp.concatenate(
                [ref[ctx_rows(gl * nb + b), :] for b in range(nb)]
                + [jnp.zeros((LANES - cw, LANES), F32)], axis=0)
            h_cat = jnp.concatenate([jnp.where(lane_c, crow(hfr_ref), crow(hbr_ref)),
                                     jnp.where(lane_c, crow(hfi_ref), crow(hbi_ref))], axis=-1).astype(BF16)
            ytc = (jnp.dot(wloc, utc_ref[g].reshape(T * GP, LANES), preferred_element_type=F32)
                   + lax.dot_general(wcar, h_cat, nt_dims, preferred_element_type=F32))
            ytc_ref[g] = _gelu(ytc).astype(BF16).reshape(T, GP, LANES)
        return carry

    lax.fori_loop(0, gpl // 2, pair, 0)

    for b in range(nb):
        for t in range(T):
            z = yt_ref[b, :, t].astype(F32).reshape(LANES, n_lat_chunks)
            o_ref[b, pl.ds(t, n_lat_chunks, stride=T), :] = z.T
    for t in range(T):
        z = ytc_ref[:, t].astype(F32).reshape(LANES, LANES).T
        for b in range(nb):
            o_ref[b, pl.ds(n_lat + t, n_ctx_chunks, stride=T), :] = z[b * n_ctx_chunks:(b + 1) * n_ctx_chunks]


def _s5_call(l, u, P, n_lat):
    B, S, W = u.shape
    nb = 4 if B % 4 == 0 else B
    T, GP = S5_CHUNK, S5_GROUP
    nlc, ncc = n_lat // T, (S - n_lat) // T
    gpl = LANES // GP
    nblk = W // LANES
    big = lambda n: pl.BlockSpec((nb, S, LANES), lambda j, hb: (hb, 0, j), pipeline_mode=pl.Buffered(n))
    wspec = lambda: pl.BlockSpec((gpl, T * GP, T * GP), lambda j, hb: (l * nblk + j, 0, 0))
    rows = 2 * nb * (nlc + ncc)
    return pl.pallas_call(
        functools.partial(_s5_kernel, nb=nb, n_lat_chunks=nlc, n_ctx_chunks=ncc),
        out_shape=jax.ShapeDtypeStruct((B, S, W), F32),
        grid=(nblk, B // nb),
        in_specs=[big(2), wspec(), wspec(), wspec(),
                  pl.BlockSpec((gpl, 2, LANES), lambda j, hb: (l * nblk + j, 0, 0))],
        out_specs=big(2),
        scratch_shapes=[pltpu.VMEM((nb, gpl, T, GP, nlc), BF16), pltpu.VMEM((gpl, T, GP, LANES), BF16),
                        pltpu.VMEM((nb, gpl, T, GP, nlc), BF16), pltpu.VMEM((gpl, T, GP, LANES), BF16)]
                       + [pltpu.VMEM((rows, LANES), F32)] * 6,
        compiler_params=pltpu.CompilerParams(dimension_semantics=("arbitrary",) * 2,
                                             vmem_limit_bytes=S5_VMEM_LIMIT), name="s5",
    )(u, P["s5_wst"], P["s5_wloc_t"], P["s5_wcar_t"], P["s5_at"])


def _out_kernel(x_ref, c_ref, mod_ref, oa_ref, oac_ref, ob_ref, gy_ref, gt_ref, wglu_ref, wout_ref,
                xo_ref, co_ref, *, n_lat_tiles):
    t = pl.program_id(1)
    cb = OUT_COLS
    nb, rows = x_ref.shape[0], x_ref.shape[1]
    stack = lambda ref: jnp.concatenate([ref[i] for i in range(nb)], axis=0)
    gyb = stack(gy_ref).astype(BF16)
    g = stack(gt_ref).astype(F32)
    oa = jnp.where(t == n_lat_tiles, stack(oac_ref), stack(oa_ref))
    m_a = (oa.astype(F32) * g[:, 0:512]).astype(BF16)
    m_b = (stack(ob_ref).astype(F32) * g[:, 512:1024]).astype(BF16)
    oc = []
    for j in range(512 // cb):
        za = jnp.dot(gyb, wglu_ref[0, :, j * cb:(j + 1) * cb], preferred_element_type=F32)
        zb = jnp.dot(gyb, wglu_ref[0, :, 512 + j * cb:512 + (j + 1) * cb], preferred_element_type=F32)
        oc.append((za * _sigmoid(zb) * g[:, 1024 + j * cb:1024 + (j + 1) * cb]).astype(BF16))
    m_c = jnp.concatenate(oc, axis=-1)
    gate = jnp.concatenate([jnp.broadcast_to(mod_ref[0, i][2:3], (rows, x_ref.shape[2])) for i in range(nb)],
                           axis=0)
    resid = jnp.where(t == n_lat_tiles, stack(c_ref), stack(x_ref))
    new = []
    for j in range(resid.shape[1] // cb):
        cols = slice(j * cb, (j + 1) * cb)
        upd = (jnp.dot(m_a, wout_ref[0, 0:512, cols], preferred_element_type=F32)
               + jnp.dot(m_b, wout_ref[0, 512:1024, cols], preferred_element_type=F32)
               + jnp.dot(m_c, wout_ref[0, 1024:1536, cols], preferred_element_type=F32))
        new.append(resid[:, cols] + gate[:, cols] * upd)

    @pl.when(t == n_lat_tiles)
    def _():
        for j, v in enumerate(new):
            for i in range(nb):
                co_ref[i, :, j * cb:(j + 1) * cb] = v[i * rows:(i + 1) * rows]

    @pl.when(t < n_lat_tiles)
    def _():
        for j, v in enumerate(new):
            for i in range(nb):
                xo_ref[i, :, j * cb:(j + 1) * cb] = v[i * rows:(i + 1) * rows]


def _out_call(l, x, ctx, mod, oa, oa_ctx, ob, gy, gt, P):
    B, L, D = x.shape
    C = ctx.shape[1]
    S = C + L
    nl = L // TOK
    nb = PROJ_NB
    xmap = lambda b, t: (b, jnp.minimum(t, nl - 1), 0)
    cmap = lambda b, t: (b, 0, 0)
    tmap = lambda b, t: (b, t, 0)
    return pl.pallas_call(
        functools.partial(_out_kernel, n_lat_tiles=nl),
        out_shape=[jax.ShapeDtypeStruct(x.shape, F32), jax.ShapeDtypeStruct(ctx.shape, F32)],
        grid=(B // nb, S // TOK),
        in_specs=[pl.BlockSpec((nb, TOK, D), xmap),
                  pl.BlockSpec((nb, TOK, D), cmap),
                  pl.BlockSpec((1, nb, 3, D), lambda b, t: (l, jnp.where(t == nl, B // nb, b), 0, 0)),
                  pl.BlockSpec((nb, TOK, 512), xmap),
                  pl.BlockSpec((nb, TOK, 512), cmap),
                  pl.BlockSpec((nb, TOK, 512), tmap),
                  pl.BlockSpec((nb, TOK, 512), tmap),
                  pl.BlockSpec((nb, TOK, 1536), tmap),
                  pl.BlockSpec((1, 512, 1024), lambda b, t: (l, 0, 0)),
                  pl.BlockSpec((1, 1536, D), lambda b, t: (l, 0, 0))],
        out_specs=[pl.BlockSpec((nb, TOK, D), xmap), pl.BlockSpec((nb, TOK, D), cmap)],
        compiler_params=_cparams(2), name="out",
    )(x, ctx, mod, oa, oa_ctx, ob, gy, gt, P["w_glu"], P["w_out"])


def _rope_angles(n_lat, n_ctx, rot_dim):
    rows = n_lat // GRID_W
    r_idx, c_idx = jnp.meshgrid(jnp.arange(rows), jnp.arange(GRID_W), indexing="ij")
    r_idx, c_idx = r_idx.reshape(-1), c_idx.reshape(-1)
    n_freq = rot_dim // 4
    freqs = ROPE_BASE ** (-jnp.arange(n_freq, dtype=F32) / n_freq)
    ang = jnp.concatenate([r_idx.astype(F32)[:, None] * freqs,
                           c_idx.astype(F32)[:, None] * freqs], axis=-1)
    return jnp.concatenate([ang, jnp.zeros((n_ctx, rot_dim // 2), F32)], axis=0)


def _rope_tables_t(n_lat, n_ctx, rot_dim):
    ang = _rope_angles(n_lat, n_ctx, rot_dim).T
    return jnp.stack([jnp.cos(ang), jnp.sin(ang)])


def _rope_tables(n_lat, n_ctx, rot_dim, lead, reps):
    ang = _rope_angles(n_lat, n_ctx, rot_dim)
    cos, sin, zero = jnp.cos(ang), jnp.sin(ang), jnp.zeros_like(ang)
    n = ang.shape[0]
    tail = LANES // reps - lead - rot_dim

    def pack(x1, x2, fill):
        unit = [jnp.full((n, lead), fill, F32), x1, x2, jnp.full((n, tail), fill, F32)]
        return jnp.concatenate(unit * reps, axis=-1)

    return jnp.stack([pack(cos, cos, 1.0), pack(zero, sin, 0.0), pack(-sin, zero, 0.0)])


def _prep_params(norm_g, w_in, w_out, mla_g_cq, mla_g_ckv, mla_w_uq, mla_w_ukv, mla_g_qn, mla_g_kn,
                 swa_g_qn, swa_g_kn, swa_sink, s5_a_re, s5_a_im, s5_log_dt, s5_b_re, s5_b_im,
                 s5_c_re, s5_c_im, s5_d, s5_w_glu):
    depth, D, _ = w_in.shape
    o_cq, o_ckv, o_kr, o_gm, o_sq, o_sk, o_sv, o_gs, o_u, o_g5, o_end = (
        0, 384, 640, 672, 1184, 1696, 1824, 1952, 2464, 2976, 3488)
    z = lambda n: jnp.zeros((depth, D, n), F32)
    w_in_p = (
        w_in[:, :, o_cq:o_kr].astype(BF16),
        jnp.concatenate([z(64), w_in[:, :, o_kr:o_gm], z(32)], axis=-1).astype(BF16),
        w_in[:, :, o_sq:o_gs].astype(BF16),
        w_in[:, :, o_u:o_g5].astype(BF16),
        jnp.concatenate([w_in[:, :, o_gm:o_sq], w_in[:, :, o_gs:o_u], w_in[:, :, o_g5:o_end]],
                        axis=-1).astype(BF16),
    )
    assert sum(w.shape[-1] for w in w_in_p) == C_END

    wq = mla_w_uq.reshape(depth, MLA_Q_RANK, MLA_HEADS, MLA_QK)
    wq = jnp.pad(wq, ((0, 0), (0, 0), (0, 0), (0, LANES - MLA_QK)))
    w_uq_p = wq.reshape(depth, MLA_Q_RANK, MLA_HEADS * LANES).astype(BF16)
    wkv = mla_w_ukv.reshape(depth, MLA_KV_RANK, MLA_HEADS, MLA_NOPE + MLA_V)
    wk = jnp.pad(wkv[..., :MLA_NOPE], ((0, 0), (0, 0), (0, 0), (0, LANES - MLA_NOPE)))
    w_ukv_p = jnp.concatenate([wk.reshape(depth, MLA_KV_RANK, MLA_HEADS * LANES),
                               wkv[..., MLA_NOPE:].reshape(depth, MLA_KV_RANK, MLA_HEADS * MLA_V)],
                              axis=-1).astype(BF16)

    pad_qk = lambda g: jnp.pad(g, ((0, 0), (0, LANES - MLA_QK)))[:, None, :]
    g_q = pad_qk(mla_g_qn * (MLA_QK ** -0.5 * LOG2E))
    g_qt = jnp.broadcast_to(jnp.swapaxes(g_q, 1, 2), (depth, LANES, LANES))
    g_k = pad_qk(mla_g_kn)
    mla_bound = 1.02 * MLA_QK * jnp.max(jnp.abs(g_q), axis=(1, 2)) * jnp.max(jnp.abs(g_k), axis=(1, 2))
    g_sqt = jnp.broadcast_to((swa_g_qn * (SWA_DIM ** -0.5 * LOG2E))[:, :, None], (depth, SWA_DIM, SWA_BLOCK))
    g_sk = jnp.tile(swa_g_kn, (1, 2))[:, None, :]
    per_kv = SWA_HEADS // SWA_KV_HEADS
    lanes_of = lambda v: jnp.broadcast_to(v.reshape(depth * SWA_KV_HEADS, 1, per_kv, 1),
                                          (depth * SWA_KV_HEADS, 8, per_kv, SWA_BLOCK)
                                          ).reshape(depth * SWA_KV_HEADS, 8, -1).astype(F32)
    sink2 = swa_sink * LOG2E
    sinkrow = lanes_of(sink2)
    swa_bound = 1.02 * SWA_DIM * jnp.max(jnp.abs(g_sqt), axis=(1, 2)) * jnp.max(jnp.abs(g_sk), axis=(1, 2))
    shift = jnp.maximum(swa_bound[:, None], sink2)
    shiftrow = lanes_of(shift)
    swa_range = (swa_bound + jnp.max(shift, axis=1)).astype(F32)

    T = S5_CHUNK
    A = lax.complex(s5_a_re, s5_a_im)
    dt = jnp.exp(s5_log_dt)[..., None]
    a_bar = jnp.exp(dt * A)
    b_bar = ((a_bar - 1.0) / A)[..., None] * lax.complex(s5_b_re, s5_b_im)
    c_mat = lax.complex(s5_c_re, s5_c_im)
    k_idx = jnp.arange(T + 1, dtype=F32)
    pw = jnp.exp(k_idx[None, None, None, :, None] * (dt * A)[:, :, :, None, :])
    hi = lax.Precision.HIGHEST
    tt = jnp.arange(T)
    GP, NG = S5_GROUP, depth * S5_GROUPS
    pw_f, pw_b = pw[:, 0], pw[:, 1]
    tap_f = jnp.einsum("lgpn,lgkn,lgnq->lgpkq", c_mat[:, 0], pw_f[:, :, T - 1 - tt], b_bar[:, 0],
                       precision=hi).real
    tap_b = jnp.einsum("lgpn,lgkn,lgnq->lgpkq", c_mat[:, 1], pw_b[:, :, tt], b_bar[:, 1],
                       precision=hi).real
    d_diag = s5_d.reshape(depth, S5_GROUPS, GP)[..., None] * jnp.eye(GP, dtype=F32)
    centre = tap_f[:, :, :, T - 1:] + tap_b[:, :, :, :1] + d_diag[:, :, :, None, :]
    krev = jnp.concatenate([tap_f[:, :, :, :T - 1], centre, tap_b[:, :, :, 1:]], axis=3)
    krev = krev.reshape(depth, S5_GROUPS, GP, (2 * T - 1) * GP)
    wloc_t = jnp.stack([krev[..., (T - 1 - t) * GP:(T - 1 - t) * GP + T * GP] for t in range(T)], axis=2)
    wloc_t = wloc_t.reshape(NG, T * GP, T * GP)
    b_t = jnp.swapaxes(b_bar, -1, -2)
    inc_f = pw_f[:, :, T - 1 - tt][:, :, :, None, :] * b_t[:, 0][:, :, None]
    inc_b = pw_b[:, :, tt][:, :, :, None, :] * b_t[:, 1][:, :, None]
    wst = jnp.concatenate([inc_f.real, inc_b.real, inc_f.imag, inc_b.imag], axis=-1)
    wst = wst.reshape(NG, T * GP, 4 * S5_STATE)
    ro_f = c_mat[:, 0][:, :, None] * pw_f[:, :, tt + 1][:, :, :, None, :]
    ro_b = c_mat[:, 1][:, :, None] * pw_b[:, :, T - tt][:, :, :, None, :]
    wcar_t = jnp.concatenate([ro_f.real, ro_b.real, -ro_f.imag, -ro_b.imag], axis=-1)
    wcar_t = wcar_t.reshape(NG, T * GP, 4 * S5_STATE)
    a_t = pw[:, :, :, T]
    at = jnp.stack([jnp.concatenate([a_t[:, 0].real, a_t[:, 1].real], axis=-1),
                    jnp.concatenate([a_t[:, 0].imag, a_t[:, 1].imag], axis=-1)], axis=2)
    at = at.reshape(NG, 2, LANES).astype(F32)

    return dict(norm_g=norm_g[:, None, :], w_in=w_in_p, g_cq=mla_g_cq[:, None, :], w_uq=w_uq_p, g_qt=g_qt, mla_bound=mla_bound.astype(F32),
                g_ckv=mla_g_ckv[:, None, :], w_ukv=w_ukv_p, g_k=g_k, g_sqt=g_sqt, g_sk=g_sk,
                sinkrow=sinkrow, shiftrow=shiftrow, swa_range=swa_range, s5_wloc_t=wloc_t.astype(BF16), s5_wst=wst.astype(BF16),
                s5_wcar_t=wcar_t.astype(BF16), s5_at=at, w_glu=s5_w_glu.astype(BF16),
                w_out=w_out.astype(BF16))


def kernel(x, c, ctx, c_ctx, norm_g, w_ada, b_ada, w_in, w_out, mla_g_cq, mla_g_ckv, mla_w_uq, mla_w_ukv, mla_g_qn, mla_g_kn, swa_g_qn, swa_g_kn, swa_sink, s5_a_re, s5_a_im, s5_log_dt, s5_b_re, s5_b_im, s5_c_re, s5_c_im, s5_d, s5_w_glu):
    B, L, D = x.shape
    C = ctx.shape[1]
    S = C + L
    depth = w_in.shape[0]
    assert B + PROJ_NB <= MOD_ROWS and C == TOK and L % (2 * TOK) == 0 and L % GRID_W == 0 and L >= 3 * SWA_BLOCK

    P = _prep_params(norm_g, w_in, w_out, mla_g_cq, mla_g_ckv, mla_w_uq, mla_w_ukv, mla_g_qn, mla_g_kn,
                     swa_g_qn, swa_g_kn, swa_sink, s5_a_re, s5_a_im, s5_log_dt, s5_b_re, s5_b_im,
                     s5_c_re, s5_c_im, s5_d, s5_w_glu)
    rope_mla = _rope_tables(L, C, MLA_ROPE, MLA_NOPE, 1)
    rope_swa = _rope_tables(L, C, SWA_DIM, 0, 2)
    rope_mla_t = _rope_tables_t(L, C, MLA_ROPE)
    rope_swa_t = jnp.transpose(_rope_tables_t(L, C, SWA_DIM).reshape(2, SWA_DIM // 2, S // SWA_BLOCK, SWA_BLOCK),
                               (2, 0, 1, 3))

    cc = jnp.concatenate([c, jnp.tile(c_ctx[None, :], (PROJ_NB, 1)),
                          jnp.zeros((MOD_ROWS - B - PROJ_NB, D), F32)], axis=0)
    mod = _ada_call(cc, w_ada, b_ada).reshape(depth, MOD_ROWS, 3, D)

    for l in range(depth):
        qa, ka, va, qb, kb, vb, u, gt = _proj_call(l, x, ctx, mod, P, rope_mla, rope_swa)
        oa, oa_ctx = _mla_call(l, qa, ka, va, P["g_qt"], rope_mla_t, P["mla_bound"], L)
        ob = _swa_call(l, qb, kb, vb, P["sinkrow"], P["shiftrow"], P["swa_range"], P["g_sqt"], rope_swa_t, C)
        gy = _s5_call(l, u, P, L)
        x, ctx = _out_call(l, x, ctx, mod, oa, oa_ctx, ob, gy, gt, P)
    return x
```
